```python
import math
import jax, jax.numpy as jnp
from jax import lax
import numpy as np

D_MODEL = 1024
BATCH = 4
SEQ = 8192
DEPTH = 1

HEAD_DIM = 64
DIL_PAIRS = ((128, 1), (512, 4), (2048, 16))
N_GROUPS = len(DIL_PAIRS)
H_A = 8
H_B = 8
DV_B = 2 * HEAD_DIM
D_FF = 4 * D_MODEL
NUM_BUCKETS = 32
T5_MAX_DISTANCE = 128
BLK = 128
LN_EPS = 1e-5
NEG_INF = -1e30
H_TOTAL = N_GROUPS * H_A + H_B
W_A_OUT = H_A * HEAD_DIM
W_B_OUT = H_B * DV_B
COLS_A = 3 * N_GROUPS * H_A * HEAD_DIM
COLS_B_QK = 4 * H_B * HEAD_DIM
COLS_B = COLS_B_QK + H_B * DV_B
COLS_GATE = 2 * D_MODEL
COLS_IN = COLS_A + COLS_B + COLS_GATE
DEEPNORM_ALPHA = (2.0 * DEPTH) ** 0.25
DEEPNORM_BETA = (8.0 * DEPTH) ** -0.25

kernel_name = "hybrid_dilated_diff_attn_gated_deepnorm"


def t5_bucket(dist):
    n = jnp.maximum(dist, 0)
    max_exact = NUM_BUCKETS // 2
    nf = jnp.maximum(n, 1).astype(jnp.float32)
    large = max_exact + (jnp.log(nf / max_exact) / math.log(T5_MAX_DISTANCE / max_exact)
                         * (NUM_BUCKETS - max_exact)).astype(jnp.int32)
    large = jnp.minimum(large, NUM_BUCKETS - 1)
    return jnp.where(n < max_exact, n, large)


def layer_norm(x, g, b):
    xf = x.astype(jnp.float32)
    mu = jnp.mean(xf, axis=-1, keepdims=True)
    var = jnp.mean(jnp.square(xf - mu), axis=-1, keepdims=True)
    return ((xf - mu) * lax.rsqrt(var + LN_EPS) * g + b).astype(x.dtype)


def rms_norm(x, g):
    xf = x.astype(jnp.float32)
    return xf * lax.rsqrt(jnp.mean(jnp.square(xf), axis=-1, keepdims=True) + LN_EPS) * g


def dilated_group_attention(q, k, v, bias_table, dil, n_steps):
    B, S, H, dh = q.shape
    L = S // dil
    nb = -(-L // BLK)
    Lp = nb * BLK

    def to_blocks(t):
        t = t.reshape(B, L, dil, H, dh).transpose(0, 2, 1, 3, 4)
        t = jnp.pad(t, ((0, 0), (0, 0), (0, Lp - L), (0, 0), (0, 0)))
        return t.reshape(B, dil, nb, BLK, H, dh)

    def with_prev(t):
        prev = jnp.pad(t, ((0, 0), (0, 0), (1, 0), (0, 0), (0, 0), (0, 0)))[:, :, :-1]
        return jnp.concatenate([prev, t], axis=3)

    qb = to_blocks(q * (HEAD_DIM ** -0.5))
    kw = with_prev(to_blocks(k))
    vw = with_prev(to_blocks(v))
    s = jnp.einsum('brnqhd,brnkhd->brnhqk', qb, kw).astype(jnp.float32)

    qi = jnp.arange(BLK)[:, None]
    ki = jnp.arange(2 * BLK)[None, :]
    steps = BLK + qi - ki
    bias = bias_table[t5_bucket(steps * dil)].astype(jnp.float32).transpose(2, 0, 1)
    in_band = (steps >= 0) & (steps <= n_steps)
    first_ok = (jnp.arange(nb)[:, None, None] > 0) | (ki >= BLK)[None]
    valid = in_band[None] & first_ok
    s = jnp.where(valid[None, None, :, None], s + bias, NEG_INF)

    m = jnp.max(s, axis=-1, keepdims=True)
    p = jnp.exp(s - m)
    den = jnp.sum(p, axis=-1, keepdims=True)
    o = jnp.einsum('brnhqk,brnkhd->brnqhd', p, vw) / jnp.moveaxis(den, 3, 4)
    lse = jnp.moveaxis((m + jnp.log(den))[..., 0], 3, 4)

    o = o.reshape(B, dil, Lp, H, dh)[:, :, :L].transpose(0, 2, 1, 3, 4).reshape(B, S, H, dh)
    lse = lse.reshape(B, dil, Lp, H)[:, :, :L].transpose(0, 2, 1, 3).reshape(B, S, H)
    return o, lse


def diff_attention(q1, q2, k1, k2, v, bias_table, lam):
    B, S, H, dh = q1.shape
    nb = S // BLK
    kpos = jnp.arange(S)
    scale = HEAD_DIM ** -0.5

    def block(n):
        qpos = n * BLK + jnp.arange(BLK)
        dist = qpos[:, None] - kpos[None, :]
        bias = bias_table[t5_bucket(dist)].astype(jnp.float32).transpose(2, 0, 1)
        causal = dist >= 0

        def attn_map(qf, kf):
            qb = lax.dynamic_slice_in_dim(qf, n * BLK, BLK, axis=1)
            s = jnp.einsum('bqhd,bkhd->bhqk', qb, kf).astype(jnp.float32) * scale + bias
            return jax.nn.softmax(jnp.where(causal, s, NEG_INF), axis=-1)

        a = attn_map(q1, k1) - lam * attn_map(q2, k2)
        return jnp.einsum('bhqk,bkhe->bqhe', a, v)

    out = lax.map(block, jnp.arange(nb))
    return out.transpose(1, 0, 2, 3, 4).reshape(B, S, H, DV_B)


def mixing_sublayer(h, w_in, b_gate, lq1, lk1, lq2, lk2, subln_g, rel_bias,
                    w_proj_a, w_proj_b, w_out, layer_idx):
    B, S, D = h.shape
    proj = h @ w_in

    pa = proj[..., :COLS_A].reshape(B, S, 3, N_GROUPS, H_A, HEAD_DIM)
    outs, lses = [], []
    for g, (win, dil) in enumerate(DIL_PAIRS):
        o, l = dilated_group_attention(pa[:, :, 0, g], pa[:, :, 1, g], pa[:, :, 2, g],
                                       rel_bias[:, g * H_A:(g + 1) * H_A], dil, win // dil)
        outs.append(o)
        lses.append(l)
    wts = jax.nn.softmax(jnp.stack(lses), axis=0)
    o_a = jnp.sum(wts[..., None] * jnp.stack(outs), axis=0)
    y_a = o_a.reshape(B, S, W_A_OUT).astype(h.dtype) @ w_proj_a

    pb = proj[..., COLS_A:COLS_A + COLS_B]
    qk = pb[..., :COLS_B_QK].reshape(B, S, 4, H_B, HEAD_DIM)
    v_b = pb[..., COLS_B_QK:].reshape(B, S, H_B, DV_B)
    lam_init = 0.8 - 0.6 * math.exp(-0.3 * layer_idx)
    lam = (jnp.exp(jnp.sum(lq1.astype(jnp.float32) * lk1.astype(jnp.float32)))
           - jnp.exp(jnp.sum(lq2.astype(jnp.float32) * lk2.astype(jnp.float32))) + lam_init)
    o_b = diff_attention(qk[:, :, 0], qk[:, :, 1], qk[:, :, 2], qk[:, :, 3], v_b,
                         rel_bias[:, N_GROUPS * H_A:], lam)
    o_b = rms_norm(o_b, subln_g) * (1.0 - lam_init)
    y_b = o_b.reshape(B, S, W_B_OUT).astype(h.dtype) @ w_proj_b

    gates = jax.nn.sigmoid(proj[..., COLS_A + COLS_B:] + b_gate).reshape(B, S, 2, D)
    return (gates[:, :, 0] * y_a + gates[:, :, 1] * y_b) @ w_out


def setup_inputs(seed: int = 0) -> dict:
    key = jax.random.key(seed)
    ks = jax.random.split(key, 20)
    nrm = jax.random.normal
    beta = DEEPNORM_BETA
    col_scale = np.ones((COLS_IN,), dtype=np.float32)
    col_scale[2 * N_GROUPS * H_A * HEAD_DIM:COLS_A] = beta
    col_scale[COLS_A + COLS_B_QK:COLS_A + COLS_B] = beta
    return {
        "x": nrm(ks[0], (BATCH, SEQ, D_MODEL), jnp.float32),
        "w_in": nrm(ks[1], (DEPTH, D_MODEL, COLS_IN), jnp.float32) * (D_MODEL ** -0.5) * jnp.asarray(col_scale),
        "b_gate": 0.1 * nrm(ks[2], (DEPTH, COLS_GATE), jnp.float32),
        "lambda_q1": 0.1 * nrm(ks[3], (DEPTH, HEAD_DIM), jnp.float32),
        "lambda_k1": 0.1 * nrm(ks[4], (DEPTH, HEAD_DIM), jnp.float32),
        "lambda_q2": 0.1 * nrm(ks[5], (DEPTH, HEAD_DIM), jnp.float32),
        "lambda_k2": 0.1 * nrm(ks[6], (DEPTH, HEAD_DIM), jnp.float32),
        "subln_g": 1.0 + 0.05 * nrm(ks[7], (DEPTH, DV_B), jnp.float32),
        "rel_bias": 0.1 * nrm(ks[8], (NUM_BUCKETS, H_TOTAL), jnp.float32),
        "w_proj_a": nrm(ks[9], (DEPTH, W_A_OUT, D_MODEL), jnp.float32) * beta * (W_A_OUT ** -0.5),
        "w_proj_b": nrm(ks[10], (DEPTH, W_B_OUT, D_MODEL), jnp.float32) * beta * (W_B_OUT ** -0.5),
        "w_out": nrm(ks[11], (DEPTH, D_MODEL, D_MODEL), jnp.float32) * beta * (D_MODEL ** -0.5),
        "ln1_g": 1.0 + 0.05 * nrm(ks[12], (DEPTH, D_MODEL), jnp.float32),
        "ln1_b": 0.02 * nrm(ks[13], (DEPTH, D_MODEL), jnp.float32),
        "ln2_g": 1.0 + 0.05 * nrm(ks[14], (DEPTH, D_MODEL), jnp.float32),
        "ln2_b": 0.02 * nrm(ks[15], (DEPTH, D_MODEL), jnp.float32),
        "w_mlp1": nrm(ks[16], (DEPTH, D_MODEL, D_FF), jnp.float32) * beta * (D_MODEL ** -0.5),
        "w_mlp2": nrm(ks[17], (DEPTH, D_FF, D_MODEL), jnp.float32) * beta * (D_FF ** -0.5),
    }


def reference(x, w_in, b_gate, lambda_q1, lambda_k1, lambda_q2, lambda_k2, subln_g, rel_bias,
              w_proj_a, w_proj_b, w_out, ln1_g, ln1_b, ln2_g, ln2_b, w_mlp1, w_mlp2):
    h = x
    for l in range(DEPTH):
        mix = mixing_sublayer(h, w_in[l], b_gate[l], lambda_q1[l], lambda_k1[l], lambda_q2[l],
                              lambda_k2[l], subln_g[l], rel_bias, w_proj_a[l], w_proj_b[l],
                              w_out[l], l)
        h = layer_norm(DEEPNORM_ALPHA * h + mix, ln1_g[l], ln1_b[l])
        ff = jnp.square(jax.nn.relu(h @ w_mlp1[l])) @ w_mlp2[l]
        h = layer_norm(DEEPNORM_ALPHA * h + ff, ln2_g[l], ln2_b[l])
    return h
```

```python
import functools
import math

import jax
import jax.numpy as jnp
import numpy as np
from jax import lax
from jax.experimental import pallas as pl
from jax.experimental.pallas import tpu as pltpu

D_MODEL = 1024
HEAD_DIM = 64
DIL_PAIRS = ((128, 1), (512, 4), (2048, 16))
N_GROUPS = len(DIL_PAIRS)
H_A = 8
H_B = 8
DV_B = 2 * HEAD_DIM
D_FF = 4 * D_MODEL
NUM_BUCKETS = 32
T5_MAX_DISTANCE = 128
BLK = 128
LN_EPS = 1e-5
NEG_INF = -1e30
W_A_OUT = H_A * HEAD_DIM
W_B_OUT = H_B * DV_B
COLS_A = 3 * N_GROUPS * H_A * HEAD_DIM
COLS_B_QK = 4 * H_B * HEAD_DIM
COLS_B = COLS_B_QK + H_B * DV_B
DEPTH = 1
DEEPNORM_ALPHA = (2.0 * DEPTH) ** 0.25
QK_SCALE = HEAD_DIM ** -0.5

LANES = 128
COLS_N = COLS_A + H_B * 2 * HEAD_DIM
COLS_T = 2 * W_B_OUT
CHUNK = 512
TQ = 512
TM = 512
VMEM_LIMIT = 56 * 1024 * 1024

BF16 = jnp.bfloat16
F32 = jnp.float32


def _dot(a, b):
    return jnp.dot(a, b, preferred_element_type=F32)


def _dot_nt(a, b):
    return lax.dot_general(a, b, (((1,), (1,)), ((), ())), preferred_element_type=F32)


def _resident(shape):
    nd = len(shape)
    return pl.BlockSpec(shape, lambda *_: (0,) * nd, pipeline_mode=pl.Buffered(1))


def _params(*sem):
    return pltpu.CompilerParams(dimension_semantics=sem, vmem_limit_bytes=VMEM_LIMIT)


def _proj_kernel(x_ref, wn_ref, wt_ref, on_ref, ot_ref, *, tq):
    xb = x_ref[...].astype(BF16)
    tm = xb.shape[0]
    for c in range(COLS_N // CHUNK):
        sl = slice(c * CHUNK, (c + 1) * CHUNK)
        on_ref[:, sl] = _dot(xb, wn_ref[:, sl]).astype(BF16)
    for c in range(COLS_T // CHUNK):
        sl = slice(c * CHUNK, (c + 1) * CHUNK)
        r = _dot_nt(wt_ref[sl, :], xb).astype(BF16)
        for t in range(tm // tq):
            ot_ref[t, sl, :] = r[:, t * tq:(t + 1) * tq]


def _proj(x, w_n, w_t, tm, tq):
    B, S, D = x.shape
    return pl.pallas_call(
        functools.partial(_proj_kernel, tq=tq),
        grid=(B, S // tm),
        in_specs=[
            pl.BlockSpec((None, tm, D), lambda b, i: (b, i, 0)),
            _resident((D, COLS_N)),
            _resident((COLS_T, D)),
        ],
        out_specs=[
            pl.BlockSpec((None, tm, COLS_N), lambda b, i: (b, i, 0)),
            pl.BlockSpec((None, tm // tq, COLS_T, tq), lambda b, i: (b, i, 0, 0)),
        ],
        out_shape=[
            jax.ShapeDtypeStruct((B, S, COLS_N), BF16),
            jax.ShapeDtypeStruct((B, S // tq, COLS_T, tq), BF16),
        ],
        compiler_params=_params("parallel", "parallel"),
        name="proj",
    )(x, w_n, w_t)


def _dilated_kernel(q_ref, k_ref, v_ref, kp_ref, vp_ref, bias_ref, o_ref, lse_ref):
    tb = q_ref.shape[0]
    first = pl.program_id(2) == 0
    lane = lax.broadcasted_iota(jnp.int32, (BLK, LANES), 1)
    lo = lane < HEAD_DIM
    for jb in range(tb // BLK):
        rows = slice(jb * BLK, (jb + 1) * BLK)
        prows = slice((jb - 1) * BLK, jb * BLK)
        for p in range(H_A // 2):
            cols = slice(p * LANES, (p + 1) * LANES)
            q2 = q_ref[rows, cols]
            zero = jnp.zeros_like(q2)
            qd = jnp.concatenate([jnp.where(lo, q2, zero), jnp.where(lo, zero, q2)], axis=0)
            kc, vc = k_ref[rows, cols], v_ref[rows, cols]
            if jb == 0:
                kp, vp = kp_ref[:, cols], vp_ref[:, cols]
            else:
                kp, vp = k_ref[prows, cols], v_ref[prows, cols]
            s_c = _dot_nt(qd, kc) + bias_ref[p, 1]
            s_p = _dot_nt(qd, kp) + bias_ref[p, 0]
            if jb == 0:
                s_p = jnp.where(first, NEG_INF, s_p)
            m = jnp.maximum(jnp.max(s_c, axis=-1, keepdims=True), jnp.max(s_p, axis=-1, keepdims=True))
            p_c = jnp.exp(s_c - m)
            p_p = jnp.exp(s_p - m)
            den = jnp.sum(p_c, axis=-1, keepdims=True) + jnp.sum(p_p, axis=-1, keepdims=True)
            o2 = (_dot(p_c.astype(BF16), vc) + _dot(p_p.astype(BF16), vp)) / den
            lse = jnp.broadcast_to(m + jnp.log(den), (2 * BLK, LANES))
            o_ref[rows, cols] = jnp.where(lo, o2[:BLK], o2[BLK:])
            lse_ref[rows, cols] = jnp.where(lo, lse[:BLK], lse[BLK:])


def _dilated(pn, bias, g, dil):
    B, S, _ = pn.shape
    L = S // dil
    tb = min(CHUNK, L)
    nblk = COLS_N // W_A_OUT
    view = pn.reshape(B, L, dil * COLS_N)
    qc, kc, vc = g, N_GROUPS + g, 2 * N_GROUPS + g
    cur = lambda c: pl.BlockSpec((None, tb, W_A_OUT), lambda b, r, n: (b, n, r * nblk + c))
    prev = lambda c: pl.BlockSpec(
        (None, BLK, W_A_OUT), lambda b, r, n: (b, jnp.maximum(n * (tb // BLK) - 1, 0), r * nblk + c))
    out = pl.BlockSpec((None, tb, W_A_OUT), lambda b, r, n: (b, n, r))
    o, lse = pl.pallas_call(
        _dilated_kernel,
        grid=(B, dil, L // tb),
        in_specs=[cur(qc), cur(kc), cur(vc), prev(kc), prev(vc), _resident(bias.shape)],
        out_specs=[out, out],
        out_shape=[jax.ShapeDtypeStruct((B, L, dil * W_A_OUT), F32)] * 2,
        compiler_params=_params("parallel", "parallel", "arbitrary"),
        name=f"dilated{g}",
    )(view, view, view, view, view, bias)
    return o.reshape(B, S, W_A_OUT), lse.reshape(B, S, W_A_OUT)


def _diff_kernel(lam_ref, qt_ref, k_ref, vt_ref, bd_ref, bp_ref, g_ref, o_ref, m_sc, l_sc, acc_sc, *, lam_init):
    tq = qt_ref.shape[1]
    qi = pl.program_id(2)
    qt = qt_ref[...]
    row = lax.broadcasted_iota(jnp.int32, qt.shape, 0)
    zero = jnp.zeros_like(qt)
    qd = jnp.concatenate([jnp.where(row < HEAD_DIM, qt, zero), jnp.where(row < HEAD_DIM, zero, qt)], axis=1)

    m_sc[...] = jnp.full(m_sc.shape, -jnp.inf, F32)
    l_sc[...] = jnp.zeros(l_sc.shape, F32)
    acc_sc[...] = jnp.zeros(acc_sc.shape, F32)

    def step(j, bias):
        kblk = k_ref[pl.ds(pl.multiple_of(j * tq, tq), tq), :]
        s = _dot(kblk, qd)
        if bias is not None:
            s = s + jnp.concatenate([bias, bias], axis=1)
        m_old = m_sc[...]
        m_new = jnp.maximum(m_old, jnp.max(s, axis=0, keepdims=True))
        p = jnp.exp(s - m_new)
        alpha = jnp.exp(m_old - m_new)
        l_sc[...] = alpha * l_sc[...] + jnp.sum(p, axis=0, keepdims=True)
        acc_sc[...] = alpha * acc_sc[...] + _dot(vt_ref[j], p.astype(BF16))
        m_sc[...] = m_new

    step(qi, bd_ref[...])

    @pl.when(qi > 0)
    def _():
        step(qi - 1, bp_ref[...])

    lax.fori_loop(0, jnp.maximum(qi - 1, 0), lambda j, c: (step(j, None), c)[1], 0)

    lam = (jnp.exp(jnp.sum(lam_ref[0:1, :] * lam_ref[1:2, :], axis=-1, keepdims=True))
           - jnp.exp(jnp.sum(lam_ref[2:3, :] * lam_ref[3:4, :], axis=-1, keepdims=True)) + lam_init)
    on = acc_sc[...] / l_sc[...]
    o = on[:, :tq] - lam * on[:, tq:]
    ms = jnp.mean(o * o, axis=0, keepdims=True)
    o = o * lax.rsqrt(ms + LN_EPS) * g_ref[...] * (1.0 - lam_init)
    o_ref[...] = o.T.astype(BF16)


def _diff(lam_p, pn, pt, bias_d, bias_p, g_col, lam_init):
    B, S, _ = pn.shape
    tq = pt.shape[3]
    nq = S // tq
    kcol0 = COLS_A // LANES
    return pl.pallas_call(
        functools.partial(_diff_kernel, lam_init=lam_init),
        grid=(B, H_B, nq),
        in_specs=[
            _resident(lam_p.shape),
            pl.BlockSpec((None, None, DV_B, tq), lambda b, h, i: (b, i, h, 0)),
            pl.BlockSpec((None, S, LANES), lambda b, h, i: (b, 0, kcol0 + h)),
            pl.BlockSpec((None, nq, DV_B, tq), lambda b, h, i: (b, 0, H_B + h, 0)),
            pl.BlockSpec((None, tq, tq), lambda b, h, i: (h, 0, 0)),
            pl.BlockSpec((None, tq, tq), lambda b, h, i: (h, 0, 0)),
            _resident(g_col.shape),
        ],
        out_specs=pl.BlockSpec((None, tq, DV_B), lambda b, h, i: (b, i, h)),
        out_shape=jax.ShapeDtypeStruct((B, S, W_B_OUT), BF16),
        scratch_shapes=[
            pltpu.VMEM((1, 2 * tq), F32),
            pltpu.VMEM((1, 2 * tq), F32),
            pltpu.VMEM((DV_B, 2 * tq), F32),
        ],
        compiler_params=_params("parallel", "parallel", "arbitrary"),
        name="diff_attn",
    )(lam_p, pt, pn, pt, bias_d, bias_p, g_col)


def _layer_norm(h, g, b):
    mu = jnp.mean(h, axis=-1, keepdims=True)
    d = h - mu
    var = jnp.mean(d * d, axis=-1, keepdims=True)
    return d * lax.rsqrt(var + LN_EPS) * g + b


def _mix_kernel(x_ref, o0_ref, o1_ref, o2_ref, l0_ref, l1_ref, l2_ref, ob_ref, wg_ref, bg_ref, wa_ref, wb_ref,
                wo_ref, g_ref, b_ref, out_ref):
    x = x_ref[...]
    xb = x.astype(BF16)
    l0, l1, l2 = l0_ref[...], l1_ref[...], l2_ref[...]
    mx = jnp.maximum(jnp.maximum(l0, l1), l2)
    e0, e1, e2 = jnp.exp(l0 - mx), jnp.exp(l1 - mx), jnp.exp(l2 - mx)
    o_a = (e0 * o0_ref[...] + e1 * o1_ref[...] + e2 * o2_ref[...]) / (e0 + e1 + e2)
    y_a = _dot(o_a.astype(BF16), wa_ref[...])
    y_b = _dot(ob_ref[...], wb_ref[...])
    gate_a = jax.nn.sigmoid(_dot(xb, wg_ref[:, :D_MODEL]) + bg_ref[:, :D_MODEL])
    gate_b = jax.nn.sigmoid(_dot(xb, wg_ref[:, D_MODEL:]) + bg_ref[:, D_MODEL:])
    merged = gate_a * y_a + gate_b * y_b
    mix = _dot(merged.astype(BF16), wo_ref[...])
    out_ref[...] = _layer_norm(DEEPNORM_ALPHA * x + mix, g_ref[...], b_ref[...])


def _mix(x, o_g, lse_g, o_b, w_gate, b_gate, w_a, w_b, w_o, ln_g, ln_b, tm):
    B, S, D = x.shape
    tok = lambda w: pl.BlockSpec((None, tm, w), lambda b, i: (b, i, 0))
    return pl.pallas_call(
        _mix_kernel,
        grid=(B, S // tm),
        in_specs=[tok(D)] + [tok(W_A_OUT)] * 6 + [tok(W_B_OUT)]
        + [_resident(a.shape) for a in (w_gate, b_gate, w_a, w_b, w_o, ln_g, ln_b)],
        out_specs=tok(D),
        out_shape=jax.ShapeDtypeStruct((B, S, D), F32),
        compiler_params=_params("parallel", "parallel"),
        name="mix",
    )(x, *o_g, *lse_g, o_b, w_gate, b_gate, w_a, w_b, w_o, ln_g, ln_b)


def _mlp_kernel(x_ref, w1_ref, w2_ref, g_ref, b_ref, out_ref):
    x = x_ref[...]
    xb = x.astype(BF16)
    ff = jnp.zeros(x.shape, F32)
    for c in range(D_FF // (2 * CHUNK)):
        sl = slice(c * 2 * CHUNK, (c + 1) * 2 * CHUNK)
        h = jnp.maximum(_dot(xb, w1_ref[:, sl]), 0.0)
        ff = ff + _dot((h * h).astype(BF16), w2_ref[sl, :])
    out_ref[...] = _layer_norm(DEEPNORM_ALPHA * x + ff, g_ref[...], b_ref[...])


def _mlp(x, w1, w2, ln_g, ln_b, tm):
    B, S, D = x.shape
    tok = pl.BlockSpec((None, tm, D), lambda b, i: (b, i, 0))
    return pl.pallas_call(
        _mlp_kernel,
        grid=(B, S // tm),
        in_specs=[tok] + [_resident(a.shape) for a in (w1, w2, ln_g, ln_b)],
        out_specs=tok,
        out_shape=jax.ShapeDtypeStruct((B, S, D), F32),
        compiler_params=_params("parallel", "parallel"),
        name="mlp",
    )(x, w1, w2, ln_g, ln_b)


def _t5_bucket(dist):
    n = jnp.maximum(dist, 0)
    max_exact = NUM_BUCKETS // 2
    nf = jnp.maximum(n, 1).astype(F32)
    large = max_exact + (jnp.log(nf / max_exact) / math.log(T5_MAX_DISTANCE / max_exact)
                         * (NUM_BUCKETS - max_exact)).astype(jnp.int32)
    large = jnp.minimum(large, NUM_BUCKETS - 1)
    return jnp.where(n < max_exact, n, large)


def _dilated_bias(rel_bias, g, dil):
    qi = jnp.arange(BLK)[:, None]
    ki = jnp.arange(2 * BLK)[None, :]
    steps = BLK + qi - ki
    tab = rel_bias[:, g * H_A:(g + 1) * H_A].astype(F32)
    bias = tab[_t5_bucket(steps * dil)].transpose(2, 0, 1)
    bias = jnp.where((steps >= 0) & (steps <= BLK), bias, NEG_INF)
    bias = bias.reshape(H_A // 2, 2, BLK, 2, BLK)
    return bias.transpose(0, 3, 1, 2, 4).reshape(H_A // 2, 2, 2 * BLK, BLK)


def _diff_bias(rel_bias, tq):
    assert tq >= T5_MAX_DISTANCE
    tab = rel_bias[:, N_GROUPS * H_A:].astype(F32)
    tab = tab - tab[NUM_BUCKETS - 1:NUM_BUCKETS]
    dist = jnp.arange(tq)[None, :] - jnp.arange(tq)[:, None]
    diag = jnp.where(dist >= 0, tab[_t5_bucket(dist)].transpose(2, 0, 1), NEG_INF)
    prev = tab[_t5_bucket(dist + tq)].transpose(2, 0, 1)
    return diag, prev


def kernel(x, w_in, b_gate, lambda_q1, lambda_k1, lambda_q2, lambda_k2, subln_g, rel_bias, w_proj_a, w_proj_b,
           w_out, ln1_g, ln1_b, ln2_g, ln2_b, w_mlp1, w_mlp2):
    B, S, D = x.shape
    tq = min(TQ, S)
    tm = min(TM, S)
    w = w_in[0]
    w_a = w[:, :COLS_A].reshape(D, 3, COLS_A // 3)
    w_a = jnp.concatenate([w_a[:, :1] * QK_SCALE, w_a[:, 1:]], axis=1).reshape(D, COLS_A)
    w_qk = w[:, COLS_A:COLS_A + COLS_B_QK].reshape(D, 4, H_B, HEAD_DIM)
    w_qb = (jnp.concatenate([w_qk[:, 0], w_qk[:, 1]], axis=-1) * QK_SCALE).reshape(D, W_B_OUT)
    w_kb = jnp.concatenate([w_qk[:, 2], w_qk[:, 3]], axis=-1).reshape(D, W_B_OUT)
    w_vb = w[:, COLS_A + COLS_B_QK:COLS_A + COLS_B]
    w_n = jnp.concatenate([w_a, w_kb], axis=1).astype(BF16)
    w_t = jnp.concatenate([w_qb, w_vb], axis=1).T.astype(BF16)
    w_gate = w[:, COLS_A + COLS_B:].astype(BF16)

    pn, pt = _proj(x, w_n, w_t, tm, tq)

    o_g, lse_g = [], []
    for g, (win, dil) in enumerate(DIL_PAIRS):
        assert win // dil == BLK
        o, lse = _dilated(pn, _dilated_bias(rel_bias, g, dil), g, dil)
        o_g.append(o)
        lse_g.append(lse)

    lam_init = 0.8 - 0.6 * math.exp(-0.3 * 0)
    lam_p = jnp.concatenate([lambda_q1, lambda_k1, lambda_q2, lambda_k2], axis=0).astype(F32)
    bias_d, bias_p = _diff_bias(rel_bias, tq)
    o_b = _diff(lam_p, pn, pt, bias_d, bias_p, subln_g[0].reshape(DV_B, 1), lam_init)

    x1 = _mix(x, o_g, lse_g, o_b, w_gate, b_gate, w_proj_a[0].astype(BF16), w_proj_b[0].astype(BF16),
              w_out[0].astype(BF16), ln1_g, ln1_b, tm)
    return _mlp(x1, w_mlp1[0].astype(BF16), w_mlp2[0].astype(BF16), ln2_g, ln2_b, tm)
```

```python
import functools
import math

import jax
import jax.numpy as jnp
import numpy as np
from jax import lax
from jax.experimental import pallas as pl
from jax.experimental.pallas import tpu as pltpu

D_MODEL = 1024
HEAD_DIM = 64
DIL_PAIRS = ((128, 1), (512, 4), (2048, 16))
N_GROUPS = len(DIL_PAIRS)
H_A = 8
H_B = 8
DV_B = 2 * HEAD_DIM
D_FF = 4 * D_MODEL
NUM_BUCKETS = 32
T5_MAX_DISTANCE = 128
BLK = 128
LN_EPS = 1e-5
NEG_INF = -1e30
W_A_OUT = H_A * HEAD_DIM
W_B_OUT = H_B * DV_B
COLS_A = 3 * N_GROUPS * H_A * HEAD_DIM
COLS_B_QK = 4 * H_B * HEAD_DIM
COLS_B = COLS_B_QK + H_B * DV_B
DEPTH = 1
DEEPNORM_ALPHA = (2.0 * DEPTH) ** 0.25
QK_SCALE = HEAD_DIM ** -0.5
LOG2E = math.log2(math.e)

LANES = 128
COLS_N = COLS_A + H_B * 2 * HEAD_DIM
COLS_T = 2 * W_B_OUT
CHUNK = 512
TQ = 512
STRIP = 256
TM = 512
VMEM_LIMIT = 56 * 1024 * 1024

BF16 = jnp.bfloat16
F32 = jnp.float32


def _dot(a, b):
    return jnp.dot(a, b, preferred_element_type=F32)


def _dot_nt(a, b):
    return lax.dot_general(a, b, (((1,), (1,)), ((), ())), preferred_element_type=F32)


def _resident(shape):
    nd = len(shape)
    return pl.BlockSpec(shape, lambda *_: (0,) * nd, pipeline_mode=pl.Buffered(1))


def _params(*sem):
    return pltpu.CompilerParams(dimension_semantics=sem, vmem_limit_bytes=VMEM_LIMIT)


def _proj_kernel(x_ref, wn_ref, wt_ref, on_ref, ot_ref, *, tq):
    xb = x_ref[...].astype(BF16)
    tm = xb.shape[0]
    for c in range(COLS_N // CHUNK):
        sl = slice(c * CHUNK, (c + 1) * CHUNK)
        on_ref[:, sl] = _dot(xb, wn_ref[:, sl]).astype(BF16)
    for c in range(COLS_T // CHUNK):
        sl = slice(c * CHUNK, (c + 1) * CHUNK)
        r = _dot_nt(wt_ref[sl, :], xb)
        if (c + 1) * CHUNK <= W_B_OUT:
            r = r * (QK_SCALE * LOG2E)
        r = r.astype(BF16)
        for t in range(tm // tq):
            ot_ref[t, sl, :] = r[:, t * tq:(t + 1) * tq]


def _proj(x, w_n, w_t, tm, tq):
    B, S, D = x.shape
    return pl.pallas_call(
        functools.partial(_proj_kernel, tq=tq),
        grid=(B, S // tm),
        in_specs=[
            pl.BlockSpec((None, tm, D), lambda b, i: (b, i, 0)),
            _resident((D, COLS_N)),
            _resident((COLS_T, D)),
        ],
        out_specs=[
            pl.BlockSpec((None, tm, COLS_N), lambda b, i: (b, i, 0)),
            pl.BlockSpec((None, tm // tq, COLS_T, tq), lambda b, i: (b, i, 0, 0)),
        ],
        out_shape=[
            jax.ShapeDtypeStruct((B, S, COLS_N), BF16),
            jax.ShapeDtypeStruct((B, S // tq, COLS_T, tq), BF16),
        ],
        compiler_params=_params("parallel", "parallel"),
        name="proj",
    )(x, w_n, w_t)


def _dilated_kernel(q_ref, k_ref, v_ref, kp_ref, vp_ref, bias_ref, o_ref, lse_ref):
    tb = q_ref.shape[0]
    first = pl.program_id(2) == 0
    lane = lax.broadcasted_iota(jnp.int32, (BLK, LANES), 1)
    lo = lane < HEAD_DIM
    for jb in range(tb // BLK):
        rows = slice(jb * BLK, (jb + 1) * BLK)
        prows = slice((jb - 1) * BLK, jb * BLK)
        for p in range(H_A // 2):
            cols = slice(p * LANES, (p + 1) * LANES)
            q2 = q_ref[rows, cols]
            zero = jnp.zeros_like(q2)
            qd = jnp.concatenate([jnp.where(lo, q2, zero), jnp.where(lo, zero, q2)], axis=0)
            kc, vc = k_ref[rows, cols], v_ref[rows, cols]
            if jb == 0:
                kp, vp = kp_ref[:, cols], vp_ref[:, cols]
            else:
                kp, vp = k_ref[prows, cols], v_ref[prows, cols]
            s_c = _dot_nt(qd, kc) + bias_ref[p, 1]
            s_p = _dot_nt(qd, kp) + bias_ref[p, 0]
            if jb == 0:
                s_p = jnp.where(first, NEG_INF, s_p)
            m = jnp.maximum(jnp.max(s_c, axis=-1, keepdims=True), jnp.max(s_p, axis=-1, keepdims=True))
            p_c = jnp.exp(s_c - m)
            p_p = jnp.exp(s_p - m)
            den = jnp.sum(p_c, axis=-1, keepdims=True) + jnp.sum(p_p, axis=-1, keepdims=True)
            o2 = (_dot(p_c.astype(BF16), vc) + _dot(p_p.astype(BF16), vp)) / den
            lse = jnp.broadcast_to(m + jnp.log(den), (2 * BLK, LANES))
            o_ref[rows, cols] = jnp.where(lo, o2[:BLK], o2[BLK:])
            lse_ref[rows, cols] = jnp.where(lo, lse[:BLK], lse[BLK:])


def _dilated(pn, bias, g, dil):
    B, S, _ = pn.shape
    L = S // dil
    tb = min(CHUNK, L)
    nblk = COLS_N // W_A_OUT
    view = pn.reshape(B, L, dil * COLS_N)
    qc, kc, vc = g, N_GROUPS + g, 2 * N_GROUPS + g
    cur = lambda c: pl.BlockSpec((None, tb, W_A_OUT), lambda b, r, n: (b, n, r * nblk + c))
    prev = lambda c: pl.BlockSpec(
        (None, BLK, W_A_OUT), lambda b, r, n: (b, jnp.maximum(n * (tb // BLK) - 1, 0), r * nblk + c))
    out = pl.BlockSpec((None, tb, W_A_OUT), lambda b, r, n: (b, n, r))
    o, lse = pl.pallas_call(
        _dilated_kernel,
        grid=(B, dil, L // tb),
        in_specs=[cur(qc), cur(kc), cur(vc), prev(kc), prev(vc), _resident(bias.shape)],
        out_specs=[out, out],
        out_shape=[jax.ShapeDtypeStruct((B, L, dil * W_A_OUT), F32)] * 2,
        compiler_params=_params("parallel", "parallel", "arbitrary"),
        name=f"dilated{g}",
    )(view, view, view, view, view, bias)
    return o.reshape(B, S, W_A_OUT), lse.reshape(B, S, W_A_OUT)


def _diff_kernel(lam_ref, qt_ref, k_ref, vt_ref, bias_ref, g_ref, o_ref, qd_sc, va_sc, s0_sc, s1_sc, p0_sc, p1_sc,
                 m_sc, acc_sc, *, lam_init):
    nq, _, tq = qt_ref.shape
    s_bufs, p_bufs = (s0_sc, s1_sc), (p0_sc, p1_sc)
    row = lax.broadcasted_iota(jnp.int32, (DV_B, tq), 0)
    for i in range(nq):
        qt = qt_ref[i]
        zero = jnp.zeros_like(qt)
        qd_sc[i, :, :tq] = jnp.where(row < HEAD_DIM, qt, zero)
        qd_sc[i, :, tq:] = jnp.where(row < HEAD_DIM, zero, qt)
        va_sc[i, :DV_B, :] = vt_ref[i]
        va_sc[i, DV_B:, :] = jnp.ones((va_sc.shape[1] - DV_B, tq), BF16)

    lam = (jnp.exp(jnp.sum(lam_ref[0:1, :] * lam_ref[1:2, :], axis=-1, keepdims=True))
           - jnp.exp(jnp.sum(lam_ref[2:3, :] * lam_ref[3:4, :], axis=-1, keepdims=True)) + lam_init)

    def scores(qi, j):
        kblk = k_ref[pl.ds(pl.multiple_of(j * tq, tq), tq), :]
        return _dot(kblk, qd_sc[qi])

    def reset(a):
        m_sc[a] = jnp.full(m_sc.shape[1:], -jnp.inf, F32)
        acc_sc[a] = jnp.zeros(acc_sc.shape[1:], F32)

    def finalize(qi, a):
        acc = acc_sc[a]
        on = acc[:DV_B] / acc[DV_B:DV_B + 1]
        o = on[:, :tq] - lam * on[:, tq:]
        ms = jnp.mean(o * o, axis=0, keepdims=True)
        o = o * lax.rsqrt(ms + LN_EPS) * g_ref[...] * (1.0 - lam_init)
        o_ref[pl.ds(pl.multiple_of(qi * tq, tq), tq), :] = o.T.astype(BF16)

    reset(0)
    s_bufs[0][...] = scores(0, 0)
    p_bufs[1][...] = jnp.zeros(p_bufs[1].shape, BF16)

    def stage(slot, first, carry):
        qi, j, qi_p, j_p = carry
        s_cur, p_cur, s_oth, p_oth = s_bufs[slot], p_bufs[slot], s_bufs[1 - slot], p_bufs[1 - slot]
        a_c, a_p = qi & 1, qi_p & 1
        last = j == qi
        j_n = jnp.where(last, 0, j + 1)
        qi_n = jnp.minimum(jnp.where(last, qi + 1, qi), nq - 1)

        @pl.when(j >= qi - 1)
        def _():
            b = bias_ref[j - qi + 1]
            s_cur[...] = s_cur[...] + jnp.concatenate([b, b], axis=1)

        kblk = k_ref[pl.ds(pl.multiple_of(j_n * tq, tq), tq), :]
        va = va_sc[j_p]
        strips = [slice(c * STRIP, (c + 1) * STRIP) for c in range(2 * tq // STRIP)]
        alphas = []
        for cs in strips:
            m_old = m_sc[a_c, :, cs]
            m_new = jnp.maximum(m_old, jnp.max(s_cur[:, cs], axis=0, keepdims=True))
            p_cur[:, cs] = jnp.exp2(s_cur[:, cs] - m_new).astype(BF16)
            alphas.append(jnp.where(j > 0, jnp.exp2(m_old - m_new), 1.0))
            m_sc[a_c, :, cs] = m_new
        for cs, alpha in zip(strips, alphas):
            s_oth[:, cs] = _dot(kblk, qd_sc[qi_n, :, cs])
            pv = _dot(va, p_oth[:, cs])
            acc_sc[a_p, :, cs] = (acc_sc[a_p, :, cs] + pv) * alpha

        @pl.when((j == 0) & jnp.logical_not(first))
        def _():
            finalize(qi_p, a_p)

        @pl.when(last)
        def _():
            reset(1 - a_c)

        return qi_n, j_n, qi, j

    npairs = nq * (nq + 1) // 2
    carry = (jnp.int32(0), jnp.int32(0), jnp.int32(1), jnp.int32(0))
    carry = lax.fori_loop(0, npairs // 2, lambda i, c: stage(1, False, stage(0, i == 0, c)), carry)
    if npairs % 2:
        stage(0, npairs == 1, carry)

    a = (nq - 1) & 1
    acc_sc[a] = acc_sc[a] + _dot(va_sc[nq - 1], p_bufs[(npairs - 1) & 1][...])
    finalize(nq - 1, a)


def _diff(lam_p, pn, pt, bias, g_col, lam_init):
    B, S, _ = pn.shape
    nq, tq = pt.shape[1], pt.shape[3]
    kcol0 = COLS_A // LANES
    va_rows = DV_B + 16
    return pl.pallas_call(
        functools.partial(_diff_kernel, lam_init=lam_init),
        grid=(B, H_B),
        in_specs=[
            _resident(lam_p.shape),
            pl.BlockSpec((None, nq, DV_B, tq), lambda b, h: (b, 0, h, 0)),
            pl.BlockSpec((None, S, LANES), lambda b, h: (b, 0, kcol0 + h)),
            pl.BlockSpec((None, nq, DV_B, tq), lambda b, h: (b, 0, H_B + h, 0)),
            pl.BlockSpec((None, 2, tq, tq), lambda b, h: (h, 0, 0, 0)),
            _resident(g_col.shape),
        ],
        out_specs=pl.BlockSpec((None, S, DV_B), lambda b, h: (b, 0, h)),
        out_shape=jax.ShapeDtypeStruct((B, S, W_B_OUT), BF16),
        scratch_shapes=[
            pltpu.VMEM((nq, DV_B, 2 * tq), BF16),
            pltpu.VMEM((nq, va_rows, tq), BF16),
            pltpu.VMEM((tq, 2 * tq), F32),
            pltpu.VMEM((tq, 2 * tq), F32),
            pltpu.VMEM((tq, 2 * tq), BF16),
            pltpu.VMEM((tq, 2 * tq), BF16),
            pltpu.VMEM((2, 1, 2 * tq), F32),
            pltpu.VMEM((2, va_rows, 2 * tq), F32),
        ],
        compiler_params=_params("parallel", "parallel"),
        name="diff_attn",
    )(lam_p, pt, pn, pt, bias, g_col)


def _toeplitz_kernel(w_ref, o_ref, *, diff):
    rows, width = o_ref.shape[-2:]
    t = pltpu.roll(jnp.broadcast_to(w_ref[...], (rows, width)), 0, 1, stride=1, stride_axis=0)
    if not diff:
        o_ref[...] = t
        return
    r = lax.broadcasted_iota(jnp.int32, (rows, width), 0)
    c = lax.broadcasted_iota(jnp.int32, (rows, width), 1)
    o_ref[0] = jnp.where(c < r, t, 0.0)
    o_ref[1] = jnp.where(c >= r, t, NEG_INF)


def _toeplitz(w, rows, diff):
    n, _, width = w.shape
    oshape = (n, 2, rows, width) if diff else (n, rows, width)
    oblock = (None,) + oshape[1:]
    return pl.pallas_call(
        functools.partial(_toeplitz_kernel, diff=diff),
        grid=(n,),
        in_specs=[pl.BlockSpec((None, 1, width), lambda i: (i, 0, 0))],
        out_specs=pl.BlockSpec(oblock, lambda i: (i,) + (0,) * (len(oshape) - 1)),
        out_shape=jax.ShapeDtypeStruct(oshape, F32),
        compiler_params=_params("parallel"),
        name="bias_diff" if diff else "bias_dilated",
    )(w)


def _layer_norm(h, g, b):
    mu = jnp.mean(h, axis=-1, keepdims=True)
    d = h - mu
    var = jnp.mean(d * d, axis=-1, keepdims=True)
    return d * lax.rsqrt(var + LN_EPS) * g + b


def _mix_kernel(x_ref, o0_ref, o1_ref, o2_ref, l0_ref, l1_ref, l2_ref, ob_ref, wg_ref, bg_ref, wa_ref, wb_ref,
                wo_ref, g_ref, b_ref, out_ref):
    x = x_ref[...]
    xb = x.astype(BF16)
    l0, l1, l2 = l0_ref[...], l1_ref[...], l2_ref[...]
    mx = jnp.maximum(jnp.maximum(l0, l1), l2)
    e0, e1, e2 = jnp.exp(l0 - mx), jnp.exp(l1 - mx), jnp.exp(l2 - mx)
    o_a = (e0 * o0_ref[...] + e1 * o1_ref[...] + e2 * o2_ref[...]) / (e0 + e1 + e2)
    y_a = _dot(o_a.astype(BF16), wa_ref[...])
    y_b = _dot(ob_ref[...], wb_ref[...])
    gate_a = jax.nn.sigmoid(_dot(xb, wg_ref[:, :D_MODEL]) + bg_ref[:, :D_MODEL])
    gate_b = jax.nn.sigmoid(_dot(xb, wg_ref[:, D_MODEL:]) + bg_ref[:, D_MODEL:])
    merged = gate_a * y_a + gate_b * y_b
    mix = _dot(merged.astype(BF16), wo_ref[...])
    out_ref[...] = _layer_norm(DEEPNORM_ALPHA * x + mix, g_ref[...], b_ref[...])


def _mix(x, o_g, lse_g, o_b, w_gate, b_gate, w_a, w_b, w_o, ln_g, ln_b, tm):
    B, S, D = x.shape
    tok = lambda w: pl.BlockSpec((None, tm, w), lambda b, i: (b, i, 0))
    return pl.pallas_call(
        _mix_kernel,
        grid=(B, S // tm),
        in_specs=[tok(D)] + [tok(W_A_OUT)] * 6 + [tok(W_B_OUT)]
        + [_resident(a.shape) for a in (w_gate, b_gate, w_a, w_b, w_o, ln_g, ln_b)],
        out_specs=tok(D),
        out_shape=jax.ShapeDtypeStruct((B, S, D), F32),
        compiler_params=_params("parallel", "parallel"),
        name="mix",
    )(x, *o_g, *lse_g, o_b, w_gate, b_gate, w_a, w_b, w_o, ln_g, ln_b)


def _mlp_kernel(x_ref, w1_ref, w2_ref, g_ref, b_ref, out_ref):
    x = x_ref[...]
    xb = x.astype(BF16)
    ff = jnp.zeros(x.shape, F32)
    for c in range(D_FF // (2 * CHUNK)):
        sl = slice(c * 2 * CHUNK, (c + 1) * 2 * CHUNK)
        h = jnp.maximum(_dot(xb, w1_ref[:, sl]), 0.0)
        ff = ff + _dot((h * h).astype(BF16), w2_ref[sl, :])
    out_ref[...] = _layer_norm(DEEPNORM_ALPHA * x + ff, g_ref[...], b_ref[...])


def _mlp(x, w1, w2, ln_g, ln_b, tm):
    B, S, D = x.shape
    tok = pl.BlockSpec((None, tm, D), lambda b, i: (b, i, 0))
    return pl.pallas_call(
        _mlp_kernel,
        grid=(B, S // tm),
        in_specs=[tok] + [_resident(a.shape) for a in (w1, w2, ln_g, ln_b)],
        out_specs=tok,
        out_shape=jax.ShapeDtypeStruct((B, S, D), F32),
        compiler_params=_params("parallel", "parallel"),
        name="mlp",
    )(x, w1, w2, ln_g, ln_b)


def _t5_bucket(dist):
    n = jnp.maximum(dist, 0)
    max_exact = NUM_BUCKETS // 2
    nf = jnp.maximum(n, 1).astype(F32)
    large = max_exact + (jnp.log(nf / max_exact) / math.log(T5_MAX_DISTANCE / max_exact)
                         * (NUM_BUCKETS - max_exact)).astype(jnp.int32)
    large = jnp.minimum(large, NUM_BUCKETS - 1)
    return jnp.where(n < max_exact, n, large)


def _dilated_bias(rel_bias, g, dil):
    j = jnp.arange(2 * BLK)
    tab = rel_bias[:, g * H_A:(g + 1) * H_A].astype(F32)
    vec = tab[_t5_bucket(jnp.maximum(BLK - j, 0) * dil)]
    vec = jnp.where((j <= BLK)[:, None], vec, NEG_INF)
    bias = _toeplitz(vec.T.reshape(H_A, 1, 2 * BLK), BLK, diff=False)
    bias = bias.reshape(H_A // 2, 2, BLK, 2, BLK)
    return bias.transpose(0, 3, 1, 2, 4).reshape(H_A // 2, 2, 2 * BLK, BLK)


def _diff_bias(rel_bias, tq):
    assert tq >= T5_MAX_DISTANCE
    tab = rel_bias[:, N_GROUPS * H_A:].astype(F32)
    tab = (tab - tab[NUM_BUCKETS - 1:NUM_BUCKETS]) * LOG2E
    vec = tab[_t5_bucket(jnp.arange(tq))]
    return _toeplitz(vec.T.reshape(H_B, 1, tq), tq, diff=True)


def kernel(x, w_in, b_gate, lambda_q1, lambda_k1, lambda_q2, lambda_k2, subln_g, rel_bias, w_proj_a, w_proj_b,
           w_out, ln1_g, ln1_b, ln2_g, ln2_b, w_mlp1, w_mlp2):
    B, S, D = x.shape
    tq = min(TQ, S)
    tm = min(TM, S)
    w = w_in[0]
    w_a = w[:, :COLS_A].reshape(D, 3, COLS_A // 3)
    w_a = jnp.concatenate([w_a[:, :1] * QK_SCALE, w_a[:, 1:]], axis=1).reshape(D, COLS_A)
    w_qk = w[:, COLS_A:COLS_A + COLS_B_QK].reshape(D, 4, H_B, HEAD_DIM)
    w_qb = jnp.concatenate([w_qk[:, 0], w_qk[:, 1]], axis=-1).reshape(D, W_B_OUT)
    w_kb = jnp.concatenate([w_qk[:, 2], w_qk[:, 3]], axis=-1).reshape(D, W_B_OUT)
    w_vb = w[:, COLS_A + COLS_B_QK:COLS_A + COLS_B]
    w_n = jnp.concatenate([w_a, w_kb], axis=1).astype(BF16)
    w_t = jnp.concatenate([w_qb, w_vb], axis=1).T.astype(BF16)
    w_gate = w[:, COLS_A + COLS_B:].astype(BF16)

    pn, pt = _proj(x, w_n, w_t, tm, tq)

    o_g, lse_g = [], []
    for g, (win, dil) in enumerate(DIL_PAIRS):
        assert win // dil == BLK
        o, lse = _dilated(pn, _dilated_bias(rel_bias, g, dil), g, dil)
        o_g.append(o)
        lse_g.append(lse)

    lam_init = 0.8 - 0.6 * math.exp(-0.3 * 0)
    lam_p = jnp.concatenate([lambda_q1, lambda_k1, lambda_q2, lambda_k2], axis=0).astype(F32)
    o_b = _diff(lam_p, pn, pt, _diff_bias(rel_bias, tq), subln_g[0].reshape(DV_B, 1), lam_init)

    x1 = _mix(x, o_g, lse_g, o_b, w_gate, b_gate, w_proj_a[0].astype(BF16), w_proj_b[0].astype(BF16),
              w_out[0].astype(BF16), ln1_g, ln1_b, tm)
    return _mlp(x1, w_mlp1[0].astype(BF16), w_mlp2[0].astype(BF16), ln2_g, ln2_b, tm)
```

```python
import functools
import math

import jax
import jax.numpy as jnp
import numpy as np
from jax import lax
from jax.experimental import pallas as pl
from jax.experimental.pallas import tpu as pltpu

D_MODEL = 1024
HEAD_DIM = 64
DIL_PAIRS = ((128, 1), (512, 4), (2048, 16))
N_GROUPS = len(DIL_PAIRS)
H_A = 8
H_B = 8
DV_B = 2 * HEAD_DIM
D_FF = 4 * D_MODEL
NUM_BUCKETS = 32
T5_MAX_DISTANCE = 128
BLK = 128
LN_EPS = 1e-5
NEG_INF = -1e30
W_A_OUT = H_A * HEAD_DIM
W_B_OUT = H_B * DV_B
COLS_A = 3 * N_GROUPS * H_A * HEAD_DIM
COLS_B_QK = 4 * H_B * HEAD_DIM
COLS_B = COLS_B_QK + H_B * DV_B
DEPTH = 1
DEEPNORM_ALPHA = (2.0 * DEPTH) ** 0.25
QK_SCALE = HEAD_DIM ** -0.5
LOG2E = math.log2(math.e)

LANES = 128
COLS_N = COLS_A + H_B * 2 * HEAD_DIM
COLS_T = 2 * W_B_OUT
CHUNK = 512
TQ = 512
STRIP = 256
TM = 512
VMEM_LIMIT = 56 * 1024 * 1024

BF16 = jnp.bfloat16
F32 = jnp.float32


def _dot(a, b):
    return jnp.dot(a, b, preferred_element_type=F32)


def _dot_nt(a, b):
    return lax.dot_general(a, b, (((1,), (1,)), ((), ())), preferred_element_type=F32)


def _resident(shape):
    nd = len(shape)
    return pl.BlockSpec(shape, lambda *_: (0,) * nd, pipeline_mode=pl.Buffered(1))


def _params(*sem):
    return pltpu.CompilerParams(dimension_semantics=sem, vmem_limit_bytes=VMEM_LIMIT)


def _proj_kernel(x_ref, wn_ref, wt_ref, a0_ref, a1_ref, a2_ref, kb_ref, ot_ref, rows_sc, *, tq):
    xb = x_ref[...].astype(BF16)
    tm = xb.shape[0]
    a_refs = (a0_ref, a1_ref, a2_ref)
    for c in range(COLS_N // CHUNK):
        r = _dot(xb, wn_ref[:, c * CHUNK:(c + 1) * CHUNK])
        g, part = divmod(c, 3)
        if g >= N_GROUPS:
            kb_ref[:, (c - 3 * N_GROUPS) * CHUNK:(c - 3 * N_GROUPS + 1) * CHUNK] = r.astype(BF16)
            continue
        sl = slice(part * W_A_OUT, (part + 1) * W_A_OUT)
        dil = DIL_PAIRS[g][1]
        if dil == 1:
            a_refs[g][0, :, sl] = r.astype(BF16)
            continue
        for k in range(CHUNK // LANES):
            rows_sc[k] = r[:, k * LANES:(k + 1) * LANES]
        for res in range(dil):
            for k in range(CHUNK // LANES):
                piece = rows_sc[k, pl.ds(res, tm // dil, stride=dil), :]
                a_refs[g][res, :, part * W_A_OUT + k * LANES:part * W_A_OUT + (k + 1) * LANES] = piece.astype(BF16)
    for c in range(COLS_T // CHUNK):
        sl = slice(c * CHUNK, (c + 1) * CHUNK)
        r = _dot_nt(wt_ref[sl, :], xb)
        if (c + 1) * CHUNK <= W_B_OUT:
            r = r * (QK_SCALE * LOG2E)
        r = r.astype(BF16)
        for t in range(tm // tq):
            ot_ref[t, sl, :] = r[:, t * tq:(t + 1) * tq]


def _proj(x, w_n, w_t, tm, tq):
    B, S, D = x.shape
    assert CHUNK == W_A_OUT
    dils = [d for _, d in DIL_PAIRS]
    return pl.pallas_call(
        functools.partial(_proj_kernel, tq=tq),
        grid=(B, S // tm),
        in_specs=[
            pl.BlockSpec((None, tm, D), lambda b, i: (b, i, 0)),
            _resident((D, COLS_N)),
            _resident((COLS_T, D)),
        ],
        out_specs=[pl.BlockSpec((None, d, tm // d, 3 * W_A_OUT), lambda b, i: (b, 0, i, 0)) for d in dils] + [
            pl.BlockSpec((None, tm, W_B_OUT), lambda b, i: (b, i, 0)),
            pl.BlockSpec((None, tm // tq, COLS_T, tq), lambda b, i: (b, i, 0, 0)),
        ],
        out_shape=[jax.ShapeDtypeStruct((B, d, S // d, 3 * W_A_OUT), BF16) for d in dils] + [
            jax.ShapeDtypeStruct((B, S, W_B_OUT), BF16),
            jax.ShapeDtypeStruct((B, S // tq, COLS_T, tq), BF16),
        ],
        scratch_shapes=[pltpu.VMEM((CHUNK // LANES, tm, LANES), F32)],
        compiler_params=_params("parallel", "parallel"),
        name="proj",
    )(x, w_n, w_t)


def _dilated_kernel(q_ref, k_ref, v_ref, kp_ref, vp_ref, bias_ref, o_ref, lse_ref):
    tb = q_ref.shape[0]
    first = pl.program_id(2) == 0
    lane = lax.broadcasted_iota(jnp.int32, (BLK, LANES), 1)
    lo = lane < HEAD_DIM
    for jb in range(tb // BLK):
        rows = slice(jb * BLK, (jb + 1) * BLK)
        prows = slice((jb - 1) * BLK, jb * BLK)
        for p in range(H_A // 2):
            cols = slice(p * LANES, (p + 1) * LANES)
            q2 = q_ref[rows, cols]
            zero = jnp.zeros_like(q2)
            qd = jnp.concatenate([jnp.where(lo, q2, zero), jnp.where(lo, zero, q2)], axis=0)
            kc, vc = k_ref[rows, cols], v_ref[rows, cols]
            if jb == 0:
                kp, vp = kp_ref[:, cols], vp_ref[:, cols]
            else:
                kp, vp = k_ref[prows, cols], v_ref[prows, cols]
            s_c = _dot_nt(qd, kc) + bias_ref[p, 1]
            s_p = _dot_nt(qd, kp) + bias_ref[p, 0]
            if jb == 0:
                s_p = jnp.where(first, NEG_INF, s_p)
            m = jnp.maximum(jnp.max(s_c, axis=-1, keepdims=True), jnp.max(s_p, axis=-1, keepdims=True))
            p_c = jnp.exp(s_c - m)
            p_p = jnp.exp(s_p - m)
            den = jnp.sum(p_c, axis=-1, keepdims=True) + jnp.sum(p_p, axis=-1, keepdims=True)
            o2 = (_dot(p_c.astype(BF16), vc) + _dot(p_p.astype(BF16), vp)) / den
            lse = jnp.broadcast_to(m + jnp.log(den), (2 * BLK, LANES))
            o_ref[rows, cols] = jnp.where(lo, o2[:BLK], o2[BLK:])
            lse_ref[rows, cols] = jnp.where(lo, lse[:BLK], lse[BLK:])


def _dilated(a, bias, g):
    B, dil, L, _ = a.shape
    tb = min(CHUNK, L)
    cur = lambda c: pl.BlockSpec((None, None, tb, W_A_OUT), lambda b, r, n: (b, r, n, c))
    prev = lambda c: pl.BlockSpec(
        (None, None, BLK, W_A_OUT), lambda b, r, n: (b, r, jnp.maximum(n * (tb // BLK) - 1, 0), c))
    out = pl.BlockSpec((None, None, tb, W_A_OUT), lambda b, r, n: (b, r, n, 0))
    return pl.pallas_call(
        _dilated_kernel,
        grid=(B, dil, L // tb),
        in_specs=[cur(0), cur(1), cur(2), prev(1), prev(2), _resident(bias.shape)],
        out_specs=[out, out],
        out_shape=[jax.ShapeDtypeStruct((B, dil, L, W_A_OUT), F32)] * 2,
        compiler_params=_params("parallel", "parallel", "arbitrary"),
        name=f"dilated{g}",
    )(a, a, a, a, a, bias)


def _diff_kernel(lam_ref, qt_ref, k_ref, vt_ref, bias_ref, g_ref, o_ref, qd_sc, va_sc, s0_sc, s1_sc, p0_sc, p1_sc,
                 m_sc, acc_sc, *, lam_init):
    nq, _, tq = qt_ref.shape
    s_bufs, p_bufs = (s0_sc, s1_sc), (p0_sc, p1_sc)
    row = lax.broadcasted_iota(jnp.int32, (DV_B, tq), 0)
    for i in range(nq):
        qt = qt_ref[i]
        zero = jnp.zeros_like(qt)
        qd_sc[i, :, :tq] = jnp.where(row < HEAD_DIM, qt, zero)
        qd_sc[i, :, tq:] = jnp.where(row < HEAD_DIM, zero, qt)
        va_sc[i, :DV_B, :] = vt_ref[i]
        va_sc[i, DV_B:, :] = jnp.ones((va_sc.shape[1] - DV_B, tq), BF16)

    lam = (jnp.exp(jnp.sum(lam_ref[0:1, :] * lam_ref[1:2, :], axis=-1, keepdims=True))
           - jnp.exp(jnp.sum(lam_ref[2:3, :] * lam_ref[3:4, :], axis=-1, keepdims=True)) + lam_init)

    def scores(qi, j):
        kblk = k_ref[pl.ds(pl.multiple_of(j * tq, tq), tq), :]
        return _dot(kblk, qd_sc[qi])

    def reset(a):
        m_sc[a] = jnp.full(m_sc.shape[1:], -jnp.inf, F32)
        acc_sc[a] = jnp.zeros(acc_sc.shape[1:], F32)

    def finalize(qi, a):
        acc = acc_sc[a]
        on = acc[:DV_B] / acc[DV_B:DV_B + 1]
        o = on[:, :tq] - lam * on[:, tq:]
        ms = jnp.mean(o * o, axis=0, keepdims=True)
        o = o * lax.rsqrt(ms + LN_EPS) * g_ref[...] * (1.0 - lam_init)
        o_ref[pl.ds(pl.multiple_of(qi * tq, tq), tq), :] = o.T.astype(BF16)

    reset(0)
    s_bufs[0][...] = scores(0, 0)
    p_bufs[1][...] = jnp.zeros(p_bufs[1].shape, BF16)

    def stage(slot, first, carry):
        qi, j, qi_p, j_p = carry
        s_cur, p_cur, s_oth, p_oth = s_bufs[slot], p_bufs[slot], s_bufs[1 - slot], p_bufs[1 - slot]
        a_c, a_p = qi & 1, qi_p & 1
        last = j == qi
        j_n = jnp.where(last, 0, j + 1)
        qi_n = jnp.minimum(jnp.where(last, qi + 1, qi), nq - 1)

        @pl.when(j >= qi - 1)
        def _():
            b = bias_ref[j - qi + 1]
            s_cur[...] = s_cur[...] + jnp.concatenate([b, b], axis=1)

        kblk = k_ref[pl.ds(pl.multiple_of(j_n * tq, tq), tq), :]
        va = va_sc[j_p]
        strips = [slice(c * STRIP, (c + 1) * STRIP) for c in range(2 * tq // STRIP)]
        alphas = []
        for cs in strips:
            m_old = m_sc[a_c, :, cs]
            m_new = jnp.maximum(m_old, jnp.max(s_cur[:, cs], axis=0, keepdims=True))
            p_cur[:, cs] = jnp.exp2(s_cur[:, cs] - m_new).astype(BF16)
            alphas.append(jnp.where(j > 0, jnp.exp2(m_old - m_new), 1.0))
            m_sc[a_c, :, cs] = m_new
        for cs, alpha in zip(strips, alphas):
            s_oth[:, cs] = _dot(kblk, qd_sc[qi_n, :, cs])
            pv = _dot(va, p_oth[:, cs])
            acc_sc[a_p, :, cs] = (acc_sc[a_p, :, cs] + pv) * alpha

        @pl.when((j == 0) & jnp.logical_not(first))
        def _():
            finalize(qi_p, a_p)

        @pl.when(last)
        def _():
            reset(1 - a_c)

        return qi_n, j_n, qi, j

    npairs = nq * (nq + 1) // 2
    carry = (jnp.int32(0), jnp.int32(0), jnp.int32(1), jnp.int32(0))
    carry = lax.fori_loop(0, npairs // 2, lambda i, c: stage(1, False, stage(0, i == 0, c)), carry)
    if npairs % 2:
        stage(0, npairs == 1, carry)

    a = (nq - 1) & 1
    acc_sc[a] = acc_sc[a] + _dot(va_sc[nq - 1], p_bufs[(npairs - 1) & 1][...])
    finalize(nq - 1, a)


def _diff(lam_p, kb, pt, bias, g_col, lam_init):
    B, S, _ = kb.shape
    nq, tq = pt.shape[1], pt.shape[3]
    va_rows = DV_B + 16
    return pl.pallas_call(
        functools.partial(_diff_kernel, lam_init=lam_init),
        grid=(B, H_B),
        in_specs=[
            _resident(lam_p.shape),
            pl.BlockSpec((None, nq, DV_B, tq), lambda b, h: (b, 0, h, 0)),
            pl.BlockSpec((None, S, LANES), lambda b, h: (b, 0, h)),
            pl.BlockSpec((None, nq, DV_B, tq), lambda b, h: (b, 0, H_B + h, 0)),
            pl.BlockSpec((None, 2, tq, tq), lambda b, h: (h, 0, 0, 0)),
            _resident(g_col.shape),
        ],
        out_specs=pl.BlockSpec((None, S, DV_B), lambda b, h: (b, 0, h)),
        out_shape=jax.ShapeDtypeStruct((B, S, W_B_OUT), BF16),
        scratch_shapes=[
            pltpu.VMEM((nq, DV_B, 2 * tq), BF16),
            pltpu.VMEM((nq, va_rows, tq), BF16),
            pltpu.VMEM((tq, 2 * tq), F32),
            pltpu.VMEM((tq, 2 * tq), F32),
            pltpu.VMEM((tq, 2 * tq), BF16),
            pltpu.VMEM((tq, 2 * tq), BF16),
            pltpu.VMEM((2, 1, 2 * tq), F32),
            pltpu.VMEM((2, va_rows, 2 * tq), F32),
        ],
        compiler_params=_params("parallel", "parallel"),
        name="diff_attn",
    )(lam_p, pt, kb, pt, bias, g_col)


def _toeplitz_kernel(w_ref, o_ref, *, diff):
    rows, width = o_ref.shape[-2:]
    t = pltpu.roll(jnp.broadcast_to(w_ref[...], (rows, width)), 0, 1, stride=1, stride_axis=0)
    if not diff:
        o_ref[...] = t
        return
    r = lax.broadcasted_iota(jnp.int32, (rows, width), 0)
    c = lax.broadcasted_iota(jnp.int32, (rows, width), 1)
    o_ref[0] = jnp.where(c < r, t, 0.0)
    o_ref[1] = jnp.where(c >= r, t, NEG_INF)


def _toeplitz(w, rows, diff):
    n, _, width = w.shape
    oshape = (n, 2, rows, width) if diff else (n, rows, width)
    oblock = (None,) + oshape[1:]
    return pl.pallas_call(
        functools.partial(_toeplitz_kernel, diff=diff),
        grid=(n,),
        in_specs=[pl.BlockSpec((None, 1, width), lambda i: (i, 0, 0))],
        out_specs=pl.BlockSpec(oblock, lambda i: (i,) + (0,) * (len(oshape) - 1)),
        out_shape=jax.ShapeDtypeStruct(oshape, F32),
        compiler_params=_params("parallel"),
        name="bias_diff" if diff else "bias_dilated",
    )(w)


def _layer_norm(h, g, b):
    mu = jnp.mean(h, axis=-1, keepdims=True)
    d = h - mu
    var = jnp.mean(d * d, axis=-1, keepdims=True)
    return d * lax.rsqrt(var + LN_EPS) * g + b


def _mix_kernel(x_ref, o0_ref, o1_ref, o2_ref, l0_ref, l1_ref, l2_ref, ob_ref, wg_ref, bg_ref, wa_ref, wb_ref,
                wo_ref, g_ref, b_ref, out_ref, *tok_sc):
    x = x_ref[...]
    xb = x.astype(BF16)

    def token_major(ref, sc):
        dil, rows, _ = ref.shape
        if dil == 1:
            return ref[0]
        nk = sc.shape[0]
        for res in range(dil):
            for k in range(nk):
                sc[k, pl.ds(res, rows, stride=dil), :] = ref[res, :, k * LANES:(k + 1) * LANES]
        return jnp.concatenate([sc[k] for k in range(nk)], axis=1)

    l0 = token_major(l0_ref, None)
    l1, l2 = token_major(l1_ref, tok_sc[0]), token_major(l2_ref, tok_sc[1])
    o0 = token_major(o0_ref, None)
    o1, o2 = token_major(o1_ref, tok_sc[2]), token_major(o2_ref, tok_sc[3])
    mx = jnp.maximum(jnp.maximum(l0, l1), l2)
    e0, e1, e2 = jnp.exp(l0 - mx), jnp.exp(l1 - mx), jnp.exp(l2 - mx)
    o_a = (e0 * o0 + e1 * o1 + e2 * o2) / (e0 + e1 + e2)
    y_a = _dot(o_a.astype(BF16), wa_ref[...])
    y_b = _dot(ob_ref[...], wb_ref[...])
    gate_a = jax.nn.sigmoid(_dot(xb, wg_ref[:, :D_MODEL]) + bg_ref[:, :D_MODEL])
    gate_b = jax.nn.sigmoid(_dot(xb, wg_ref[:, D_MODEL:]) + bg_ref[:, D_MODEL:])
    merged = gate_a * y_a + gate_b * y_b
    mix = _dot(merged.astype(BF16), wo_ref[...])
    out_ref[...] = _layer_norm(DEEPNORM_ALPHA * x + mix, g_ref[...], b_ref[...])


def _mix(x, o_g, lse_g, o_b, w_gate, b_gate, w_a, w_b, w_o, ln_g, ln_b, tm):
    B, S, D = x.shape
    tok = lambda w: pl.BlockSpec((None, tm, w), lambda b, i: (b, i, 0))
    res = [pl.BlockSpec((None, a.shape[1], tm // a.shape[1], W_A_OUT), lambda b, i: (b, 0, i, 0)) for a in o_g]
    n_sc = 2 * sum(a.shape[1] > 1 for a in o_g)
    return pl.pallas_call(
        _mix_kernel,
        grid=(B, S // tm),
        in_specs=[tok(D)] + res + res + [tok(W_B_OUT)]
        + [_resident(a.shape) for a in (w_gate, b_gate, w_a, w_b, w_o, ln_g, ln_b)],
        out_specs=tok(D),
        out_shape=jax.ShapeDtypeStruct((B, S, D), F32),
        scratch_shapes=[pltpu.VMEM((W_A_OUT // LANES, tm, LANES), F32)] * n_sc,
        compiler_params=_params("parallel", "parallel"),
        name="mix",
    )(x, *o_g, *lse_g, o_b, w_gate, b_gate, w_a, w_b, w_o, ln_g, ln_b)


def _mlp_kernel(x_ref, w1_ref, w2_ref, g_ref, b_ref, out_ref):
    x = x_ref[...]
    xb = x.astype(BF16)
    ff = jnp.zeros(x.shape, F32)
    for c in range(D_FF // (2 * CHUNK)):
        sl = slice(c * 2 * CHUNK, (c + 1) * 2 * CHUNK)
        h = jnp.maximum(_dot(xb, w1_ref[:, sl]), 0.0)
        ff = ff + _dot((h * h).astype(BF16), w2_ref[sl, :])
    out_ref[...] = _layer_norm(DEEPNORM_ALPHA * x + ff, g_ref[...], b_ref[...])


def _mlp(x, w1, w2, ln_g, ln_b, tm):
    B, S, D = x.shape
    tok = pl.BlockSpec((None, tm, D), lambda b, i: (b, i, 0))
    return pl.pallas_call(
        _mlp_kernel,
        grid=(B, S // tm),
        in_specs=[tok] + [_resident(a.shape) for a in (w1, w2, ln_g, ln_b)],
        out_specs=tok,
        out_shape=jax.ShapeDtypeStruct((B, S, D), F32),
        compiler_params=_params("parallel", "parallel"),
        name="mlp",
    )(x, w1, w2, ln_g, ln_b)


def _t5_bucket(dist):
    n = jnp.maximum(dist, 0)
    max_exact = NUM_BUCKETS // 2
    nf = jnp.maximum(n, 1).astype(F32)
    large = max_exact + (jnp.log(nf / max_exact) / math.log(T5_MAX_DISTANCE / max_exact)
                         * (NUM_BUCKETS - max_exact)).astype(jnp.int32)
    large = jnp.minimum(large, NUM_BUCKETS - 1)
    return jnp.where(n < max_exact, n, large)


def _dilated_bias(rel_bias, g, dil):
    j = jnp.arange(2 * BLK)
    tab = rel_bias[:, g * H_A:(g + 1) * H_A].astype(F32)
    vec = tab[_t5_bucket(jnp.maximum(BLK - j, 0) * dil)]
    vec = jnp.where((j <= BLK)[:, None], vec, NEG_INF)
    bias = _toeplitz(vec.T.reshape(H_A, 1, 2 * BLK), BLK, diff=False)
    bias = bias.reshape(H_A // 2, 2, BLK, 2, BLK)
    return bias.transpose(0, 3, 1, 2, 4).reshape(H_A // 2, 2, 2 * BLK, BLK)


def _diff_bias(rel_bias, tq):
    assert tq >= T5_MAX_DISTANCE
    tab = rel_bias[:, N_GROUPS * H_A:].astype(F32)
    tab = (tab - tab[NUM_BUCKETS - 1:NUM_BUCKETS]) * LOG2E
    vec = tab[_t5_bucket(jnp.arange(tq))]
    return _toeplitz(vec.T.reshape(H_B, 1, tq), tq, diff=True)


def kernel(x, w_in, b_gate, lambda_q1, lambda_k1, lambda_q2, lambda_k2, subln_g, rel_bias, w_proj_a, w_proj_b,
           w_out, ln1_g, ln1_b, ln2_g, ln2_b, w_mlp1, w_mlp2):
    B, S, D = x.shape
    tq = min(TQ, S)
    tm = min(TM, S)
    w = w_in[0]
    w_a = w[:, :COLS_A].reshape(D, 3, N_GROUPS, W_A_OUT)
    w_a = jnp.concatenate([w_a[:, :1] * QK_SCALE, w_a[:, 1:]], axis=1).transpose(0, 2, 1, 3).reshape(D, COLS_A)
    w_qk = w[:, COLS_A:COLS_A + COLS_B_QK].reshape(D, 4, H_B, HEAD_DIM)
    w_qb = jnp.concatenate([w_qk[:, 0], w_qk[:, 1]], axis=-1).reshape(D, W_B_OUT)
    w_kb = jnp.concatenate([w_qk[:, 2], w_qk[:, 3]], axis=-1).reshape(D, W_B_OUT)
    w_vb = w[:, COLS_A + COLS_B_QK:COLS_A + COLS_B]
    w_n = jnp.concatenate([w_a, w_kb], axis=1).astype(BF16)
    w_t = jnp.concatenate([w_qb, w_vb], axis=1).T.astype(BF16)
    w_gate = w[:, COLS_A + COLS_B:].astype(BF16)

    *a_g, kb, pt = _proj(x, w_n, w_t, tm, tq)

    o_g, lse_g = [], []
    for g, (win, dil) in enumerate(DIL_PAIRS):
        assert win // dil == BLK
        o, lse = _dilated(a_g[g], _dilated_bias(rel_bias, g, dil), g)
        o_g.append(o)
        lse_g.append(lse)

    lam_init = 0.8 - 0.6 * math.exp(-0.3 * 0)
    lam_p = jnp.concatenate([lambda_q1, lambda_k1, lambda_q2, lambda_k2], axis=0).astype(F32)
    o_b = _diff(lam_p, kb, pt, _diff_bias(rel_bias, tq), subln_g[0].reshape(DV_B, 1), lam_init)

    x1 = _mix(x, o_g, lse_g, o_b, w_gate, b_gate, w_proj_a[0].astype(BF16), w_proj_b[0].astype(BF16),
              w_out[0].astype(BF16), ln1_g, ln1_b, tm)
    return _mlp(x1, w_mlp1[0].astype(BF16), w_mlp2[0].astype(BF16), ln2_g, ln2_b, tm)
```

```python
import functools
import math

import jax
import jax.numpy as jnp
import numpy as np
from jax import lax
from jax.experimental import pallas as pl
from jax.experimental.pallas import tpu as pltpu

D_MODEL = 1024
HEAD_DIM = 64
DIL_PAIRS = ((128, 1), (512, 4), (2048, 16))
N_GROUPS = len(DIL_PAIRS)
H_A = 8
H_B = 8
DV_B = 2 * HEAD_DIM
D_FF = 4 * D_MODEL
NUM_BUCKETS = 32
T5_MAX_DISTANCE = 128
BLK = 128
LN_EPS = 1e-5
NEG_INF = -1e30
W_A_OUT = H_A * HEAD_DIM
W_B_OUT = H_B * DV_B
COLS_A = 3 * N_GROUPS * H_A * HEAD_DIM
COLS_B_QK = 4 * H_B * HEAD_DIM
COLS_B = COLS_B_QK + H_B * DV_B
DEPTH = 1
DEEPNORM_ALPHA = (2.0 * DEPTH) ** 0.25
QK_SCALE = HEAD_DIM ** -0.5
LOG2E = math.log2(math.e)

LANES = 128
COLS_N = COLS_A + H_B * 2 * HEAD_DIM
COLS_T = 2 * W_B_OUT
CHUNK = 512
TQ = 512
STRIP = 256
TM = 512
VMEM_LIMIT = 56 * 1024 * 1024

BF16 = jnp.bfloat16
F32 = jnp.float32


def _dot(a, b):
    return jnp.dot(a, b, preferred_element_type=F32)


def _dot_nt(a, b):
    return lax.dot_general(a, b, (((1,), (1,)), ((), ())), preferred_element_type=F32)


def _resident(shape):
    nd = len(shape)
    return pl.BlockSpec(shape, lambda *_: (0,) * nd, pipeline_mode=pl.Buffered(1))


def _params(*sem, flags=None):
    return pltpu.CompilerParams(dimension_semantics=sem, vmem_limit_bytes=VMEM_LIMIT, flags=flags)


def _proj_kernel(x_ref, wn_ref, wt_ref, a0_ref, a1_ref, a2_ref, kb_ref, ot_ref, rows_sc, *, tq):
    xb = x_ref[...].astype(BF16)
    tm = xb.shape[0]
    a_refs = (a0_ref, a1_ref, a2_ref)
    for c in range(COLS_N // CHUNK):
        r = _dot(xb, wn_ref[:, c * CHUNK:(c + 1) * CHUNK])
        g, part = divmod(c, 3)
        if g >= N_GROUPS:
            kb_ref[:, (c - 3 * N_GROUPS) * CHUNK:(c - 3 * N_GROUPS + 1) * CHUNK] = r.astype(BF16)
            continue
        sl = slice(part * W_A_OUT, (part + 1) * W_A_OUT)
        dil = DIL_PAIRS[g][1]
        if dil == 1:
            a_refs[g][0, :, sl] = r.astype(BF16)
            continue
        for k in range(CHUNK // LANES):
            rows_sc[k] = r[:, k * LANES:(k + 1) * LANES]
        for res in range(dil):
            for k in range(CHUNK // LANES):
                piece = rows_sc[k, pl.ds(res, tm // dil, stride=dil), :]
                a_refs[g][res, :, part * W_A_OUT + k * LANES:part * W_A_OUT + (k + 1) * LANES] = piece.astype(BF16)
    for c in range(COLS_T // CHUNK):
        sl = slice(c * CHUNK, (c + 1) * CHUNK)
        r = _dot_nt(wt_ref[sl, :], xb)
        if (c + 1) * CHUNK <= W_B_OUT:
            r = r * (QK_SCALE * LOG2E)
        r = r.astype(BF16)
        for t in range(tm // tq):
            ot_ref[t, sl, :] = r[:, t * tq:(t + 1) * tq]


def _proj(x, w_n, w_t, tm, tq):
    B, S, D = x.shape
    assert CHUNK == W_A_OUT
    dils = [d for _, d in DIL_PAIRS]
    return pl.pallas_call(
        functools.partial(_proj_kernel, tq=tq),
        grid=(B, S // tm),
        in_specs=[
            pl.BlockSpec((None, tm, D), lambda b, i: (b, i, 0)),
            _resident((D, COLS_N)),
            _resident((COLS_T, D)),
        ],
        out_specs=[pl.BlockSpec((None, d, tm // d, 3 * W_A_OUT), lambda b, i: (b, 0, i, 0)) for d in dils] + [
            pl.BlockSpec((None, tm, W_B_OUT), lambda b, i: (b, i, 0)),
            pl.BlockSpec((None, tm // tq, COLS_T, tq), lambda b, i: (b, i, 0, 0)),
        ],
        out_shape=[jax.ShapeDtypeStruct((B, d, S // d, 3 * W_A_OUT), BF16) for d in dils] + [
            jax.ShapeDtypeStruct((B, S, W_B_OUT), BF16),
            jax.ShapeDtypeStruct((B, S // tq, COLS_T, tq), BF16),
        ],
        scratch_shapes=[pltpu.VMEM((CHUNK // LANES, tm, LANES), F32)],
        compiler_params=_params("parallel", "parallel"),
        name="proj",
    )(x, w_n, w_t)


def _dilated_kernel(q_ref, k_ref, v_ref, kp_ref, vp_ref, bias_ref, o_ref, lse_ref):
    tb = q_ref.shape[0]
    first = pl.program_id(2) == 0
    lane = lax.broadcasted_iota(jnp.int32, (BLK, LANES), 1)
    lo = lane < HEAD_DIM
    for jb in range(tb // BLK):
        rows = slice(jb * BLK, (jb + 1) * BLK)
        prows = slice((jb - 1) * BLK, jb * BLK)
        for p in range(H_A // 2):
            cols = slice(p * LANES, (p + 1) * LANES)
            q2 = q_ref[rows, cols]
            zero = jnp.zeros_like(q2)
            qd = jnp.concatenate([jnp.where(lo, q2, zero), jnp.where(lo, zero, q2)], axis=0)
            kc, vc = k_ref[rows, cols], v_ref[rows, cols]
            if jb == 0:
                kp, vp = kp_ref[:, cols], vp_ref[:, cols]
            else:
                kp, vp = k_ref[prows, cols], v_ref[prows, cols]
            s_c = _dot_nt(qd, kc) + bias_ref[p, 1]
            s_p = _dot_nt(qd, kp) + bias_ref[p, 0]
            if jb == 0:
                s_p = jnp.where(first, NEG_INF, s_p)
            m = jnp.maximum(jnp.max(s_c, axis=-1, keepdims=True), jnp.max(s_p, axis=-1, keepdims=True))
            p_c = jnp.exp(s_c - m)
            p_p = jnp.exp(s_p - m)
            den = jnp.sum(p_c, axis=-1, keepdims=True) + jnp.sum(p_p, axis=-1, keepdims=True)
            o2 = (_dot(p_c.astype(BF16), vc) + _dot(p_p.astype(BF16), vp)) / den
            lse = jnp.broadcast_to(m + jnp.log(den), (2 * BLK, LANES))
            o_ref[rows, cols] = jnp.where(lo, o2[:BLK], o2[BLK:])
            lse_ref[rows, cols] = jnp.where(lo, lse[:BLK], lse[BLK:])


def _dilated(a, bias, g):
    B, dil, L, _ = a.shape
    tb = min(CHUNK, L)
    cur = lambda c: pl.BlockSpec((None, None, tb, W_A_OUT), lambda b, r, n: (b, r, n, c))
    prev = lambda c: pl.BlockSpec(
        (None, None, BLK, W_A_OUT), lambda b, r, n: (b, r, jnp.maximum(n * (tb // BLK) - 1, 0), c))
    out = pl.BlockSpec((None, None, tb, W_A_OUT), lambda b, r, n: (b, r, n, 0))
    return pl.pallas_call(
        _dilated_kernel,
        grid=(B, dil, L // tb),
        in_specs=[cur(0), cur(1), cur(2), prev(1), prev(2), _resident(bias.shape)],
        out_specs=[out, out],
        out_shape=[jax.ShapeDtypeStruct((B, dil, L, W_A_OUT), F32)] * 2,
        compiler_params=_params("parallel", "parallel", "arbitrary"),
        name=f"dilated{g}",
    )(a, a, a, a, a, bias)


def _diff_kernel(lam_ref, qt_ref, k_ref, vt_ref, bias_ref, g_ref, o_ref, qd_sc, va_sc, s0_sc, s1_sc, x0_sc, x1_sc,
                 m_sc, acc_sc, *, lam_init):
    nq, _, tq = qt_ref.shape
    s_bufs, x_bufs = (s0_sc, s1_sc), (x0_sc, x1_sc)
    row = lax.broadcasted_iota(jnp.int32, (DV_B, tq), 0)
    for i in range(nq):
        qt = qt_ref[i]
        zero = jnp.zeros_like(qt)
        qd_sc[i, :, :tq] = jnp.where(row < HEAD_DIM, qt, zero)
        qd_sc[i, :, tq:] = jnp.where(row < HEAD_DIM, zero, qt)
        va_sc[i, :DV_B, :] = vt_ref[i]
        va_sc[i, DV_B:, :] = jnp.ones((va_sc.shape[1] - DV_B, tq), BF16)

    lam = (jnp.exp(jnp.sum(lam_ref[0:1, :] * lam_ref[1:2, :], axis=-1, keepdims=True))
           - jnp.exp(jnp.sum(lam_ref[2:3, :] * lam_ref[3:4, :], axis=-1, keepdims=True)) + lam_init)

    def reset():
        m_sc[...] = jnp.full(m_sc.shape, -jnp.inf, F32)
        acc_sc[...] = jnp.zeros(acc_sc.shape, F32)

    def finalize(qi):
        acc = acc_sc[...]
        on = acc[:DV_B] / acc[DV_B:DV_B + 1]
        o = on[:, :tq] - lam * on[:, tq:]
        ms = jnp.mean(o * o, axis=0, keepdims=True)
        o = o * lax.rsqrt(ms + LN_EPS) * g_ref[...] * (1.0 - lam_init)
        o_ref[pl.ds(pl.multiple_of(qi * tq, tq), tq), :] = o.T.astype(BF16)

    reset()
    s0 = _dot(k_ref[0:tq, :], qd_sc[0])
    s_bufs[0][...] = s0
    x_bufs[0][...] = jnp.max(s0, axis=0, keepdims=True)

    def stage(slot, carry):
        qi, j = carry
        s_cur, x_cur, s_oth, x_oth = s_bufs[slot], x_bufs[slot], s_bufs[1 - slot], x_bufs[1 - slot]
        last = j == qi
        j_n = jnp.where(last, 0, j + 1)
        qi_n = jnp.minimum(jnp.where(last, qi + 1, qi), nq - 1)

        @pl.when(j >= qi - 1)
        def _():
            b = bias_ref[j - qi + 1]
            sb = s_cur[...] + jnp.concatenate([b, b], axis=1)
            s_cur[...] = sb
            x_cur[...] = jnp.max(sb, axis=0, keepdims=True)

        kblk = k_ref[pl.ds(pl.multiple_of(j_n * tq, tq), tq), :]
        va = va_sc[j]
        for c in range(2 * tq // STRIP):
            cs = slice(c * STRIP, (c + 1) * STRIP)
            m_old = m_sc[:, cs]
            m_new = jnp.maximum(m_old, x_cur[:, cs])
            p = jnp.exp2(s_cur[:, cs] - m_new).astype(BF16)
            alpha = jnp.exp2(m_old - m_new)
            m_sc[:, cs] = m_new
            acc_sc[:, cs] = acc_sc[:, cs] * alpha + _dot(va, p)
            s_nxt = _dot(kblk, qd_sc[qi_n, :, cs])
            s_oth[:, cs] = s_nxt
            x_oth[:, cs] = jnp.max(s_nxt, axis=0, keepdims=True)

        @pl.when(last)
        def _():
            finalize(qi)
            reset()

        return qi_n, j_n

    npairs = nq * (nq + 1) // 2
    carry = lax.fori_loop(0, npairs // 2, lambda i, c: stage(1, stage(0, c)), (jnp.int32(0), jnp.int32(0)))
    if npairs % 2:
        stage(0, carry)


def _diff(lam_p, kb, pt, bias, g_col, lam_init):
    B, S, _ = kb.shape
    nq, tq = pt.shape[1], pt.shape[3]
    va_rows = DV_B + 16
    return pl.pallas_call(
        functools.partial(_diff_kernel, lam_init=lam_init),
        grid=(B, H_B),
        in_specs=[
            _resident(lam_p.shape),
            pl.BlockSpec((None, nq, DV_B, tq), lambda b, h: (b, 0, h, 0)),
            pl.BlockSpec((None, S, LANES), lambda b, h: (b, 0, h)),
            pl.BlockSpec((None, nq, DV_B, tq), lambda b, h: (b, 0, H_B + h, 0)),
            pl.BlockSpec((None, 2, tq, tq), lambda b, h: (h, 0, 0, 0)),
            _resident(g_col.shape),
        ],
        out_specs=pl.BlockSpec((None, S, DV_B), lambda b, h: (b, 0, h)),
        out_shape=jax.ShapeDtypeStruct((B, S, W_B_OUT), BF16),
        scratch_shapes=[
            pltpu.VMEM((nq, DV_B, 2 * tq), BF16),
            pltpu.VMEM((nq, va_rows, tq), BF16),
            pltpu.VMEM((tq, 2 * tq), F32),
            pltpu.VMEM((tq, 2 * tq), F32),
            pltpu.VMEM((1, 2 * tq), F32),
            pltpu.VMEM((1, 2 * tq), F32),
            pltpu.VMEM((1, 2 * tq), F32),
            pltpu.VMEM((va_rows, 2 * tq), F32),
        ],
        compiler_params=_params("parallel", "parallel"),
        name="diff_attn",
    )(lam_p, pt, kb, pt, bias, g_col)


def _toeplitz_kernel(w_ref, o_ref, *, diff):
    rows, width = o_ref.shape[-2:]
    t = pltpu.roll(jnp.broadcast_to(w_ref[...], (rows, width)), 0, 1, stride=1, stride_axis=0)
    if not diff:
        o_ref[...] = t
        return
    r = lax.broadcasted_iota(jnp.int32, (rows, width), 0)
    c = lax.broadcasted_iota(jnp.int32, (rows, width), 1)
    o_ref[0] = jnp.where(c < r, t, 0.0)
    o_ref[1] = jnp.where(c >= r, t, NEG_INF)


def _toeplitz(w, rows, diff):
    n, _, width = w.shape
    oshape = (n, 2, rows, width) if diff else (n, rows, width)
    oblock = (None,) + oshape[1:]
    return pl.pallas_call(
        functools.partial(_toeplitz_kernel, diff=diff),
        grid=(n,),
        in_specs=[pl.BlockSpec((None, 1, width), lambda i: (i, 0, 0))],
        out_specs=pl.BlockSpec(oblock, lambda i: (i,) + (0,) * (len(oshape) - 1)),
        out_shape=jax.ShapeDtypeStruct(oshape, F32),
        compiler_params=_params("parallel"),
        name="bias_diff" if diff else "bias_dilated",
    )(w)


def _layer_norm(h, g, b):
    mu = jnp.mean(h, axis=-1, keepdims=True)
    d = h - mu
    var = jnp.mean(d * d, axis=-1, keepdims=True)
    return d * lax.rsqrt(var + LN_EPS) * g + b


def _mix_kernel(x_ref, o0_ref, o1_ref, o2_ref, l0_ref, l1_ref, l2_ref, ob_ref, wg_ref, bg_ref, wa_ref, wb_ref,
                wo_ref, g_ref, b_ref, out_ref, *tok_sc):
    x = x_ref[...]
    xb = x.astype(BF16)

    def token_major(ref, sc):
        dil, rows, _ = ref.shape
        if dil == 1:
            return ref[0]
        nk = sc.shape[0]
        for res in range(dil):
            for k in range(nk):
                sc[k, pl.ds(res, rows, stride=dil), :] = ref[res, :, k * LANES:(k + 1) * LANES]
        return jnp.concatenate([sc[k] for k in range(nk)], axis=1)

    l0 = token_major(l0_ref, None)
    l1, l2 = token_major(l1_ref, tok_sc[0]), token_major(l2_ref, tok_sc[1])
    o0 = token_major(o0_ref, None)
    o1, o2 = token_major(o1_ref, tok_sc[2]), token_major(o2_ref, tok_sc[3])
    mx = jnp.maximum(jnp.maximum(l0, l1), l2)
    e0, e1, e2 = jnp.exp(l0 - mx), jnp.exp(l1 - mx), jnp.exp(l2 - mx)
    o_a = (e0 * o0 + e1 * o1 + e2 * o2) / (e0 + e1 + e2)
    y_a = _dot(o_a.astype(BF16), wa_ref[...])
    y_b = _dot(ob_ref[...], wb_ref[...])
    gate_a = jax.nn.sigmoid(_dot(xb, wg_ref[:, :D_MODEL]) + bg_ref[:, :D_MODEL])
    gate_b = jax.nn.sigmoid(_dot(xb, wg_ref[:, D_MODEL:]) + bg_ref[:, D_MODEL:])
    merged = gate_a * y_a + gate_b * y_b
    mix = _dot(merged.astype(BF16), wo_ref[...])
    out_ref[...] = _layer_norm(DEEPNORM_ALPHA * x + mix, g_ref[...], b_ref[...])


def _mix(x, o_g, lse_g, o_b, w_gate, b_gate, w_a, w_b, w_o, ln_g, ln_b, tm):
    B, S, D = x.shape
    tok = lambda w: pl.BlockSpec((None, tm, w), lambda b, i: (b, i, 0))
    res = [pl.BlockSpec((None, a.shape[1], tm // a.shape[1], W_A_OUT), lambda b, i: (b, 0, i, 0)) for a in o_g]
    n_sc = 2 * sum(a.shape[1] > 1 for a in o_g)
    return pl.pallas_call(
        _mix_kernel,
        grid=(B, S // tm),
        in_specs=[tok(D)] + res + res + [tok(W_B_OUT)]
        + [_resident(a.shape) for a in (w_gate, b_gate, w_a, w_b, w_o, ln_g, ln_b)],
        out_specs=tok(D),
        out_shape=jax.ShapeDtypeStruct((B, S, D), F32),
        scratch_shapes=[pltpu.VMEM((W_A_OUT // LANES, tm, LANES), F32)] * n_sc,
        compiler_params=_params("parallel", "parallel"),
        name="mix",
    )(x, *o_g, *lse_g, o_b, w_gate, b_gate, w_a, w_b, w_o, ln_g, ln_b)


def _mlp_kernel(x_ref, w1_ref, w2_ref, g_ref, b_ref, out_ref):
    x = x_ref[...]
    xb = x.astype(BF16)
    ff = jnp.zeros(x.shape, F32)
    for c in range(D_FF // (2 * CHUNK)):
        sl = slice(c * 2 * CHUNK, (c + 1) * 2 * CHUNK)
        h = jnp.maximum(_dot(xb, w1_ref[:, sl]), 0.0)
        ff = ff + _dot((h * h).astype(BF16), w2_ref[sl, :])
    out_ref[...] = _layer_norm(DEEPNORM_ALPHA * x + ff, g_ref[...], b_ref[...])


def _mlp(x, w1, w2, ln_g, ln_b, tm):
    B, S, D = x.shape
    tok = pl.BlockSpec((None, tm, D), lambda b, i: (b, i, 0))
    return pl.pallas_call(
        _mlp_kernel,
        grid=(B, S // tm),
        in_specs=[tok] + [_resident(a.shape) for a in (w1, w2, ln_g, ln_b)],
        out_specs=tok,
        out_shape=jax.ShapeDtypeStruct((B, S, D), F32),
        compiler_params=_params("parallel", "parallel"),
        name="mlp",
    )(x, w1, w2, ln_g, ln_b)


def _t5_bucket(dist):
    n = jnp.maximum(dist, 0)
    max_exact = NUM_BUCKETS // 2
    nf = jnp.maximum(n, 1).astype(F32)
    large = max_exact + (jnp.log(nf / max_exact) / math.log(T5_MAX_DISTANCE / max_exact)
                         * (NUM_BUCKETS - max_exact)).astype(jnp.int32)
    large = jnp.minimum(large, NUM_BUCKETS - 1)
    return jnp.where(n < max_exact, n, large)


def _dilated_bias(rel_bias, g, dil):
    j = jnp.arange(2 * BLK)
    tab = rel_bias[:, g * H_A:(g + 1) * H_A].astype(F32)
    vec = tab[_t5_bucket(jnp.maximum(BLK - j, 0) * dil)]
    vec = jnp.where((j <= BLK)[:, None], vec, NEG_INF)
    bias = _toeplitz(vec.T.reshape(H_A, 1, 2 * BLK), BLK, diff=False)
    bias = bias.reshape(H_A // 2, 2, BLK, 2, BLK)
    return bias.transpose(0, 3, 1, 2, 4).reshape(H_A // 2, 2, 2 * BLK, BLK)


def _diff_bias(rel_bias, tq):
    assert tq >= T5_MAX_DISTANCE
    tab = rel_bias[:, N_GROUPS * H_A:].astype(F32)
    tab = (tab - tab[NUM_BUCKETS - 1:NUM_BUCKETS]) * LOG2E
    vec = tab[_t5_bucket(jnp.arange(tq))]
    return _toeplitz(vec.T.reshape(H_B, 1, tq), tq, diff=True)


def kernel(x, w_in, b_gate, lambda_q1, lambda_k1, lambda_q2, lambda_k2, subln_g, rel_bias, w_proj_a, w_proj_b,
           w_out, ln1_g, ln1_b, ln2_g, ln2_b, w_mlp1, w_mlp2):
    B, S, D = x.shape
    tq = min(TQ, S)
    tm = min(TM, S)
    w = w_in[0]
    w_a = w[:, :COLS_A].reshape(D, 3, N_GROUPS, W_A_OUT)
    w_a = jnp.concatenate([w_a[:, :1] * QK_SCALE, w_a[:, 1:]], axis=1).transpose(0, 2, 1, 3).reshape(D, COLS_A)
    w_qk = w[:, COLS_A:COLS_A + COLS_B_QK].reshape(D, 4, H_B, HEAD_DIM)
    w_qb = jnp.concatenate([w_qk[:, 0], w_qk[:, 1]], axis=-1).reshape(D, W_B_OUT)
    w_kb = jnp.concatenate([w_qk[:, 2], w_qk[:, 3]], axis=-1).reshape(D, W_B_OUT)
    w_vb = w[:, COLS_A + COLS_B_QK:COLS_A + COLS_B]
    w_n = jnp.concatenate([w_a, w_kb], axis=1).astype(BF16)
    w_t = jnp.concatenate([w_qb, w_vb], axis=1).T.astype(BF16)
    w_gate = w[:, COLS_A + COLS_B:].astype(BF16)

    *a_g, kb, pt = _proj(x, w_n, w_t, tm, tq)

    o_g, lse_g = [], []
    for g, (win, dil) in enumerate(DIL_PAIRS):
        assert win // dil == BLK
        o, lse = _dilated(a_g[g], _dilated_bias(rel_bias, g, dil), g)
        o_g.append(o)
        lse_g.append(lse)

    lam_init = 0.8 - 0.6 * math.exp(-0.3 * 0)
    lam_p = jnp.concatenate([lambda_q1, lambda_k1, lambda_q2, lambda_k2], axis=0).astype(F32)
    o_b = _diff(lam_p, kb, pt, _diff_bias(rel_bias, tq), subln_g[0].reshape(DV_B, 1), lam_init)

    x1 = _mix(x, o_g, lse_g, o_b, w_gate, b_gate, w_proj_a[0].astype(BF16), w_proj_b[0].astype(BF16),
              w_out[0].astype(BF16), ln1_g, ln1_b, tm)
    return _mlp(x1, w_mlp1[0].astype(BF16), w_mlp2[0].astype(BF16), ln2_g, ln2_b, tm)
```

```python
import functools
import math

import jax
import jax.numpy as jnp
import numpy as np
from jax import lax
from jax.experimental import pallas as pl
from jax.experimental.pallas import tpu as pltpu

D_MODEL = 1024
HEAD_DIM = 64
DIL_PAIRS = ((128, 1), (512, 4), (2048, 16))
N_GROUPS = len(DIL_PAIRS)
H_A = 8
H_B = 8
DV_B = 2 * HEAD_DIM
D_FF = 4 * D_MODEL
NUM_BUCKETS = 32
T5_MAX_DISTANCE = 128
BLK = 128
LN_EPS = 1e-5
NEG_INF = -1e30
W_A_OUT = H_A * HEAD_DIM
W_B_OUT = H_B * DV_B
COLS_A = 3 * N_GROUPS * H_A * HEAD_DIM
COLS_B_QK = 4 * H_B * HEAD_DIM
COLS_B = COLS_B_QK + H_B * DV_B
DEPTH = 1
DEEPNORM_ALPHA = (2.0 * DEPTH) ** 0.25
QK_SCALE = HEAD_DIM ** -0.5
LOG2E = math.log2(math.e)

LANES = 128
COLS_N = COLS_A + H_B * 2 * HEAD_DIM
COLS_T = 2 * W_B_OUT
CHUNK = 512
TQ = 512
STRIP = 256
KV_PER_Q = 2
TM = 512
VMEM_LIMIT = 56 * 1024 * 1024

BF16 = jnp.bfloat16
F32 = jnp.float32


def _dot(a, b):
    return jnp.dot(a, b, preferred_element_type=F32)


def _dot_nt(a, b):
    return lax.dot_general(a, b, (((1,), (1,)), ((), ())), preferred_element_type=F32)


def _resident(shape):
    nd = len(shape)
    return pl.BlockSpec(shape, lambda *_: (0,) * nd, pipeline_mode=pl.Buffered(1))


def _params(*sem, flags=None):
    return pltpu.CompilerParams(dimension_semantics=sem, vmem_limit_bytes=VMEM_LIMIT, flags=flags)


def _proj_kernel(x_ref, wn_ref, wt_ref, a0_ref, a1_ref, a2_ref, kb_ref, ot_ref, rows_sc, *, tq):
    xb = x_ref[...].astype(BF16)
    tm = xb.shape[0]
    a_refs = (a0_ref, a1_ref, a2_ref)
    for c in range(COLS_N // CHUNK):
        r = _dot(xb, wn_ref[:, c * CHUNK:(c + 1) * CHUNK])
        g, part = divmod(c, 3)
        if g >= N_GROUPS:
            kb_ref[:, (c - 3 * N_GROUPS) * CHUNK:(c - 3 * N_GROUPS + 1) * CHUNK] = r.astype(BF16)
            continue
        sl = slice(part * W_A_OUT, (part + 1) * W_A_OUT)
        dil = DIL_PAIRS[g][1]
        if dil == 1:
            a_refs[g][0, :, sl] = r.astype(BF16)
            continue
        for k in range(CHUNK // LANES):
            rows_sc[k] = r[:, k * LANES:(k + 1) * LANES]
        for res in range(dil):
            for k in range(CHUNK // LANES):
                piece = rows_sc[k, pl.ds(res, tm // dil, stride=dil), :]
                a_refs[g][res, :, part * W_A_OUT + k * LANES:part * W_A_OUT + (k + 1) * LANES] = piece.astype(BF16)
    for c in range(COLS_T // CHUNK):
        sl = slice(c * CHUNK, (c + 1) * CHUNK)
        r = _dot_nt(wt_ref[sl, :], xb)
        if (c + 1) * CHUNK <= W_B_OUT:
            r = r * (QK_SCALE * LOG2E)
        r = r.astype(BF16)
        for t in range(tm // tq):
            ot_ref[t, sl, :] = r[:, t * tq:(t + 1) * tq]


def _proj(x, w_n, w_t, tm, tq):
    B, S, D = x.shape
    assert CHUNK == W_A_OUT
    dils = [d for _, d in DIL_PAIRS]
    return pl.pallas_call(
        functools.partial(_proj_kernel, tq=tq),
        grid=(B, S // tm),
        in_specs=[
            pl.BlockSpec((None, tm, D), lambda b, i: (b, i, 0)),
            _resident((D, COLS_N)),
            _resident((COLS_T, D)),
        ],
        out_specs=[pl.BlockSpec((None, d, tm // d, 3 * W_A_OUT), lambda b, i: (b, 0, i, 0)) for d in dils] + [
            pl.BlockSpec((None, tm, W_B_OUT), lambda b, i: (b, i, 0)),
            pl.BlockSpec((None, tm // tq, COLS_T, tq), lambda b, i: (b, i, 0, 0)),
        ],
        out_shape=[jax.ShapeDtypeStruct((B, d, S // d, 3 * W_A_OUT), BF16) for d in dils] + [
            jax.ShapeDtypeStruct((B, S, W_B_OUT), BF16),
            jax.ShapeDtypeStruct((B, S // tq, COLS_T, tq), BF16),
        ],
        scratch_shapes=[pltpu.VMEM((CHUNK // LANES, tm, LANES), F32)],
        compiler_params=_params("parallel", "parallel"),
        name="proj",
    )(x, w_n, w_t)


def _dilated_kernel(q_ref, k_ref, v_ref, kp_ref, vp_ref, bias_ref, o_ref, lse_ref):
    tb = q_ref.shape[0]
    first = pl.program_id(2) == 0
    lane = lax.broadcasted_iota(jnp.int32, (BLK, LANES), 1)
    lo = lane < HEAD_DIM
    for jb in range(tb // BLK):
        rows = slice(jb * BLK, (jb + 1) * BLK)
        prows = slice((jb - 1) * BLK, jb * BLK)
        for p in range(H_A // 2):
            cols = slice(p * LANES, (p + 1) * LANES)
            q2 = q_ref[rows, cols]
            zero = jnp.zeros_like(q2)
            qd = jnp.concatenate([jnp.where(lo, q2, zero), jnp.where(lo, zero, q2)], axis=0)
            kc, vc = k_ref[rows, cols], v_ref[rows, cols]
            if jb == 0:
                kp, vp = kp_ref[:, cols], vp_ref[:, cols]
            else:
                kp, vp = k_ref[prows, cols], v_ref[prows, cols]
            s_c = _dot_nt(qd, kc) + bias_ref[p, 1]
            s_p = _dot_nt(qd, kp) + bias_ref[p, 0]
            if jb == 0:
                s_p = jnp.where(first, NEG_INF, s_p)
            m = jnp.maximum(jnp.max(s_c, axis=-1, keepdims=True), jnp.max(s_p, axis=-1, keepdims=True))
            p_c = jnp.exp(s_c - m)
            p_p = jnp.exp(s_p - m)
            den = jnp.sum(p_c, axis=-1, keepdims=True) + jnp.sum(p_p, axis=-1, keepdims=True)
            o2 = (_dot(p_c.astype(BF16), vc) + _dot(p_p.astype(BF16), vp)) / den
            lse = jnp.broadcast_to(m + jnp.log(den), (2 * BLK, LANES))
            o_ref[rows, cols] = jnp.where(lo, o2[:BLK], o2[BLK:])
            lse_ref[rows, cols] = jnp.where(lo, lse[:BLK], lse[BLK:])


def _dilated(a, bias, g):
    B, dil, L, _ = a.shape
    tb = min(CHUNK, L)
    cur = lambda c: pl.BlockSpec((None, None, tb, W_A_OUT), lambda b, r, n: (b, r, n, c))
    prev = lambda c: pl.BlockSpec(
        (None, None, BLK, W_A_OUT), lambda b, r, n: (b, r, jnp.maximum(n * (tb // BLK) - 1, 0), c))
    out = pl.BlockSpec((None, None, tb, W_A_OUT), lambda b, r, n: (b, r, n, 0))
    return pl.pallas_call(
        _dilated_kernel,
        grid=(B, dil, L // tb),
        in_specs=[cur(0), cur(1), cur(2), prev(1), prev(2), _resident(bias.shape)],
        out_specs=[out, out],
        out_shape=[jax.ShapeDtypeStruct((B, dil, L, W_A_OUT), F32)] * 2,
        compiler_params=_params("parallel", "parallel", "arbitrary"),
        name=f"dilated{g}",
    )(a, a, a, a, a, bias)


def _diff_kernel(lam_ref, qt_ref, k_ref, vt_ref, bias_ref, g_ref, o_ref, qd_sc, va_sc, s0_sc, s1_sc, x0_sc, x1_sc,
                 m_sc, acc_sc, *, lam_init):
    nq, _, tq = qt_ref.shape
    tk = KV_PER_Q * tq
    s_bufs, x_bufs = (s0_sc, s1_sc), (x0_sc, x1_sc)
    row = lax.broadcasted_iota(jnp.int32, (DV_B, tq), 0)
    for i in range(nq):
        qt = qt_ref[i]
        zero = jnp.zeros_like(qt)
        qd_sc[i, :, :tq] = jnp.where(row < HEAD_DIM, qt, zero)
        qd_sc[i, :, tq:] = jnp.where(row < HEAD_DIM, zero, qt)
        va_sc[i // KV_PER_Q, :DV_B, (i % KV_PER_Q) * tq:(i % KV_PER_Q + 1) * tq] = vt_ref[i]
    for i in range(nq // KV_PER_Q):
        va_sc[i, DV_B:, :] = jnp.ones((va_sc.shape[1] - DV_B, tk), BF16)

    lam = (jnp.exp(jnp.sum(lam_ref[0:1, :] * lam_ref[1:2, :], axis=-1, keepdims=True))
           - jnp.exp(jnp.sum(lam_ref[2:3, :] * lam_ref[3:4, :], axis=-1, keepdims=True)) + lam_init)

    def reset():
        m_sc[...] = jnp.full(m_sc.shape, -jnp.inf, F32)
        acc_sc[...] = jnp.zeros(acc_sc.shape, F32)

    def finalize(qi):
        acc = acc_sc[...]
        on = acc[:DV_B] / acc[DV_B:DV_B + 1]
        o = on[:, :tq] - lam * on[:, tq:]
        ms = jnp.mean(o * o, axis=0, keepdims=True)
        o = o * lax.rsqrt(ms + LN_EPS) * g_ref[...] * (1.0 - lam_init)
        o_ref[pl.ds(pl.multiple_of(qi * tq, tq), tq), :] = o.T.astype(BF16)

    reset()
    s0 = _dot(k_ref[0:tk, :], qd_sc[0])
    s_bufs[0][...] = s0
    x_bufs[0][...] = jnp.max(s0, axis=0, keepdims=True)

    def stage(slot, carry):
        qi, j = carry
        s_cur, x_cur, s_oth, x_oth = s_bufs[slot], x_bufs[slot], s_bufs[1 - slot], x_bufs[1 - slot]
        odd = qi & 1
        last = j == qi // KV_PER_Q
        j_n = jnp.where(last, 0, j + 1)
        qi_n = jnp.minimum(jnp.where(last, qi + 1, qi), nq - 1)

        @pl.when(last | ((odd == 0) & (j == qi // KV_PER_Q - 1)))
        def _():
            b = bias_ref[jnp.where(last, 2 - odd, 0)]
            sb = s_cur[...] + jnp.concatenate([b, b], axis=1)
            s_cur[...] = sb
            x_cur[...] = jnp.max(sb, axis=0, keepdims=True)

        kblk = k_ref[pl.ds(pl.multiple_of(j_n * tk, tk), tk), :]
        va = va_sc[j]
        for c in range(2 * tq // STRIP):
            cs = slice(c * STRIP, (c + 1) * STRIP)
            m_old = m_sc[:, cs]
            m_new = jnp.maximum(m_old, x_cur[:, cs])
            p = jnp.exp2(s_cur[:, cs] - m_new).astype(BF16)
            alpha = jnp.exp2(m_old - m_new)
            m_sc[:, cs] = m_new
            acc_sc[:, cs] = acc_sc[:, cs] * alpha + _dot(va, p)
            s_nxt = _dot(kblk, qd_sc[qi_n, :, cs])
            s_oth[:, cs] = s_nxt
            x_oth[:, cs] = jnp.max(s_nxt, axis=0, keepdims=True)

        @pl.when(last)
        def _():
            finalize(qi)
            reset()

        return qi_n, j_n

    npairs = sum(qi // KV_PER_Q + 1 for qi in range(nq))
    carry = lax.fori_loop(0, npairs // 2, lambda i, c: stage(1, stage(0, c)), (jnp.int32(0), jnp.int32(0)))
    if npairs % 2:
        stage(0, carry)


def _diff(lam_p, kb, pt, bias, g_col, lam_init):
    B, S, _ = kb.shape
    nq, tq = pt.shape[1], pt.shape[3]
    va_rows = DV_B + 16
    tk = KV_PER_Q * tq
    assert KV_PER_Q == 2 and nq % KV_PER_Q == 0
    return pl.pallas_call(
        functools.partial(_diff_kernel, lam_init=lam_init),
        grid=(B, H_B),
        in_specs=[
            _resident(lam_p.shape),
            pl.BlockSpec((None, nq, DV_B, tq), lambda b, h: (b, 0, h, 0)),
            pl.BlockSpec((None, S, LANES), lambda b, h: (b, 0, h)),
            pl.BlockSpec((None, nq, DV_B, tq), lambda b, h: (b, 0, H_B + h, 0)),
            pl.BlockSpec((None, 3, tk, tq), lambda b, h: (h, 0, 0, 0)),
            _resident(g_col.shape),
        ],
        out_specs=pl.BlockSpec((None, S, DV_B), lambda b, h: (b, 0, h)),
        out_shape=jax.ShapeDtypeStruct((B, S, W_B_OUT), BF16),
        scratch_shapes=[
            pltpu.VMEM((nq, DV_B, 2 * tq), BF16),
            pltpu.VMEM((S // tk, va_rows, tk), BF16),
            pltpu.VMEM((tk, 2 * tq), F32),
            pltpu.VMEM((tk, 2 * tq), F32),
            pltpu.VMEM((1, 2 * tq), F32),
            pltpu.VMEM((1, 2 * tq), F32),
            pltpu.VMEM((1, 2 * tq), F32),
            pltpu.VMEM((va_rows, 2 * tq), F32),
        ],
        compiler_params=_params("parallel", "parallel"),
        name="diff_attn",
    )(lam_p, pt, kb, pt, bias, g_col)


def _toeplitz_kernel(w_ref, o_ref, *, diff):
    width = o_ref.shape[-1]
    rows = width if diff else o_ref.shape[-2]
    t = pltpu.roll(jnp.broadcast_to(w_ref[...], (rows, width)), 0, 1, stride=1, stride_axis=0)
    if not diff:
        o_ref[...] = t
        return
    r = lax.broadcasted_iota(jnp.int32, (rows, width), 0)
    c = lax.broadcasted_iota(jnp.int32, (rows, width), 1)
    prev = jnp.where(c < r, t, 0.0)
    diag = jnp.where(c >= r, t, NEG_INF)
    o_ref[0, :rows] = jnp.zeros((rows, width), F32)
    o_ref[0, rows:] = prev
    o_ref[1, :rows] = prev
    o_ref[1, rows:] = diag
    o_ref[2, :rows] = diag
    o_ref[2, rows:] = jnp.full((rows, width), NEG_INF, F32)


def _toeplitz(w, rows, diff):
    n, _, width = w.shape
    oshape = (n, 3, KV_PER_Q * rows, width) if diff else (n, rows, width)
    oblock = (None,) + oshape[1:]
    return pl.pallas_call(
        functools.partial(_toeplitz_kernel, diff=diff),
        grid=(n,),
        in_specs=[pl.BlockSpec((None, 1, width), lambda i: (i, 0, 0))],
        out_specs=pl.BlockSpec(oblock, lambda i: (i,) + (0,) * (len(oshape) - 1)),
        out_shape=jax.ShapeDtypeStruct(oshape, F32),
        compiler_params=_params("parallel"),
        name="bias_diff" if diff else "bias_dilated",
    )(w)


def _layer_norm(h, g, b):
    mu = jnp.mean(h, axis=-1, keepdims=True)
    d = h - mu
    var = jnp.mean(d * d, axis=-1, keepdims=True)
    return d * lax.rsqrt(var + LN_EPS) * g + b


def _mix_kernel(x_ref, o0_ref, o1_ref, o2_ref, l0_ref, l1_ref, l2_ref, ob_ref, wg_ref, bg_ref, wa_ref, wb_ref,
                wo_ref, g_ref, b_ref, out_ref, *tok_sc):
    x = x_ref[...]
    xb = x.astype(BF16)

    def token_major(ref, sc):
        dil, rows, _ = ref.shape
        if dil == 1:
            return ref[0]
        nk = sc.shape[0]
        for res in range(dil):
            for k in range(nk):
                sc[k, pl.ds(res, rows, stride=dil), :] = ref[res, :, k * LANES:(k + 1) * LANES]
        return jnp.concatenate([sc[k] for k in range(nk)], axis=1)

    l0 = token_major(l0_ref, None)
    l1, l2 = token_major(l1_ref, tok_sc[0]), token_major(l2_ref, tok_sc[1])
    o0 = token_major(o0_ref, None)
    o1, o2 = token_major(o1_ref, tok_sc[2]), token_major(o2_ref, tok_sc[3])
    mx = jnp.maximum(jnp.maximum(l0, l1), l2)
    e0, e1, e2 = jnp.exp(l0 - mx), jnp.exp(l1 - mx), jnp.exp(l2 - mx)
    o_a = (e0 * o0 + e1 * o1 + e2 * o2) / (e0 + e1 + e2)
    y_a = _dot(o_a.astype(BF16), wa_ref[...])
    y_b = _dot(ob_ref[...], wb_ref[...])
    gate_a = jax.nn.sigmoid(_dot(xb, wg_ref[:, :D_MODEL]) + bg_ref[:, :D_MODEL])
    gate_b = jax.nn.sigmoid(_dot(xb, wg_ref[:, D_MODEL:]) + bg_ref[:, D_MODEL:])
    merged = gate_a * y_a + gate_b * y_b
    mix = _dot(merged.astype(BF16), wo_ref[...])
    out_ref[...] = _layer_norm(DEEPNORM_ALPHA * x + mix, g_ref[...], b_ref[...])


def _mix(x, o_g, lse_g, o_b, w_gate, b_gate, w_a, w_b, w_o, ln_g, ln_b, tm):
    B, S, D = x.shape
    tok = lambda w: pl.BlockSpec((None, tm, w), lambda b, i: (b, i, 0))
    res = [pl.BlockSpec((None, a.shape[1], tm // a.shape[1], W_A_OUT), lambda b, i: (b, 0, i, 0)) for a in o_g]
    n_sc = 2 * sum(a.shape[1] > 1 for a in o_g)
    return pl.pallas_call(
        _mix_kernel,
        grid=(B, S // tm),
        in_specs=[tok(D)] + res + res + [tok(W_B_OUT)]
        + [_resident(a.shape) for a in (w_gate, b_gate, w_a, w_b, w_o, ln_g, ln_b)],
        out_specs=tok(D),
        out_shape=jax.ShapeDtypeStruct((B, S, D), F32),
        scratch_shapes=[pltpu.VMEM((W_A_OUT // LANES, tm, LANES), F32)] * n_sc,
        compiler_params=_params("parallel", "parallel"),
        name="mix",
    )(x, *o_g, *lse_g, o_b, w_gate, b_gate, w_a, w_b, w_o, ln_g, ln_b)


def _mlp_kernel(x_ref, w1_ref, w2_ref, g_ref, b_ref, out_ref):
    x = x_ref[...]
    xb = x.astype(BF16)
    ff = jnp.zeros(x.shape, F32)
    for c in range(D_FF // (2 * CHUNK)):
        sl = slice(c * 2 * CHUNK, (c + 1) * 2 * CHUNK)
        h = jnp.maximum(_dot(xb, w1_ref[:, sl]), 0.0)
        ff = ff + _dot((h * h).astype(BF16), w2_ref[sl, :])
    out_ref[...] = _layer_norm(DEEPNORM_ALPHA * x + ff, g_ref[...], b_ref[...])


def _mlp(x, w1, w2, ln_g, ln_b, tm):
    B, S, D = x.shape
    tok = pl.BlockSpec((None, tm, D), lambda b, i: (b, i, 0))
    return pl.pallas_call(
        _mlp_kernel,
        grid=(B, S // tm),
        in_specs=[tok] + [_resident(a.shape) for a in (w1, w2, ln_g, ln_b)],
        out_specs=tok,
        out_shape=jax.ShapeDtypeStruct((B, S, D), F32),
        compiler_params=_params("parallel", "parallel"),
        name="mlp",
    )(x, w1, w2, ln_g, ln_b)


def _t5_bucket(dist):
    n = jnp.maximum(dist, 0)
    max_exact = NUM_BUCKETS // 2
    nf = jnp.maximum(n, 1).astype(F32)
    large = max_exact + (jnp.log(nf / max_exact) / math.log(T5_MAX_DISTANCE / max_exact)
                         * (NUM_BUCKETS - max_exact)).astype(jnp.int32)
    large = jnp.minimum(large, NUM_BUCKETS - 1)
    return jnp.where(n < max_exact, n, large)


def _dilated_bias(rel_bias, g, dil):
    j = jnp.arange(2 * BLK)
    tab = rel_bias[:, g * H_A:(g + 1) * H_A].astype(F32)
    vec = tab[_t5_bucket(jnp.maximum(BLK - j, 0) * dil)]
    vec = jnp.where((j <= BLK)[:, None], vec, NEG_INF)
    bias = _toeplitz(vec.T.reshape(H_A, 1, 2 * BLK), BLK, diff=False)
    bias = bias.reshape(H_A // 2, 2, BLK, 2, BLK)
    return bias.transpose(0, 3, 1, 2, 4).reshape(H_A // 2, 2, 2 * BLK, BLK)


def _diff_bias(rel_bias, tq):
    assert tq >= T5_MAX_DISTANCE
    tab = rel_bias[:, N_GROUPS * H_A:].astype(F32)
    tab = (tab - tab[NUM_BUCKETS - 1:NUM_BUCKETS]) * LOG2E
    vec = tab[_t5_bucket(jnp.arange(tq))]
    return _toeplitz(vec.T.reshape(H_B, 1, tq), tq, diff=True)


def kernel(x, w_in, b_gate, lambda_q1, lambda_k1, lambda_q2, lambda_k2, subln_g, rel_bias, w_proj_a, w_proj_b,
           w_out, ln1_g, ln1_b, ln2_g, ln2_b, w_mlp1, w_mlp2):
    B, S, D = x.shape
    tq = min(TQ, S)
    tm = min(TM, S)
    w = w_in[0]
    w_a = w[:, :COLS_A].reshape(D, 3, N_GROUPS, W_A_OUT)
    w_a = jnp.concatenate([w_a[:, :1] * QK_SCALE, w_a[:, 1:]], axis=1).transpose(0, 2, 1, 3).reshape(D, COLS_A)
    w_qk = w[:, COLS_A:COLS_A + COLS_B_QK].reshape(D, 4, H_B, HEAD_DIM)
    w_qb = jnp.concatenate([w_qk[:, 0], w_qk[:, 1]], axis=-1).reshape(D, W_B_OUT)
    w_kb = jnp.concatenate([w_qk[:, 2], w_qk[:, 3]], axis=-1).reshape(D, W_B_OUT)
    w_vb = w[:, COLS_A + COLS_B_QK:COLS_A + COLS_B]
    w_n = jnp.concatenate([w_a, w_kb], axis=1).astype(BF16)
    w_t = jnp.concatenate([w_qb, w_vb], axis=1).T.astype(BF16)
    w_gate = w[:, COLS_A + COLS_B:].astype(BF16)

    *a_g, kb, pt = _proj(x, w_n, w_t, tm, tq)

    o_g, lse_g = [], []
    for g, (win, dil) in enumerate(DIL_PAIRS):
        assert win // dil == BLK
        o, lse = _dilated(a_g[g], _dilated_bias(rel_bias, g, dil), g)
        o_g.append(o)
        lse_g.append(lse)

    lam_init = 0.8 - 0.6 * math.exp(-0.3 * 0)
    lam_p = jnp.concatenate([lambda_q1, lambda_k1, lambda_q2, lambda_k2], axis=0).astype(F32)
    o_b = _diff(lam_p, kb, pt, _diff_bias(rel_bias, tq), subln_g[0].reshape(DV_B, 1), lam_init)

    x1 = _mix(x, o_g, lse_g, o_b, w_gate, b_gate, w_proj_a[0].astype(BF16), w_proj_b[0].astype(BF16),
              w_out[0].astype(BF16), ln1_g, ln1_b, tm)
    return _mlp(x1, w_mlp1[0].astype(BF16), w_mlp2[0].astype(BF16), ln2_g, ln2_b, tm)
```

```python
import functools
import math

import jax
import jax.numpy as jnp
import numpy as np
from jax import lax
from jax.experimental import pallas as pl
from jax.experimental.pallas import tpu as pltpu

D_MODEL = 1024
HEAD_DIM = 64
DIL_PAIRS = ((128, 1), (512, 4), (2048, 16))
N_GROUPS = len(DIL_PAIRS)
H_A = 8
H_B = 8
DV_B = 2 * HEAD_DIM
D_FF = 4 * D_MODEL
NUM_BUCKETS = 32
T5_MAX_DISTANCE = 128
BLK = 128
LN_EPS = 1e-5
NEG_INF = -1e30
W_A_OUT = H_A * HEAD_DIM
W_B_OUT = H_B * DV_B
COLS_A = 3 * N_GROUPS * H_A * HEAD_DIM
COLS_B_QK = 4 * H_B * HEAD_DIM
COLS_B = COLS_B_QK + H_B * DV_B
DEPTH = 1
DEEPNORM_ALPHA = (2.0 * DEPTH) ** 0.25
QK_SCALE = HEAD_DIM ** -0.5
LOG2E = math.log2(math.e)

LANES = 128
COLS_N = COLS_A + H_B * 2 * HEAD_DIM
COLS_T = 2 * W_B_OUT
CHUNK = 512
TQ = 512
STRIP = 256
KV_PER_Q = 2
TM = 512
VMEM_LIMIT = 56 * 1024 * 1024

BF16 = jnp.bfloat16
F32 = jnp.float32


def _dot(a, b):
    return jnp.dot(a, b, preferred_element_type=F32)


def _dot_nt(a, b):
    return lax.dot_general(a, b, (((1,), (1,)), ((), ())), preferred_element_type=F32)


def _resident(shape):
    nd = len(shape)
    return pl.BlockSpec(shape, lambda *_: (0,) * nd, pipeline_mode=pl.Buffered(1))


def _params(*sem, flags=None):
    return pltpu.CompilerParams(dimension_semantics=sem, vmem_limit_bytes=VMEM_LIMIT, flags=flags)


def _proj_kernel(x_ref, wn_ref, wt_ref, a0_ref, a1_ref, a2_ref, kb_ref, ot_ref, rows_sc, *, tq):
    xb = x_ref[...].astype(BF16)
    tm = xb.shape[0]
    a_refs = (a0_ref, a1_ref, a2_ref)
    for c in range(COLS_N // CHUNK):
        r = _dot(xb, wn_ref[:, c * CHUNK:(c + 1) * CHUNK])
        g, part = divmod(c, 3)
        if g >= N_GROUPS:
            kb_ref[:, (c - 3 * N_GROUPS) * CHUNK:(c - 3 * N_GROUPS + 1) * CHUNK] = r.astype(BF16)
            continue
        sl = slice(part * W_A_OUT, (part + 1) * W_A_OUT)
        dil = DIL_PAIRS[g][1]
        if dil == 1:
            a_refs[g][0, :, sl] = r.astype(BF16)
            continue
        for k in range(CHUNK // LANES):
            rows_sc[k] = r[:, k * LANES:(k + 1) * LANES]
        for res in range(dil):
            for k in range(CHUNK // LANES):
                piece = rows_sc[k, pl.ds(res, tm // dil, stride=dil), :]
                a_refs[g][res, :, part * W_A_OUT + k * LANES:part * W_A_OUT + (k + 1) * LANES] = piece.astype(BF16)
    for c in range(COLS_T // CHUNK):
        sl = slice(c * CHUNK, (c + 1) * CHUNK)
        r = _dot_nt(wt_ref[sl, :], xb)
        if (c + 1) * CHUNK <= W_B_OUT:
            r = r * (QK_SCALE * LOG2E)
        r = r.astype(BF16)
        for t in range(tm // tq):
            ot_ref[t, sl, :] = r[:, t * tq:(t + 1) * tq]


def _proj(x, w_n, w_t, tm, tq):
    B, S, D = x.shape
    assert CHUNK == W_A_OUT
    dils = [d for _, d in DIL_PAIRS]
    return pl.pallas_call(
        functools.partial(_proj_kernel, tq=tq),
        grid=(B, S // tm),
        in_specs=[
            pl.BlockSpec((None, tm, D), lambda b, i: (b, i, 0)),
            _resident((D, COLS_N)),
            _resident((COLS_T, D)),
        ],
        out_specs=[pl.BlockSpec((None, d, tm // d, 3 * W_A_OUT), lambda b, i: (b, 0, i, 0)) for d in dils] + [
            pl.BlockSpec((None, tm, W_B_OUT), lambda b, i: (b, i, 0)),
            pl.BlockSpec((None, tm // tq, COLS_T, tq), lambda b, i: (b, i, 0, 0)),
        ],
        out_shape=[jax.ShapeDtypeStruct((B, d, S // d, 3 * W_A_OUT), BF16) for d in dils] + [
            jax.ShapeDtypeStruct((B, S, W_B_OUT), BF16),
            jax.ShapeDtypeStruct((B, S // tq, COLS_T, tq), BF16),
        ],
        scratch_shapes=[pltpu.VMEM((CHUNK // LANES, tm, LANES), F32)],
        compiler_params=_params("parallel", "parallel"),
        name="proj",
    )(x, w_n, w_t)


def _dilated_kernel(q_ref, k_ref, v_ref, kp_ref, vp_ref, bias_ref, o_ref, lse_ref):
    tb = q_ref.shape[0]
    first = pl.program_id(2) == 0
    lane = lax.broadcasted_iota(jnp.int32, (BLK, LANES), 1)
    lo = lane < HEAD_DIM
    for jb in range(tb // BLK):
        rows = slice(jb * BLK, (jb + 1) * BLK)
        prows = slice((jb - 1) * BLK, jb * BLK)
        for p in range(H_A // 2):
            cols = slice(p * LANES, (p + 1) * LANES)
            q2 = q_ref[rows, cols]
            zero = jnp.zeros_like(q2)
            qd = jnp.concatenate([jnp.where(lo, q2, zero), jnp.where(lo, zero, q2)], axis=0)
            kc, vc = k_ref[rows, cols], v_ref[rows, cols]
            if jb == 0:
                kp, vp = kp_ref[:, cols], vp_ref[:, cols]
            else:
                kp, vp = k_ref[prows, cols], v_ref[prows, cols]
            s_c = _dot_nt(qd, kc) + bias_ref[p, 1]
            s_p = _dot_nt(qd, kp) + bias_ref[p, 0]
            if jb == 0:
                s_p = jnp.where(first, NEG_INF, s_p)
            m = jnp.maximum(jnp.max(s_c, axis=-1, keepdims=True), jnp.max(s_p, axis=-1, keepdims=True))
            p_c = jnp.exp(s_c - m)
            p_p = jnp.exp(s_p - m)
            den = jnp.sum(p_c, axis=-1, keepdims=True) + jnp.sum(p_p, axis=-1, keepdims=True)
            o2 = (_dot(p_c.astype(BF16), vc) + _dot(p_p.astype(BF16), vp)) / den
            lse = jnp.broadcast_to(m + jnp.log(den), (2 * BLK, LANES))
            o_ref[rows, cols] = jnp.where(lo, o2[:BLK], o2[BLK:])
            lse_ref[rows, cols] = jnp.where(lo, lse[:BLK], lse[BLK:])


def _dilated(a, bias, g):
    B, dil, L, _ = a.shape
    tb = min(CHUNK, L)
    cur = lambda c: pl.BlockSpec((None, None, tb, W_A_OUT), lambda b, r, n: (b, r, n, c))
    prev = lambda c: pl.BlockSpec(
        (None, None, BLK, W_A_OUT), lambda b, r, n: (b, r, jnp.maximum(n * (tb // BLK) - 1, 0), c))
    out = pl.BlockSpec((None, None, tb, W_A_OUT), lambda b, r, n: (b, r, n, 0))
    return pl.pallas_call(
        _dilated_kernel,
        grid=(B, dil, L // tb),
        in_specs=[cur(0), cur(1), cur(2), prev(1), prev(2), _resident(bias.shape)],
        out_specs=[out, out],
        out_shape=[jax.ShapeDtypeStruct((B, dil, L, W_A_OUT), F32)] * 2,
        compiler_params=_params("parallel", "parallel", "arbitrary"),
        name=f"dilated{g}",
    )(a, a, a, a, a, bias)


def _diff_kernel(lam_ref, qt_ref, k_ref, vt_ref, bias_ref, g_ref, o_ref, qd_sc, va_sc, s0_sc, s1_sc, x0_sc, x1_sc,
                 m_sc, acc_sc, *, lam_init):
    nq, _, tq = qt_ref.shape
    tk = KV_PER_Q * tq
    s_bufs, x_bufs = (s0_sc, s1_sc), (x0_sc, x1_sc)
    row = lax.broadcasted_iota(jnp.int32, (DV_B, tq), 0)
    for i in range(nq):
        qt = qt_ref[i]
        zero = jnp.zeros_like(qt)
        qd_sc[i, :, :tq] = jnp.where(row < HEAD_DIM, qt, zero)
        qd_sc[i, :, tq:] = jnp.where(row < HEAD_DIM, zero, qt)
        va_sc[i // KV_PER_Q, :DV_B, (i % KV_PER_Q) * tq:(i % KV_PER_Q + 1) * tq] = vt_ref[i]
    for i in range(nq // KV_PER_Q):
        va_sc[i, DV_B:, :] = jnp.ones((va_sc.shape[1] - DV_B, tk), BF16)

    lam = (jnp.exp(jnp.sum(lam_ref[0:1, :] * lam_ref[1:2, :], axis=-1, keepdims=True))
           - jnp.exp(jnp.sum(lam_ref[2:3, :] * lam_ref[3:4, :], axis=-1, keepdims=True)) + lam_init)

    def reset():
        m_sc[...] = jnp.full(m_sc.shape, -jnp.inf, F32)
        acc_sc[...] = jnp.zeros(acc_sc.shape, F32)

    def finalize(qi):
        acc = acc_sc[...]
        on = acc[:DV_B] / acc[DV_B:DV_B + 1]
        o = on[:, :tq] - lam * on[:, tq:]
        ms = jnp.mean(o * o, axis=0, keepdims=True)
        o = o * lax.rsqrt(ms + LN_EPS) * g_ref[...] * (1.0 - lam_init)
        o_ref[pl.ds(pl.multiple_of(qi * tq, tq), tq), :] = o.T.astype(BF16)

    reset()
    s0 = _dot(k_ref[0:tk, :], qd_sc[0])
    s_bufs[0][...] = s0
    x_bufs[0][...] = jnp.max(s0, axis=0, keepdims=True)

    def stage(slot, carry):
        qi, j = carry
        s_cur, x_cur, s_oth, x_oth = s_bufs[slot], x_bufs[slot], s_bufs[1 - slot], x_bufs[1 - slot]
        odd = qi & 1
        last = j == qi // KV_PER_Q
        j_n = jnp.where(last, 0, j + 1)
        qi_n = jnp.minimum(jnp.where(last, qi + 1, qi), nq - 1)

        @pl.when(last | ((odd == 0) & (j == qi // KV_PER_Q - 1)))
        def _():
            b = bias_ref[jnp.where(last, 2 - odd, 0)]
            sb = s_cur[...] + jnp.concatenate([b, b], axis=1)
            s_cur[...] = sb
            x_cur[...] = jnp.max(sb, axis=0, keepdims=True)

        kblk = k_ref[pl.ds(pl.multiple_of(j_n * tk, tk), tk), :]
        va = va_sc[j]
        for c in range(2 * tq // STRIP):
            cs = slice(c * STRIP, (c + 1) * STRIP)
            m_old = m_sc[:, cs]
            m_new = jnp.maximum(m_old, x_cur[:, cs])
            p = jnp.exp2((s_cur[:, cs] - m_new).astype(BF16))
            alpha = jnp.exp2(m_old - m_new)
            m_sc[:, cs] = m_new
            acc_sc[:, cs] = acc_sc[:, cs] * alpha + _dot(va, p)
            s_nxt = _dot(kblk, qd_sc[qi_n, :, cs])
            s_oth[:, cs] = s_nxt
            x_oth[:, cs] = jnp.max(s_nxt, axis=0, keepdims=True)

        @pl.when(last)
        def _():
            finalize(qi)
            reset()

        return qi_n, j_n

    npairs = sum(qi // KV_PER_Q + 1 for qi in range(nq))
    carry = lax.fori_loop(0, npairs // 2, lambda i, c: stage(1, stage(0, c)), (jnp.int32(0), jnp.int32(0)))
    if npairs % 2:
        stage(0, carry)


def _diff(lam_p, kb, pt, bias, g_col, lam_init):
    B, S, _ = kb.shape
    nq, tq = pt.shape[1], pt.shape[3]
    va_rows = DV_B + 16
    tk = KV_PER_Q * tq
    assert KV_PER_Q == 2 and nq % KV_PER_Q == 0
    return pl.pallas_call(
        functools.partial(_diff_kernel, lam_init=lam_init),
        grid=(B, H_B),
        in_specs=[
            _resident(lam_p.shape),
            pl.BlockSpec((None, nq, DV_B, tq), lambda b, h: (b, 0, h, 0)),
            pl.BlockSpec((None, S, LANES), lambda b, h: (b, 0, h)),
            pl.BlockSpec((None, nq, DV_B, tq), lambda b, h: (b, 0, H_B + h, 0)),
            pl.BlockSpec((None, 3, tk, tq), lambda b, h: (h, 0, 0, 0)),
            _resident(g_col.shape),
        ],
        out_specs=pl.BlockSpec((None, S, DV_B), lambda b, h: (b, 0, h)),
        out_shape=jax.ShapeDtypeStruct((B, S, W_B_OUT), BF16),
        scratch_shapes=[
            pltpu.VMEM((nq, DV_B, 2 * tq), BF16),
            pltpu.VMEM((S // tk, va_rows, tk), BF16),
            pltpu.VMEM((tk, 2 * tq), F32),
            pltpu.VMEM((tk, 2 * tq), F32),
            pltpu.VMEM((1, 2 * tq), F32),
            pltpu.VMEM((1, 2 * tq), F32),
            pltpu.VMEM((1, 2 * tq), F32),
            pltpu.VMEM((va_rows, 2 * tq), F32),
        ],
        compiler_params=_params("parallel", "parallel"),
        name="diff_attn",
    )(lam_p, pt, kb, pt, bias, g_col)


def _toeplitz_kernel(w_ref, o_ref, *, diff):
    width = o_ref.shape[-1]
    rows = width if diff else o_ref.shape[-2]
    t = pltpu.roll(jnp.broadcast_to(w_ref[...], (rows, width)), 0, 1, stride=1, stride_axis=0)
    if not diff:
        o_ref[...] = t
        return
    r = lax.broadcasted_iota(jnp.int32, (rows, width), 0)
    c = lax.broadcasted_iota(jnp.int32, (rows, width), 1)
    prev = jnp.where(c < r, t, 0.0)
    diag = jnp.where(c >= r, t, NEG_INF)
    o_ref[0, :rows] = jnp.zeros((rows, width), F32)
    o_ref[0, rows:] = prev
    o_ref[1, :rows] = prev
    o_ref[1, rows:] = diag
    o_ref[2, :rows] = diag
    o_ref[2, rows:] = jnp.full((rows, width), NEG_INF, F32)


def _toeplitz(w, rows, diff):
    n, _, width = w.shape
    oshape = (n, 3, KV_PER_Q * rows, width) if diff else (n, rows, width)
    oblock = (None,) + oshape[1:]
    return pl.pallas_call(
        functools.partial(_toeplitz_kernel, diff=diff),
        grid=(n,),
        in_specs=[pl.BlockSpec((None, 1, width), lambda i: (i, 0, 0))],
        out_specs=pl.BlockSpec(oblock, lambda i: (i,) + (0,) * (len(oshape) - 1)),
        out_shape=jax.ShapeDtypeStruct(oshape, F32),
        compiler_params=_params("parallel"),
        name="bias_diff" if diff else "bias_dilated",
    )(w)


def _layer_norm(h, g, b):
    mu = jnp.mean(h, axis=-1, keepdims=True)
    d = h - mu
    var = jnp.mean(d * d, axis=-1, keepdims=True)
    return d * lax.rsqrt(var + LN_EPS) * g + b


def _mix_kernel(x_ref, o0_ref, o1_ref, o2_ref, l0_ref, l1_ref, l2_ref, ob_ref, wg_ref, bg_ref, wa_ref, wb_ref,
                wo_ref, g_ref, b_ref, out_ref, *tok_sc):
    x = x_ref[...]
    xb = x.astype(BF16)

    def token_major(ref, sc):
        dil, rows, _ = ref.shape
        if dil == 1:
            return ref[0]
        nk = sc.shape[0]
        for res in range(dil):
            for k in range(nk):
                sc[k, pl.ds(res, rows, stride=dil), :] = ref[res, :, k * LANES:(k + 1) * LANES]
        return jnp.concatenate([sc[k] for k in range(nk)], axis=1)

    l0 = token_major(l0_ref, None)
    l1, l2 = token_major(l1_ref, tok_sc[0]), token_major(l2_ref, tok_sc[1])
    o0 = token_major(o0_ref, None)
    o1, o2 = token_major(o1_ref, tok_sc[2]), token_major(o2_ref, tok_sc[3])
    mx = jnp.maximum(jnp.maximum(l0, l1), l2)
    e0, e1, e2 = jnp.exp(l0 - mx), jnp.exp(l1 - mx), jnp.exp(l2 - mx)
    o_a = (e0 * o0 + e1 * o1 + e2 * o2) / (e0 + e1 + e2)
    y_a = _dot(o_a.astype(BF16), wa_ref[...])
    y_b = _dot(ob_ref[...], wb_ref[...])
    gate_a = jax.nn.sigmoid(_dot(xb, wg_ref[:, :D_MODEL]) + bg_ref[:, :D_MODEL])
    gate_b = jax.nn.sigmoid(_dot(xb, wg_ref[:, D_MODEL:]) + bg_ref[:, D_MODEL:])
    merged = gate_a * y_a + gate_b * y_b
    mix = _dot(merged.astype(BF16), wo_ref[...])
    out_ref[...] = _layer_norm(DEEPNORM_ALPHA * x + mix, g_ref[...], b_ref[...])


def _mix(x, o_g, lse_g, o_b, w_gate, b_gate, w_a, w_b, w_o, ln_g, ln_b, tm):
    B, S, D = x.shape
    tok = lambda w: pl.BlockSpec((None, tm, w), lambda b, i: (b, i, 0))
    res = [pl.BlockSpec((None, a.shape[1], tm // a.shape[1], W_A_OUT), lambda b, i: (b, 0, i, 0)) for a in o_g]
    n_sc = 2 * sum(a.shape[1] > 1 for a in o_g)
    return pl.pallas_call(
        _mix_kernel,
        grid=(B, S // tm),
        in_specs=[tok(D)] + res + res + [tok(W_B_OUT)]
        + [_resident(a.shape) for a in (w_gate, b_gate, w_a, w_b, w_o, ln_g, ln_b)],
        out_specs=tok(D),
        out_shape=jax.ShapeDtypeStruct((B, S, D), F32),
        scratch_shapes=[pltpu.VMEM((W_A_OUT // LANES, tm, LANES), F32)] * n_sc,
        compiler_params=_params("parallel", "parallel"),
        name="mix",
    )(x, *o_g, *lse_g, o_b, w_gate, b_gate, w_a, w_b, w_o, ln_g, ln_b)


def _mlp_kernel(x_ref, w1_ref, w2_ref, g_ref, b_ref, out_ref):
    x = x_ref[...]
    xb = x.astype(BF16)
    ff = jnp.zeros(x.shape, F32)
    for c in range(D_FF // (2 * CHUNK)):
        sl = slice(c * 2 * CHUNK, (c + 1) * 2 * CHUNK)
        h = jnp.maximum(_dot(xb, w1_ref[:, sl]), 0.0)
        ff = ff + _dot((h * h).astype(BF16), w2_ref[sl, :])
    out_ref[...] = _layer_norm(DEEPNORM_ALPHA * x + ff, g_ref[...], b_ref[...])


def _mlp(x, w1, w2, ln_g, ln_b, tm):
    B, S, D = x.shape
    tok = pl.BlockSpec((None, tm, D), lambda b, i: (b, i, 0))
    return pl.pallas_call(
        _mlp_kernel,
        grid=(B, S // tm),
        in_specs=[tok] + [_resident(a.shape) for a in (w1, w2, ln_g, ln_b)],
        out_specs=tok,
        out_shape=jax.ShapeDtypeStruct((B, S, D), F32),
        compiler_params=_params("parallel", "parallel"),
        name="mlp",
    )(x, w1, w2, ln_g, ln_b)


def _t5_bucket(dist):
    n = jnp.maximum(dist, 0)
    max_exact = NUM_BUCKETS // 2
    nf = jnp.maximum(n, 1).astype(F32)
    large = max_exact + (jnp.log(nf / max_exact) / math.log(T5_MAX_DISTANCE / max_exact)
                         * (NUM_BUCKETS - max_exact)).astype(jnp.int32)
    large = jnp.minimum(large, NUM_BUCKETS - 1)
    return jnp.where(n < max_exact, n, large)


def _dilated_bias(rel_bias, g, dil):
    j = jnp.arange(2 * BLK)
    tab = rel_bias[:, g * H_A:(g + 1) * H_A].astype(F32)
    vec = tab[_t5_bucket(jnp.maximum(BLK - j, 0) * dil)]
    vec = jnp.where((j <= BLK)[:, None], vec, NEG_INF)
    bias = _toeplitz(vec.T.reshape(H_A, 1, 2 * BLK), BLK, diff=False)
    bias = bias.reshape(H_A // 2, 2, BLK, 2, BLK)
    return bias.transpose(0, 3, 1, 2, 4).reshape(H_A // 2, 2, 2 * BLK, BLK)


def _diff_bias(rel_bias, tq):
    assert tq >= T5_MAX_DISTANCE
    tab = rel_bias[:, N_GROUPS * H_A:].astype(F32)
    tab = (tab - tab[NUM_BUCKETS - 1:NUM_BUCKETS]) * LOG2E
    vec = tab[_t5_bucket(jnp.arange(tq))]
    return _toeplitz(vec.T.reshape(H_B, 1, tq), tq, diff=True)


def kernel(x, w_in, b_gate, lambda_q1, lambda_k1, lambda_q2, lambda_k2, subln_g, rel_bias, w_proj_a, w_proj_b,
           w_out, ln1_g, ln1_b, ln2_g, ln2_b, w_mlp1, w_mlp2):
    B, S, D = x.shape
    tq = min(TQ, S)
    tm = min(TM, S)
    w = w_in[0]
    w_a = w[:, :COLS_A].reshape(D, 3, N_GROUPS, W_A_OUT)
    w_a = jnp.concatenate([w_a[:, :1] * QK_SCALE, w_a[:, 1:]], axis=1).transpose(0, 2, 1, 3).reshape(D, COLS_A)
    w_qk = w[:, COLS_A:COLS_A + COLS_B_QK].reshape(D, 4, H_B, HEAD_DIM)
    w_qb = jnp.concatenate([w_qk[:, 0], w_qk[:, 1]], axis=-1).reshape(D, W_B_OUT)
    w_kb = jnp.concatenate([w_qk[:, 2], w_qk[:, 3]], axis=-1).reshape(D, W_B_OUT)
    w_vb = w[:, COLS_A + COLS_B_QK:COLS_A + COLS_B]
    w_n = jnp.concatenate([w_a, w_kb], axis=1).astype(BF16)
    w_t = jnp.concatenate([w_qb, w_vb], axis=1).T.astype(BF16)
    w_gate = w[:, COLS_A + COLS_B:].astype(BF16)

    *a_g, kb, pt = _proj(x, w_n, w_t, tm, tq)

    o_g, lse_g = [], []
    for g, (win, dil) in enumerate(DIL_PAIRS):
        assert win // dil == BLK
        o, lse = _dilated(a_g[g], _dilated_bias(rel_bias, g, dil), g)
        o_g.append(o)
        lse_g.append(lse)

    lam_init = 0.8 - 0.6 * math.exp(-0.3 * 0)
    lam_p = jnp.concatenate([lambda_q1, lambda_k1, lambda_q2, lambda_k2], axis=0).astype(F32)
    o_b = _diff(lam_p, kb, pt, _diff_bias(rel_bias, tq), subln_g[0].reshape(DV_B, 1), lam_init)

    x1 = _mix(x, o_g, lse_g, o_b, w_gate, b_gate, w_proj_a[0].astype(BF16), w_proj_b[0].astype(BF16),
              w_out[0].astype(BF16), ln1_g, ln1_b, tm)
    return _mlp(x1, w_mlp1[0].astype(BF16), w_mlp2[0].astype(BF16), ln2_g, ln2_b, tm)
```

```python
import functools
import math

import jax
import jax.numpy as jnp
import numpy as np
from jax import lax
from jax.experimental import pallas as pl
from jax.experimental.pallas import tpu as pltpu

D_MODEL = 1024
HEAD_DIM = 64
DIL_PAIRS = ((128, 1), (512, 4), (2048, 16))
N_GROUPS = len(DIL_PAIRS)
H_A = 8
H_B = 8
DV_B = 2 * HEAD_DIM
D_FF = 4 * D_MODEL
NUM_BUCKETS = 32
T5_MAX_DISTANCE = 128
BLK = 128
LN_EPS = 1e-5
NEG_INF = -1e30
W_A_OUT = H_A * HEAD_DIM
W_B_OUT = H_B * DV_B
COLS_A = 3 * N_GROUPS * H_A * HEAD_DIM
COLS_B_QK = 4 * H_B * HEAD_DIM
COLS_B = COLS_B_QK + H_B * DV_B
DEPTH = 1
DEEPNORM_ALPHA = (2.0 * DEPTH) ** 0.25
QK_SCALE = HEAD_DIM ** -0.5
LOG2E = math.log2(math.e)

LANES = 128
COLS_N = COLS_A + H_B * 2 * HEAD_DIM
COLS_T = 2 * W_B_OUT
CHUNK = 512
TQ = 512
STRIP = 256
KV_PER_Q = 2
TM = 512
VMEM_LIMIT = 56 * 1024 * 1024

BF16 = jnp.bfloat16
F32 = jnp.float32


def _dot(a, b):
    return jnp.dot(a, b, preferred_element_type=F32)


def _dot_nt(a, b):
    return lax.dot_general(a, b, (((1,), (1,)), ((), ())), preferred_element_type=F32)


def _resident(shape):
    nd = len(shape)
    return pl.BlockSpec(shape, lambda *_: (0,) * nd, pipeline_mode=pl.Buffered(1))


def _params(*sem, flags=None):
    return pltpu.CompilerParams(dimension_semantics=sem, vmem_limit_bytes=VMEM_LIMIT, flags=flags)


def _proj_kernel(x_ref, wn_ref, wt_ref, a0_ref, a1_ref, a2_ref, kb_ref, ot_ref, rows_sc, *, tq):
    xb = x_ref[...].astype(BF16)
    tm = xb.shape[0]
    a_refs = (a0_ref, a1_ref, a2_ref)
    for c in range(COLS_N // CHUNK):
        r = _dot(xb, wn_ref[:, c * CHUNK:(c + 1) * CHUNK])
        g, part = divmod(c, 3)
        if g >= N_GROUPS:
            kb_ref[:, (c - 3 * N_GROUPS) * CHUNK:(c - 3 * N_GROUPS + 1) * CHUNK] = r.astype(BF16)
            continue
        sl = slice(part * W_A_OUT, (part + 1) * W_A_OUT)
        dil = DIL_PAIRS[g][1]
        if dil == 1:
            a_refs[g][0, :, sl] = r.astype(BF16)
            continue
        for k in range(CHUNK // LANES):
            rows_sc[k] = r[:, k * LANES:(k + 1) * LANES]
        for res in range(dil):
            for k in range(CHUNK // LANES):
                piece = rows_sc[k, pl.ds(res, tm // dil, stride=dil), :]
                a_refs[g][res, :, part * W_A_OUT + k * LANES:part * W_A_OUT + (k + 1) * LANES] = piece.astype(BF16)
    for c in range(COLS_T // CHUNK):
        sl = slice(c * CHUNK, (c + 1) * CHUNK)
        r = _dot_nt(wt_ref[sl, :], xb)
        if (c + 1) * CHUNK <= W_B_OUT:
            r = r * (QK_SCALE * LOG2E)
        r = r.astype(BF16)
        for t in range(tm // tq):
            ot_ref[t, sl, :] = r[:, t * tq:(t + 1) * tq]


def _proj(x, w_n, w_t, tm, tq):
    B, S, D = x.shape
    assert CHUNK == W_A_OUT
    dils = [d for _, d in DIL_PAIRS]
    return pl.pallas_call(
        functools.partial(_proj_kernel, tq=tq),
        grid=(B, S // tm),
        in_specs=[
            pl.BlockSpec((None, tm, D), lambda b, i: (b, i, 0)),
            _resident((D, COLS_N)),
            _resident((COLS_T, D)),
        ],
        out_specs=[pl.BlockSpec((None, d, tm // d, 3 * W_A_OUT), lambda b, i: (b, 0, i, 0)) for d in dils] + [
            pl.BlockSpec((None, tm, W_B_OUT), lambda b, i: (b, i, 0)),
            pl.BlockSpec((None, tm // tq, COLS_T, tq), lambda b, i: (b, i, 0, 0)),
        ],
        out_shape=[jax.ShapeDtypeStruct((B, d, S // d, 3 * W_A_OUT), BF16) for d in dils] + [
            jax.ShapeDtypeStruct((B, S, W_B_OUT), BF16),
            jax.ShapeDtypeStruct((B, S // tq, COLS_T, tq), BF16),
        ],
        scratch_shapes=[pltpu.VMEM((CHUNK // LANES, tm, LANES), F32)],
        compiler_params=_params("parallel", "parallel"),
        name="proj",
    )(x, w_n, w_t)


def _dilated_kernel(q_ref, k_ref, v_ref, kp_ref, vp_ref, bias_ref, o_ref, lse_ref):
    tb = q_ref.shape[0]
    first = pl.program_id(2) == 0
    lane = lax.broadcasted_iota(jnp.int32, (BLK, LANES), 1)
    lo = lane < HEAD_DIM
    for jb in range(tb // BLK):
        rows = slice(jb * BLK, (jb + 1) * BLK)
        prows = slice((jb - 1) * BLK, jb * BLK)
        for p in range(H_A // 2):
            cols = slice(p * LANES, (p + 1) * LANES)
            q2 = q_ref[rows, cols]
            zero = jnp.zeros_like(q2)
            qd = jnp.concatenate([jnp.where(lo, q2, zero), jnp.where(lo, zero, q2)], axis=0)
            kc, vc = k_ref[rows, cols], v_ref[rows, cols]
            if jb == 0:
                kp, vp = kp_ref[:, cols], vp_ref[:, cols]
            else:
                kp, vp = k_ref[prows, cols], v_ref[prows, cols]
            s_c = _dot_nt(qd, kc) + bias_ref[p, 1]
            s_p = _dot_nt(qd, kp) + bias_ref[p, 0]
            if jb == 0:
                s_p = jnp.where(first, NEG_INF, s_p)
            m = jnp.maximum(jnp.max(s_c, axis=-1, keepdims=True), jnp.max(s_p, axis=-1, keepdims=True))
            p_c = jnp.exp(s_c - m)
            p_p = jnp.exp(s_p - m)
            den = jnp.sum(p_c, axis=-1, keepdims=True) + jnp.sum(p_p, axis=-1, keepdims=True)
            o2 = (_dot(p_c.astype(BF16), vc) + _dot(p_p.astype(BF16), vp)) / den
            lse = jnp.broadcast_to(m + jnp.log(den), (2 * BLK, LANES))
            o_ref[rows, cols] = jnp.where(lo, o2[:BLK], o2[BLK:])
            lse_ref[rows, cols] = jnp.where(lo, lse[:BLK], lse[BLK:])


def _dilated(a, bias, g):
    B, dil, L, _ = a.shape
    tb = min(CHUNK, L)
    cur = lambda c: pl.BlockSpec((None, None, tb, W_A_OUT), lambda b, r, n: (b, r, n, c))
    prev = lambda c: pl.BlockSpec(
        (None, None, BLK, W_A_OUT), lambda b, r, n: (b, r, jnp.maximum(n * (tb // BLK) - 1, 0), c))
    out = pl.BlockSpec((None, None, tb, W_A_OUT), lambda b, r, n: (b, r, n, 0))
    return pl.pallas_call(
        _dilated_kernel,
        grid=(B, dil, L // tb),
        in_specs=[cur(0), cur(1), cur(2), prev(1), prev(2), _resident(bias.shape)],
        out_specs=[out, out],
        out_shape=[jax.ShapeDtypeStruct((B, dil, L, W_A_OUT), F32)] * 2,
        compiler_params=_params("parallel", "parallel", "arbitrary"),
        name=f"dilated{g}",
    )(a, a, a, a, a, bias)


def _diff_kernel(lam_ref, qt_ref, k_ref, vt_ref, bias_ref, g_ref, o_ref, qd_sc, va_sc, s0_sc, s1_sc, x0_sc, x1_sc,
                 m_sc, acc_sc, *, lam_init):
    nq, _, tq = qt_ref.shape
    tk = KV_PER_Q * tq
    s_bufs, x_bufs = (s0_sc, s1_sc), (x0_sc, x1_sc)
    row = lax.broadcasted_iota(jnp.int32, (DV_B, tq), 0)
    for i in range(nq):
        qt = qt_ref[i]
        zero = jnp.zeros_like(qt)
        qd_sc[i, :, :tq] = jnp.where(row < HEAD_DIM, qt, zero)
        qd_sc[i, :, tq:] = jnp.where(row < HEAD_DIM, zero, qt)
        va_sc[i // KV_PER_Q, :DV_B, (i % KV_PER_Q) * tq:(i % KV_PER_Q + 1) * tq] = vt_ref[i]
    for i in range(nq // KV_PER_Q):
        va_sc[i, DV_B:, :] = jnp.ones((va_sc.shape[1] - DV_B, tk), BF16)

    lam = (jnp.exp(jnp.sum(lam_ref[0:1, :] * lam_ref[1:2, :], axis=-1, keepdims=True))
           - jnp.exp(jnp.sum(lam_ref[2:3, :] * lam_ref[3:4, :], axis=-1, keepdims=True)) + lam_init)

    def reset(a):
        m_sc[a] = jnp.full(m_sc.shape[1:], -jnp.inf, F32)
        acc_sc[a] = jnp.zeros(acc_sc.shape[1:], F32)

    def finalize(qi):
        a = qi & 1
        acc = acc_sc[a]
        on = acc[:DV_B] / acc[DV_B:DV_B + 1]
        o = on[:, :tq] - lam * on[:, tq:]
        ms = jnp.mean(o * o, axis=0, keepdims=True)
        o = o * lax.rsqrt(ms + LN_EPS) * g_ref[...] * (1.0 - lam_init)
        o_ref[pl.ds(pl.multiple_of(qi * tq, tq), tq), :] = o.T.astype(BF16)
        reset(a)

    def is_last(qi, j):
        return j == qi // KV_PER_Q

    def bias_index(qi, j):
        odd = qi & 1
        near = (odd == 0) & (j == qi // KV_PER_Q - 1)
        return jnp.where(is_last(qi, j), 2 - odd, jnp.where(near, 0, 3))

    strips = [slice(c * STRIP, (c + 1) * STRIP) for c in range(2 * tq // STRIP)]

    def score_strip(kblk, qi, bi, cs):
        bcs = slice(cs.start % tq, cs.start % tq + STRIP)
        s = _dot(kblk, qd_sc[qi, :, cs]) + bias_ref[bi, :, bcs]
        return s, jnp.max(s, axis=0, keepdims=True)

    reset(0)
    reset(1)
    for cs in strips:
        s_bufs[0][:, cs], x_bufs[0][:, cs] = score_strip(k_ref[0:tk, :], 0, bias_index(0, 0), cs)

    def half(slot, qi, j):
        s_cur, x_cur, s_oth, x_oth = s_bufs[slot], x_bufs[slot], s_bufs[1 - slot], x_bufs[1 - slot]
        a = qi & 1
        last = is_last(qi, j)
        j_n = jnp.where(last, 0, j + 1)
        qi_n = jnp.minimum(jnp.where(last, qi + 1, qi), nq - 1)
        bi_n = bias_index(qi_n, j_n)
        kblk = k_ref[pl.ds(pl.multiple_of(j_n * tk, tk), tk), :]
        va = va_sc[j]
        for cs in strips:
            m_old = m_sc[a, :, cs]
            m_new = jnp.maximum(m_old, x_cur[:, cs])
            p = jnp.exp2((s_cur[:, cs] - m_new).astype(BF16))
            alpha = jnp.exp2(m_old - m_new)
            m_sc[a, :, cs] = m_new
            acc_sc[a, :, cs] = acc_sc[a, :, cs] * alpha + _dot(va, p)
            s_oth[:, cs], x_oth[:, cs] = score_strip(kblk, qi_n, bi_n, cs)
        return qi_n, j_n

    def body(_, carry):
        qi, j = carry
        qi1, j1 = half(0, qi, j)
        qi2, j2 = half(1, qi1, j1)

        @pl.when(is_last(qi, j))
        def _():
            finalize(qi)

        @pl.when(is_last(qi1, j1))
        def _():
            finalize(qi1)

        return qi2, j2

    npairs = sum(qi // KV_PER_Q + 1 for qi in range(nq))
    qi, j = lax.fori_loop(0, npairs // 2, body, (jnp.int32(0), jnp.int32(0)))
    if npairs % 2:
        half(0, qi, j)
        finalize(nq - 1)


def _diff(lam_p, kb, pt, bias, g_col, lam_init):
    B, S, _ = kb.shape
    nq, tq = pt.shape[1], pt.shape[3]
    va_rows = DV_B + 16
    tk = KV_PER_Q * tq
    assert KV_PER_Q == 2 and nq % KV_PER_Q == 0
    return pl.pallas_call(
        functools.partial(_diff_kernel, lam_init=lam_init),
        grid=(B, H_B),
        in_specs=[
            _resident(lam_p.shape),
            pl.BlockSpec((None, nq, DV_B, tq), lambda b, h: (b, 0, h, 0)),
            pl.BlockSpec((None, S, LANES), lambda b, h: (b, 0, h)),
            pl.BlockSpec((None, nq, DV_B, tq), lambda b, h: (b, 0, H_B + h, 0)),
            pl.BlockSpec((None, 4, tk, tq), lambda b, h: (h, 0, 0, 0)),
            _resident(g_col.shape),
        ],
        out_specs=pl.BlockSpec((None, S, DV_B), lambda b, h: (b, 0, h)),
        out_shape=jax.ShapeDtypeStruct((B, S, W_B_OUT), BF16),
        scratch_shapes=[
            pltpu.VMEM((nq, DV_B, 2 * tq), BF16),
            pltpu.VMEM((S // tk, va_rows, tk), BF16),
            pltpu.VMEM((tk, 2 * tq), F32),
            pltpu.VMEM((tk, 2 * tq), F32),
            pltpu.VMEM((1, 2 * tq), F32),
            pltpu.VMEM((1, 2 * tq), F32),
            pltpu.VMEM((2, 1, 2 * tq), F32),
            pltpu.VMEM((2, va_rows, 2 * tq), F32),
        ],
        compiler_params=_params("parallel", "parallel"),
        name="diff_attn",
    )(lam_p, pt, kb, pt, bias, g_col)


def _toeplitz_kernel(w_ref, o_ref, *, diff):
    width = o_ref.shape[-1]
    rows = width if diff else o_ref.shape[-2]
    t = pltpu.roll(jnp.broadcast_to(w_ref[...], (rows, width)), 0, 1, stride=1, stride_axis=0)
    if not diff:
        o_ref[...] = t
        return
    r = lax.broadcasted_iota(jnp.int32, (rows, width), 0)
    c = lax.broadcasted_iota(jnp.int32, (rows, width), 1)
    prev = jnp.where(c < r, t, 0.0)
    diag = jnp.where(c >= r, t, NEG_INF)
    zeros = jnp.zeros((rows, width), F32)
    o_ref[0, :rows] = zeros
    o_ref[0, rows:] = prev
    o_ref[1, :rows] = prev
    o_ref[1, rows:] = diag
    o_ref[2, :rows] = diag
    o_ref[2, rows:] = jnp.full((rows, width), NEG_INF, F32)
    o_ref[3, :rows] = zeros
    o_ref[3, rows:] = zeros


def _toeplitz(w, rows, diff):
    n, _, width = w.shape
    oshape = (n, 4, KV_PER_Q * rows, width) if diff else (n, rows, width)
    oblock = (None,) + oshape[1:]
    return pl.pallas_call(
        functools.partial(_toeplitz_kernel, diff=diff),
        grid=(n,),
        in_specs=[pl.BlockSpec((None, 1, width), lambda i: (i, 0, 0))],
        out_specs=pl.BlockSpec(oblock, lambda i: (i,) + (0,) * (len(oshape) - 1)),
        out_shape=jax.ShapeDtypeStruct(oshape, F32),
        compiler_params=_params("parallel"),
        name="bias_diff" if diff else "bias_dilated",
    )(w)


def _layer_norm(h, g, b):
    mu = jnp.mean(h, axis=-1, keepdims=True)
    d = h - mu
    var = jnp.mean(d * d, axis=-1, keepdims=True)
    return d * lax.rsqrt(var + LN_EPS) * g + b


def _mix_kernel(x_ref, o0_ref, o1_ref, o2_ref, l0_ref, l1_ref, l2_ref, ob_ref, wg_ref, bg_ref, wa_ref, wb_ref,
                wo_ref, g_ref, b_ref, out_ref, *tok_sc):
    x = x_ref[...]
    xb = x.astype(BF16)

    def token_major(ref, sc):
        dil, rows, _ = ref.shape
        if dil == 1:
            return ref[0]
        nk = sc.shape[0]
        for res in range(dil):
            for k in range(nk):
                sc[k, pl.ds(res, rows, stride=dil), :] = ref[res, :, k * LANES:(k + 1) * LANES]
        return jnp.concatenate([sc[k] for k in range(nk)], axis=1)

    l0 = token_major(l0_ref, None)
    l1, l2 = token_major(l1_ref, tok_sc[0]), token_major(l2_ref, tok_sc[1])
    o0 = token_major(o0_ref, None)
    o1, o2 = token_major(o1_ref, tok_sc[2]), token_major(o2_ref, tok_sc[3])
    mx = jnp.maximum(jnp.maximum(l0, l1), l2)
    e0, e1, e2 = jnp.exp(l0 - mx), jnp.exp(l1 - mx), jnp.exp(l2 - mx)
    o_a = (e0 * o0 + e1 * o1 + e2 * o2) / (e0 + e1 + e2)
    y_a = _dot(o_a.astype(BF16), wa_ref[...])
    y_b = _dot(ob_ref[...], wb_ref[...])
    gate_a = jax.nn.sigmoid(_dot(xb, wg_ref[:, :D_MODEL]) + bg_ref[:, :D_MODEL])
    gate_b = jax.nn.sigmoid(_dot(xb, wg_ref[:, D_MODEL:]) + bg_ref[:, D_MODEL:])
    merged = gate_a * y_a + gate_b * y_b
    mix = _dot(merged.astype(BF16), wo_ref[...])
    out_ref[...] = _layer_norm(DEEPNORM_ALPHA * x + mix, g_ref[...], b_ref[...])


def _mix(x, o_g, lse_g, o_b, w_gate, b_gate, w_a, w_b, w_o, ln_g, ln_b, tm):
    B, S, D = x.shape
    tok = lambda w: pl.BlockSpec((None, tm, w), lambda b, i: (b, i, 0))
    res = [pl.BlockSpec((None, a.shape[1], tm // a.shape[1], W_A_OUT), lambda b, i: (b, 0, i, 0)) for a in o_g]
    n_sc = 2 * sum(a.shape[1] > 1 for a in o_g)
    return pl.pallas_call(
        _mix_kernel,
        grid=(B, S // tm),
        in_specs=[tok(D)] + res + res + [tok(W_B_OUT)]
        + [_resident(a.shape) for a in (w_gate, b_gate, w_a, w_b, w_o, ln_g, ln_b)],
        out_specs=tok(D),
        out_shape=jax.ShapeDtypeStruct((B, S, D), F32),
        scratch_shapes=[pltpu.VMEM((W_A_OUT // LANES, tm, LANES), F32)] * n_sc,
        compiler_params=_params("parallel", "parallel"),
        name="mix",
    )(x, *o_g, *lse_g, o_b, w_gate, b_gate, w_a, w_b, w_o, ln_g, ln_b)


def _mlp_kernel(x_ref, w1_ref, w2_ref, g_ref, b_ref, out_ref):
    x = x_ref[...]
    xb = x.astype(BF16)
    ff = jnp.zeros(x.shape, F32)
    for c in range(D_FF // (2 * CHUNK)):
        sl = slice(c * 2 * CHUNK, (c + 1) * 2 * CHUNK)
        h = jnp.maximum(_dot(xb, w1_ref[:, sl]), 0.0)
        ff = ff + _dot((h * h).astype(BF16), w2_ref[sl, :])
    out_ref[...] = _layer_norm(DEEPNORM_ALPHA * x + ff, g_ref[...], b_ref[...])


def _mlp(x, w1, w2, ln_g, ln_b, tm):
    B, S, D = x.shape
    tok = pl.BlockSpec((None, tm, D), lambda b, i: (b, i, 0))
    return pl.pallas_call(
        _mlp_kernel,
        grid=(B, S // tm),
        in_specs=[tok] + [_resident(a.shape) for a in (w1, w2, ln_g, ln_b)],
        out_specs=tok,
        out_shape=jax.ShapeDtypeStruct((B, S, D), F32),
        compiler_params=_params("parallel", "parallel"),
        name="mlp",
    )(x, w1, w2, ln_g, ln_b)


def _t5_bucket(dist):
    n = jnp.maximum(dist, 0)
    max_exact = NUM_BUCKETS // 2
    nf = jnp.maximum(n, 1).astype(F32)
    large = max_exact + (jnp.log(nf / max_exact) / math.log(T5_MAX_DISTANCE / max_exact)
                         * (NUM_BUCKETS - max_exact)).astype(jnp.int32)
    large = jnp.minimum(large, NUM_BUCKETS - 1)
    return jnp.where(n < max_exact, n, large)


def _dilated_bias(rel_bias, g, dil):
    j = jnp.arange(2 * BLK)
    tab = rel_bias[:, g * H_A:(g + 1) * H_A].astype(F32)
    vec = tab[_t5_bucket(jnp.maximum(BLK - j, 0) * dil)]
    vec = jnp.where((j <= BLK)[:, None], vec, NEG_INF)
    bias = _toeplitz(vec.T.reshape(H_A, 1, 2 * BLK), BLK, diff=False)
    bias = bias.reshape(H_A // 2, 2, BLK, 2, BLK)
    return bias.transpose(0, 3, 1, 2, 4).reshape(H_A // 2, 2, 2 * BLK, BLK)


def _diff_bias(rel_bias, tq):
    assert tq >= T5_MAX_DISTANCE
    tab = rel_bias[:, N_GROUPS * H_A:].astype(F32)
    tab = (tab - tab[NUM_BUCKETS - 1:NUM_BUCKETS]) * LOG2E
    vec = tab[_t5_bucket(jnp.arange(tq))]
    return _toeplitz(vec.T.reshape(H_B, 1, tq), tq, diff=True)


def kernel(x, w_in, b_gate, lambda_q1, lambda_k1, lambda_q2, lambda_k2, subln_g, rel_bias, w_proj_a, w_proj_b,
           w_out, ln1_g, ln1_b, ln2_g, ln2_b, w_mlp1, w_mlp2):
    B, S, D = x.shape
    tq = min(TQ, S)
    tm = min(TM, S)
    w = w_in[0]
    w_a = w[:, :COLS_A].reshape(D, 3, N_GROUPS, W_A_OUT)
    w_a = jnp.concatenate([w_a[:, :1] * QK_SCALE, w_a[:, 1:]], axis=1).transpose(0, 2, 1, 3).reshape(D, COLS_A)
    w_qk = w[:, COLS_A:COLS_A + COLS_B_QK].reshape(D, 4, H_B, HEAD_DIM)
    w_qb = jnp.concatenate([w_qk[:, 0], w_qk[:, 1]], axis=-1).reshape(D, W_B_OUT)
    w_kb = jnp.concatenate([w_qk[:, 2], w_qk[:, 3]], axis=-1).reshape(D, W_B_OUT)
    w_vb = w[:, COLS_A + COLS_B_QK:COLS_A + COLS_B]
    w_n = jnp.concatenate([w_a, w_kb], axis=1).astype(BF16)
    w_t = jnp.concatenate([w_qb, w_vb], axis=1).T.astype(BF16)
    w_gate = w[:, COLS_A + COLS_B:].astype(BF16)

    *a_g, kb, pt = _proj(x, w_n, w_t, tm, tq)

    o_g, lse_g = [], []
    for g, (win, dil) in enumerate(DIL_PAIRS):
        assert win // dil == BLK
        o, lse = _dilated(a_g[g], _dilated_bias(rel_bias, g, dil), g)
        o_g.append(o)
        lse_g.append(lse)

    lam_init = 0.8 - 0.6 * math.exp(-0.3 * 0)
    lam_p = jnp.concatenate([lambda_q1, lambda_k1, lambda_q2, lambda_k2], axis=0).astype(F32)
    o_b = _diff(lam_p, kb, pt, _diff_bias(rel_bias, tq), subln_g[0].reshape(DV_B, 1), lam_init)

    x1 = _mix(x, o_g, lse_g, o_b, w_gate, b_gate, w_proj_a[0].astype(BF16), w_proj_b[0].astype(BF16),
              w_out[0].astype(BF16), ln1_g, ln1_b, tm)
    return _mlp(x1, w_mlp1[0].astype(BF16), w_mlp2[0].astype(BF16), ln2_g, ln2_b, tm)
```

```python
import functools
import math

import jax
import jax.numpy as jnp
import numpy as np
from jax import lax
from jax.experimental import pallas as pl
from jax.experimental.pallas import tpu as pltpu

D_MODEL = 1024
HEAD_DIM = 64
DIL_PAIRS = ((128, 1), (512, 4), (2048, 16))
N_GROUPS = len(DIL_PAIRS)
H_A = 8
H_B = 8
DV_B = 2 * HEAD_DIM
D_FF = 4 * D_MODEL
NUM_BUCKETS = 32
T5_MAX_DISTANCE = 128
BLK = 128
LN_EPS = 1e-5
NEG_INF = -1e30
W_A_OUT = H_A * HEAD_DIM
W_B_OUT = H_B * DV_B
COLS_A = 3 * N_GROUPS * H_A * HEAD_DIM
COLS_B_QK = 4 * H_B * HEAD_DIM
COLS_B = COLS_B_QK + H_B * DV_B
DEPTH = 1
DEEPNORM_ALPHA = (2.0 * DEPTH) ** 0.25
QK_SCALE = HEAD_DIM ** -0.5
LOG2E = math.log2(math.e)

LANES = 128
COLS_N = COLS_A + H_B * 2 * HEAD_DIM
COLS_T = 2 * W_B_OUT
CHUNK = 512
TQ = 512
STRIP = 256
KV_PER_Q = 2
TM = 512
VMEM_LIMIT = 56 * 1024 * 1024

BF16 = jnp.bfloat16
F32 = jnp.float32


def _dot(a, b):
    return jnp.dot(a, b, preferred_element_type=F32)


def _dot_nt(a, b):
    return lax.dot_general(a, b, (((1,), (1,)), ((), ())), preferred_element_type=F32)


def _resident(shape):
    nd = len(shape)
    return pl.BlockSpec(shape, lambda *_: (0,) * nd, pipeline_mode=pl.Buffered(1))


def _params(*sem, flags=None):
    return pltpu.CompilerParams(dimension_semantics=sem, vmem_limit_bytes=VMEM_LIMIT, flags=flags)


def _proj_kernel(x_ref, wn_ref, wt_ref, a0_ref, a1_ref, a2_ref, kb_ref, ot_ref, rows_sc, *, tq):
    xb = x_ref[...].astype(BF16)
    tm = xb.shape[0]
    a_refs = (a0_ref, a1_ref, a2_ref)
    for c in range(COLS_N // CHUNK):
        r = _dot(xb, wn_ref[:, c * CHUNK:(c + 1) * CHUNK])
        g, part = divmod(c, 3)
        if g >= N_GROUPS:
            kb_ref[:, (c - 3 * N_GROUPS) * CHUNK:(c - 3 * N_GROUPS + 1) * CHUNK] = r.astype(BF16)
            continue
        sl = slice(part * W_A_OUT, (part + 1) * W_A_OUT)
        dil = DIL_PAIRS[g][1]
        if dil == 1:
            a_refs[g][0, :, sl] = r.astype(BF16)
            continue
        for k in range(CHUNK // LANES):
            rows_sc[k] = r[:, k * LANES:(k + 1) * LANES]
        for res in range(dil):
            for k in range(CHUNK // LANES):
                piece = rows_sc[k, pl.ds(res, tm // dil, stride=dil), :]
                a_refs[g][res, :, part * W_A_OUT + k * LANES:part * W_A_OUT + (k + 1) * LANES] = piece.astype(BF16)
    for c in range(COLS_T // CHUNK):
        sl = slice(c * CHUNK, (c + 1) * CHUNK)
        r = _dot_nt(wt_ref[sl, :], xb)
        if (c + 1) * CHUNK <= W_B_OUT:
            r = r * (QK_SCALE * LOG2E)
        r = r.astype(BF16)
        for t in range(tm // tq):
            ot_ref[t, sl, :] = r[:, t * tq:(t + 1) * tq]


def _proj(x, w_n, w_t, tm, tq):
    B, S, D = x.shape
    assert CHUNK == W_A_OUT
    dils = [d for _, d in DIL_PAIRS]
    return pl.pallas_call(
        functools.partial(_proj_kernel, tq=tq),
        grid=(B, S // tm),
        in_specs=[
            pl.BlockSpec((None, tm, D), lambda b, i: (b, i, 0)),
            _resident((D, COLS_N)),
            _resident((COLS_T, D)),
        ],
        out_specs=[pl.BlockSpec((None, d, tm // d, 3 * W_A_OUT), lambda b, i: (b, 0, i, 0)) for d in dils] + [
            pl.BlockSpec((None, tm, W_B_OUT), lambda b, i: (b, i, 0)),
            pl.BlockSpec((None, tm // tq, COLS_T, tq), lambda b, i: (b, i, 0, 0)),
        ],
        out_shape=[jax.ShapeDtypeStruct((B, d, S // d, 3 * W_A_OUT), BF16) for d in dils] + [
            jax.ShapeDtypeStruct((B, S, W_B_OUT), BF16),
            jax.ShapeDtypeStruct((B, S // tq, COLS_T, tq), BF16),
        ],
        scratch_shapes=[pltpu.VMEM((CHUNK // LANES, tm, LANES), F32)],
        compiler_params=_params("parallel", "parallel"),
        name="proj",
    )(x, w_n, w_t)


def _dilated_kernel(q_ref, k_ref, v_ref, kp_ref, vp_ref, bias_ref, o_ref, lse_ref, kw_sc, vw_sc):
    tb = q_ref.shape[0]
    first = pl.program_id(2) == 0
    lane = lax.broadcasted_iota(jnp.int32, (BLK, LANES), 1)
    lo = lane < HEAD_DIM
    in_prev = lax.broadcasted_iota(jnp.int32, (2 * BLK, 2 * BLK), 1) < BLK
    kw_sc[:BLK], kw_sc[BLK:] = kp_ref[...], k_ref[...]
    vw_sc[:BLK], vw_sc[BLK:] = vp_ref[...], v_ref[...]
    for jb in range(tb // BLK):
        rows = slice(jb * BLK, (jb + 1) * BLK)
        win = slice(jb * BLK, (jb + 2) * BLK)
        for p in range(H_A // 2):
            cols = slice(p * LANES, (p + 1) * LANES)
            q2 = q_ref[rows, cols]
            zero = jnp.zeros_like(q2)
            qd = jnp.concatenate([jnp.where(lo, q2, zero), jnp.where(lo, zero, q2)], axis=0)
            s = _dot_nt(qd, kw_sc[win, cols]) + bias_ref[p]
            if jb == 0:
                s = jnp.where(first & in_prev, NEG_INF, s)
            m = jnp.max(s, axis=-1, keepdims=True)
            pw = jnp.exp(s - m)
            den = jnp.sum(pw, axis=-1, keepdims=True)
            o2 = _dot(pw.astype(BF16), vw_sc[win, cols]) / den
            lse = jnp.broadcast_to(m + jnp.log(den), (2 * BLK, LANES))
            o_ref[rows, cols] = jnp.where(lo, o2[:BLK], o2[BLK:])
            lse_ref[rows, cols] = jnp.where(lo, lse[:BLK], lse[BLK:])


def _dilated(a, bias, g):
    B, dil, L, _ = a.shape
    tb = min(CHUNK, L)
    cur = lambda c: pl.BlockSpec((None, None, tb, W_A_OUT), lambda b, r, n: (b, r, n, c))
    prev = lambda c: pl.BlockSpec(
        (None, None, BLK, W_A_OUT), lambda b, r, n: (b, r, jnp.maximum(n * (tb // BLK) - 1, 0), c))
    out = pl.BlockSpec((None, None, tb, W_A_OUT), lambda b, r, n: (b, r, n, 0))
    return pl.pallas_call(
        _dilated_kernel,
        grid=(B, dil, L // tb),
        in_specs=[cur(0), cur(1), cur(2), prev(1), prev(2), _resident(bias.shape)],
        out_specs=[out, out],
        out_shape=[jax.ShapeDtypeStruct((B, dil, L, W_A_OUT), F32)] * 2,
        scratch_shapes=[pltpu.VMEM((BLK + tb, W_A_OUT), BF16)] * 2,
        compiler_params=_params("parallel", "parallel", "arbitrary"),
        name=f"dilated{g}",
    )(a, a, a, a, a, bias)


def _diff_kernel(lam_ref, qt_ref, k_ref, vt_ref, bias_ref, g_ref, o_ref, qd_sc, va_sc, s0_sc, s1_sc, x0_sc, x1_sc,
                 m_sc, acc_sc, *, lam_init):
    nq, _, tq = qt_ref.shape
    tk = KV_PER_Q * tq
    s_bufs, x_bufs = (s0_sc, s1_sc), (x0_sc, x1_sc)
    row = lax.broadcasted_iota(jnp.int32, (DV_B, tq), 0)
    for i in range(nq):
        qt = qt_ref[i]
        zero = jnp.zeros_like(qt)
        qd_sc[i, :, :tq] = jnp.where(row < HEAD_DIM, qt, zero)
        qd_sc[i, :, tq:] = jnp.where(row < HEAD_DIM, zero, qt)
        va_sc[i // KV_PER_Q, :DV_B, (i % KV_PER_Q) * tq:(i % KV_PER_Q + 1) * tq] = vt_ref[i]
    for i in range(nq // KV_PER_Q):
        va_sc[i, DV_B:, :] = jnp.ones((va_sc.shape[1] - DV_B, tk), BF16)

    lam = (jnp.exp(jnp.sum(lam_ref[0:1, :] * lam_ref[1:2, :], axis=-1, keepdims=True))
           - jnp.exp(jnp.sum(lam_ref[2:3, :] * lam_ref[3:4, :], axis=-1, keepdims=True)) + lam_init)

    def reset(a):
        m_sc[a] = jnp.full(m_sc.shape[1:], -jnp.inf, F32)
        acc_sc[a] = jnp.zeros(acc_sc.shape[1:], F32)

    def finalize(qi):
        a = qi & 1
        acc = acc_sc[a]
        on = acc[:DV_B] / acc[DV_B:DV_B + 1]
        o = on[:, :tq] - lam * on[:, tq:]
        ms = jnp.mean(o * o, axis=0, keepdims=True)
        o = o * lax.rsqrt(ms + LN_EPS) * g_ref[...] * (1.0 - lam_init)
        o_ref[pl.ds(pl.multiple_of(qi * tq, tq), tq), :] = o.T.astype(BF16)
        reset(a)

    def is_last(qi, j):
        return j == qi // KV_PER_Q

    def bias_index(qi, j):
        odd = qi & 1
        near = (odd == 0) & (j == qi // KV_PER_Q - 1)
        return jnp.where(is_last(qi, j), 2 - odd, jnp.where(near, 0, 3))

    strips = [slice(c * STRIP, (c + 1) * STRIP) for c in range(2 * tq // STRIP)]

    def score_strip(kblk, qi, bi, cs):
        bcs = slice(cs.start % tq, cs.start % tq + STRIP)
        s = _dot(kblk, qd_sc[qi, :, cs]) + bias_ref[bi, :, bcs]
        return s, jnp.max(s, axis=0, keepdims=True)

    reset(0)
    reset(1)
    for cs in strips:
        s_bufs[0][:, cs], x_bufs[0][:, cs] = score_strip(k_ref[0:tk, :], 0, bias_index(0, 0), cs)

    def half(slot, qi, j):
        s_cur, x_cur, s_oth, x_oth = s_bufs[slot], x_bufs[slot], s_bufs[1 - slot], x_bufs[1 - slot]
        a = qi & 1
        last = is_last(qi, j)
        j_n = jnp.where(last, 0, j + 1)
        qi_n = jnp.minimum(jnp.where(last, qi + 1, qi), nq - 1)
        bi_n = bias_index(qi_n, j_n)
        kblk = k_ref[pl.ds(pl.multiple_of(j_n * tk, tk), tk), :]
        va = va_sc[j]
        for cs in strips:
            m_old = m_sc[a, :, cs]
            m_new = jnp.maximum(m_old, x_cur[:, cs])
            p = jnp.exp2((s_cur[:, cs] - m_new).astype(BF16))
            alpha = jnp.exp2(m_old - m_new)
            m_sc[a, :, cs] = m_new
            acc_sc[a, :, cs] = acc_sc[a, :, cs] * alpha + _dot(va, p)
            s_oth[:, cs], x_oth[:, cs] = score_strip(kblk, qi_n, bi_n, cs)
        return qi_n, j_n

    def body(_, carry):
        qi, j = carry
        qi1, j1 = half(0, qi, j)
        qi2, j2 = half(1, qi1, j1)

        @pl.when(is_last(qi, j))
        def _():
            finalize(qi)

        @pl.when(is_last(qi1, j1))
        def _():
            finalize(qi1)

        return qi2, j2

    npairs = sum(qi // KV_PER_Q + 1 for qi in range(nq))
    qi, j = lax.fori_loop(0, npairs // 2, body, (jnp.int32(0), jnp.int32(0)))
    if npairs % 2:
        half(0, qi, j)
        finalize(nq - 1)


def _diff(lam_p, kb, pt, bias, g_col, lam_init):
    B, S, _ = kb.shape
    nq, tq = pt.shape[1], pt.shape[3]
    va_rows = DV_B + 16
    tk = KV_PER_Q * tq
    assert KV_PER_Q == 2 and nq % KV_PER_Q == 0
    return pl.pallas_call(
        functools.partial(_diff_kernel, lam_init=lam_init),
        grid=(B, H_B),
        in_specs=[
            _resident(lam_p.shape),
            pl.BlockSpec((None, nq, DV_B, tq), lambda b, h: (b, 0, h, 0)),
            pl.BlockSpec((None, S, LANES), lambda b, h: (b, 0, h)),
            pl.BlockSpec((None, nq, DV_B, tq), lambda b, h: (b, 0, H_B + h, 0)),
            pl.BlockSpec((None, 4, tk, tq), lambda b, h: (h, 0, 0, 0)),
            _resident(g_col.shape),
        ],
        out_specs=pl.BlockSpec((None, S, DV_B), lambda b, h: (b, 0, h)),
        out_shape=jax.ShapeDtypeStruct((B, S, W_B_OUT), BF16),
        scratch_shapes=[
            pltpu.VMEM((nq, DV_B, 2 * tq), BF16),
            pltpu.VMEM((S // tk, va_rows, tk), BF16),
            pltpu.VMEM((tk, 2 * tq), F32),
            pltpu.VMEM((tk, 2 * tq), F32),
            pltpu.VMEM((1, 2 * tq), F32),
            pltpu.VMEM((1, 2 * tq), F32),
            pltpu.VMEM((2, 1, 2 * tq), F32),
            pltpu.VMEM((2, va_rows, 2 * tq), F32),
        ],
        compiler_params=_params("parallel", "parallel"),
        name="diff_attn",
    )(lam_p, pt, kb, pt, bias, g_col)


def _toeplitz_kernel(w_ref, o_ref, *, diff):
    width = o_ref.shape[-1]
    rows = width if diff else o_ref.shape[-2]
    t = pltpu.roll(jnp.broadcast_to(w_ref[...], (rows, width)), 0, 1, stride=1, stride_axis=0)
    if not diff:
        o_ref[...] = t
        return
    r = lax.broadcasted_iota(jnp.int32, (rows, width), 0)
    c = lax.broadcasted_iota(jnp.int32, (rows, width), 1)
    prev = jnp.where(c < r, t, 0.0)
    diag = jnp.where(c >= r, t, NEG_INF)
    zeros = jnp.zeros((rows, width), F32)
    o_ref[0, :rows] = zeros
    o_ref[0, rows:] = prev
    o_ref[1, :rows] = prev
    o_ref[1, rows:] = diag
    o_ref[2, :rows] = diag
    o_ref[2, rows:] = jnp.full((rows, width), NEG_INF, F32)
    o_ref[3, :rows] = zeros
    o_ref[3, rows:] = zeros


def _toeplitz(w, rows, diff):
    n, _, width = w.shape
    oshape = (n, 4, KV_PER_Q * rows, width) if diff else (n, rows, width)
    oblock = (None,) + oshape[1:]
    return pl.pallas_call(
        functools.partial(_toeplitz_kernel, diff=diff),
        grid=(n,),
        in_specs=[pl.BlockSpec((None, 1, width), lambda i: (i, 0, 0))],
        out_specs=pl.BlockSpec(oblock, lambda i: (i,) + (0,) * (len(oshape) - 1)),
        out_shape=jax.ShapeDtypeStruct(oshape, F32),
        compiler_params=_params("parallel"),
        name="bias_diff" if diff else "bias_dilated",
    )(w)


def _layer_norm(h, g, b):
    mu = jnp.mean(h, axis=-1, keepdims=True)
    d = h - mu
    var = jnp.mean(d * d, axis=-1, keepdims=True)
    return d * lax.rsqrt(var + LN_EPS) * g + b


def _mix_kernel(x_ref, o0_ref, o1_ref, o2_ref, l0_ref, l1_ref, l2_ref, ob_ref, wg_ref, bg_ref, wa_ref, wb_ref,
                wo_ref, g_ref, b_ref, out_ref, *tok_sc):
    x = x_ref[...]
    xb = x.astype(BF16)

    def token_major(ref, sc):
        dil, rows, _ = ref.shape
        if dil == 1:
            return ref[0]
        nk = sc.shape[0]
        for res in range(dil):
            for k in range(nk):
                sc[k, pl.ds(res, rows, stride=dil), :] = ref[res, :, k * LANES:(k + 1) * LANES]
        return jnp.concatenate([sc[k] for k in range(nk)], axis=1)

    l0 = token_major(l0_ref, None)
    l1, l2 = token_major(l1_ref, tok_sc[0]), token_major(l2_ref, tok_sc[1])
    o0 = token_major(o0_ref, None)
    o1, o2 = token_major(o1_ref, tok_sc[2]), token_major(o2_ref, tok_sc[3])
    mx = jnp.maximum(jnp.maximum(l0, l1), l2)
    e0, e1, e2 = jnp.exp(l0 - mx), jnp.exp(l1 - mx), jnp.exp(l2 - mx)
    o_a = (e0 * o0 + e1 * o1 + e2 * o2) / (e0 + e1 + e2)
    y_a = _dot(o_a.astype(BF16), wa_ref[...])
    y_b = _dot(ob_ref[...], wb_ref[...])
    gate_a = jax.nn.sigmoid(_dot(xb, wg_ref[:, :D_MODEL]) + bg_ref[:, :D_MODEL])
    gate_b = jax.nn.sigmoid(_dot(xb, wg_ref[:, D_MODEL:]) + bg_ref[:, D_MODEL:])
    merged = gate_a * y_a + gate_b * y_b
    mix = _dot(merged.astype(BF16), wo_ref[...])
    out_ref[...] = _layer_norm(DEEPNORM_ALPHA * x + mix, g_ref[...], b_ref[...])


def _mix(x, o_g, lse_g, o_b, w_gate, b_gate, w_a, w_b, w_o, ln_g, ln_b, tm):
    B, S, D = x.shape
    tok = lambda w: pl.BlockSpec((None, tm, w), lambda b, i: (b, i, 0))
    res = [pl.BlockSpec((None, a.shape[1], tm // a.shape[1], W_A_OUT), lambda b, i: (b, 0, i, 0)) for a in o_g]
    n_sc = 2 * sum(a.shape[1] > 1 for a in o_g)
    return pl.pallas_call(
        _mix_kernel,
        grid=(B, S // tm),
        in_specs=[tok(D)] + res + res + [tok(W_B_OUT)]
        + [_resident(a.shape) for a in (w_gate, b_gate, w_a, w_b, w_o, ln_g, ln_b)],
        out_specs=tok(D),
        out_shape=jax.ShapeDtypeStruct((B, S, D), F32),
        scratch_shapes=[pltpu.VMEM((W_A_OUT // LANES, tm, LANES), F32)] * n_sc,
        compiler_params=_params("parallel", "parallel"),
        name="mix",
    )(x, *o_g, *lse_g, o_b, w_gate, b_gate, w_a, w_b, w_o, ln_g, ln_b)


def _mlp_kernel(x_ref, w1_ref, w2_ref, g_ref, b_ref, out_ref):
    x = x_ref[...]
    xb = x.astype(BF16)
    ff = jnp.zeros(x.shape, F32)
    for c in range(D_FF // (2 * CHUNK)):
        sl = slice(c * 2 * CHUNK, (c + 1) * 2 * CHUNK)
        h = jnp.maximum(_dot(xb, w1_ref[:, sl]), 0.0)
        ff = ff + _dot((h * h).astype(BF16), w2_ref[sl, :])
    out_ref[...] = _layer_norm(DEEPNORM_ALPHA * x + ff, g_ref[...], b_ref[...])


def _mlp(x, w1, w2, ln_g, ln_b, tm):
    B, S, D = x.shape
    tok = pl.BlockSpec((None, tm, D), lambda b, i: (b, i, 0))
    return pl.pallas_call(
        _mlp_kernel,
        grid=(B, S // tm),
        in_specs=[tok] + [_resident(a.shape) for a in (w1, w2, ln_g, ln_b)],
        out_specs=tok,
        out_shape=jax.ShapeDtypeStruct((B, S, D), F32),
        compiler_params=_params("parallel", "parallel"),
        name="mlp",
    )(x, w1, w2, ln_g, ln_b)


def _t5_bucket(dist):
    n = jnp.maximum(dist, 0)
    max_exact = NUM_BUCKETS // 2
    nf = jnp.maximum(n, 1).astype(F32)
    large = max_exact + (jnp.log(nf / max_exact) / math.log(T5_MAX_DISTANCE / max_exact)
                         * (NUM_BUCKETS - max_exact)).astype(jnp.int32)
    large = jnp.minimum(large, NUM_BUCKETS - 1)
    return jnp.where(n < max_exact, n, large)


def _dilated_bias(rel_bias, g, dil):
    j = jnp.arange(2 * BLK)
    tab = rel_bias[:, g * H_A:(g + 1) * H_A].astype(F32)
    vec = tab[_t5_bucket(jnp.maximum(BLK - j, 0) * dil)]
    vec = jnp.where((j <= BLK)[:, None], vec, NEG_INF)
    bias = _toeplitz(vec.T.reshape(H_A, 1, 2 * BLK), BLK, diff=False)
    return bias.reshape(H_A // 2, 2 * BLK, 2 * BLK)


def _diff_bias(rel_bias, tq):
    assert tq >= T5_MAX_DISTANCE
    tab = rel_bias[:, N_GROUPS * H_A:].astype(F32)
    tab = (tab - tab[NUM_BUCKETS - 1:NUM_BUCKETS]) * LOG2E
    vec = tab[_t5_bucket(jnp.arange(tq))]
    return _toeplitz(vec.T.reshape(H_B, 1, tq), tq, diff=True)


def kernel(x, w_in, b_gate, lambda_q1, lambda_k1, lambda_q2, lambda_k2, subln_g, rel_bias, w_proj_a, w_proj_b,
           w_out, ln1_g, ln1_b, ln2_g, ln2_b, w_mlp1, w_mlp2):
    B, S, D = x.shape
    tq = min(TQ, S)
    tm = min(TM, S)
    w = w_in[0]
    w_a = w[:, :COLS_A].reshape(D, 3, N_GROUPS, W_A_OUT)
    w_a = jnp.concatenate([w_a[:, :1] * QK_SCALE, w_a[:, 1:]], axis=1).transpose(0, 2, 1, 3).reshape(D, COLS_A)
    w_qk = w[:, COLS_A:COLS_A + COLS_B_QK].reshape(D, 4, H_B, HEAD_DIM)
    w_qb = jnp.concatenate([w_qk[:, 0], w_qk[:, 1]], axis=-1).reshape(D, W_B_OUT)
    w_kb = jnp.concatenate([w_qk[:, 2], w_qk[:, 3]], axis=-1).reshape(D, W_B_OUT)
    w_vb = w[:, COLS_A + COLS_B_QK:COLS_A + COLS_B]
    w_n = jnp.concatenate([w_a, w_kb], axis=1).astype(BF16)
    w_t = jnp.concatenate([w_qb, w_vb], axis=1).T.astype(BF16)
    w_gate = w[:, COLS_A + COLS_B:].astype(BF16)

    *a_g, kb, pt = _proj(x, w_n, w_t, tm, tq)

    o_g, lse_g = [], []
    for g, (win, dil) in enumerate(DIL_PAIRS):
        assert win // dil == BLK
        o, lse = _dilated(a_g[g], _dilated_bias(rel_bias, g, dil), g)
        o_g.append(o)
        lse_g.append(lse)

    lam_init = 0.8 - 0.6 * math.exp(-0.3 * 0)
    lam_p = jnp.concatenate([lambda_q1, lambda_k1, lambda_q2, lambda_k2], axis=0).astype(F32)
    o_b = _diff(lam_p, kb, pt, _diff_bias(rel_bias, tq), subln_g[0].reshape(DV_B, 1), lam_init)

    x1 = _mix(x, o_g, lse_g, o_b, w_gate, b_gate, w_proj_a[0].astype(BF16), w_proj_b[0].astype(BF16),
              w_out[0].astype(BF16), ln1_g, ln1_b, tm)
    return _mlp(x1, w_mlp1[0].astype(BF16), w_mlp2[0].astype(BF16), ln2_g, ln2_b, tm)
```

```python
import functools
import math

import jax
import jax.numpy as jnp
import numpy as np
from jax import lax
from jax.experimental import pallas as pl
from jax.experimental.pallas import tpu as pltpu

D_MODEL = 1024
HEAD_DIM = 64
DIL_PAIRS = ((128, 1), (512, 4), (2048, 16))
N_GROUPS = len(DIL_PAIRS)
H_A = 8
H_B = 8
DV_B = 2 * HEAD_DIM
D_FF = 4 * D_MODEL
NUM_BUCKETS = 32
T5_MAX_DISTANCE = 128
BLK = 128
LN_EPS = 1e-5
NEG_INF = -1e30
W_A_OUT = H_A * HEAD_DIM
W_B_OUT = H_B * DV_B
COLS_A = 3 * N_GROUPS * H_A * HEAD_DIM
COLS_B_QK = 4 * H_B * HEAD_DIM
COLS_B = COLS_B_QK + H_B * DV_B
DEPTH = 1
DEEPNORM_ALPHA = (2.0 * DEPTH) ** 0.25
QK_SCALE = HEAD_DIM ** -0.5
LOG2E = math.log2(math.e)

LANES = 128
COLS_N = COLS_A + H_B * 2 * HEAD_DIM
COLS_T = 2 * W_B_OUT
CHUNK = 512
TQ = 512
STRIP = 512
KV_PER_Q = 2
TM = 512
VMEM_LIMIT = 56 * 1024 * 1024

BF16 = jnp.bfloat16
F32 = jnp.float32


def _dot(a, b):
    return jnp.dot(a, b, preferred_element_type=F32)


def _dot_nt(a, b):
    return lax.dot_general(a, b, (((1,), (1,)), ((), ())), preferred_element_type=F32)


def _resident(shape):
    nd = len(shape)
    return pl.BlockSpec(shape, lambda *_: (0,) * nd, pipeline_mode=pl.Buffered(1))


def _params(*sem, flags=None):
    return pltpu.CompilerParams(dimension_semantics=sem, vmem_limit_bytes=VMEM_LIMIT, flags=flags)


def _proj_kernel(x_ref, wn_ref, wt_ref, a0_ref, a1_ref, a2_ref, kb_ref, ot_ref, rows_sc, *, tq):
    xb = x_ref[...].astype(BF16)
    tm = xb.shape[0]
    a_refs = (a0_ref, a1_ref, a2_ref)
    for c in range(COLS_N // CHUNK):
        r = _dot(xb, wn_ref[:, c * CHUNK:(c + 1) * CHUNK])
        g, part = divmod(c, 3)
        if g >= N_GROUPS:
            kb_ref[:, (c - 3 * N_GROUPS) * CHUNK:(c - 3 * N_GROUPS + 1) * CHUNK] = r.astype(BF16)
            continue
        sl = slice(part * W_A_OUT, (part + 1) * W_A_OUT)
        dil = DIL_PAIRS[g][1]
        if dil == 1:
            a_refs[g][0, :, sl] = r.astype(BF16)
            continue
        for k in range(CHUNK // LANES):
            rows_sc[k] = r[:, k * LANES:(k + 1) * LANES]
        for res in range(dil):
            for k in range(CHUNK // LANES):
                piece = rows_sc[k, pl.ds(res, tm // dil, stride=dil), :]
                a_refs[g][res, :, part * W_A_OUT + k * LANES:part * W_A_OUT + (k + 1) * LANES] = piece.astype(BF16)
    for c in range(COLS_T // CHUNK):
        sl = slice(c * CHUNK, (c + 1) * CHUNK)
        r = _dot_nt(wt_ref[sl, :], xb)
        if (c + 1) * CHUNK <= W_B_OUT:
            r = r * (QK_SCALE * LOG2E)
        r = r.astype(BF16)
        for t in range(tm // tq):
            ot_ref[t, sl, :] = r[:, t * tq:(t + 1) * tq]


def _proj(x, w_n, w_t, tm, tq):
    B, S, D = x.shape
    assert CHUNK == W_A_OUT
    dils = [d for _, d in DIL_PAIRS]
    return pl.pallas_call(
        functools.partial(_proj_kernel, tq=tq),
        grid=(B, S // tm),
        in_specs=[
            pl.BlockSpec((None, tm, D), lambda b, i: (b, i, 0)),
            _resident((D, COLS_N)),
            _resident((COLS_T, D)),
        ],
        out_specs=[pl.BlockSpec((None, d, tm // d, 3 * W_A_OUT), lambda b, i: (b, 0, i, 0)) for d in dils] + [
            pl.BlockSpec((None, tm, W_B_OUT), lambda b, i: (b, i, 0)),
            pl.BlockSpec((None, tm // tq, COLS_T, tq), lambda b, i: (b, i, 0, 0)),
        ],
        out_shape=[jax.ShapeDtypeStruct((B, d, S // d, 3 * W_A_OUT), BF16) for d in dils] + [
            jax.ShapeDtypeStruct((B, S, W_B_OUT), BF16),
            jax.ShapeDtypeStruct((B, S // tq, COLS_T, tq), BF16),
        ],
        scratch_shapes=[pltpu.VMEM((CHUNK // LANES, tm, LANES), F32)],
        compiler_params=_params("parallel", "parallel"),
        name="proj",
    )(x, w_n, w_t)


def _dilated_kernel(q_ref, k_ref, v_ref, kp_ref, vp_ref, bias_ref, o_ref, lse_ref, kw_sc, vw_sc):
    tb = q_ref.shape[0]
    first = pl.program_id(2) == 0
    lane = lax.broadcasted_iota(jnp.int32, (BLK, LANES), 1)
    lo = lane < HEAD_DIM
    in_prev = lax.broadcasted_iota(jnp.int32, (2 * BLK, 2 * BLK), 1) < BLK
    kw_sc[:BLK], kw_sc[BLK:] = kp_ref[...], k_ref[...]
    vw_sc[:BLK], vw_sc[BLK:] = vp_ref[...], v_ref[...]
    for jb in range(tb // BLK):
        rows = slice(jb * BLK, (jb + 1) * BLK)
        win = slice(jb * BLK, (jb + 2) * BLK)
        for p in range(H_A // 2):
            cols = slice(p * LANES, (p + 1) * LANES)
            q2 = q_ref[rows, cols]
            zero = jnp.zeros_like(q2)
            qd = jnp.concatenate([jnp.where(lo, q2, zero), jnp.where(lo, zero, q2)], axis=0)
            s = _dot_nt(qd, kw_sc[win, cols]) + bias_ref[p]
            if jb == 0:
                s = jnp.where(first & in_prev, NEG_INF, s)
            m = jnp.max(s, axis=-1, keepdims=True)
            pw = jnp.exp(s - m)
            den = jnp.sum(pw, axis=-1, keepdims=True)
            o2 = _dot(pw.astype(BF16), vw_sc[win, cols]) / den
            lse = jnp.broadcast_to(m + jnp.log(den), (2 * BLK, LANES))
            o_ref[rows, cols] = jnp.where(lo, o2[:BLK], o2[BLK:])
            lse_ref[rows, cols] = jnp.where(lo, lse[:BLK], lse[BLK:])


def _dilated(a, bias, g):
    B, dil, L, _ = a.shape
    tb = min(CHUNK, L)
    cur = lambda c: pl.BlockSpec((None, None, tb, W_A_OUT), lambda b, r, n: (b, r, n, c))
    prev = lambda c: pl.BlockSpec(
        (None, None, BLK, W_A_OUT), lambda b, r, n: (b, r, jnp.maximum(n * (tb // BLK) - 1, 0), c))
    out = pl.BlockSpec((None, None, tb, W_A_OUT), lambda b, r, n: (b, r, n, 0))
    return pl.pallas_call(
        _dilated_kernel,
        grid=(B, dil, L // tb),
        in_specs=[cur(0), cur(1), cur(2), prev(1), prev(2), _resident(bias.shape)],
        out_specs=[out, out],
        out_shape=[jax.ShapeDtypeStruct((B, dil, L, W_A_OUT), F32)] * 2,
        scratch_shapes=[pltpu.VMEM((BLK + tb, W_A_OUT), BF16)] * 2,
        compiler_params=_params("parallel", "parallel", "arbitrary"),
        name=f"dilated{g}",
    )(a, a, a, a, a, bias)


def _diff_kernel(lam_ref, qt_ref, k_ref, vt_ref, bias_ref, g_ref, o_ref, qd_sc, va_sc, s0_sc, s1_sc, x0_sc, x1_sc,
                 m_sc, acc_sc, *, lam_init):
    nq, _, tq = qt_ref.shape
    tk = KV_PER_Q * tq
    s_bufs, x_bufs = (s0_sc, s1_sc), (x0_sc, x1_sc)
    row = lax.broadcasted_iota(jnp.int32, (DV_B, tq), 0)
    for i in range(nq):
        qt = qt_ref[i]
        zero = jnp.zeros_like(qt)
        qd_sc[i, :, :tq] = jnp.where(row < HEAD_DIM, qt, zero)
        qd_sc[i, :, tq:] = jnp.where(row < HEAD_DIM, zero, qt)
        va_sc[i // KV_PER_Q, :DV_B, (i % KV_PER_Q) * tq:(i % KV_PER_Q + 1) * tq] = vt_ref[i]
    for i in range(nq // KV_PER_Q):
        va_sc[i, DV_B:, :] = jnp.ones((va_sc.shape[1] - DV_B, tk), BF16)

    lam = (jnp.exp(jnp.sum(lam_ref[0:1, :] * lam_ref[1:2, :], axis=-1, keepdims=True))
           - jnp.exp(jnp.sum(lam_ref[2:3, :] * lam_ref[3:4, :], axis=-1, keepdims=True)) + lam_init)

    def reset(a):
        m_sc[a] = jnp.full(m_sc.shape[1:], -jnp.inf, F32)
        acc_sc[a] = jnp.zeros(acc_sc.shape[1:], F32)

    def finalize(qi):
        a = qi & 1
        acc = acc_sc[a]
        on = acc[:DV_B] / acc[DV_B:DV_B + 1]
        o = on[:, :tq] - lam * on[:, tq:]
        ms = jnp.mean(o * o, axis=0, keepdims=True)
        o = o * lax.rsqrt(ms + LN_EPS) * g_ref[...] * (1.0 - lam_init)
        o_ref[pl.ds(pl.multiple_of(qi * tq, tq), tq), :] = o.T.astype(BF16)
        reset(a)

    def is_last(qi, j):
        return j == qi // KV_PER_Q

    def bias_index(qi, j):
        odd = qi & 1
        near = (odd == 0) & (j == qi // KV_PER_Q - 1)
        return jnp.where(is_last(qi, j), 2 - odd, jnp.where(near, 0, 3))

    strips = [slice(c * STRIP, (c + 1) * STRIP) for c in range(2 * tq // STRIP)]

    def score_strip(kblk, qi, bi, cs):
        bcs = slice(cs.start % tq, cs.start % tq + STRIP)
        s = _dot(kblk, qd_sc[qi, :, cs]) + bias_ref[bi, :, bcs]
        return s, jnp.max(s, axis=0, keepdims=True)

    reset(0)
    reset(1)
    for cs in strips:
        s_bufs[0][:, cs], x_bufs[0][:, cs] = score_strip(k_ref[0:tk, :], 0, bias_index(0, 0), cs)

    def half(slot, qi, j):
        s_cur, x_cur, s_oth, x_oth = s_bufs[slot], x_bufs[slot], s_bufs[1 - slot], x_bufs[1 - slot]
        a = qi & 1
        last = is_last(qi, j)
        j_n = jnp.where(last, 0, j + 1)
        qi_n = jnp.minimum(jnp.where(last, qi + 1, qi), nq - 1)
        bi_n = bias_index(qi_n, j_n)
        kblk = k_ref[pl.ds(pl.multiple_of(j_n * tk, tk), tk), :]
        va = va_sc[j]
        for cs in strips:
            m_old = m_sc[a, :, cs]
            m_new = jnp.maximum(m_old, x_cur[:, cs])
            p = jnp.exp2((s_cur[:, cs] - m_new).astype(BF16))
            alpha = jnp.exp2(m_old - m_new)
            m_sc[a, :, cs] = m_new
            acc_sc[a, :, cs] = acc_sc[a, :, cs] * alpha + _dot(va, p)
            s_oth[:, cs], x_oth[:, cs] = score_strip(kblk, qi_n, bi_n, cs)
        return qi_n, j_n

    def body(_, carry):
        qi, j = carry
        qi1, j1 = half(0, qi, j)
        qi2, j2 = half(1, qi1, j1)

        @pl.when(is_last(qi, j))
        def _():
            finalize(qi)

        @pl.when(is_last(qi1, j1))
        def _():
            finalize(qi1)

        return qi2, j2

    npairs = sum(qi // KV_PER_Q + 1 for qi in range(nq))
    qi, j = lax.fori_loop(0, npairs // 2, body, (jnp.int32(0), jnp.int32(0)))
    if npairs % 2:
        half(0, qi, j)
        finalize(nq - 1)


def _diff(lam_p, kb, pt, bias, g_col, lam_init):
    B, S, _ = kb.shape
    nq, tq = pt.shape[1], pt.shape[3]
    va_rows = DV_B + 16
    tk = KV_PER_Q * tq
    assert KV_PER_Q == 2 and nq % KV_PER_Q == 0
    return pl.pallas_call(
        functools.partial(_diff_kernel, lam_init=lam_init),
        grid=(B, H_B),
        in_specs=[
            _resident(lam_p.shape),
            pl.BlockSpec((None, nq, DV_B, tq), lambda b, h: (b, 0, h, 0)),
            pl.BlockSpec((None, S, LANES), lambda b, h: (b, 0, h)),
            pl.BlockSpec((None, nq, DV_B, tq), lambda b, h: (b, 0, H_B + h, 0)),
            pl.BlockSpec((None, 4, tk, tq), lambda b, h: (h, 0, 0, 0)),
            _resident(g_col.shape),
        ],
        out_specs=pl.BlockSpec((None, S, DV_B), lambda b, h: (b, 0, h)),
        out_shape=jax.ShapeDtypeStruct((B, S, W_B_OUT), BF16),
        scratch_shapes=[
            pltpu.VMEM((nq, DV_B, 2 * tq), BF16),
            pltpu.VMEM((S // tk, va_rows, tk), BF16),
            pltpu.VMEM((tk, 2 * tq), F32),
            pltpu.VMEM((tk, 2 * tq), F32),
            pltpu.VMEM((1, 2 * tq), F32),
            pltpu.VMEM((1, 2 * tq), F32),
            pltpu.VMEM((2, 1, 2 * tq), F32),
            pltpu.VMEM((2, va_rows, 2 * tq), F32),
        ],
        compiler_params=_params("parallel", "parallel"),
        name="diff_attn",
    )(lam_p, pt, kb, pt, bias, g_col)


def _toeplitz_kernel(w_ref, o_ref, *, diff):
    width = o_ref.shape[-1]
    rows = width if diff else o_ref.shape[-2]
    t = pltpu.roll(jnp.broadcast_to(w_ref[...], (rows, width)), 0, 1, stride=1, stride_axis=0)
    if not diff:
        o_ref[...] = t
        return
    r = lax.broadcasted_iota(jnp.int32, (rows, width), 0)
    c = lax.broadcasted_iota(jnp.int32, (rows, width), 1)
    prev = jnp.where(c < r, t, 0.0)
    diag = jnp.where(c >= r, t, NEG_INF)
    zeros = jnp.zeros((rows, width), F32)
    o_ref[0, :rows] = zeros
    o_ref[0, rows:] = prev
    o_ref[1, :rows] = prev
    o_ref[1, rows:] = diag
    o_ref[2, :rows] = diag
    o_ref[2, rows:] = jnp.full((rows, width), NEG_INF, F32)
    o_ref[3, :rows] = zeros
    o_ref[3, rows:] = zeros


def _toeplitz(w, rows, diff):
    n, _, width = w.shape
    oshape = (n, 4, KV_PER_Q * rows, width) if diff else (n, rows, width)
    oblock = (None,) + oshape[1:]
    return pl.pallas_call(
        functools.partial(_toeplitz_kernel, diff=diff),
        grid=(n,),
        in_specs=[pl.BlockSpec((None, 1, width), lambda i: (i, 0, 0))],
        out_specs=pl.BlockSpec(oblock, lambda i: (i,) + (0,) * (len(oshape) - 1)),
        out_shape=jax.ShapeDtypeStruct(oshape, F32),
        compiler_params=_params("parallel"),
        name="bias_diff" if diff else "bias_dilated",
    )(w)


def _layer_norm(h, g, b):
    mu = jnp.mean(h, axis=-1, keepdims=True)
    d = h - mu
    var = jnp.mean(d * d, axis=-1, keepdims=True)
    return d * lax.rsqrt(var + LN_EPS) * g + b


def _mix_kernel(x_ref, o0_ref, o1_ref, o2_ref, l0_ref, l1_ref, l2_ref, ob_ref, wg_ref, bg_ref, wa_ref, wb_ref,
                wo_ref, g_ref, b_ref, out_ref, *tok_sc):
    x = x_ref[...]
    xb = x.astype(BF16)

    def token_major(ref, sc):
        dil, rows, _ = ref.shape
        if dil == 1:
            return ref[0]
        nk = sc.shape[0]
        for res in range(dil):
            for k in range(nk):
                sc[k, pl.ds(res, rows, stride=dil), :] = ref[res, :, k * LANES:(k + 1) * LANES]
        return jnp.concatenate([sc[k] for k in range(nk)], axis=1)

    l0 = token_major(l0_ref, None)
    l1, l2 = token_major(l1_ref, tok_sc[0]), token_major(l2_ref, tok_sc[1])
    o0 = token_major(o0_ref, None)
    o1, o2 = token_major(o1_ref, tok_sc[2]), token_major(o2_ref, tok_sc[3])
    mx = jnp.maximum(jnp.maximum(l0, l1), l2)
    e0, e1, e2 = jnp.exp(l0 - mx), jnp.exp(l1 - mx), jnp.exp(l2 - mx)
    o_a = (e0 * o0 + e1 * o1 + e2 * o2) / (e0 + e1 + e2)
    y_a = _dot(o_a.astype(BF16), wa_ref[...])
    y_b = _dot(ob_ref[...], wb_ref[...])
    gate_a = jax.nn.sigmoid(_dot(xb, wg_ref[:, :D_MODEL]) + bg_ref[:, :D_MODEL])
    gate_b = jax.nn.sigmoid(_dot(xb, wg_ref[:, D_MODEL:]) + bg_ref[:, D_MODEL:])
    merged = gate_a * y_a + gate_b * y_b
    mix = _dot(merged.astype(BF16), wo_ref[...])
    out_ref[...] = _layer_norm(DEEPNORM_ALPHA * x + mix, g_ref[...], b_ref[...])


def _mix(x, o_g, lse_g, o_b, w_gate, b_gate, w_a, w_b, w_o, ln_g, ln_b, tm):
    B, S, D = x.shape
    tok = lambda w: pl.BlockSpec((None, tm, w), lambda b, i: (b, i, 0))
    res = [pl.BlockSpec((None, a.shape[1], tm // a.shape[1], W_A_OUT), lambda b, i: (b, 0, i, 0)) for a in o_g]
    n_sc = 2 * sum(a.shape[1] > 1 for a in o_g)
    return pl.pallas_call(
        _mix_kernel,
        grid=(B, S // tm),
        in_specs=[tok(D)] + res + res + [tok(W_B_OUT)]
        + [_resident(a.shape) for a in (w_gate, b_gate, w_a, w_b, w_o, ln_g, ln_b)],
        out_specs=tok(D),
        out_shape=jax.ShapeDtypeStruct((B, S, D), F32),
        scratch_shapes=[pltpu.VMEM((W_A_OUT // LANES, tm, LANES), F32)] * n_sc,
        compiler_params=_params("parallel", "parallel"),
        name="mix",
    )(x, *o_g, *lse_g, o_b, w_gate, b_gate, w_a, w_b, w_o, ln_g, ln_b)


def _mlp_kernel(x_ref, w1_ref, w2_ref, g_ref, b_ref, out_ref):
    x = x_ref[...]
    xb = x.astype(BF16)
    ff = jnp.zeros(x.shape, F32)
    for c in range(D_FF // (2 * CHUNK)):
        sl = slice(c * 2 * CHUNK, (c + 1) * 2 * CHUNK)
        h = jnp.maximum(_dot(xb, w1_ref[:, sl]), 0.0)
        ff = ff + _dot((h * h).astype(BF16), w2_ref[sl, :])
    out_ref[...] = _layer_norm(DEEPNORM_ALPHA * x + ff, g_ref[...], b_ref[...])


def _mlp(x, w1, w2, ln_g, ln_b, tm):
    B, S, D = x.shape
    tok = pl.BlockSpec((None, tm, D), lambda b, i: (b, i, 0))
    return pl.pallas_call(
        _mlp_kernel,
        grid=(B, S // tm),
        in_specs=[tok] + [_resident(a.shape) for a in (w1, w2, ln_g, ln_b)],
        out_specs=tok,
        out_shape=jax.ShapeDtypeStruct((B, S, D), F32),
        compiler_params=_params("parallel", "parallel"),
        name="mlp",
    )(x, w1, w2, ln_g, ln_b)


def _t5_bucket(dist):
    n = jnp.maximum(dist, 0)
    max_exact = NUM_BUCKETS // 2
    nf = jnp.maximum(n, 1).astype(F32)
    large = max_exact + (jnp.log(nf / max_exact) / math.log(T5_MAX_DISTANCE / max_exact)
                         * (NUM_BUCKETS - max_exact)).astype(jnp.int32)
    large = jnp.minimum(large, NUM_BUCKETS - 1)
    return jnp.where(n < max_exact, n, large)


def _dilated_bias(rel_bias, g, dil):
    j = jnp.arange(2 * BLK)
    tab = rel_bias[:, g * H_A:(g + 1) * H_A].astype(F32)
    vec = tab[_t5_bucket(jnp.maximum(BLK - j, 0) * dil)]
    vec = jnp.where((j <= BLK)[:, None], vec, NEG_INF)
    bias = _toeplitz(vec.T.reshape(H_A, 1, 2 * BLK), BLK, diff=False)
    return bias.reshape(H_A // 2, 2 * BLK, 2 * BLK)


def _diff_bias(rel_bias, tq):
    assert tq >= T5_MAX_DISTANCE
    tab = rel_bias[:, N_GROUPS * H_A:].astype(F32)
    tab = (tab - tab[NUM_BUCKETS - 1:NUM_BUCKETS]) * LOG2E
    vec = tab[_t5_bucket(jnp.arange(tq))]
    return _toeplitz(vec.T.reshape(H_B, 1, tq), tq, diff=True)


def kernel(x, w_in, b_gate, lambda_q1, lambda_k1, lambda_q2, lambda_k2, subln_g, rel_bias, w_proj_a, w_proj_b,
           w_out, ln1_g, ln1_b, ln2_g, ln2_b, w_mlp1, w_mlp2):
    B, S, D = x.shape
    tq = min(TQ, S)
    tm = min(TM, S)
    w = w_in[0]
    w_a = w[:, :COLS_A].reshape(D, 3, N_GROUPS, W_A_OUT)
    w_a = jnp.concatenate([w_a[:, :1] * QK_SCALE, w_a[:, 1:]], axis=1).transpose(0, 2, 1, 3).reshape(D, COLS_A)
    w_qk = w[:, COLS_A:COLS_A + COLS_B_QK].reshape(D, 4, H_B, HEAD_DIM)
    w_qb = jnp.concatenate([w_qk[:, 0], w_qk[:, 1]], axis=-1).reshape(D, W_B_OUT)
    w_kb = jnp.concatenate([w_qk[:, 2], w_qk[:, 3]], axis=-1).reshape(D, W_B_OUT)
    w_vb = w[:, COLS_A + COLS_B_QK:COLS_A + COLS_B]
    w_n = jnp.concatenate([w_a, w_kb], axis=1).astype(BF16)
    w_t = jnp.concatenate([w_qb, w_vb], axis=1).T.astype(BF16)
    w_gate = w[:, COLS_A + COLS_B:].astype(BF16)

    *a_g, kb, pt = _proj(x, w_n, w_t, tm, tq)

    o_g, lse_g = [], []
    for g, (win, dil) in enumerate(DIL_PAIRS):
        assert win // dil == BLK
        o, lse = _dilated(a_g[g], _dilated_bias(rel_bias, g, dil), g)
        o_g.append(o)
        lse_g.append(lse)

    lam_init = 0.8 - 0.6 * math.exp(-0.3 * 0)
    lam_p = jnp.concatenate([lambda_q1, lambda_k1, lambda_q2, lambda_k2], axis=0).astype(F32)
    o_b = _diff(lam_p, kb, pt, _diff_bias(rel_bias, tq), subln_g[0].reshape(DV_B, 1), lam_init)

    x1 = _mix(x, o_g, lse_g, o_b, w_gate, b_gate, w_proj_a[0].astype(BF16), w_proj_b[0].astype(BF16),
              w_out[0].astype(BF16), ln1_g, ln1_b, tm)
    return _mlp(x1, w_mlp1[0].astype(BF16), w_mlp2[0].astype(BF16), ln2_g, ln2_b, tm)
```

```python
import functools
import math

import jax
import jax.numpy as jnp
import numpy as np
from jax import lax
from jax.experimental import pallas as pl
from jax.experimental.pallas import tpu as pltpu

D_MODEL = 1024
HEAD_DIM = 64
DIL_PAIRS = ((128, 1), (512, 4), (2048, 16))
N_GROUPS = len(DIL_PAIRS)
H_A = 8
H_B = 8
DV_B = 2 * HEAD_DIM
D_FF = 4 * D_MODEL
NUM_BUCKETS = 32
T5_MAX_DISTANCE = 128
BLK = 128
LN_EPS = 1e-5
NEG_INF = -1e30
W_A_OUT = H_A * HEAD_DIM
W_B_OUT = H_B * DV_B
COLS_A = 3 * N_GROUPS * H_A * HEAD_DIM
COLS_B_QK = 4 * H_B * HEAD_DIM
COLS_B = COLS_B_QK + H_B * DV_B
DEPTH = 1
DEEPNORM_ALPHA = (2.0 * DEPTH) ** 0.25
QK_SCALE = HEAD_DIM ** -0.5
LOG2E = math.log2(math.e)

LANES = 128
COLS_N = COLS_A + H_B * 2 * HEAD_DIM
COLS_T = 2 * W_B_OUT
CHUNK = 512
TQ = 512
DIFF_T = 1024
STRIP = 512
TM = 512
VMEM_LIMIT = 56 * 1024 * 1024

BF16 = jnp.bfloat16
F32 = jnp.float32


def _dot(a, b):
    return jnp.dot(a, b, preferred_element_type=F32)


def _dot_nt(a, b):
    return lax.dot_general(a, b, (((1,), (1,)), ((), ())), preferred_element_type=F32)


def _resident(shape):
    nd = len(shape)
    return pl.BlockSpec(shape, lambda *_: (0,) * nd, pipeline_mode=pl.Buffered(1))


def _params(*sem, flags=None):
    return pltpu.CompilerParams(dimension_semantics=sem, vmem_limit_bytes=VMEM_LIMIT, flags=flags)


def _proj_kernel(x_ref, wn_ref, wt_ref, a0_ref, a1_ref, a2_ref, kb_ref, ot_ref, rows_sc, *, tq):
    xb = x_ref[...].astype(BF16)
    tm = xb.shape[0]
    a_refs = (a0_ref, a1_ref, a2_ref)
    for c in range(COLS_N // CHUNK):
        r = _dot(xb, wn_ref[:, c * CHUNK:(c + 1) * CHUNK])
        g, part = divmod(c, 3)
        if g >= N_GROUPS:
            kb_ref[:, (c - 3 * N_GROUPS) * CHUNK:(c - 3 * N_GROUPS + 1) * CHUNK] = r.astype(BF16)
            continue
        sl = slice(part * W_A_OUT, (part + 1) * W_A_OUT)
        dil = DIL_PAIRS[g][1]
        if dil == 1:
            a_refs[g][0, :, sl] = r.astype(BF16)
            continue
        for k in range(CHUNK // LANES):
            rows_sc[k] = r[:, k * LANES:(k + 1) * LANES]
        for res in range(dil):
            for k in range(CHUNK // LANES):
                piece = rows_sc[k, pl.ds(res, tm // dil, stride=dil), :]
                a_refs[g][res, :, part * W_A_OUT + k * LANES:part * W_A_OUT + (k + 1) * LANES] = piece.astype(BF16)
    for c in range(COLS_T // CHUNK):
        sl = slice(c * CHUNK, (c + 1) * CHUNK)
        r = _dot_nt(wt_ref[sl, :], xb)
        if (c + 1) * CHUNK <= W_B_OUT:
            r = r * (QK_SCALE * LOG2E)
        r = r.astype(BF16)
        for t in range(tm // tq):
            ot_ref[t, sl, :] = r[:, t * tq:(t + 1) * tq]


def _proj(x, w_n, w_t, tm, tq):
    B, S, D = x.shape
    assert CHUNK == W_A_OUT
    dils = [d for _, d in DIL_PAIRS]
    return pl.pallas_call(
        functools.partial(_proj_kernel, tq=tq),
        grid=(B, S // tm),
        in_specs=[
            pl.BlockSpec((None, tm, D), lambda b, i: (b, i, 0)),
            _resident((D, COLS_N)),
            _resident((COLS_T, D)),
        ],
        out_specs=[pl.BlockSpec((None, d, tm // d, 3 * W_A_OUT), lambda b, i: (b, 0, i, 0)) for d in dils] + [
            pl.BlockSpec((None, tm, W_B_OUT), lambda b, i: (b, i, 0)),
            pl.BlockSpec((None, tm // tq, COLS_T, tq), lambda b, i: (b, i, 0, 0)),
        ],
        out_shape=[jax.ShapeDtypeStruct((B, d, S // d, 3 * W_A_OUT), BF16) for d in dils] + [
            jax.ShapeDtypeStruct((B, S, W_B_OUT), BF16),
            jax.ShapeDtypeStruct((B, S // tq, COLS_T, tq), BF16),
        ],
        scratch_shapes=[pltpu.VMEM((CHUNK // LANES, tm, LANES), F32)],
        compiler_params=_params("parallel", "parallel"),
        name="proj",
    )(x, w_n, w_t)


def _dilated_kernel(q_ref, k_ref, v_ref, kp_ref, vp_ref, bias_ref, o_ref, lse_ref, kw_sc, vw_sc):
    tb = q_ref.shape[0]
    first = pl.program_id(2) == 0
    lane = lax.broadcasted_iota(jnp.int32, (BLK, LANES), 1)
    lo = lane < HEAD_DIM
    in_prev = lax.broadcasted_iota(jnp.int32, (2 * BLK, 2 * BLK), 1) < BLK
    kw_sc[:BLK], kw_sc[BLK:] = kp_ref[...], k_ref[...]
    vw_sc[:BLK], vw_sc[BLK:] = vp_ref[...], v_ref[...]
    for jb in range(tb // BLK):
        rows = slice(jb * BLK, (jb + 1) * BLK)
        win = slice(jb * BLK, (jb + 2) * BLK)
        for p in range(H_A // 2):
            cols = slice(p * LANES, (p + 1) * LANES)
            q2 = q_ref[rows, cols]
            zero = jnp.zeros_like(q2)
            qd = jnp.concatenate([jnp.where(lo, q2, zero), jnp.where(lo, zero, q2)], axis=0)
            s = _dot_nt(qd, kw_sc[win, cols]) + bias_ref[p]
            if jb == 0:
                s = jnp.where(first & in_prev, NEG_INF, s)
            m = jnp.max(s, axis=-1, keepdims=True)
            pw = jnp.exp(s - m)
            den = jnp.sum(pw, axis=-1, keepdims=True)
            o2 = _dot(pw.astype(BF16), vw_sc[win, cols]) / den
            lse = jnp.broadcast_to(m + jnp.log(den), (2 * BLK, LANES))
            o_ref[rows, cols] = jnp.where(lo, o2[:BLK], o2[BLK:])
            lse_ref[rows, cols] = jnp.where(lo, lse[:BLK], lse[BLK:])


def _dilated(a, bias, g):
    B, dil, L, _ = a.shape
    tb = min(CHUNK, L)
    cur = lambda c: pl.BlockSpec((None, None, tb, W_A_OUT), lambda b, r, n: (b, r, n, c))
    prev = lambda c: pl.BlockSpec(
        (None, None, BLK, W_A_OUT), lambda b, r, n: (b, r, jnp.maximum(n * (tb // BLK) - 1, 0), c))
    out = pl.BlockSpec((None, None, tb, W_A_OUT), lambda b, r, n: (b, r, n, 0))
    return pl.pallas_call(
        _dilated_kernel,
        grid=(B, dil, L // tb),
        in_specs=[cur(0), cur(1), cur(2), prev(1), prev(2), _resident(bias.shape)],
        out_specs=[out, out],
        out_shape=[jax.ShapeDtypeStruct((B, dil, L, W_A_OUT), F32)] * 2,
        scratch_shapes=[pltpu.VMEM((BLK + tb, W_A_OUT), BF16)] * 2,
        compiler_params=_params("parallel", "parallel", "arbitrary"),
        name=f"dilated{g}",
    )(a, a, a, a, a, bias)


def _diff_kernel(lam_ref, qt_ref, k_ref, vt_ref, bias_ref, g_ref, o_ref, qd_sc, va_sc, s0_sc, s1_sc, x0_sc, x1_sc,
                 m_sc, acc_sc, *, lam_init):
    nb, _, tb = qt_ref.shape
    tq = tk = bias_ref.shape[-1]
    qb = tq // tb
    nq = nb // qb
    s_bufs, x_bufs = (s0_sc, s1_sc), (x0_sc, x1_sc)
    row = lax.broadcasted_iota(jnp.int32, (DV_B, tb), 0)
    for i in range(nb):
        blk, part = divmod(i, qb)
        qt = qt_ref[i]
        zero = jnp.zeros_like(qt)
        qd_sc[blk, :, part * tb:(part + 1) * tb] = jnp.where(row < HEAD_DIM, qt, zero)
        qd_sc[blk, :, tq + part * tb:tq + (part + 1) * tb] = jnp.where(row < HEAD_DIM, zero, qt)
        va_sc[blk, :DV_B, part * tb:(part + 1) * tb] = vt_ref[i]
    for i in range(nq):
        va_sc[i, DV_B:, :] = jnp.ones((va_sc.shape[1] - DV_B, tk), BF16)

    lam = (jnp.exp(jnp.sum(lam_ref[0:1, :] * lam_ref[1:2, :], axis=-1, keepdims=True))
           - jnp.exp(jnp.sum(lam_ref[2:3, :] * lam_ref[3:4, :], axis=-1, keepdims=True)) + lam_init)

    def reset(a):
        m_sc[a] = jnp.full(m_sc.shape[1:], -jnp.inf, F32)
        acc_sc[a] = jnp.zeros(acc_sc.shape[1:], F32)

    def finalize(qi):
        a = qi & 1
        acc = acc_sc[a]
        on = acc[:DV_B] / acc[DV_B:DV_B + 1]
        o = on[:, :tq] - lam * on[:, tq:]
        ms = jnp.mean(o * o, axis=0, keepdims=True)
        o = o * lax.rsqrt(ms + LN_EPS) * g_ref[...] * (1.0 - lam_init)
        o_ref[pl.ds(pl.multiple_of(qi * tq, tq), tq), :] = o.T.astype(BF16)
        reset(a)

    def is_last(qi, j):
        return j == qi

    def bias_index(qi, j):
        return jnp.where(j == qi, 0, jnp.where(j == qi - 1, 1, 2))

    strips = [slice(c * STRIP, (c + 1) * STRIP) for c in range(2 * tq // STRIP)]

    def score_strip(kblk, qi, bi, cs):
        bcs = slice(cs.start % tq, cs.start % tq + STRIP)
        s = _dot(kblk, qd_sc[qi, :, cs]) + bias_ref[bi, :, bcs].astype(F32)
        return s, jnp.max(s, axis=0, keepdims=True)

    reset(0)
    reset(1)
    for cs in strips:
        s_bufs[0][:, cs], x_bufs[0][:, cs] = score_strip(k_ref[0:tk, :], 0, bias_index(0, 0), cs)

    def half(slot, qi, j):
        s_cur, x_cur, s_oth, x_oth = s_bufs[slot], x_bufs[slot], s_bufs[1 - slot], x_bufs[1 - slot]
        a = qi & 1
        last = is_last(qi, j)
        j_n = jnp.where(last, 0, j + 1)
        qi_n = jnp.minimum(jnp.where(last, qi + 1, qi), nq - 1)
        bi_n = bias_index(qi_n, j_n)
        kblk = k_ref[pl.ds(pl.multiple_of(j_n * tk, tk), tk), :]
        va = va_sc[j]
        for cs in strips:
            m_old = m_sc[a, :, cs]
            m_new = jnp.maximum(m_old, x_cur[:, cs])
            p = jnp.exp2((s_cur[:, cs] - m_new).astype(BF16))
            alpha = jnp.exp2(m_old - m_new)
            m_sc[a, :, cs] = m_new
            acc_sc[a, :, cs] = acc_sc[a, :, cs] * alpha + _dot(va, p)
            s_oth[:, cs], x_oth[:, cs] = score_strip(kblk, qi_n, bi_n, cs)
        return qi_n, j_n

    def body(_, carry):
        qi, j = carry
        qi1, j1 = half(0, qi, j)
        qi2, j2 = half(1, qi1, j1)

        @pl.when(is_last(qi, j))
        def _():
            finalize(qi)

        @pl.when(is_last(qi1, j1))
        def _():
            finalize(qi1)

        return qi2, j2

    npairs = nq * (nq + 1) // 2
    qi, j = lax.fori_loop(0, npairs // 2, body, (jnp.int32(0), jnp.int32(0)))
    if npairs % 2:
        half(0, qi, j)
        finalize(nq - 1)


def _diff(lam_p, kb, pt, bias, g_col, lam_init):
    B, S, _ = kb.shape
    nb, tb = pt.shape[1], pt.shape[3]
    tq = tk = bias.shape[-1]
    nq = S // tq
    assert tq % tb == 0 and tq % STRIP == 0 and bias.shape[1:] == (3, tk, tq)
    va_rows = DV_B + 16
    return pl.pallas_call(
        functools.partial(_diff_kernel, lam_init=lam_init),
        grid=(B, H_B),
        in_specs=[
            _resident(lam_p.shape),
            pl.BlockSpec((None, nb, DV_B, tb), lambda b, h: (b, 0, h, 0)),
            pl.BlockSpec((None, S, LANES), lambda b, h: (b, 0, h)),
            pl.BlockSpec((None, nb, DV_B, tb), lambda b, h: (b, 0, H_B + h, 0)),
            pl.BlockSpec((None, 3, tk, tq), lambda b, h: (h, 0, 0, 0)),
            _resident(g_col.shape),
        ],
        out_specs=pl.BlockSpec((None, S, DV_B), lambda b, h: (b, 0, h)),
        out_shape=jax.ShapeDtypeStruct((B, S, W_B_OUT), BF16),
        scratch_shapes=[
            pltpu.VMEM((nq, DV_B, 2 * tq), BF16),
            pltpu.VMEM((S // tk, va_rows, tk), BF16),
            pltpu.VMEM((tk, 2 * tq), F32),
            pltpu.VMEM((tk, 2 * tq), F32),
            pltpu.VMEM((1, 2 * tq), F32),
            pltpu.VMEM((1, 2 * tq), F32),
            pltpu.VMEM((2, 1, 2 * tq), F32),
            pltpu.VMEM((2, va_rows, 2 * tq), F32),
        ],
        compiler_params=_params("parallel", "parallel"),
        name="diff_attn",
    )(lam_p, pt, kb, pt, bias, g_col)


def _toeplitz_kernel(w_ref, o_ref, *, diff):
    width = o_ref.shape[-1]
    rows = width if diff else o_ref.shape[-2]
    t = pltpu.roll(jnp.broadcast_to(w_ref[...], (rows, width)), 0, 1, stride=1, stride_axis=0)
    if not diff:
        o_ref[...] = t
        return
    r = lax.broadcasted_iota(jnp.int32, (rows, width), 0)
    c = lax.broadcasted_iota(jnp.int32, (rows, width), 1)
    o_ref[0] = jnp.where(c >= r, t, NEG_INF).astype(o_ref.dtype)
    o_ref[1] = jnp.where(c < r, t, 0.0).astype(o_ref.dtype)
    o_ref[2] = jnp.zeros((rows, width), o_ref.dtype)


def _toeplitz(w, rows, diff):
    n, _, width = w.shape
    oshape = (n, 3, rows, width) if diff else (n, rows, width)
    oblock = (None,) + oshape[1:]
    return pl.pallas_call(
        functools.partial(_toeplitz_kernel, diff=diff),
        grid=(n,),
        in_specs=[pl.BlockSpec((None, 1, width), lambda i: (i, 0, 0))],
        out_specs=pl.BlockSpec(oblock, lambda i: (i,) + (0,) * (len(oshape) - 1)),
        out_shape=jax.ShapeDtypeStruct(oshape, BF16 if diff else F32),
        compiler_params=_params("parallel"),
        name="bias_diff" if diff else "bias_dilated",
    )(w)


def _layer_norm(h, g, b):
    mu = jnp.mean(h, axis=-1, keepdims=True)
    d = h - mu
    var = jnp.mean(d * d, axis=-1, keepdims=True)
    return d * lax.rsqrt(var + LN_EPS) * g + b


def _mix_kernel(x_ref, o0_ref, o1_ref, o2_ref, l0_ref, l1_ref, l2_ref, ob_ref, wg_ref, bg_ref, wa_ref, wb_ref,
                wo_ref, g_ref, b_ref, out_ref, *tok_sc):
    x = x_ref[...]
    xb = x.astype(BF16)

    def token_major(ref, sc):
        dil, rows, _ = ref.shape
        if dil == 1:
            return ref[0]
        nk = sc.shape[0]
        for res in range(dil):
            for k in range(nk):
                sc[k, pl.ds(res, rows, stride=dil), :] = ref[res, :, k * LANES:(k + 1) * LANES]
        return jnp.concatenate([sc[k] for k in range(nk)], axis=1)

    l0 = token_major(l0_ref, None)
    l1, l2 = token_major(l1_ref, tok_sc[0]), token_major(l2_ref, tok_sc[1])
    o0 = token_major(o0_ref, None)
    o1, o2 = token_major(o1_ref, tok_sc[2]), token_major(o2_ref, tok_sc[3])
    mx = jnp.maximum(jnp.maximum(l0, l1), l2)
    e0, e1, e2 = jnp.exp(l0 - mx), jnp.exp(l1 - mx), jnp.exp(l2 - mx)
    o_a = (e0 * o0 + e1 * o1 + e2 * o2) / (e0 + e1 + e2)
    y_a = _dot(o_a.astype(BF16), wa_ref[...])
    y_b = _dot(ob_ref[...], wb_ref[...])
    gate_a = jax.nn.sigmoid(_dot(xb, wg_ref[:, :D_MODEL]) + bg_ref[:, :D_MODEL])
    gate_b = jax.nn.sigmoid(_dot(xb, wg_ref[:, D_MODEL:]) + bg_ref[:, D_MODEL:])
    merged = gate_a * y_a + gate_b * y_b
    mix = _dot(merged.astype(BF16), wo_ref[...])
    out_ref[...] = _layer_norm(DEEPNORM_ALPHA * x + mix, g_ref[...], b_ref[...])


def _mix(x, o_g, lse_g, o_b, w_gate, b_gate, w_a, w_b, w_o, ln_g, ln_b, tm):
    B, S, D = x.shape
    tok = lambda w: pl.BlockSpec((None, tm, w), lambda b, i: (b, i, 0))
    res = [pl.BlockSpec((None, a.shape[1], tm // a.shape[1], W_A_OUT), lambda b, i: (b, 0, i, 0)) for a in o_g]
    n_sc = 2 * sum(a.shape[1] > 1 for a in o_g)
    return pl.pallas_call(
        _mix_kernel,
        grid=(B, S // tm),
        in_specs=[tok(D)] + res + res + [tok(W_B_OUT)]
        + [_resident(a.shape) for a in (w_gate, b_gate, w_a, w_b, w_o, ln_g, ln_b)],
        out_specs=tok(D),
        out_shape=jax.ShapeDtypeStruct((B, S, D), F32),
        scratch_shapes=[pltpu.VMEM((W_A_OUT // LANES, tm, LANES), F32)] * n_sc,
        compiler_params=_params("parallel", "parallel"),
        name="mix",
    )(x, *o_g, *lse_g, o_b, w_gate, b_gate, w_a, w_b, w_o, ln_g, ln_b)


def _mlp_kernel(x_ref, w1_ref, w2_ref, g_ref, b_ref, out_ref):
    x = x_ref[...]
    xb = x.astype(BF16)
    ff = jnp.zeros(x.shape, F32)
    for c in range(D_FF // (2 * CHUNK)):
        sl = slice(c * 2 * CHUNK, (c + 1) * 2 * CHUNK)
        h = jnp.maximum(_dot(xb, w1_ref[:, sl]), 0.0)
        ff = ff + _dot((h * h).astype(BF16), w2_ref[sl, :])
    out_ref[...] = _layer_norm(DEEPNORM_ALPHA * x + ff, g_ref[...], b_ref[...])


def _mlp(x, w1, w2, ln_g, ln_b, tm):
    B, S, D = x.shape
    tok = pl.BlockSpec((None, tm, D), lambda b, i: (b, i, 0))
    return pl.pallas_call(
        _mlp_kernel,
        grid=(B, S // tm),
        in_specs=[tok] + [_resident(a.shape) for a in (w1, w2, ln_g, ln_b)],
        out_specs=tok,
        out_shape=jax.ShapeDtypeStruct((B, S, D), F32),
        compiler_params=_params("parallel", "parallel"),
        name="mlp",
    )(x, w1, w2, ln_g, ln_b)


def _t5_bucket(dist):
    n = jnp.maximum(dist, 0)
    max_exact = NUM_BUCKETS // 2
    nf = jnp.maximum(n, 1).astype(F32)
    large = max_exact + (jnp.log(nf / max_exact) / math.log(T5_MAX_DISTANCE / max_exact)
                         * (NUM_BUCKETS - max_exact)).astype(jnp.int32)
    large = jnp.minimum(large, NUM_BUCKETS - 1)
    return jnp.where(n < max_exact, n, large)


def _dilated_bias(rel_bias, g, dil):
    j = jnp.arange(2 * BLK)
    tab = rel_bias[:, g * H_A:(g + 1) * H_A].astype(F32)
    vec = tab[_t5_bucket(jnp.maximum(BLK - j, 0) * dil)]
    vec = jnp.where((j <= BLK)[:, None], vec, NEG_INF)
    bias = _toeplitz(vec.T.reshape(H_A, 1, 2 * BLK), BLK, diff=False)
    return bias.reshape(H_A // 2, 2 * BLK, 2 * BLK)


def _diff_bias(rel_bias, tq):
    assert tq >= T5_MAX_DISTANCE
    tab = rel_bias[:, N_GROUPS * H_A:].astype(F32)
    tab = (tab - tab[NUM_BUCKETS - 1:NUM_BUCKETS]) * LOG2E
    vec = tab[_t5_bucket(jnp.arange(tq))]
    return _toeplitz(vec.T.reshape(H_B, 1, tq), tq, diff=True)


def kernel(x, w_in, b_gate, lambda_q1, lambda_k1, lambda_q2, lambda_k2, subln_g, rel_bias, w_proj_a, w_proj_b,
           w_out, ln1_g, ln1_b, ln2_g, ln2_b, w_mlp1, w_mlp2):
    B, S, D = x.shape
    tq = min(TQ, S)
    tm = min(TM, S)
    w = w_in[0]
    w_a = w[:, :COLS_A].reshape(D, 3, N_GROUPS, W_A_OUT)
    w_a = jnp.concatenate([w_a[:, :1] * QK_SCALE, w_a[:, 1:]], axis=1).transpose(0, 2, 1, 3).reshape(D, COLS_A)
    w_qk = w[:, COLS_A:COLS_A + COLS_B_QK].reshape(D, 4, H_B, HEAD_DIM)
    w_qb = jnp.concatenate([w_qk[:, 0], w_qk[:, 1]], axis=-1).reshape(D, W_B_OUT)
    w_kb = jnp.concatenate([w_qk[:, 2], w_qk[:, 3]], axis=-1).reshape(D, W_B_OUT)
    w_vb = w[:, COLS_A + COLS_B_QK:COLS_A + COLS_B]
    w_n = jnp.concatenate([w_a, w_kb], axis=1).astype(BF16)
    w_t = jnp.concatenate([w_qb, w_vb], axis=1).T.astype(BF16)
    w_gate = w[:, COLS_A + COLS_B:].astype(BF16)

    *a_g, kb, pt = _proj(x, w_n, w_t, tm, tq)

    o_g, lse_g = [], []
    for g, (win, dil) in enumerate(DIL_PAIRS):
        assert win // dil == BLK
        o, lse = _dilated(a_g[g], _dilated_bias(rel_bias, g, dil), g)
        o_g.append(o)
        lse_g.append(lse)

    lam_init = 0.8 - 0.6 * math.exp(-0.3 * 0)
    lam_p = jnp.concatenate([lambda_q1, lambda_k1, lambda_q2, lambda_k2], axis=0).astype(F32)
    o_b = _diff(lam_p, kb, pt, _diff_bias(rel_bias, min(DIFF_T, S)), subln_g[0].reshape(DV_B, 1), lam_init)

    x1 = _mix(x, o_g, lse_g, o_b, w_gate, b_gate, w_proj_a[0].astype(BF16), w_proj_b[0].astype(BF16),
              w_out[0].astype(BF16), ln1_g, ln1_b, tm)
    return _mlp(x1, w_mlp1[0].astype(BF16), w_mlp2[0].astype(BF16), ln2_g, ln2_b, tm)
```

```python
import functools
import math

import jax
import jax.numpy as jnp
import numpy as np
from jax import lax
from jax.experimental import pallas as pl
from jax.experimental.pallas import tpu as pltpu

D_MODEL = 1024
HEAD_DIM = 64
DIL_PAIRS = ((128, 1), (512, 4), (2048, 16))
N_GROUPS = len(DIL_PAIRS)
H_A = 8
H_B = 8
DV_B = 2 * HEAD_DIM
D_FF = 4 * D_MODEL
NUM_BUCKETS = 32
T5_MAX_DISTANCE = 128
BLK = 128
LN_EPS = 1e-5
NEG_INF = -1e30
W_A_OUT = H_A * HEAD_DIM
W_B_OUT = H_B * DV_B
COLS_A = 3 * N_GROUPS * H_A * HEAD_DIM
COLS_B_QK = 4 * H_B * HEAD_DIM
COLS_B = COLS_B_QK + H_B * DV_B
DEPTH = 1
DEEPNORM_ALPHA = (2.0 * DEPTH) ** 0.25
QK_SCALE = HEAD_DIM ** -0.5
LOG2E = math.log2(math.e)

LANES = 128
COLS_N = COLS_A + H_B * 2 * HEAD_DIM
COLS_T = 2 * W_B_OUT
CHUNK = 512
TQ = 512
DIFF_T = 1024
STRIP = 512
TM = 512
VMEM_LIMIT = 56 * 1024 * 1024

BF16 = jnp.bfloat16
F32 = jnp.float32


def _dot(a, b):
    return jnp.dot(a, b, preferred_element_type=F32)


def _dot_nt(a, b):
    return lax.dot_general(a, b, (((1,), (1,)), ((), ())), preferred_element_type=F32)


def _resident(shape):
    nd = len(shape)
    return pl.BlockSpec(shape, lambda *_: (0,) * nd, pipeline_mode=pl.Buffered(1))


def _params(*sem, flags=None):
    return pltpu.CompilerParams(dimension_semantics=sem, vmem_limit_bytes=VMEM_LIMIT, flags=flags)


def _proj_kernel(x_ref, wn_ref, wt_ref, a0_ref, a1_ref, a2_ref, kb_ref, ot_ref, rows_sc, *, tq):
    xb = x_ref[...].astype(BF16)
    tm = xb.shape[0]
    a_refs = (a0_ref, a1_ref, a2_ref)
    for c in range(COLS_N // CHUNK):
        r = _dot(xb, wn_ref[:, c * CHUNK:(c + 1) * CHUNK])
        g, part = divmod(c, 3)
        if g >= N_GROUPS:
            kb_ref[:, (c - 3 * N_GROUPS) * CHUNK:(c - 3 * N_GROUPS + 1) * CHUNK] = r.astype(BF16)
            continue
        sl = slice(part * W_A_OUT, (part + 1) * W_A_OUT)
        dil = DIL_PAIRS[g][1]
        if dil == 1:
            a_refs[g][0, :, sl] = r.astype(BF16)
            continue
        for k in range(CHUNK // LANES):
            rows_sc[k] = r[:, k * LANES:(k + 1) * LANES]
        for res in range(dil):
            for k in range(CHUNK // LANES):
                piece = rows_sc[k, pl.ds(res, tm // dil, stride=dil), :]
                a_refs[g][res, :, part * W_A_OUT + k * LANES:part * W_A_OUT + (k + 1) * LANES] = piece.astype(BF16)
    for c in range(COLS_T // CHUNK):
        sl = slice(c * CHUNK, (c + 1) * CHUNK)
        r = _dot_nt(wt_ref[sl, :], xb)
        if (c + 1) * CHUNK <= W_B_OUT:
            r = r * (QK_SCALE * LOG2E)
        r = r.astype(BF16)
        for t in range(tm // tq):
            ot_ref[t, sl, :] = r[:, t * tq:(t + 1) * tq]


def _proj(x, w_n, w_t, tm, tq):
    B, S, D = x.shape
    assert CHUNK == W_A_OUT
    dils = [d for _, d in DIL_PAIRS]
    return pl.pallas_call(
        functools.partial(_proj_kernel, tq=tq),
        grid=(B, S // tm),
        in_specs=[
            pl.BlockSpec((None, tm, D), lambda b, i: (b, i, 0)),
            _resident((D, COLS_N)),
            _resident((COLS_T, D)),
        ],
        out_specs=[pl.BlockSpec((None, d, tm // d, 3 * W_A_OUT), lambda b, i: (b, 0, i, 0)) for d in dils] + [
            pl.BlockSpec((None, tm, W_B_OUT), lambda b, i: (b, i, 0)),
            pl.BlockSpec((None, tm // tq, COLS_T, tq), lambda b, i: (b, i, 0, 0)),
        ],
        out_shape=[jax.ShapeDtypeStruct((B, d, S // d, 3 * W_A_OUT), BF16) for d in dils] + [
            jax.ShapeDtypeStruct((B, S, W_B_OUT), BF16),
            jax.ShapeDtypeStruct((B, S // tq, COLS_T, tq), BF16),
        ],
        scratch_shapes=[pltpu.VMEM((CHUNK // LANES, tm, LANES), F32)],
        compiler_params=_params("parallel", "parallel"),
        name="proj",
    )(x, w_n, w_t)


def _dilated_kernel(q_ref, k_ref, v_ref, kp_ref, vp_ref, bias_ref, o_ref, lse_ref, kw_sc, vw_sc):
    tb = q_ref.shape[0]
    first = pl.program_id(2) == 0
    lane = lax.broadcasted_iota(jnp.int32, (BLK, LANES), 1)
    lo = lane < HEAD_DIM
    in_prev = lax.broadcasted_iota(jnp.int32, (2 * BLK, 2 * BLK), 1) < BLK
    ones = jnp.ones((2 * BLK, LANES), BF16)
    kw_sc[:BLK], kw_sc[BLK:] = kp_ref[...], k_ref[...]
    vw_sc[:BLK], vw_sc[BLK:] = vp_ref[...], v_ref[...]
    for jb in range(tb // BLK):
        rows = slice(jb * BLK, (jb + 1) * BLK)
        win = slice(jb * BLK, (jb + 2) * BLK)
        for p in range(H_A // 2):
            cols = slice(p * LANES, (p + 1) * LANES)
            q2 = q_ref[rows, cols]
            zero = jnp.zeros_like(q2)
            qd = jnp.concatenate([jnp.where(lo, q2, zero), jnp.where(lo, zero, q2)], axis=0)
            s = _dot_nt(qd, kw_sc[win, cols]) + bias_ref[p]
            if jb == 0:
                s = jnp.where(first & in_prev, NEG_INF, s)
            m = jnp.max(s, axis=-1, keepdims=True)
            pw = jnp.exp((s - m).astype(BF16))
            oa = _dot(pw, jnp.concatenate([vw_sc[win, cols], ones], axis=1))
            den = oa[:, LANES:]
            o2 = oa[:, :LANES] / den
            lse = m + jnp.log(den)
            o_ref[rows, cols] = jnp.where(lo, o2[:BLK], o2[BLK:])
            lse_ref[rows, cols] = jnp.where(lo, lse[:BLK], lse[BLK:])


def _dilated(a, bias, g):
    B, dil, L, _ = a.shape
    tb = min(CHUNK, L)
    cur = lambda c: pl.BlockSpec((None, None, tb, W_A_OUT), lambda b, r, n: (b, r, n, c))
    prev = lambda c: pl.BlockSpec(
        (None, None, BLK, W_A_OUT), lambda b, r, n: (b, r, jnp.maximum(n * (tb // BLK) - 1, 0), c))
    out = pl.BlockSpec((None, None, tb, W_A_OUT), lambda b, r, n: (b, r, n, 0))
    return pl.pallas_call(
        _dilated_kernel,
        grid=(B, dil, L // tb),
        in_specs=[cur(0), cur(1), cur(2), prev(1), prev(2), _resident(bias.shape)],
        out_specs=[out, out],
        out_shape=[jax.ShapeDtypeStruct((B, dil, L, W_A_OUT), F32)] * 2,
        scratch_shapes=[pltpu.VMEM((BLK + tb, W_A_OUT), BF16)] * 2,
        compiler_params=_params("parallel", "parallel", "arbitrary"),
        name=f"dilated{g}",
    )(a, a, a, a, a, bias)


def _diff_kernel(lam_ref, qt_ref, k_ref, vt_ref, bias_ref, g_ref, o_ref, qd_sc, va_sc, s0_sc, s1_sc, x0_sc, x1_sc,
                 m_sc, acc_sc, *, lam_init):
    nb, _, tb = qt_ref.shape
    tq = tk = bias_ref.shape[-1]
    qb = tq // tb
    nq = nb // qb
    s_bufs, x_bufs = (s0_sc, s1_sc), (x0_sc, x1_sc)
    row = lax.broadcasted_iota(jnp.int32, (DV_B, tb), 0)
    for i in range(nb):
        blk, part = divmod(i, qb)
        qt = qt_ref[i]
        zero = jnp.zeros_like(qt)
        qd_sc[blk, :, part * tb:(part + 1) * tb] = jnp.where(row < HEAD_DIM, qt, zero)
        qd_sc[blk, :, tq + part * tb:tq + (part + 1) * tb] = jnp.where(row < HEAD_DIM, zero, qt)
        va_sc[blk, :DV_B, part * tb:(part + 1) * tb] = vt_ref[i]
    for i in range(nq):
        va_sc[i, DV_B:, :] = jnp.ones((va_sc.shape[1] - DV_B, tk), BF16)

    lam = (jnp.exp(jnp.sum(lam_ref[0:1, :] * lam_ref[1:2, :], axis=-1, keepdims=True))
           - jnp.exp(jnp.sum(lam_ref[2:3, :] * lam_ref[3:4, :], axis=-1, keepdims=True)) + lam_init)

    def reset(a):
        m_sc[a] = jnp.full(m_sc.shape[1:], -jnp.inf, F32)
        acc_sc[a] = jnp.zeros(acc_sc.shape[1:], F32)

    def finalize(qi):
        a = qi & 1
        acc = acc_sc[a]
        on = acc[:DV_B] / acc[DV_B:DV_B + 1]
        o = on[:, :tq] - lam * on[:, tq:]
        ms = jnp.mean(o * o, axis=0, keepdims=True)
        o = o * lax.rsqrt(ms + LN_EPS) * g_ref[...] * (1.0 - lam_init)
        o_ref[pl.ds(pl.multiple_of(qi * tq, tq), tq), :] = o.T.astype(BF16)
        reset(a)

    def is_last(qi, j):
        return j == qi

    def bias_index(qi, j):
        return jnp.where(j == qi, 0, jnp.where(j == qi - 1, 1, 2))

    strips = [slice(c * STRIP, (c + 1) * STRIP) for c in range(2 * tq // STRIP)]

    def score_strip(kblk, qi, bi, cs):
        bcs = slice(cs.start % tq, cs.start % tq + STRIP)
        s = _dot(kblk, qd_sc[qi, :, cs]) + bias_ref[bi, :, bcs].astype(F32)
        return s, jnp.max(s, axis=0, keepdims=True)

    reset(0)
    reset(1)
    for cs in strips:
        s_bufs[0][:, cs], x_bufs[0][:, cs] = score_strip(k_ref[0:tk, :], 0, bias_index(0, 0), cs)

    def half(slot, qi, j):
        s_cur, x_cur, s_oth, x_oth = s_bufs[slot], x_bufs[slot], s_bufs[1 - slot], x_bufs[1 - slot]
        a = qi & 1
        last = is_last(qi, j)
        j_n = jnp.where(last, 0, j + 1)
        qi_n = jnp.minimum(jnp.where(last, qi + 1, qi), nq - 1)
        bi_n = bias_index(qi_n, j_n)
        kblk = k_ref[pl.ds(pl.multiple_of(j_n * tk, tk), tk), :]
        va = va_sc[j]
        for cs in strips:
            m_old = m_sc[a, :, cs]
            m_new = jnp.maximum(m_old, x_cur[:, cs])
            p = jnp.exp2((s_cur[:, cs] - m_new).astype(BF16))
            alpha = jnp.exp2(m_old - m_new)
            m_sc[a, :, cs] = m_new
            acc_sc[a, :, cs] = acc_sc[a, :, cs] * alpha + _dot(va, p)
            s_oth[:, cs], x_oth[:, cs] = score_strip(kblk, qi_n, bi_n, cs)
        return qi_n, j_n

    def body(_, carry):
        qi, j = carry
        qi1, j1 = half(0, qi, j)
        qi2, j2 = half(1, qi1, j1)

        @pl.when(is_last(qi, j))
        def _():
            finalize(qi)

        @pl.when(is_last(qi1, j1))
        def _():
            finalize(qi1)

        return qi2, j2

    npairs = nq * (nq + 1) // 2
    qi, j = lax.fori_loop(0, npairs // 2, body, (jnp.int32(0), jnp.int32(0)))
    if npairs % 2:
        half(0, qi, j)
        finalize(nq - 1)


def _diff(lam_p, kb, pt, bias, g_col, lam_init):
    B, S, _ = kb.shape
    nb, tb = pt.shape[1], pt.shape[3]
    tq = tk = bias.shape[-1]
    nq = S // tq
    assert tq % tb == 0 and tq % STRIP == 0 and bias.shape[1:] == (3, tk, tq)
    va_rows = DV_B + 16
    return pl.pallas_call(
        functools.partial(_diff_kernel, lam_init=lam_init),
        grid=(B, H_B),
        in_specs=[
            _resident(lam_p.shape),
            pl.BlockSpec((None, nb, DV_B, tb), lambda b, h: (b, 0, h, 0)),
            pl.BlockSpec((None, S, LANES), lambda b, h: (b, 0, h)),
            pl.BlockSpec((None, nb, DV_B, tb), lambda b, h: (b, 0, H_B + h, 0)),
            pl.BlockSpec((None, 3, tk, tq), lambda b, h: (h, 0, 0, 0)),
            _resident(g_col.shape),
        ],
        out_specs=pl.BlockSpec((None, S, DV_B), lambda b, h: (b, 0, h)),
        out_shape=jax.ShapeDtypeStruct((B, S, W_B_OUT), BF16),
        scratch_shapes=[
            pltpu.VMEM((nq, DV_B, 2 * tq), BF16),
            pltpu.VMEM((S // tk, va_rows, tk), BF16),
            pltpu.VMEM((tk, 2 * tq), F32),
            pltpu.VMEM((tk, 2 * tq), F32),
            pltpu.VMEM((1, 2 * tq), F32),
            pltpu.VMEM((1, 2 * tq), F32),
            pltpu.VMEM((2, 1, 2 * tq), F32),
            pltpu.VMEM((2, va_rows, 2 * tq), F32),
        ],
        compiler_params=_params("parallel", "parallel"),
        name="diff_attn",
    )(lam_p, pt, kb, pt, bias, g_col)


def _toeplitz_kernel(w_ref, o_ref, *, diff):
    width = o_ref.shape[-1]
    rows = width if diff else o_ref.shape[-2]
    t = pltpu.roll(jnp.broadcast_to(w_ref[...], (rows, width)), 0, 1, stride=1, stride_axis=0)
    if not diff:
        o_ref[...] = t
        return
    r = lax.broadcasted_iota(jnp.int32, (rows, width), 0)
    c = lax.broadcasted_iota(jnp.int32, (rows, width), 1)
    o_ref[0] = jnp.where(c >= r, t, NEG_INF).astype(o_ref.dtype)
    o_ref[1] = jnp.where(c < r, t, 0.0).astype(o_ref.dtype)
    o_ref[2] = jnp.zeros((rows, width), o_ref.dtype)


def _toeplitz(w, rows, diff):
    n, _, width = w.shape
    oshape = (n, 3, rows, width) if diff else (n, rows, width)
    oblock = (None,) + oshape[1:]
    return pl.pallas_call(
        functools.partial(_toeplitz_kernel, diff=diff),
        grid=(n,),
        in_specs=[pl.BlockSpec((None, 1, width), lambda i: (i, 0, 0))],
        out_specs=pl.BlockSpec(oblock, lambda i: (i,) + (0,) * (len(oshape) - 1)),
        out_shape=jax.ShapeDtypeStruct(oshape, BF16 if diff else F32),
        compiler_params=_params("parallel"),
        name="bias_diff" if diff else "bias_dilated",
    )(w)


def _layer_norm(h, g, b):
    mu = jnp.mean(h, axis=-1, keepdims=True)
    d = h - mu
    var = jnp.mean(d * d, axis=-1, keepdims=True)
    return d * lax.rsqrt(var + LN_EPS) * g + b


def _mix_kernel(x_ref, o0_ref, o1_ref, o2_ref, l0_ref, l1_ref, l2_ref, ob_ref, wg_ref, bg_ref, wa_ref, wb_ref,
                wo_ref, g_ref, b_ref, out_ref, *tok_sc):
    x = x_ref[...]
    xb = x.astype(BF16)

    def token_major(ref, sc):
        dil, rows, _ = ref.shape
        if dil == 1:
            return ref[0]
        nk = sc.shape[0]
        for res in range(dil):
            for k in range(nk):
                sc[k, pl.ds(res, rows, stride=dil), :] = ref[res, :, k * LANES:(k + 1) * LANES]
        return jnp.concatenate([sc[k] for k in range(nk)], axis=1)

    l0 = token_major(l0_ref, None)
    l1, l2 = token_major(l1_ref, tok_sc[0]), token_major(l2_ref, tok_sc[1])
    o0 = token_major(o0_ref, None)
    o1, o2 = token_major(o1_ref, tok_sc[2]), token_major(o2_ref, tok_sc[3])
    mx = jnp.maximum(jnp.maximum(l0, l1), l2)
    e0, e1, e2 = jnp.exp(l0 - mx), jnp.exp(l1 - mx), jnp.exp(l2 - mx)
    o_a = (e0 * o0 + e1 * o1 + e2 * o2) / (e0 + e1 + e2)
    y_a = _dot(o_a.astype(BF16), wa_ref[...])
    y_b = _dot(ob_ref[...], wb_ref[...])
    gate_a = jax.nn.sigmoid(_dot(xb, wg_ref[:, :D_MODEL]) + bg_ref[:, :D_MODEL])
    gate_b = jax.nn.sigmoid(_dot(xb, wg_ref[:, D_MODEL:]) + bg_ref[:, D_MODEL:])
    merged = gate_a * y_a + gate_b * y_b
    mix = _dot(merged.astype(BF16), wo_ref[...])
    out_ref[...] = _layer_norm(DEEPNORM_ALPHA * x + mix, g_ref[...], b_ref[...])


def _mix(x, o_g, lse_g, o_b, w_gate, b_gate, w_a, w_b, w_o, ln_g, ln_b, tm):
    B, S, D = x.shape
    tok = lambda w: pl.BlockSpec((None, tm, w), lambda b, i: (b, i, 0))
    res = [pl.BlockSpec((None, a.shape[1], tm // a.shape[1], W_A_OUT), lambda b, i: (b, 0, i, 0)) for a in o_g]
    n_sc = 2 * sum(a.shape[1] > 1 for a in o_g)
    return pl.pallas_call(
        _mix_kernel,
        grid=(B, S // tm),
        in_specs=[tok(D)] + res + res + [tok(W_B_OUT)]
        + [_resident(a.shape) for a in (w_gate, b_gate, w_a, w_b, w_o, ln_g, ln_b)],
        out_specs=tok(D),
        out_shape=jax.ShapeDtypeStruct((B, S, D), F32),
        scratch_shapes=[pltpu.VMEM((W_A_OUT // LANES, tm, LANES), F32)] * n_sc,
        compiler_params=_params("parallel", "parallel"),
        name="mix",
    )(x, *o_g, *lse_g, o_b, w_gate, b_gate, w_a, w_b, w_o, ln_g, ln_b)


def _mlp_kernel(x_ref, w1_ref, w2_ref, g_ref, b_ref, out_ref):
    x = x_ref[...]
    xb = x.astype(BF16)
    ff = jnp.zeros(x.shape, F32)
    for c in range(D_FF // (2 * CHUNK)):
        sl = slice(c * 2 * CHUNK, (c + 1) * 2 * CHUNK)
        h = jnp.maximum(_dot(xb, w1_ref[:, sl]), 0.0)
        ff = ff + _dot((h * h).astype(BF16), w2_ref[sl, :])
    out_ref[...] = _layer_norm(DEEPNORM_ALPHA * x + ff, g_ref[...], b_ref[...])


def _mlp(x, w1, w2, ln_g, ln_b, tm):
    B, S, D = x.shape
    tok = pl.BlockSpec((None, tm, D), lambda b, i: (b, i, 0))
    return pl.pallas_call(
        _mlp_kernel,
        grid=(B, S // tm),
        in_specs=[tok] + [_resident(a.shape) for a in (w1, w2, ln_g, ln_b)],
        out_specs=tok,
        out_shape=jax.ShapeDtypeStruct((B, S, D), F32),
        compiler_params=_params("parallel", "parallel"),
        name="mlp",
    )(x, w1, w2, ln_g, ln_b)


def _t5_bucket(dist):
    n = jnp.maximum(dist, 0)
    max_exact = NUM_BUCKETS // 2
    nf = jnp.maximum(n, 1).astype(F32)
    large = max_exact + (jnp.log(nf / max_exact) / math.log(T5_MAX_DISTANCE / max_exact)
                         * (NUM_BUCKETS - max_exact)).astype(jnp.int32)
    large = jnp.minimum(large, NUM_BUCKETS - 1)
    return jnp.where(n < max_exact, n, large)


def _dilated_bias(rel_bias, g, dil):
    j = jnp.arange(2 * BLK)
    tab = rel_bias[:, g * H_A:(g + 1) * H_A].astype(F32)
    vec = tab[_t5_bucket(jnp.maximum(BLK - j, 0) * dil)]
    vec = jnp.where((j <= BLK)[:, None], vec, NEG_INF)
    bias = _toeplitz(vec.T.reshape(H_A, 1, 2 * BLK), BLK, diff=False)
    return bias.reshape(H_A // 2, 2 * BLK, 2 * BLK)


def _diff_bias(rel_bias, tq):
    assert tq >= T5_MAX_DISTANCE
    tab = rel_bias[:, N_GROUPS * H_A:].astype(F32)
    tab = (tab - tab[NUM_BUCKETS - 1:NUM_BUCKETS]) * LOG2E
    vec = tab[_t5_bucket(jnp.arange(tq))]
    return _toeplitz(vec.T.reshape(H_B, 1, tq), tq, diff=True)


def kernel(x, w_in, b_gate, lambda_q1, lambda_k1, lambda_q2, lambda_k2, subln_g, rel_bias, w_proj_a, w_proj_b,
           w_out, ln1_g, ln1_b, ln2_g, ln2_b, w_mlp1, w_mlp2):
    B, S, D = x.shape
    tq = min(TQ, S)
    tm = min(TM, S)
    w = w_in[0]
    w_a = w[:, :COLS_A].reshape(D, 3, N_GROUPS, W_A_OUT)
    w_a = jnp.concatenate([w_a[:, :1] * QK_SCALE, w_a[:, 1:]], axis=1).transpose(0, 2, 1, 3).reshape(D, COLS_A)
    w_qk = w[:, COLS_A:COLS_A + COLS_B_QK].reshape(D, 4, H_B, HEAD_DIM)
    w_qb = jnp.concatenate([w_qk[:, 0], w_qk[:, 1]], axis=-1).reshape(D, W_B_OUT)
    w_kb = jnp.concatenate([w_qk[:, 2], w_qk[:, 3]], axis=-1).reshape(D, W_B_OUT)
    w_vb = w[:, COLS_A + COLS_B_QK:COLS_A + COLS_B]
    w_n = jnp.concatenate([w_a, w_kb], axis=1).astype(BF16)
    w_t = jnp.concatenate([w_qb, w_vb], axis=1).T.astype(BF16)
    w_gate = w[:, COLS_A + COLS_B:].astype(BF16)

    *a_g, kb, pt = _proj(x, w_n, w_t, tm, tq)

    o_g, lse_g = [], []
    for g, (win, dil) in enumerate(DIL_PAIRS):
        assert win // dil == BLK
        o, lse = _dilated(a_g[g], _dilated_bias(rel_bias, g, dil), g)
        o_g.append(o)
        lse_g.append(lse)

    lam_init = 0.8 - 0.6 * math.exp(-0.3 * 0)
    lam_p = jnp.concatenate([lambda_q1, lambda_k1, lambda_q2, lambda_k2], axis=0).astype(F32)
    o_b = _diff(lam_p, kb, pt, _diff_bias(rel_bias, min(DIFF_T, S)), subln_g[0].reshape(DV_B, 1), lam_init)

    x1 = _mix(x, o_g, lse_g, o_b, w_gate, b_gate, w_proj_a[0].astype(BF16), w_proj_b[0].astype(BF16),
              w_out[0].astype(BF16), ln1_g, ln1_b, tm)
    return _mlp(x1, w_mlp1[0].astype(BF16), w_mlp2[0].astype(BF16), ln2_g, ln2_b, tm)
```

```python
import functools
import math

import jax
import jax.numpy as jnp
import numpy as np
from jax import lax
from jax.experimental import pallas as pl
from jax.experimental.pallas import tpu as pltpu

D_MODEL = 1024
HEAD_DIM = 64
DIL_PAIRS = ((128, 1), (512, 4), (2048, 16))
N_GROUPS = len(DIL_PAIRS)
H_A = 8
H_B = 8
DV_B = 2 * HEAD_DIM
D_FF = 4 * D_MODEL
NUM_BUCKETS = 32
T5_MAX_DISTANCE = 128
BLK = 128
LN_EPS = 1e-5
NEG_INF = -1e30
W_A_OUT = H_A * HEAD_DIM
W_B_OUT = H_B * DV_B
COLS_A = 3 * N_GROUPS * H_A * HEAD_DIM
COLS_B_QK = 4 * H_B * HEAD_DIM
COLS_B = COLS_B_QK + H_B * DV_B
DEPTH = 1
DEEPNORM_ALPHA = (2.0 * DEPTH) ** 0.25
QK_SCALE = HEAD_DIM ** -0.5
LOG2E = math.log2(math.e)

LANES = 128
COLS_N = COLS_A + H_B * 2 * HEAD_DIM
COLS_T = 2 * W_B_OUT
CHUNK = 512
TQ = 512
DIFF_T = 1024
STRIP = 512
PAIRS_PER_BLOCK = 2
ACC_SLOTS = PAIRS_PER_BLOCK
TM = 512
VMEM_LIMIT = 56 * 1024 * 1024

BF16 = jnp.bfloat16
F32 = jnp.float32


def _dot(a, b):
    return jnp.dot(a, b, preferred_element_type=F32)


def _dot_nt(a, b):
    return lax.dot_general(a, b, (((1,), (1,)), ((), ())), preferred_element_type=F32)


def _resident(shape):
    nd = len(shape)
    return pl.BlockSpec(shape, lambda *_: (0,) * nd, pipeline_mode=pl.Buffered(1))


def _params(*sem, flags=None):
    return pltpu.CompilerParams(dimension_semantics=sem, vmem_limit_bytes=VMEM_LIMIT, flags=flags)


def _proj_kernel(x_ref, wn_ref, wt_ref, a0_ref, a1_ref, a2_ref, kb_ref, ot_ref, rows_sc, *, tq):
    xb = x_ref[...].astype(BF16)
    tm = xb.shape[0]
    a_refs = (a0_ref, a1_ref, a2_ref)
    for c in range(COLS_N // CHUNK):
        r = _dot(xb, wn_ref[:, c * CHUNK:(c + 1) * CHUNK])
        g, part = divmod(c, 3)
        if g >= N_GROUPS:
            kb_ref[:, (c - 3 * N_GROUPS) * CHUNK:(c - 3 * N_GROUPS + 1) * CHUNK] = r.astype(BF16)
            continue
        sl = slice(part * W_A_OUT, (part + 1) * W_A_OUT)
        dil = DIL_PAIRS[g][1]
        if dil == 1:
            a_refs[g][0, :, sl] = r.astype(BF16)
            continue
        for k in range(CHUNK // LANES):
            rows_sc[k] = r[:, k * LANES:(k + 1) * LANES]
        for res in range(dil):
            for k in range(CHUNK // LANES):
                piece = rows_sc[k, pl.ds(res, tm // dil, stride=dil), :]
                a_refs[g][res, :, part * W_A_OUT + k * LANES:part * W_A_OUT + (k + 1) * LANES] = piece.astype(BF16)
    for c in range(COLS_T // CHUNK):
        sl = slice(c * CHUNK, (c + 1) * CHUNK)
        r = _dot_nt(wt_ref[sl, :], xb)
        if (c + 1) * CHUNK <= W_B_OUT:
            r = r * (QK_SCALE * LOG2E)
        r = r.astype(BF16)
        for t in range(tm // tq):
            ot_ref[t, sl, :] = r[:, t * tq:(t + 1) * tq]


def _proj(x, w_n, w_t, tm, tq):
    B, S, D = x.shape
    assert CHUNK == W_A_OUT
    dils = [d for _, d in DIL_PAIRS]
    return pl.pallas_call(
        functools.partial(_proj_kernel, tq=tq),
        grid=(B, S // tm),
        in_specs=[
            pl.BlockSpec((None, tm, D), lambda b, i: (b, i, 0)),
            _resident((D, COLS_N)),
            _resident((COLS_T, D)),
        ],
        out_specs=[pl.BlockSpec((None, d, tm // d, 3 * W_A_OUT), lambda b, i: (b, 0, i, 0)) for d in dils] + [
            pl.BlockSpec((None, tm, W_B_OUT), lambda b, i: (b, i, 0)),
            pl.BlockSpec((None, tm // tq, COLS_T, tq), lambda b, i: (b, i, 0, 0)),
        ],
        out_shape=[jax.ShapeDtypeStruct((B, d, S // d, 3 * W_A_OUT), BF16) for d in dils] + [
            jax.ShapeDtypeStruct((B, S, W_B_OUT), BF16),
            jax.ShapeDtypeStruct((B, S // tq, COLS_T, tq), BF16),
        ],
        scratch_shapes=[pltpu.VMEM((CHUNK // LANES, tm, LANES), F32)],
        compiler_params=_params("parallel", "parallel"),
        name="proj",
    )(x, w_n, w_t)


def _dilated_kernel(q_ref, k_ref, v_ref, kp_ref, vp_ref, bias_ref, o_ref, lse_ref, kw_sc, vw_sc):
    tb = q_ref.shape[0]
    first = pl.program_id(2) == 0
    lane = lax.broadcasted_iota(jnp.int32, (BLK, LANES), 1)
    lo = lane < HEAD_DIM
    in_prev = lax.broadcasted_iota(jnp.int32, (2 * BLK, 2 * BLK), 1) < BLK
    ones = jnp.ones((2 * BLK, LANES), BF16)
    kw_sc[:BLK], kw_sc[BLK:] = kp_ref[...], k_ref[...]
    vw_sc[:BLK], vw_sc[BLK:] = vp_ref[...], v_ref[...]
    for jb in range(tb // BLK):
        rows = slice(jb * BLK, (jb + 1) * BLK)
        win = slice(jb * BLK, (jb + 2) * BLK)
        for p in range(H_A // 2):
            cols = slice(p * LANES, (p + 1) * LANES)
            q2 = q_ref[rows, cols]
            zero = jnp.zeros_like(q2)
            qd = jnp.concatenate([jnp.where(lo, q2, zero), jnp.where(lo, zero, q2)], axis=0)
            s = _dot_nt(qd, kw_sc[win, cols]) + bias_ref[p]
            if jb == 0:
                s = jnp.where(first & in_prev, NEG_INF, s)
            m = jnp.max(s, axis=-1, keepdims=True)
            pw = jnp.exp((s - m).astype(BF16))
            oa = _dot(pw, jnp.concatenate([vw_sc[win, cols], ones], axis=1))
            den = oa[:, LANES:]
            o2 = oa[:, :LANES] / den
            lse = m + jnp.log(den)
            o_ref[rows, cols] = jnp.where(lo, o2[:BLK], o2[BLK:])
            lse_ref[rows, cols] = jnp.where(lo, lse[:BLK], lse[BLK:])


def _dilated(a, bias, g):
    B, dil, L, _ = a.shape
    tb = min(CHUNK, L)
    cur = lambda c: pl.BlockSpec((None, None, tb, W_A_OUT), lambda b, r, n: (b, r, n, c))
    prev = lambda c: pl.BlockSpec(
        (None, None, BLK, W_A_OUT), lambda b, r, n: (b, r, jnp.maximum(n * (tb // BLK) - 1, 0), c))
    out = pl.BlockSpec((None, None, tb, W_A_OUT), lambda b, r, n: (b, r, n, 0))
    return pl.pallas_call(
        _dilated_kernel,
        grid=(B, dil, L // tb),
        in_specs=[cur(0), cur(1), cur(2), prev(1), prev(2), _resident(bias.shape)],
        out_specs=[out, out],
        out_shape=[jax.ShapeDtypeStruct((B, dil, L, W_A_OUT), F32)] * 2,
        scratch_shapes=[pltpu.VMEM((BLK + tb, W_A_OUT), BF16)] * 2,
        compiler_params=_params("parallel", "parallel", "arbitrary"),
        name=f"dilated{g}",
    )(a, a, a, a, a, bias)


def _diff_kernel(lam_ref, qt_ref, k_ref, vt_ref, bias_ref, g_ref, o_ref, qd_sc, va_sc, s0_sc, s1_sc, x0_sc, x1_sc,
                 m_sc, acc_sc, *, lam_init):
    nb, _, tb = qt_ref.shape
    tq = tk = bias_ref.shape[-1]
    qb = tq // tb
    nq = nb // qb
    s_bufs, x_bufs = (s0_sc, s1_sc), (x0_sc, x1_sc)
    row = lax.broadcasted_iota(jnp.int32, (DV_B, tb), 0)
    for i in range(nb):
        blk, part = divmod(i, qb)
        qt = qt_ref[i]
        zero = jnp.zeros_like(qt)
        qd_sc[blk, :, part * tb:(part + 1) * tb] = jnp.where(row < HEAD_DIM, qt, zero)
        qd_sc[blk, :, tq + part * tb:tq + (part + 1) * tb] = jnp.where(row < HEAD_DIM, zero, qt)
        va_sc[blk, :DV_B, part * tb:(part + 1) * tb] = vt_ref[i]
    for i in range(nq):
        va_sc[i, DV_B:, :] = jnp.ones((va_sc.shape[1] - DV_B, tk), BF16)

    lam = (jnp.exp(jnp.sum(lam_ref[0:1, :] * lam_ref[1:2, :], axis=-1, keepdims=True))
           - jnp.exp(jnp.sum(lam_ref[2:3, :] * lam_ref[3:4, :], axis=-1, keepdims=True)) + lam_init)

    def reset(a):
        m_sc[a] = jnp.full(m_sc.shape[1:], -jnp.inf, F32)
        acc_sc[a] = jnp.zeros(acc_sc.shape[1:], F32)

    def finalize(qi):
        a = qi % ACC_SLOTS
        acc = acc_sc[a]
        on = acc[:DV_B] / acc[DV_B:DV_B + 1]
        o = on[:, :tq] - lam * on[:, tq:]
        ms = jnp.mean(o * o, axis=0, keepdims=True)
        o = o * lax.rsqrt(ms + LN_EPS) * g_ref[...] * (1.0 - lam_init)
        o_ref[pl.ds(pl.multiple_of(qi * tq, tq), tq), :] = o.T.astype(BF16)
        reset(a)

    def is_last(qi, j):
        return j == qi

    strips = [slice(c * STRIP, (c + 1) * STRIP) for c in range(2 * tq // STRIP)]

    def score_strip(kblk, qi, bi, cs):
        s = _dot(kblk, qd_sc[qi, :, cs])
        if bi is not None:
            bcs = slice(cs.start % tq, cs.start % tq + STRIP)
            s = s + bias_ref[bi, :, bcs].astype(F32)
        return s, jnp.max(s, axis=0, keepdims=True)

    for a in range(ACC_SLOTS):
        reset(a)
    for cs in strips:
        s_bufs[0][:, cs], x_bufs[0][:, cs] = score_strip(k_ref[0:tk, :], 0, 0, cs)

    def half(slot, qi, j):
        s_cur, x_cur, s_oth, x_oth = s_bufs[slot], x_bufs[slot], s_bufs[1 - slot], x_bufs[1 - slot]
        a = qi % ACC_SLOTS
        last = is_last(qi, j)
        j_n = jnp.where(last, 0, j + 1)
        qi_n = jnp.minimum(jnp.where(last, qi + 1, qi), nq - 1)
        near = j_n >= qi_n - 1
        kblk = k_ref[pl.ds(pl.multiple_of(j_n * tk, tk), tk), :]
        va = va_sc[j]

        def run(bi):
            for cs in strips:
                m_old = m_sc[a, :, cs]
                m_new = jnp.maximum(m_old, x_cur[:, cs])
                p = jnp.exp2((s_cur[:, cs] - m_new).astype(BF16))
                alpha = jnp.exp2(m_old - m_new)
                m_sc[a, :, cs] = m_new
                acc_sc[a, :, cs] = acc_sc[a, :, cs] * alpha + _dot(va, p)
                s_oth[:, cs], x_oth[:, cs] = score_strip(kblk, qi_n, bi, cs)

        pl.when(near)(functools.partial(run, qi_n - j_n))
        pl.when(jnp.logical_not(near))(functools.partial(run, None))
        return qi_n, j_n

    def run_pairs(count, carry):
        qi, j = carry
        done = []
        for h in range(count):
            done.append((qi, j))
            qi, j = half(h % 2, qi, j)
        for q, jj in done:
            pl.when(is_last(q, jj))(functools.partial(finalize, q))
        return qi, j

    npairs = nq * (nq + 1) // 2
    carry = lax.fori_loop(0, npairs // PAIRS_PER_BLOCK, lambda _, c: run_pairs(PAIRS_PER_BLOCK, c),
                          (jnp.int32(0), jnp.int32(0)))
    if npairs % PAIRS_PER_BLOCK:
        run_pairs(npairs % PAIRS_PER_BLOCK, carry)


def _diff(lam_p, kb, pt, bias, g_col, lam_init):
    B, S, _ = kb.shape
    nb, tb = pt.shape[1], pt.shape[3]
    tq = tk = bias.shape[-1]
    nq = S // tq
    assert tq % tb == 0 and tq % STRIP == 0 and bias.shape[1:] == (2, tk, tq)
    va_rows = DV_B + 16
    return pl.pallas_call(
        functools.partial(_diff_kernel, lam_init=lam_init),
        grid=(B, H_B),
        in_specs=[
            _resident(lam_p.shape),
            pl.BlockSpec((None, nb, DV_B, tb), lambda b, h: (b, 0, h, 0)),
            pl.BlockSpec((None, S, LANES), lambda b, h: (b, 0, h)),
            pl.BlockSpec((None, nb, DV_B, tb), lambda b, h: (b, 0, H_B + h, 0)),
            pl.BlockSpec((None, 2, tk, tq), lambda b, h: (h, 0, 0, 0)),
            _resident(g_col.shape),
        ],
        out_specs=pl.BlockSpec((None, S, DV_B), lambda b, h: (b, 0, h)),
        out_shape=jax.ShapeDtypeStruct((B, S, W_B_OUT), BF16),
        scratch_shapes=[
            pltpu.VMEM((nq, DV_B, 2 * tq), BF16),
            pltpu.VMEM((S // tk, va_rows, tk), BF16),
            pltpu.VMEM((tk, 2 * tq), F32),
            pltpu.VMEM((tk, 2 * tq), F32),
            pltpu.VMEM((1, 2 * tq), F32),
            pltpu.VMEM((1, 2 * tq), F32),
            pltpu.VMEM((ACC_SLOTS, 1, 2 * tq), F32),
            pltpu.VMEM((ACC_SLOTS, va_rows, 2 * tq), F32),
        ],
        compiler_params=_params("parallel", "parallel"),
        name="diff_attn",
    )(lam_p, pt, kb, pt, bias, g_col)


def _toeplitz_kernel(w_ref, o_ref, *, diff):
    width = o_ref.shape[-1]
    rows = width if diff else o_ref.shape[-2]
    t = pltpu.roll(jnp.broadcast_to(w_ref[...], (rows, width)), 0, 1, stride=1, stride_axis=0)
    if not diff:
        o_ref[...] = t
        return
    r = lax.broadcasted_iota(jnp.int32, (rows, width), 0)
    c = lax.broadcasted_iota(jnp.int32, (rows, width), 1)
    o_ref[0] = jnp.where(c >= r, t, NEG_INF).astype(o_ref.dtype)
    o_ref[1] = jnp.where(c < r, t, 0.0).astype(o_ref.dtype)


def _toeplitz(w, rows, diff):
    n, _, width = w.shape
    oshape = (n, 2, rows, width) if diff else (n, rows, width)
    oblock = (None,) + oshape[1:]
    return pl.pallas_call(
        functools.partial(_toeplitz_kernel, diff=diff),
        grid=(n,),
        in_specs=[pl.BlockSpec((None, 1, width), lambda i: (i, 0, 0))],
        out_specs=pl.BlockSpec(oblock, lambda i: (i,) + (0,) * (len(oshape) - 1)),
        out_shape=jax.ShapeDtypeStruct(oshape, BF16 if diff else F32),
        compiler_params=_params("parallel"),
        name="bias_diff" if diff else "bias_dilated",
    )(w)


def _layer_norm(h, g, b):
    mu = jnp.mean(h, axis=-1, keepdims=True)
    d = h - mu
    var = jnp.mean(d * d, axis=-1, keepdims=True)
    return d * lax.rsqrt(var + LN_EPS) * g + b


def _mix_kernel(x_ref, o0_ref, o1_ref, o2_ref, l0_ref, l1_ref, l2_ref, ob_ref, wg_ref, bg_ref, wa_ref, wb_ref,
                wo_ref, g_ref, b_ref, out_ref, *tok_sc):
    x = x_ref[...]
    xb = x.astype(BF16)

    def token_major(ref, sc):
        dil, rows, _ = ref.shape
        if dil == 1:
            return ref[0]
        nk = sc.shape[0]
        for res in range(dil):
            for k in range(nk):
                sc[k, pl.ds(res, rows, stride=dil), :] = ref[res, :, k * LANES:(k + 1) * LANES]
        return jnp.concatenate([sc[k] for k in range(nk)], axis=1)

    l0 = token_major(l0_ref, None)
    l1, l2 = token_major(l1_ref, tok_sc[0]), token_major(l2_ref, tok_sc[1])
    o0 = token_major(o0_ref, None)
    o1, o2 = token_major(o1_ref, tok_sc[2]), token_major(o2_ref, tok_sc[3])
    mx = jnp.maximum(jnp.maximum(l0, l1), l2)
    e0, e1, e2 = jnp.exp(l0 - mx), jnp.exp(l1 - mx), jnp.exp(l2 - mx)
    o_a = (e0 * o0 + e1 * o1 + e2 * o2) / (e0 + e1 + e2)
    y_a = _dot(o_a.astype(BF16), wa_ref[...])
    y_b = _dot(ob_ref[...], wb_ref[...])
    gate_a = jax.nn.sigmoid(_dot(xb, wg_ref[:, :D_MODEL]) + bg_ref[:, :D_MODEL])
    gate_b = jax.nn.sigmoid(_dot(xb, wg_ref[:, D_MODEL:]) + bg_ref[:, D_MODEL:])
    merged = gate_a * y_a + gate_b * y_b
    mix = _dot(merged.astype(BF16), wo_ref[...])
    out_ref[...] = _layer_norm(DEEPNORM_ALPHA * x + mix, g_ref[...], b_ref[...])


def _mix(x, o_g, lse_g, o_b, w_gate, b_gate, w_a, w_b, w_o, ln_g, ln_b, tm):
    B, S, D = x.shape
    tok = lambda w: pl.BlockSpec((None, tm, w), lambda b, i: (b, i, 0))
    res = [pl.BlockSpec((None, a.shape[1], tm // a.shape[1], W_A_OUT), lambda b, i: (b, 0, i, 0)) for a in o_g]
    n_sc = 2 * sum(a.shape[1] > 1 for a in o_g)
    return pl.pallas_call(
        _mix_kernel,
        grid=(B, S // tm),
        in_specs=[tok(D)] + res + res + [tok(W_B_OUT)]
        + [_resident(a.shape) for a in (w_gate, b_gate, w_a, w_b, w_o, ln_g, ln_b)],
        out_specs=tok(D),
        out_shape=jax.ShapeDtypeStruct((B, S, D), F32),
        scratch_shapes=[pltpu.VMEM((W_A_OUT // LANES, tm, LANES), F32)] * n_sc,
        compiler_params=_params("parallel", "parallel"),
        name="mix",
    )(x, *o_g, *lse_g, o_b, w_gate, b_gate, w_a, w_b, w_o, ln_g, ln_b)


def _mlp_kernel(x_ref, w1_ref, w2_ref, g_ref, b_ref, out_ref):
    x = x_ref[...]
    xb = x.astype(BF16)
    ff = jnp.zeros(x.shape, F32)
    for c in range(D_FF // (2 * CHUNK)):
        sl = slice(c * 2 * CHUNK, (c + 1) * 2 * CHUNK)
        h = jnp.maximum(_dot(xb, w1_ref[:, sl]), 0.0)
        ff = ff + _dot((h * h).astype(BF16), w2_ref[sl, :])
    out_ref[...] = _layer_norm(DEEPNORM_ALPHA * x + ff, g_ref[...], b_ref[...])


def _mlp(x, w1, w2, ln_g, ln_b, tm):
    B, S, D = x.shape
    tok = pl.BlockSpec((None, tm, D), lambda b, i: (b, i, 0))
    return pl.pallas_call(
        _mlp_kernel,
        grid=(B, S // tm),
        in_specs=[tok] + [_resident(a.shape) for a in (w1, w2, ln_g, ln_b)],
        out_specs=tok,
        out_shape=jax.ShapeDtypeStruct((B, S, D), F32),
        compiler_params=_params("parallel", "parallel"),
        name="mlp",
    )(x, w1, w2, ln_g, ln_b)


def _t5_bucket(dist):
    n = jnp.maximum(dist, 0)
    max_exact = NUM_BUCKETS // 2
    nf = jnp.maximum(n, 1).astype(F32)
    large = max_exact + (jnp.log(nf / max_exact) / math.log(T5_MAX_DISTANCE / max_exact)
                         * (NUM_BUCKETS - max_exact)).astype(jnp.int32)
    large = jnp.minimum(large, NUM_BUCKETS - 1)
    return jnp.where(n < max_exact, n, large)


def _dilated_bias(rel_bias, g, dil):
    j = jnp.arange(2 * BLK)
    tab = rel_bias[:, g * H_A:(g + 1) * H_A].astype(F32)
    vec = tab[_t5_bucket(jnp.maximum(BLK - j, 0) * dil)]
    vec = jnp.where((j <= BLK)[:, None], vec, NEG_INF)
    bias = _toeplitz(vec.T.reshape(H_A, 1, 2 * BLK), BLK, diff=False)
    return bias.reshape(H_A // 2, 2 * BLK, 2 * BLK)


def _diff_bias(rel_bias, tq):
    assert tq >= T5_MAX_DISTANCE
    tab = rel_bias[:, N_GROUPS * H_A:].astype(F32)
    tab = (tab - tab[NUM_BUCKETS - 1:NUM_BUCKETS]) * LOG2E
    vec = tab[_t5_bucket(jnp.arange(tq))]
    return _toeplitz(vec.T.reshape(H_B, 1, tq), tq, diff=True)


def kernel(x, w_in, b_gate, lambda_q1, lambda_k1, lambda_q2, lambda_k2, subln_g, rel_bias, w_proj_a, w_proj_b,
           w_out, ln1_g, ln1_b, ln2_g, ln2_b, w_mlp1, w_mlp2):
    B, S, D = x.shape
    tq = min(TQ, S)
    tm = min(TM, S)
    w = w_in[0]
    w_a = w[:, :COLS_A].reshape(D, 3, N_GROUPS, W_A_OUT)
    w_a = jnp.concatenate([w_a[:, :1] * QK_SCALE, w_a[:, 1:]], axis=1).transpose(0, 2, 1, 3).reshape(D, COLS_A)
    w_qk = w[:, COLS_A:COLS_A + COLS_B_QK].reshape(D, 4, H_B, HEAD_DIM)
    w_qb = jnp.concatenate([w_qk[:, 0], w_qk[:, 1]], axis=-1).reshape(D, W_B_OUT)
    w_kb = jnp.concatenate([w_qk[:, 2], w_qk[:, 3]], axis=-1).reshape(D, W_B_OUT)
    w_vb = w[:, COLS_A + COLS_B_QK:COLS_A + COLS_B]
    w_n = jnp.concatenate([w_a, w_kb], axis=1).astype(BF16)
    w_t = jnp.concatenate([w_qb, w_vb], axis=1).T.astype(BF16)
    w_gate = w[:, COLS_A + COLS_B:].astype(BF16)

    *a_g, kb, pt = _proj(x, w_n, w_t, tm, tq)

    o_g, lse_g = [], []
    for g, (win, dil) in enumerate(DIL_PAIRS):
        assert win // dil == BLK
        o, lse = _dilated(a_g[g], _dilated_bias(rel_bias, g, dil), g)
        o_g.append(o)
        lse_g.append(lse)

    lam_init = 0.8 - 0.6 * math.exp(-0.3 * 0)
    lam_p = jnp.concatenate([lambda_q1, lambda_k1, lambda_q2, lambda_k2], axis=0).astype(F32)
    o_b = _diff(lam_p, kb, pt, _diff_bias(rel_bias, min(DIFF_T, S)), subln_g[0].reshape(DV_B, 1), lam_init)

    x1 = _mix(x, o_g, lse_g, o_b, w_gate, b_gate, w_proj_a[0].astype(BF16), w_proj_b[0].astype(BF16),
              w_out[0].astype(BF16), ln1_g, ln1_b, tm)
    return _mlp(x1, w_mlp1[0].astype(BF16), w_mlp2[0].astype(BF16), ln2_g, ln2_b, tm)
```

```python
import functools
import math

import jax
import jax.numpy as jnp
import numpy as np
from jax import lax
from jax.experimental import pallas as pl
from jax.experimental.pallas import tpu as pltpu

D_MODEL = 1024
HEAD_DIM = 64
DIL_PAIRS = ((128, 1), (512, 4), (2048, 16))
N_GROUPS = len(DIL_PAIRS)
H_A = 8
H_B = 8
DV_B = 2 * HEAD_DIM
D_FF = 4 * D_MODEL
NUM_BUCKETS = 32
T5_MAX_DISTANCE = 128
BLK = 128
LN_EPS = 1e-5
NEG_INF = -1e30
W_A_OUT = H_A * HEAD_DIM
W_B_OUT = H_B * DV_B
COLS_A = 3 * N_GROUPS * H_A * HEAD_DIM
COLS_B_QK = 4 * H_B * HEAD_DIM
COLS_B = COLS_B_QK + H_B * DV_B
DEPTH = 1
DEEPNORM_ALPHA = (2.0 * DEPTH) ** 0.25
QK_SCALE = HEAD_DIM ** -0.5
LOG2E = math.log2(math.e)

LANES = 128
COLS_N = COLS_A + H_B * 2 * HEAD_DIM
COLS_T = 2 * W_B_OUT
CHUNK = 512
TQ = 512
DIFF_T = 1024
STRIP = 512
PAIRS_PER_BLOCK = 2
ACC_SLOTS = PAIRS_PER_BLOCK
TM = 512
ROW_SLAB = 256
VMEM_LIMIT = 56 * 1024 * 1024

BF16 = jnp.bfloat16
F32 = jnp.float32


def _dot(a, b):
    return jnp.dot(a, b, preferred_element_type=F32)


def _dot_nt(a, b):
    return lax.dot_general(a, b, (((1,), (1,)), ((), ())), preferred_element_type=F32)


def _resident(shape):
    nd = len(shape)
    return pl.BlockSpec(shape, lambda *_: (0,) * nd, pipeline_mode=pl.Buffered(1))


def _params(*sem, flags=None):
    return pltpu.CompilerParams(dimension_semantics=sem, vmem_limit_bytes=VMEM_LIMIT, flags=flags)


def _proj_kernel(x_ref, wn_ref, wt_ref, a0_ref, a1_ref, a2_ref, kb_ref, ot_ref, rows_sc, *, tq):
    xb = x_ref[...].astype(BF16)
    tm = xb.shape[0]
    a_refs = (a0_ref, a1_ref, a2_ref)
    for c in range(COLS_N // CHUNK):
        r = _dot(xb, wn_ref[:, c * CHUNK:(c + 1) * CHUNK])
        g, part = divmod(c, 3)
        if g >= N_GROUPS:
            kb_ref[:, (c - 3 * N_GROUPS) * CHUNK:(c - 3 * N_GROUPS + 1) * CHUNK] = r.astype(BF16)
            continue
        sl = slice(part * W_A_OUT, (part + 1) * W_A_OUT)
        dil = DIL_PAIRS[g][1]
        if dil == 1:
            a_refs[g][0, :, sl] = r.astype(BF16)
            continue
        for k in range(CHUNK // LANES):
            rows_sc[k] = r[:, k * LANES:(k + 1) * LANES]
        for res in range(dil):
            for k in range(CHUNK // LANES):
                piece = rows_sc[k, pl.ds(res, tm // dil, stride=dil), :]
                a_refs[g][res, :, part * W_A_OUT + k * LANES:part * W_A_OUT + (k + 1) * LANES] = piece.astype(BF16)
    for c in range(COLS_T // CHUNK):
        sl = slice(c * CHUNK, (c + 1) * CHUNK)
        r = _dot_nt(wt_ref[sl, :], xb)
        if (c + 1) * CHUNK <= W_B_OUT:
            r = r * (QK_SCALE * LOG2E)
        r = r.astype(BF16)
        for t in range(tm // tq):
            ot_ref[t, sl, :] = r[:, t * tq:(t + 1) * tq]


def _proj(x, w_n, w_t, tm, tq):
    B, S, D = x.shape
    assert CHUNK == W_A_OUT
    dils = [d for _, d in DIL_PAIRS]
    return pl.pallas_call(
        functools.partial(_proj_kernel, tq=tq),
        grid=(B, S // tm),
        in_specs=[
            pl.BlockSpec((None, tm, D), lambda b, i: (b, i, 0)),
            _resident((D, COLS_N)),
            _resident((COLS_T, D)),
        ],
        out_specs=[pl.BlockSpec((None, d, tm // d, 3 * W_A_OUT), lambda b, i: (b, 0, i, 0)) for d in dils] + [
            pl.BlockSpec((None, tm, W_B_OUT), lambda b, i: (b, i, 0)),
            pl.BlockSpec((None, tm // tq, COLS_T, tq), lambda b, i: (b, i, 0, 0)),
        ],
        out_shape=[jax.ShapeDtypeStruct((B, d, S // d, 3 * W_A_OUT), BF16) for d in dils] + [
            jax.ShapeDtypeStruct((B, S, W_B_OUT), BF16),
            jax.ShapeDtypeStruct((B, S // tq, COLS_T, tq), BF16),
        ],
        scratch_shapes=[pltpu.VMEM((CHUNK // LANES, tm, LANES), F32)],
        compiler_params=_params("parallel", "parallel"),
        name="proj",
    )(x, w_n, w_t)


def _dilated_kernel(q_ref, k_ref, v_ref, kp_ref, vp_ref, bias_ref, o_ref, lse_ref, kw_sc, vw_sc):
    tb = q_ref.shape[0]
    first = pl.program_id(2) == 0
    lane = lax.broadcasted_iota(jnp.int32, (BLK, LANES), 1)
    lo = lane < HEAD_DIM
    in_prev = lax.broadcasted_iota(jnp.int32, (2 * BLK, 2 * BLK), 1) < BLK
    ones = jnp.ones((2 * BLK, LANES), BF16)
    kw_sc[:BLK], kw_sc[BLK:] = kp_ref[...], k_ref[...]
    vw_sc[:BLK], vw_sc[BLK:] = vp_ref[...], v_ref[...]
    for jb in range(tb // BLK):
        rows = slice(jb * BLK, (jb + 1) * BLK)
        win = slice(jb * BLK, (jb + 2) * BLK)
        for p in range(H_A // 2):
            cols = slice(p * LANES, (p + 1) * LANES)
            q2 = q_ref[rows, cols]
            zero = jnp.zeros_like(q2)
            qd = jnp.concatenate([jnp.where(lo, q2, zero), jnp.where(lo, zero, q2)], axis=0)
            s = _dot_nt(qd, kw_sc[win, cols]) + bias_ref[p]
            if jb == 0:
                s = jnp.where(first & in_prev, NEG_INF, s)
            m = jnp.max(s, axis=-1, keepdims=True)
            pw = jnp.exp((s - m).astype(BF16))
            oa = _dot(pw, jnp.concatenate([vw_sc[win, cols], ones], axis=1))
            den = oa[:, LANES:]
            o2 = oa[:, :LANES] / den
            lse = m + jnp.log(den)
            o_ref[rows, cols] = jnp.where(lo, o2[:BLK], o2[BLK:])
            lse_ref[rows, cols] = jnp.where(lo, lse[:BLK], lse[BLK:])


def _dilated(a, bias, g):
    B, dil, L, _ = a.shape
    tb = min(CHUNK, L)
    cur = lambda c: pl.BlockSpec((None, None, tb, W_A_OUT), lambda b, r, n: (b, r, n, c))
    prev = lambda c: pl.BlockSpec(
        (None, None, BLK, W_A_OUT), lambda b, r, n: (b, r, jnp.maximum(n * (tb // BLK) - 1, 0), c))
    out = pl.BlockSpec((None, None, tb, W_A_OUT), lambda b, r, n: (b, r, n, 0))
    return pl.pallas_call(
        _dilated_kernel,
        grid=(B, dil, L // tb),
        in_specs=[cur(0), cur(1), cur(2), prev(1), prev(2), _resident(bias.shape)],
        out_specs=[out, out],
        out_shape=[jax.ShapeDtypeStruct((B, dil, L, W_A_OUT), F32)] * 2,
        scratch_shapes=[pltpu.VMEM((BLK + tb, W_A_OUT), BF16)] * 2,
        compiler_params=_params("parallel", "parallel", "arbitrary"),
        name=f"dilated{g}",
    )(a, a, a, a, a, bias)


def _diff_kernel(lam_ref, qt_ref, k_ref, vt_ref, bias_ref, g_ref, o_ref, qd_sc, va_sc, s0_sc, s1_sc, x0_sc, x1_sc,
                 m_sc, acc_sc, *, lam_init):
    nb, _, tb = qt_ref.shape
    tq = tk = bias_ref.shape[-1]
    qb = tq // tb
    nq = nb // qb
    s_bufs, x_bufs = (s0_sc, s1_sc), (x0_sc, x1_sc)
    row = lax.broadcasted_iota(jnp.int32, (DV_B, tb), 0)
    for i in range(nb):
        blk, part = divmod(i, qb)
        qt = qt_ref[i]
        zero = jnp.zeros_like(qt)
        qd_sc[blk, :, part * tb:(part + 1) * tb] = jnp.where(row < HEAD_DIM, qt, zero)
        qd_sc[blk, :, tq + part * tb:tq + (part + 1) * tb] = jnp.where(row < HEAD_DIM, zero, qt)
        va_sc[blk, :DV_B, part * tb:(part + 1) * tb] = vt_ref[i]
    for i in range(nq):
        va_sc[i, DV_B:, :] = jnp.ones((va_sc.shape[1] - DV_B, tk), BF16)

    lam = (jnp.exp(jnp.sum(lam_ref[0:1, :] * lam_ref[1:2, :], axis=-1, keepdims=True))
           - jnp.exp(jnp.sum(lam_ref[2:3, :] * lam_ref[3:4, :], axis=-1, keepdims=True)) + lam_init)

    def reset(a):
        m_sc[a] = jnp.full(m_sc.shape[1:], -jnp.inf, F32)
        acc_sc[a] = jnp.zeros(acc_sc.shape[1:], F32)

    def finalize(qi):
        a = qi % ACC_SLOTS
        acc = acc_sc[a]
        on = acc[:DV_B] / acc[DV_B:DV_B + 1]
        o = on[:, :tq] - lam * on[:, tq:]
        ms = jnp.mean(o * o, axis=0, keepdims=True)
        o = o * lax.rsqrt(ms + LN_EPS) * g_ref[...] * (1.0 - lam_init)
        o_ref[pl.ds(pl.multiple_of(qi * tq, tq), tq), :] = o.T.astype(BF16)
        reset(a)

    def is_last(qi, j):
        return j == qi

    strips = [slice(c * STRIP, (c + 1) * STRIP) for c in range(2 * tq // STRIP)]

    def score_strip(kblk, qi, bi, cs):
        s = _dot(kblk, qd_sc[qi, :, cs])
        if bi is not None:
            bcs = slice(cs.start % tq, cs.start % tq + STRIP)
            s = s + bias_ref[bi, :, bcs].astype(F32)
        return s, jnp.max(s, axis=0, keepdims=True)

    for a in range(ACC_SLOTS):
        reset(a)
    for cs in strips:
        s_bufs[0][:, cs], x_bufs[0][:, cs] = score_strip(k_ref[0:tk, :], 0, 0, cs)

    def half(slot, qi, j):
        s_cur, x_cur, s_oth, x_oth = s_bufs[slot], x_bufs[slot], s_bufs[1 - slot], x_bufs[1 - slot]
        a = qi % ACC_SLOTS
        last = is_last(qi, j)
        j_n = jnp.where(last, 0, j + 1)
        qi_n = jnp.minimum(jnp.where(last, qi + 1, qi), nq - 1)
        near = j_n >= qi_n - 1
        kblk = k_ref[pl.ds(pl.multiple_of(j_n * tk, tk), tk), :]
        va = va_sc[j]

        def run(bi):
            for cs in strips:
                m_old = m_sc[a, :, cs]
                m_new = jnp.maximum(m_old, x_cur[:, cs])
                p = jnp.exp2((s_cur[:, cs] - m_new).astype(BF16))
                alpha = jnp.exp2(m_old - m_new)
                m_sc[a, :, cs] = m_new
                acc_sc[a, :, cs] = acc_sc[a, :, cs] * alpha + _dot(va, p)
                s_oth[:, cs], x_oth[:, cs] = score_strip(kblk, qi_n, bi, cs)

        pl.when(near)(functools.partial(run, qi_n - j_n))
        pl.when(jnp.logical_not(near))(functools.partial(run, None))
        return qi_n, j_n

    def run_pairs(count, carry):
        qi, j = carry
        done = []
        for h in range(count):
            done.append((qi, j))
            qi, j = half(h % 2, qi, j)
        for q, jj in done:
            pl.when(is_last(q, jj))(functools.partial(finalize, q))
        return qi, j

    npairs = nq * (nq + 1) // 2
    carry = lax.fori_loop(0, npairs // PAIRS_PER_BLOCK, lambda _, c: run_pairs(PAIRS_PER_BLOCK, c),
                          (jnp.int32(0), jnp.int32(0)))
    if npairs % PAIRS_PER_BLOCK:
        run_pairs(npairs % PAIRS_PER_BLOCK, carry)


def _diff(lam_p, kb, pt, bias, g_col, lam_init):
    B, S, _ = kb.shape
    nb, tb = pt.shape[1], pt.shape[3]
    tq = tk = bias.shape[-1]
    nq = S // tq
    assert tq % tb == 0 and tq % STRIP == 0 and bias.shape[1:] == (2, tk, tq)
    va_rows = DV_B + 16
    return pl.pallas_call(
        functools.partial(_diff_kernel, lam_init=lam_init),
        grid=(B, H_B),
        in_specs=[
            _resident(lam_p.shape),
            pl.BlockSpec((None, nb, DV_B, tb), lambda b, h: (b, 0, h, 0)),
            pl.BlockSpec((None, S, LANES), lambda b, h: (b, 0, h)),
            pl.BlockSpec((None, nb, DV_B, tb), lambda b, h: (b, 0, H_B + h, 0)),
            pl.BlockSpec((None, 2, tk, tq), lambda b, h: (h, 0, 0, 0)),
            _resident(g_col.shape),
        ],
        out_specs=pl.BlockSpec((None, S, DV_B), lambda b, h: (b, 0, h)),
        out_shape=jax.ShapeDtypeStruct((B, S, W_B_OUT), BF16),
        scratch_shapes=[
            pltpu.VMEM((nq, DV_B, 2 * tq), BF16),
            pltpu.VMEM((S // tk, va_rows, tk), BF16),
            pltpu.VMEM((tk, 2 * tq), F32),
            pltpu.VMEM((tk, 2 * tq), F32),
            pltpu.VMEM((1, 2 * tq), F32),
            pltpu.VMEM((1, 2 * tq), F32),
            pltpu.VMEM((ACC_SLOTS, 1, 2 * tq), F32),
            pltpu.VMEM((ACC_SLOTS, va_rows, 2 * tq), F32),
        ],
        compiler_params=_params("parallel", "parallel"),
        name="diff_attn",
    )(lam_p, pt, kb, pt, bias, g_col)


def _toeplitz_kernel(w_ref, o_ref, *, diff):
    width = o_ref.shape[-1]
    rows = width if diff else o_ref.shape[-2]
    t = pltpu.roll(jnp.broadcast_to(w_ref[...], (rows, width)), 0, 1, stride=1, stride_axis=0)
    if not diff:
        o_ref[...] = t
        return
    r = lax.broadcasted_iota(jnp.int32, (rows, width), 0)
    c = lax.broadcasted_iota(jnp.int32, (rows, width), 1)
    o_ref[0] = jnp.where(c >= r, t, NEG_INF).astype(o_ref.dtype)
    o_ref[1] = jnp.where(c < r, t, 0.0).astype(o_ref.dtype)


def _toeplitz(w, rows, diff):
    n, _, width = w.shape
    oshape = (n, 2, rows, width) if diff else (n, rows, width)
    oblock = (None,) + oshape[1:]
    return pl.pallas_call(
        functools.partial(_toeplitz_kernel, diff=diff),
        grid=(n,),
        in_specs=[pl.BlockSpec((None, 1, width), lambda i: (i, 0, 0))],
        out_specs=pl.BlockSpec(oblock, lambda i: (i,) + (0,) * (len(oshape) - 1)),
        out_shape=jax.ShapeDtypeStruct(oshape, BF16 if diff else F32),
        compiler_params=_params("parallel"),
        name="bias_diff" if diff else "bias_dilated",
    )(w)


def _layer_norm(h, g, b):
    mu = jnp.mean(h, axis=-1, keepdims=True)
    d = h - mu
    var = jnp.mean(d * d, axis=-1, keepdims=True)
    return d * lax.rsqrt(var + LN_EPS) * g + b


def _mix_kernel(x_ref, o0_ref, o1_ref, o2_ref, l0_ref, l1_ref, l2_ref, ob_ref, wg_ref, bg_ref, wa_ref, wb_ref,
                wo_ref, g_ref, b_ref, out_ref, *tok_sc):
    def token_major(ref, sc):
        dil, rows, _ = ref.shape
        if dil == 1:
            return lambda sl: ref[0, sl]
        nk = sc.shape[0]
        for res in range(dil):
            for k in range(nk):
                sc[k, pl.ds(res, rows, stride=dil), :] = ref[res, :, k * LANES:(k + 1) * LANES]
        return lambda sl: jnp.concatenate([sc[k, sl] for k in range(nk)], axis=1)

    l_g = (token_major(l0_ref, None), token_major(l1_ref, tok_sc[0]), token_major(l2_ref, tok_sc[1]))
    o_g = (token_major(o0_ref, None), token_major(o1_ref, tok_sc[2]), token_major(o2_ref, tok_sc[3]))
    tm = x_ref.shape[0]
    for r0 in range(0, tm, ROW_SLAB):
        sl = slice(r0, r0 + ROW_SLAB)
        x = x_ref[sl]
        xb = x.astype(BF16)
        l0, l1, l2 = (f(sl) for f in l_g)
        mx = jnp.maximum(jnp.maximum(l0, l1), l2)
        e0, e1, e2 = jnp.exp(l0 - mx), jnp.exp(l1 - mx), jnp.exp(l2 - mx)
        o_a = (e0 * o_g[0](sl) + e1 * o_g[1](sl) + e2 * o_g[2](sl)) / (e0 + e1 + e2)
        y_a = _dot(o_a.astype(BF16), wa_ref[...])
        y_b = _dot(ob_ref[sl], wb_ref[...])
        gate_a = jax.nn.sigmoid(_dot(xb, wg_ref[:, :D_MODEL]) + bg_ref[:, :D_MODEL])
        gate_b = jax.nn.sigmoid(_dot(xb, wg_ref[:, D_MODEL:]) + bg_ref[:, D_MODEL:])
        merged = gate_a * y_a + gate_b * y_b
        mix = _dot(merged.astype(BF16), wo_ref[...])
        out_ref[sl] = _layer_norm(DEEPNORM_ALPHA * x + mix, g_ref[...], b_ref[...])


def _mix(x, o_g, lse_g, o_b, w_gate, b_gate, w_a, w_b, w_o, ln_g, ln_b, tm):
    B, S, D = x.shape
    tok = lambda w: pl.BlockSpec((None, tm, w), lambda b, i: (b, i, 0))
    res = [pl.BlockSpec((None, a.shape[1], tm // a.shape[1], W_A_OUT), lambda b, i: (b, 0, i, 0)) for a in o_g]
    n_sc = 2 * sum(a.shape[1] > 1 for a in o_g)
    return pl.pallas_call(
        _mix_kernel,
        grid=(B, S // tm),
        in_specs=[tok(D)] + res + res + [tok(W_B_OUT)]
        + [_resident(a.shape) for a in (w_gate, b_gate, w_a, w_b, w_o, ln_g, ln_b)],
        out_specs=tok(D),
        out_shape=jax.ShapeDtypeStruct((B, S, D), F32),
        scratch_shapes=[pltpu.VMEM((W_A_OUT // LANES, tm, LANES), F32)] * n_sc,
        compiler_params=_params("parallel", "parallel"),
        name="mix",
    )(x, *o_g, *lse_g, o_b, w_gate, b_gate, w_a, w_b, w_o, ln_g, ln_b)


def _mlp_kernel(x_ref, w1_ref, w2_ref, g_ref, b_ref, out_ref):
    for r0 in range(0, x_ref.shape[0], ROW_SLAB):
        rows = slice(r0, r0 + ROW_SLAB)
        x = x_ref[rows]
        xb = x.astype(BF16)
        ff = jnp.zeros(x.shape, F32)
        for c in range(D_FF // (2 * CHUNK)):
            sl = slice(c * 2 * CHUNK, (c + 1) * 2 * CHUNK)
            h = jnp.maximum(_dot(xb, w1_ref[:, sl]), 0.0)
            ff = ff + _dot((h * h).astype(BF16), w2_ref[sl, :])
        out_ref[rows] = _layer_norm(DEEPNORM_ALPHA * x + ff, g_ref[...], b_ref[...])


def _mlp(x, w1, w2, ln_g, ln_b, tm):
    B, S, D = x.shape
    tok = pl.BlockSpec((None, tm, D), lambda b, i: (b, i, 0))
    return pl.pallas_call(
        _mlp_kernel,
        grid=(B, S // tm),
        in_specs=[tok] + [_resident(a.shape) for a in (w1, w2, ln_g, ln_b)],
        out_specs=tok,
        out_shape=jax.ShapeDtypeStruct((B, S, D), F32),
        compiler_params=_params("parallel", "parallel"),
        name="mlp",
    )(x, w1, w2, ln_g, ln_b)


def _t5_bucket(dist):
    n = jnp.maximum(dist, 0)
    max_exact = NUM_BUCKETS // 2
    nf = jnp.maximum(n, 1).astype(F32)
    large = max_exact + (jnp.log(nf / max_exact) / math.log(T5_MAX_DISTANCE / max_exact)
                         * (NUM_BUCKETS - max_exact)).astype(jnp.int32)
    large = jnp.minimum(large, NUM_BUCKETS - 1)
    return jnp.where(n < max_exact, n, large)


def _dilated_bias(rel_bias, g, dil):
    j = jnp.arange(2 * BLK)
    tab = rel_bias[:, g * H_A:(g + 1) * H_A].astype(F32)
    vec = tab[_t5_bucket(jnp.maximum(BLK - j, 0) * dil)]
    vec = jnp.where((j <= BLK)[:, None], vec, NEG_INF)
    bias = _toeplitz(vec.T.reshape(H_A, 1, 2 * BLK), BLK, diff=False)
    return bias.reshape(H_A // 2, 2 * BLK, 2 * BLK)


def _diff_bias(rel_bias, tq):
    assert tq >= T5_MAX_DISTANCE
    tab = rel_bias[:, N_GROUPS * H_A:].astype(F32)
    tab = (tab - tab[NUM_BUCKETS - 1:NUM_BUCKETS]) * LOG2E
    vec = tab[_t5_bucket(jnp.arange(tq))]
    return _toeplitz(vec.T.reshape(H_B, 1, tq), tq, diff=True)


def kernel(x, w_in, b_gate, lambda_q1, lambda_k1, lambda_q2, lambda_k2, subln_g, rel_bias, w_proj_a, w_proj_b,
           w_out, ln1_g, ln1_b, ln2_g, ln2_b, w_mlp1, w_mlp2):
    B, S, D = x.shape
    tq = min(TQ, S)
    tm = min(TM, S)
    w = w_in[0]
    w_a = w[:, :COLS_A].reshape(D, 3, N_GROUPS, W_A_OUT)
    w_a = jnp.concatenate([w_a[:, :1] * QK_SCALE, w_a[:, 1:]], axis=1).transpose(0, 2, 1, 3).reshape(D, COLS_A)
    w_qk = w[:, COLS_A:COLS_A + COLS_B_QK].reshape(D, 4, H_B, HEAD_DIM)
    w_qb = jnp.concatenate([w_qk[:, 0], w_qk[:, 1]], axis=-1).reshape(D, W_B_OUT)
    w_kb = jnp.concatenate([w_qk[:, 2], w_qk[:, 3]], axis=-1).reshape(D, W_B_OUT)
    w_vb = w[:, COLS_A + COLS_B_QK:COLS_A + COLS_B]
    w_n = jnp.concatenate([w_a, w_kb], axis=1).astype(BF16)
    w_t = jnp.concatenate([w_qb, w_vb], axis=1).T.astype(BF16)
    w_gate = w[:, COLS_A + COLS_B:].astype(BF16)

    *a_g, kb, pt = _proj(x, w_n, w_t, tm, tq)

    o_g, lse_g = [], []
    for g, (win, dil) in enumerate(DIL_PAIRS):
        assert win // dil == BLK
        o, lse = _dilated(a_g[g], _dilated_bias(rel_bias, g, dil), g)
        o_g.append(o)
        lse_g.append(lse)

    lam_init = 0.8 - 0.6 * math.exp(-0.3 * 0)
    lam_p = jnp.concatenate([lambda_q1, lambda_k1, lambda_q2, lambda_k2], axis=0).astype(F32)
    o_b = _diff(lam_p, kb, pt, _diff_bias(rel_bias, min(DIFF_T, S)), subln_g[0].reshape(DV_B, 1), lam_init)

    x1 = _mix(x, o_g, lse_g, o_b, w_gate, b_gate, w_proj_a[0].astype(BF16), w_proj_b[0].astype(BF16),
              w_out[0].astype(BF16), ln1_g, ln1_b, tm)
    return _mlp(x1, w_mlp1[0].astype(BF16), w_mlp2[0].astype(BF16), ln2_g, ln2_b, tm)
```

```python
import functools
import math

import jax
import jax.numpy as jnp
import numpy as np
from jax import lax
from jax.experimental import pallas as pl
from jax.experimental.pallas import tpu as pltpu

D_MODEL = 1024
HEAD_DIM = 64
DIL_PAIRS = ((128, 1), (512, 4), (2048, 16))
N_GROUPS = len(DIL_PAIRS)
H_A = 8
H_B = 8
DV_B = 2 * HEAD_DIM
D_FF = 4 * D_MODEL
NUM_BUCKETS = 32
T5_MAX_DISTANCE = 128
BLK = 128
LN_EPS = 1e-5
NEG_INF = -1e30
W_A_OUT = H_A * HEAD_DIM
W_B_OUT = H_B * DV_B
COLS_A = 3 * N_GROUPS * H_A * HEAD_DIM
COLS_B_QK = 4 * H_B * HEAD_DIM
COLS_B = COLS_B_QK + H_B * DV_B
DEPTH = 1
DEEPNORM_ALPHA = (2.0 * DEPTH) ** 0.25
QK_SCALE = HEAD_DIM ** -0.5
LOG2E = math.log2(math.e)

LANES = 128
COLS_N = COLS_A + H_B * 2 * HEAD_DIM
COLS_T = 2 * W_B_OUT
CHUNK = 512
TQ = 512
DIFF_T = 1024
STRIP = 512
PAIRS_PER_BLOCK = 2
ACC_SLOTS = PAIRS_PER_BLOCK
TM = 512
ROW_SLAB = 256
DILATED_ROWS = 1024
VMEM_LIMIT = 56 * 1024 * 1024

BF16 = jnp.bfloat16
F32 = jnp.float32


def _dot(a, b):
    return jnp.dot(a, b, preferred_element_type=F32)


def _dot_nt(a, b):
    return lax.dot_general(a, b, (((1,), (1,)), ((), ())), preferred_element_type=F32)


def _resident(shape):
    nd = len(shape)
    return pl.BlockSpec(shape, lambda *_: (0,) * nd, pipeline_mode=pl.Buffered(1))


def _params(*sem, flags=None):
    return pltpu.CompilerParams(dimension_semantics=sem, vmem_limit_bytes=VMEM_LIMIT, flags=flags)


def _proj_kernel(x_ref, wn_ref, wt_ref, a0_ref, a1_ref, a2_ref, kb_ref, ot_ref, rows_sc, *, tq):
    xb = x_ref[...].astype(BF16)
    tm = xb.shape[0]
    a_refs = (a0_ref, a1_ref, a2_ref)
    for c in range(COLS_N // CHUNK):
        r = _dot(xb, wn_ref[:, c * CHUNK:(c + 1) * CHUNK])
        g, part = divmod(c, 3)
        if g >= N_GROUPS:
            kb_ref[:, (c - 3 * N_GROUPS) * CHUNK:(c - 3 * N_GROUPS + 1) * CHUNK] = r.astype(BF16)
            continue
        sl = slice(part * W_A_OUT, (part + 1) * W_A_OUT)
        dil = DIL_PAIRS[g][1]
        if dil == 1:
            a_refs[g][0, :, sl] = r.astype(BF16)
            continue
        for k in range(CHUNK // LANES):
            rows_sc[k] = r[:, k * LANES:(k + 1) * LANES]
        for res in range(dil):
            for k in range(CHUNK // LANES):
                piece = rows_sc[k, pl.ds(res, tm // dil, stride=dil), :]
                a_refs[g][res, :, part * W_A_OUT + k * LANES:part * W_A_OUT + (k + 1) * LANES] = piece.astype(BF16)
    for c in range(COLS_T // CHUNK):
        sl = slice(c * CHUNK, (c + 1) * CHUNK)
        r = _dot_nt(wt_ref[sl, :], xb)
        if (c + 1) * CHUNK <= W_B_OUT:
            r = r * (QK_SCALE * LOG2E)
        r = r.astype(BF16)
        for t in range(tm // tq):
            ot_ref[t, sl, :] = r[:, t * tq:(t + 1) * tq]


def _proj(x, w_n, w_t, tm, tq):
    B, S, D = x.shape
    assert CHUNK == W_A_OUT
    dils = [d for _, d in DIL_PAIRS]
    return pl.pallas_call(
        functools.partial(_proj_kernel, tq=tq),
        grid=(B, S // tm),
        in_specs=[
            pl.BlockSpec((None, tm, D), lambda b, i: (b, i, 0)),
            _resident((D, COLS_N)),
            _resident((COLS_T, D)),
        ],
        out_specs=[pl.BlockSpec((None, d, tm // d, 3 * W_A_OUT), lambda b, i: (b, 0, i, 0)) for d in dils] + [
            pl.BlockSpec((None, tm, W_B_OUT), lambda b, i: (b, i, 0)),
            pl.BlockSpec((None, tm // tq, COLS_T, tq), lambda b, i: (b, i, 0, 0)),
        ],
        out_shape=[jax.ShapeDtypeStruct((B, d, S // d, 3 * W_A_OUT), BF16) for d in dils] + [
            jax.ShapeDtypeStruct((B, S, W_B_OUT), BF16),
            jax.ShapeDtypeStruct((B, S // tq, COLS_T, tq), BF16),
        ],
        scratch_shapes=[pltpu.VMEM((CHUNK // LANES, tm, LANES), F32)],
        compiler_params=_params("parallel", "parallel"),
        name="proj",
    )(x, w_n, w_t)


def _dilated_kernel(q_ref, k_ref, v_ref, kp_ref, vp_ref, bias_ref, o_ref, lse_ref, kw_sc, vw_sc):
    tb = q_ref.shape[0]
    first = pl.program_id(2) == 0
    lane = lax.broadcasted_iota(jnp.int32, (BLK, LANES), 1)
    lo = lane < HEAD_DIM
    in_prev = lax.broadcasted_iota(jnp.int32, (2 * BLK, 2 * BLK), 1) < BLK
    ones = jnp.ones((2 * BLK, LANES), BF16)
    kw_sc[:BLK], kw_sc[BLK:] = kp_ref[...], k_ref[...]
    vw_sc[:BLK], vw_sc[BLK:] = vp_ref[...], v_ref[...]
    for jb in range(tb // BLK):
        rows = slice(jb * BLK, (jb + 1) * BLK)
        win = slice(jb * BLK, (jb + 2) * BLK)
        for p in range(H_A // 2):
            cols = slice(p * LANES, (p + 1) * LANES)
            q2 = q_ref[rows, cols]
            zero = jnp.zeros_like(q2)
            qd = jnp.concatenate([jnp.where(lo, q2, zero), jnp.where(lo, zero, q2)], axis=0)
            s = _dot_nt(qd, kw_sc[win, cols]) + bias_ref[p]
            if jb == 0:
                s = jnp.where(first & in_prev, NEG_INF, s)
            m = jnp.max(s, axis=-1, keepdims=True)
            pw = jnp.exp((s - m).astype(BF16))
            oa = _dot(pw, jnp.concatenate([vw_sc[win, cols], ones], axis=1))
            den = oa[:, LANES:]
            o2 = oa[:, :LANES] / den
            lse = m + jnp.log(den)
            o_ref[rows, cols] = jnp.where(lo, o2[:BLK], o2[BLK:])
            lse_ref[rows, cols] = jnp.where(lo, lse[:BLK], lse[BLK:])


def _dilated(a, bias, g):
    B, dil, L, _ = a.shape
    tb = min(DILATED_ROWS, L)
    cur = lambda c: pl.BlockSpec((None, None, tb, W_A_OUT), lambda b, r, n: (b, r, n, c))
    prev = lambda c: pl.BlockSpec(
        (None, None, BLK, W_A_OUT), lambda b, r, n: (b, r, jnp.maximum(n * (tb // BLK) - 1, 0), c))
    out = pl.BlockSpec((None, None, tb, W_A_OUT), lambda b, r, n: (b, r, n, 0))
    return pl.pallas_call(
        _dilated_kernel,
        grid=(B, dil, L // tb),
        in_specs=[cur(0), cur(1), cur(2), prev(1), prev(2), _resident(bias.shape)],
        out_specs=[out, out],
        out_shape=[jax.ShapeDtypeStruct((B, dil, L, W_A_OUT), F32)] * 2,
        scratch_shapes=[pltpu.VMEM((BLK + tb, W_A_OUT), BF16)] * 2,
        compiler_params=_params("parallel", "parallel", "arbitrary"),
        name=f"dilated{g}",
    )(a, a, a, a, a, bias)


def _diff_kernel(lam_ref, qt_ref, k_ref, vt_ref, bias_ref, g_ref, o_ref, qd_sc, va_sc, s0_sc, s1_sc, x0_sc, x1_sc,
                 m_sc, acc_sc, *, lam_init):
    nb, _, tb = qt_ref.shape
    tq = tk = bias_ref.shape[-1]
    qb = tq // tb
    nq = nb // qb
    s_bufs, x_bufs = (s0_sc, s1_sc), (x0_sc, x1_sc)
    row = lax.broadcasted_iota(jnp.int32, (DV_B, tb), 0)
    for i in range(nb):
        blk, part = divmod(i, qb)
        qt = qt_ref[i]
        zero = jnp.zeros_like(qt)
        qd_sc[blk, :, part * tb:(part + 1) * tb] = jnp.where(row < HEAD_DIM, qt, zero)
        qd_sc[blk, :, tq + part * tb:tq + (part + 1) * tb] = jnp.where(row < HEAD_DIM, zero, qt)
        va_sc[blk, :DV_B, part * tb:(part + 1) * tb] = vt_ref[i]
    for i in range(nq):
        va_sc[i, DV_B:, :] = jnp.ones((va_sc.shape[1] - DV_B, tk), BF16)

    lam = (jnp.exp(jnp.sum(lam_ref[0:1, :] * lam_ref[1:2, :], axis=-1, keepdims=True))
           - jnp.exp(jnp.sum(lam_ref[2:3, :] * lam_ref[3:4, :], axis=-1, keepdims=True)) + lam_init)

    def reset(a):
        m_sc[a] = jnp.full(m_sc.shape[1:], -jnp.inf, F32)
        acc_sc[a] = jnp.zeros(acc_sc.shape[1:], F32)

    def finalize(qi):
        a = qi % ACC_SLOTS
        acc = acc_sc[a]
        on = acc[:DV_B] / acc[DV_B:DV_B + 1]
        o = on[:, :tq] - lam * on[:, tq:]
        ms = jnp.mean(o * o, axis=0, keepdims=True)
        o = o * lax.rsqrt(ms + LN_EPS) * g_ref[...] * (1.0 - lam_init)
        o_ref[pl.ds(pl.multiple_of(qi * tq, tq), tq), :] = o.T.astype(BF16)
        reset(a)

    def is_last(qi, j):
        return j == qi

    strips = [slice(c * STRIP, (c + 1) * STRIP) for c in range(2 * tq // STRIP)]

    def score_strip(kblk, qi, bi, cs):
        s = _dot(kblk, qd_sc[qi, :, cs])
        if bi is not None:
            bcs = slice(cs.start % tq, cs.start % tq + STRIP)
            s = s + bias_ref[bi, :, bcs].astype(F32)
        return s, jnp.max(s, axis=0, keepdims=True)

    for a in range(ACC_SLOTS):
        reset(a)
    for cs in strips:
        s_bufs[0][:, cs], x_bufs[0][:, cs] = score_strip(k_ref[0:tk, :], 0, 0, cs)

    def half(slot, qi, j):
        s_cur, x_cur, s_oth, x_oth = s_bufs[slot], x_bufs[slot], s_bufs[1 - slot], x_bufs[1 - slot]
        a = qi % ACC_SLOTS
        last = is_last(qi, j)
        j_n = jnp.where(last, 0, j + 1)
        qi_n = jnp.minimum(jnp.where(last, qi + 1, qi), nq - 1)
        near = j_n >= qi_n - 1
        kblk = k_ref[pl.ds(pl.multiple_of(j_n * tk, tk), tk), :]
        va = va_sc[j]

        def run(bi):
            for cs in strips:
                m_old = m_sc[a, :, cs]
                m_new = jnp.maximum(m_old, x_cur[:, cs])
                p = jnp.exp2((s_cur[:, cs] - m_new).astype(BF16))
                alpha = jnp.exp2(m_old - m_new)
                m_sc[a, :, cs] = m_new
                acc_sc[a, :, cs] = acc_sc[a, :, cs] * alpha + _dot(va, p)
                s_oth[:, cs], x_oth[:, cs] = score_strip(kblk, qi_n, bi, cs)

        pl.when(near)(functools.partial(run, qi_n - j_n))
        pl.when(jnp.logical_not(near))(functools.partial(run, None))
        return qi_n, j_n

    def run_pairs(count, carry):
        qi, j = carry
        done = []
        for h in range(count):
            done.append((qi, j))
            qi, j = half(h % 2, qi, j)
        for q, jj in done:
            pl.when(is_last(q, jj))(functools.partial(finalize, q))
        return qi, j

    npairs = nq * (nq + 1) // 2
    carry = lax.fori_loop(0, npairs // PAIRS_PER_BLOCK, lambda _, c: run_pairs(PAIRS_PER_BLOCK, c),
                          (jnp.int32(0), jnp.int32(0)))
    if npairs % PAIRS_PER_BLOCK:
        run_pairs(npairs % PAIRS_PER_BLOCK, carry)


def _diff(lam_p, kb, pt, bias, g_col, lam_init):
    B, S, _ = kb.shape
    nb, tb = pt.shape[1], pt.shape[3]
    tq = tk = bias.shape[-1]
    nq = S // tq
    assert tq % tb == 0 and tq % STRIP == 0 and bias.shape[1:] == (2, tk, tq)
    va_rows = DV_B + 16
    return pl.pallas_call(
        functools.partial(_diff_kernel, lam_init=lam_init),
        grid=(B, H_B),
        in_specs=[
            _resident(lam_p.shape),
            pl.BlockSpec((None, nb, DV_B, tb), lambda b, h: (b, 0, h, 0)),
            pl.BlockSpec((None, S, LANES), lambda b, h: (b, 0, h)),
            pl.BlockSpec((None, nb, DV_B, tb), lambda b, h: (b, 0, H_B + h, 0)),
            pl.BlockSpec((None, 2, tk, tq), lambda b, h: (h, 0, 0, 0)),
            _resident(g_col.shape),
        ],
        out_specs=pl.BlockSpec((None, S, DV_B), lambda b, h: (b, 0, h)),
        out_shape=jax.ShapeDtypeStruct((B, S, W_B_OUT), BF16),
        scratch_shapes=[
            pltpu.VMEM((nq, DV_B, 2 * tq), BF16),
            pltpu.VMEM((S // tk, va_rows, tk), BF16),
            pltpu.VMEM((tk, 2 * tq), F32),
            pltpu.VMEM((tk, 2 * tq), F32),
            pltpu.VMEM((1, 2 * tq), F32),
            pltpu.VMEM((1, 2 * tq), F32),
            pltpu.VMEM((ACC_SLOTS, 1, 2 * tq), F32),
            pltpu.VMEM((ACC_SLOTS, va_rows, 2 * tq), F32),
        ],
        compiler_params=_params("parallel", "parallel"),
        name="diff_attn",
    )(lam_p, pt, kb, pt, bias, g_col)


def _toeplitz_kernel(w_ref, o_ref, *, diff):
    width = o_ref.shape[-1]
    rows = width if diff else o_ref.shape[-2]
    t = pltpu.roll(jnp.broadcast_to(w_ref[...], (rows, width)), 0, 1, stride=1, stride_axis=0)
    if not diff:
        o_ref[...] = t
        return
    r = lax.broadcasted_iota(jnp.int32, (rows, width), 0)
    c = lax.broadcasted_iota(jnp.int32, (rows, width), 1)
    o_ref[0] = jnp.where(c >= r, t, NEG_INF).astype(o_ref.dtype)
    o_ref[1] = jnp.where(c < r, t, 0.0).astype(o_ref.dtype)


def _toeplitz(w, rows, diff):
    n, _, width = w.shape
    oshape = (n, 2, rows, width) if diff else (n, rows, width)
    oblock = (None,) + oshape[1:]
    return pl.pallas_call(
        functools.partial(_toeplitz_kernel, diff=diff),
        grid=(n,),
        in_specs=[pl.BlockSpec((None, 1, width), lambda i: (i, 0, 0))],
        out_specs=pl.BlockSpec(oblock, lambda i: (i,) + (0,) * (len(oshape) - 1)),
        out_shape=jax.ShapeDtypeStruct(oshape, BF16 if diff else F32),
        compiler_params=_params("parallel"),
        name="bias_diff" if diff else "bias_dilated",
    )(w)


def _layer_norm(h, g, b):
    mu = jnp.mean(h, axis=-1, keepdims=True)
    d = h - mu
    var = jnp.mean(d * d, axis=-1, keepdims=True)
    return d * lax.rsqrt(var + LN_EPS) * g + b


def _mix_kernel(x_ref, o0_ref, o1_ref, o2_ref, l0_ref, l1_ref, l2_ref, ob_ref, wg_ref, bg_ref, wa_ref, wb_ref,
                wo_ref, g_ref, b_ref, out_ref, *tok_sc):
    def token_major(ref, sc):
        dil, rows, _ = ref.shape
        if dil == 1:
            return lambda sl: ref[0, sl]
        nk = sc.shape[0]
        for res in range(dil):
            for k in range(nk):
                sc[k, pl.ds(res, rows, stride=dil), :] = ref[res, :, k * LANES:(k + 1) * LANES]
        return lambda sl: jnp.concatenate([sc[k, sl] for k in range(nk)], axis=1)

    l_g = (token_major(l0_ref, None), token_major(l1_ref, tok_sc[0]), token_major(l2_ref, tok_sc[1]))
    o_g = (token_major(o0_ref, None), token_major(o1_ref, tok_sc[2]), token_major(o2_ref, tok_sc[3]))
    tm = x_ref.shape[0]
    for r0 in range(0, tm, ROW_SLAB):
        sl = slice(r0, r0 + ROW_SLAB)
        x = x_ref[sl]
        xb = x.astype(BF16)
        l0, l1, l2 = (f(sl) for f in l_g)
        mx = jnp.maximum(jnp.maximum(l0, l1), l2)
        e0, e1, e2 = jnp.exp(l0 - mx), jnp.exp(l1 - mx), jnp.exp(l2 - mx)
        o_a = (e0 * o_g[0](sl) + e1 * o_g[1](sl) + e2 * o_g[2](sl)) / (e0 + e1 + e2)
        y_a = _dot(o_a.astype(BF16), wa_ref[...])
        y_b = _dot(ob_ref[sl], wb_ref[...])
        gate_a = jax.nn.sigmoid(_dot(xb, wg_ref[:, :D_MODEL]) + bg_ref[:, :D_MODEL])
        gate_b = jax.nn.sigmoid(_dot(xb, wg_ref[:, D_MODEL:]) + bg_ref[:, D_MODEL:])
        merged = gate_a * y_a + gate_b * y_b
        mix = _dot(merged.astype(BF16), wo_ref[...])
        out_ref[sl] = _layer_norm(DEEPNORM_ALPHA * x + mix, g_ref[...], b_ref[...])


def _mix(x, o_g, lse_g, o_b, w_gate, b_gate, w_a, w_b, w_o, ln_g, ln_b, tm):
    B, S, D = x.shape
    tok = lambda w: pl.BlockSpec((None, tm, w), lambda b, i: (b, i, 0))
    res = [pl.BlockSpec((None, a.shape[1], tm // a.shape[1], W_A_OUT), lambda b, i: (b, 0, i, 0)) for a in o_g]
    n_sc = 2 * sum(a.shape[1] > 1 for a in o_g)
    return pl.pallas_call(
        _mix_kernel,
        grid=(B, S // tm),
        in_specs=[tok(D)] + res + res + [tok(W_B_OUT)]
        + [_resident(a.shape) for a in (w_gate, b_gate, w_a, w_b, w_o, ln_g, ln_b)],
        out_specs=tok(D),
        out_shape=jax.ShapeDtypeStruct((B, S, D), F32),
        scratch_shapes=[pltpu.VMEM((W_A_OUT // LANES, tm, LANES), F32)] * n_sc,
        compiler_params=_params("parallel", "parallel"),
        name="mix",
    )(x, *o_g, *lse_g, o_b, w_gate, b_gate, w_a, w_b, w_o, ln_g, ln_b)


def _mlp_kernel(x_ref, w1_ref, w2_ref, g_ref, b_ref, out_ref):
    for r0 in range(0, x_ref.shape[0], ROW_SLAB):
        rows = slice(r0, r0 + ROW_SLAB)
        x = x_ref[rows]
        xb = x.astype(BF16)
        ff = jnp.zeros(x.shape, F32)
        for c in range(D_FF // (2 * CHUNK)):
            sl = slice(c * 2 * CHUNK, (c + 1) * 2 * CHUNK)
            h = jnp.maximum(_dot(xb, w1_ref[:, sl]), 0.0)
            ff = ff + _dot((h * h).astype(BF16), w2_ref[sl, :])
        out_ref[rows] = _layer_norm(DEEPNORM_ALPHA * x + ff, g_ref[...], b_ref[...])


def _mlp(x, w1, w2, ln_g, ln_b, tm):
    B, S, D = x.shape
    tok = pl.BlockSpec((None, tm, D), lambda b, i: (b, i, 0))
    return pl.pallas_call(
        _mlp_kernel,
        grid=(B, S // tm),
        in_specs=[tok] + [_resident(a.shape) for a in (w1, w2, ln_g, ln_b)],
        out_specs=tok,
        out_shape=jax.ShapeDtypeStruct((B, S, D), F32),
        compiler_params=_params("parallel", "parallel"),
        name="mlp",
    )(x, w1, w2, ln_g, ln_b)


def _t5_bucket(dist):
    n = jnp.maximum(dist, 0)
    max_exact = NUM_BUCKETS // 2
    nf = jnp.maximum(n, 1).astype(F32)
    large = max_exact + (jnp.log(nf / max_exact) / math.log(T5_MAX_DISTANCE / max_exact)
                         * (NUM_BUCKETS - max_exact)).astype(jnp.int32)
    large = jnp.minimum(large, NUM_BUCKETS - 1)
    return jnp.where(n < max_exact, n, large)


def _dilated_bias(rel_bias, g, dil):
    j = jnp.arange(2 * BLK)
    tab = rel_bias[:, g * H_A:(g + 1) * H_A].astype(F32)
    vec = tab[_t5_bucket(jnp.maximum(BLK - j, 0) * dil)]
    vec = jnp.where((j <= BLK)[:, None], vec, NEG_INF)
    bias = _toeplitz(vec.T.reshape(H_A, 1, 2 * BLK), BLK, diff=False)
    return bias.reshape(H_A // 2, 2 * BLK, 2 * BLK)


def _diff_bias(rel_bias, tq):
    assert tq >= T5_MAX_DISTANCE
    tab = rel_bias[:, N_GROUPS * H_A:].astype(F32)
    tab = (tab - tab[NUM_BUCKETS - 1:NUM_BUCKETS]) * LOG2E
    vec = tab[_t5_bucket(jnp.arange(tq))]
    return _toeplitz(vec.T.reshape(H_B, 1, tq), tq, diff=True)


def kernel(x, w_in, b_gate, lambda_q1, lambda_k1, lambda_q2, lambda_k2, subln_g, rel_bias, w_proj_a, w_proj_b,
           w_out, ln1_g, ln1_b, ln2_g, ln2_b, w_mlp1, w_mlp2):
    B, S, D = x.shape
    tq = min(TQ, S)
    tm = min(TM, S)
    w = w_in[0]
    w_a = w[:, :COLS_A].reshape(D, 3, N_GROUPS, W_A_OUT)
    w_a = jnp.concatenate([w_a[:, :1] * QK_SCALE, w_a[:, 1:]], axis=1).transpose(0, 2, 1, 3).reshape(D, COLS_A)
    w_qk = w[:, COLS_A:COLS_A + COLS_B_QK].reshape(D, 4, H_B, HEAD_DIM)
    w_qb = jnp.concatenate([w_qk[:, 0], w_qk[:, 1]], axis=-1).reshape(D, W_B_OUT)
    w_kb = jnp.concatenate([w_qk[:, 2], w_qk[:, 3]], axis=-1).reshape(D, W_B_OUT)
    w_vb = w[:, COLS_A + COLS_B_QK:COLS_A + COLS_B]
    w_n = jnp.concatenate([w_a, w_kb], axis=1).astype(BF16)
    w_t = jnp.concatenate([w_qb, w_vb], axis=1).T.astype(BF16)
    w_gate = w[:, COLS_A + COLS_B:].astype(BF16)

    *a_g, kb, pt = _proj(x, w_n, w_t, tm, tq)

    o_g, lse_g = [], []
    for g, (win, dil) in enumerate(DIL_PAIRS):
        assert win // dil == BLK
        o, lse = _dilated(a_g[g], _dilated_bias(rel_bias, g, dil), g)
        o_g.append(o)
        lse_g.append(lse)

    lam_init = 0.8 - 0.6 * math.exp(-0.3 * 0)
    lam_p = jnp.concatenate([lambda_q1, lambda_k1, lambda_q2, lambda_k2], axis=0).astype(F32)
    o_b = _diff(lam_p, kb, pt, _diff_bias(rel_bias, min(DIFF_T, S)), subln_g[0].reshape(DV_B, 1), lam_init)

    x1 = _mix(x, o_g, lse_g, o_b, w_gate, b_gate, w_proj_a[0].astype(BF16), w_proj_b[0].astype(BF16),
              w_out[0].astype(BF16), ln1_g, ln1_b, tm)
    return _mlp(x1, w_mlp1[0].astype(BF16), w_mlp2[0].astype(BF16), ln2_g, ln2_b, tm)
```

```python
import functools
import math

import jax
import jax.numpy as jnp
import numpy as np
from jax import lax
from jax.experimental import pallas as pl
from jax.experimental.pallas import tpu as pltpu

D_MODEL = 1024
HEAD_DIM = 64
DIL_PAIRS = ((128, 1), (512, 4), (2048, 16))
N_GROUPS = len(DIL_PAIRS)
H_A = 8
H_B = 8
DV_B = 2 * HEAD_DIM
D_FF = 4 * D_MODEL
NUM_BUCKETS = 32
T5_MAX_DISTANCE = 128
BLK = 128
LN_EPS = 1e-5
NEG_INF = -1e30
W_A_OUT = H_A * HEAD_DIM
W_B_OUT = H_B * DV_B
COLS_A = 3 * N_GROUPS * H_A * HEAD_DIM
COLS_B_QK = 4 * H_B * HEAD_DIM
COLS_B = COLS_B_QK + H_B * DV_B
DEPTH = 1
DEEPNORM_ALPHA = (2.0 * DEPTH) ** 0.25
QK_SCALE = HEAD_DIM ** -0.5
LOG2E = math.log2(math.e)

LANES = 128
COLS_N = COLS_A + H_B * 2 * HEAD_DIM
COLS_T = 2 * W_B_OUT
CHUNK = 512
TQ = 512
DIFF_T = 1024
STRIP = 512
PAIRS_PER_BLOCK = 2
ACC_SLOTS = PAIRS_PER_BLOCK
TM = 512
ROW_SLAB = 256
DILATED_ROWS = 1024
VMEM_LIMIT = 56 * 1024 * 1024

BF16 = jnp.bfloat16
F32 = jnp.float32


def _dot(a, b):
    return jnp.dot(a, b, preferred_element_type=F32)


def _dot_nt(a, b):
    return lax.dot_general(a, b, (((1,), (1,)), ((), ())), preferred_element_type=F32)


def _resident(shape):
    nd = len(shape)
    return pl.BlockSpec(shape, lambda *_: (0,) * nd, pipeline_mode=pl.Buffered(1))


def _params(*sem, flags=None):
    return pltpu.CompilerParams(dimension_semantics=sem, vmem_limit_bytes=VMEM_LIMIT, flags=flags)


def _proj_kernel(x_ref, wn_ref, wt_ref, a0_ref, a1_ref, a2_ref, kb_ref, ot_ref, rows_sc, *, tq):
    xb = x_ref[...].astype(BF16)
    tm = xb.shape[0]
    a_refs = (a0_ref, a1_ref, a2_ref)
    for c in range(COLS_N // CHUNK):
        r = _dot(xb, wn_ref[:, c * CHUNK:(c + 1) * CHUNK])
        g, part = divmod(c, 3)
        if g >= N_GROUPS:
            kb_ref[:, (c - 3 * N_GROUPS) * CHUNK:(c - 3 * N_GROUPS + 1) * CHUNK] = r.astype(BF16)
            continue
        sl = slice(part * W_A_OUT, (part + 1) * W_A_OUT)
        dil = DIL_PAIRS[g][1]
        if dil == 1:
            a_refs[g][0, :, sl] = r.astype(BF16)
            continue
        for k in range(CHUNK // LANES):
            rows_sc[k] = r[:, k * LANES:(k + 1) * LANES]
        for res in range(dil):
            for k in range(CHUNK // LANES):
                piece = rows_sc[k, pl.ds(res, tm // dil, stride=dil), :]
                a_refs[g][res, :, part * W_A_OUT + k * LANES:part * W_A_OUT + (k + 1) * LANES] = piece.astype(BF16)
    for c in range(COLS_T // CHUNK):
        sl = slice(c * CHUNK, (c + 1) * CHUNK)
        r = _dot_nt(wt_ref[sl, :], xb)
        if (c + 1) * CHUNK <= W_B_OUT:
            r = r * (QK_SCALE * LOG2E)
        r = r.astype(BF16)
        for t in range(tm // tq):
            ot_ref[t, sl, :] = r[:, t * tq:(t + 1) * tq]


def _proj(x, w_n, w_t, tm, tq):
    B, S, D = x.shape
    assert CHUNK == W_A_OUT
    dils = [d for _, d in DIL_PAIRS]
    return pl.pallas_call(
        functools.partial(_proj_kernel, tq=tq),
        grid=(B, S // tm),
        in_specs=[
            pl.BlockSpec((None, tm, D), lambda b, i: (b, i, 0)),
            _resident((D, COLS_N)),
            _resident((COLS_T, D)),
        ],
        out_specs=[pl.BlockSpec((None, d, tm // d, 3 * W_A_OUT), lambda b, i: (b, 0, i, 0)) for d in dils] + [
            pl.BlockSpec((None, tm, W_B_OUT), lambda b, i: (b, i, 0)),
            pl.BlockSpec((None, tm // tq, COLS_T, tq), lambda b, i: (b, i, 0, 0)),
        ],
        out_shape=[jax.ShapeDtypeStruct((B, d, S // d, 3 * W_A_OUT), BF16) for d in dils] + [
            jax.ShapeDtypeStruct((B, S, W_B_OUT), BF16),
            jax.ShapeDtypeStruct((B, S // tq, COLS_T, tq), BF16),
        ],
        scratch_shapes=[pltpu.VMEM((CHUNK // LANES, tm, LANES), F32)],
        compiler_params=_params("parallel", "parallel"),
        name="proj",
    )(x, w_n, w_t)


def _dilated_kernel(q_ref, k_ref, v_ref, kp_ref, vp_ref, bias_ref, o_ref, lse_ref, kw_sc, vw_sc):
    nres, tb, _ = q_ref.shape
    first = pl.program_id(2) == 0
    lane = lax.broadcasted_iota(jnp.int32, (BLK, LANES), 1)
    lo = lane < HEAD_DIM
    in_prev = lax.broadcasted_iota(jnp.int32, (2 * BLK, 2 * BLK), 1) < BLK
    ones = jnp.ones((2 * BLK, LANES), BF16)
    for r in range(nres):
        kw_sc[r, :BLK], kw_sc[r, BLK:] = kp_ref[r], k_ref[r]
        vw_sc[r, :BLK], vw_sc[r, BLK:] = vp_ref[r], v_ref[r]
        for jb in range(tb // BLK):
            rows = slice(jb * BLK, (jb + 1) * BLK)
            win = slice(jb * BLK, (jb + 2) * BLK)
            for p in range(H_A // 2):
                cols = slice(p * LANES, (p + 1) * LANES)
                q2 = q_ref[r, rows, cols]
                zero = jnp.zeros_like(q2)
                qd = jnp.concatenate([jnp.where(lo, q2, zero), jnp.where(lo, zero, q2)], axis=0)
                s = _dot_nt(qd, kw_sc[r, win, cols]) + bias_ref[p]
                if jb == 0:
                    s = jnp.where(first & in_prev, NEG_INF, s)
                m = jnp.max(s, axis=-1, keepdims=True)
                pw = jnp.exp((s - m).astype(BF16))
                oa = _dot(pw, jnp.concatenate([vw_sc[r, win, cols], ones], axis=1))
                den = oa[:, LANES:]
                o2 = oa[:, :LANES] / den
                lse = m + jnp.log(den)
                o_ref[r, rows, cols] = jnp.where(lo, o2[:BLK], o2[BLK:])
                lse_ref[r, rows, cols] = jnp.where(lo, lse[:BLK], lse[BLK:])


def _dilated(a, bias, g):
    B, dil, L, _ = a.shape
    tb = min(DILATED_ROWS, L)
    nres = min(DILATED_ROWS // tb, dil)
    cur = lambda c: pl.BlockSpec((None, nres, tb, W_A_OUT), lambda b, r, n: (b, r, n, c))
    prev = lambda c: pl.BlockSpec(
        (None, nres, BLK, W_A_OUT), lambda b, r, n: (b, r, jnp.maximum(n * (tb // BLK) - 1, 0), c))
    out = pl.BlockSpec((None, nres, tb, W_A_OUT), lambda b, r, n: (b, r, n, 0))
    return pl.pallas_call(
        _dilated_kernel,
        grid=(B, dil // nres, L // tb),
        in_specs=[cur(0), cur(1), cur(2), prev(1), prev(2), _resident(bias.shape)],
        out_specs=[out, out],
        out_shape=[jax.ShapeDtypeStruct((B, dil, L, W_A_OUT), F32)] * 2,
        scratch_shapes=[pltpu.VMEM((nres, BLK + tb, W_A_OUT), BF16)] * 2,
        compiler_params=_params("parallel", "parallel", "arbitrary"),
        name=f"dilated{g}",
    )(a, a, a, a, a, bias)


def _diff_kernel(lam_ref, qt_ref, k_ref, vt_ref, bias_ref, g_ref, o_ref, qd_sc, va_sc, s0_sc, s1_sc, x0_sc, x1_sc,
                 m_sc, acc_sc, *, lam_init):
    nb, _, tb = qt_ref.shape
    tq = tk = bias_ref.shape[-1]
    qb = tq // tb
    nq = nb // qb
    s_bufs, x_bufs = (s0_sc, s1_sc), (x0_sc, x1_sc)
    row = lax.broadcasted_iota(jnp.int32, (DV_B, tb), 0)
    for i in range(nb):
        blk, part = divmod(i, qb)
        qt = qt_ref[i]
        zero = jnp.zeros_like(qt)
        qd_sc[blk, :, part * tb:(part + 1) * tb] = jnp.where(row < HEAD_DIM, qt, zero)
        qd_sc[blk, :, tq + part * tb:tq + (part + 1) * tb] = jnp.where(row < HEAD_DIM, zero, qt)
        va_sc[blk, :DV_B, part * tb:(part + 1) * tb] = vt_ref[i]
    for i in range(nq):
        va_sc[i, DV_B:, :] = jnp.ones((va_sc.shape[1] - DV_B, tk), BF16)

    lam = (jnp.exp(jnp.sum(lam_ref[0:1, :] * lam_ref[1:2, :], axis=-1, keepdims=True))
           - jnp.exp(jnp.sum(lam_ref[2:3, :] * lam_ref[3:4, :], axis=-1, keepdims=True)) + lam_init)

    def reset(a):
        m_sc[a] = jnp.full(m_sc.shape[1:], -jnp.inf, F32)
        acc_sc[a] = jnp.zeros(acc_sc.shape[1:], F32)

    def finalize(qi):
        a = qi % ACC_SLOTS
        acc = acc_sc[a]
        on = acc[:DV_B] / acc[DV_B:DV_B + 1]
        o = on[:, :tq] - lam * on[:, tq:]
        ms = jnp.mean(o * o, axis=0, keepdims=True)
        o = o * lax.rsqrt(ms + LN_EPS) * g_ref[...] * (1.0 - lam_init)
        o_ref[pl.ds(pl.multiple_of(qi * tq, tq), tq), :] = o.T.astype(BF16)
        reset(a)

    def is_last(qi, j):
        return j == qi

    strips = [slice(c * STRIP, (c + 1) * STRIP) for c in range(2 * tq // STRIP)]

    def score_strip(kblk, qi, bi, cs):
        s = _dot(kblk, qd_sc[qi, :, cs])
        if bi is not None:
            bcs = slice(cs.start % tq, cs.start % tq + STRIP)
            s = s + bias_ref[bi, :, bcs].astype(F32)
        return s, jnp.max(s, axis=0, keepdims=True)

    for a in range(ACC_SLOTS):
        reset(a)
    for cs in strips:
        s_bufs[0][:, cs], x_bufs[0][:, cs] = score_strip(k_ref[0:tk, :], 0, 0, cs)

    def half(slot, qi, j):
        s_cur, x_cur, s_oth, x_oth = s_bufs[slot], x_bufs[slot], s_bufs[1 - slot], x_bufs[1 - slot]
        a = qi % ACC_SLOTS
        last = is_last(qi, j)
        j_n = jnp.where(last, 0, j + 1)
        qi_n = jnp.minimum(jnp.where(last, qi + 1, qi), nq - 1)
        near = j_n >= qi_n - 1
        kblk = k_ref[pl.ds(pl.multiple_of(j_n * tk, tk), tk), :]
        va = va_sc[j]

        def run(bi):
            for cs in strips:
                m_old = m_sc[a, :, cs]
                m_new = jnp.maximum(m_old, x_cur[:, cs])
                p = jnp.exp2((s_cur[:, cs] - m_new).astype(BF16))
                alpha = jnp.exp2(m_old - m_new)
                m_sc[a, :, cs] = m_new
                acc_sc[a, :, cs] = acc_sc[a, :, cs] * alpha + _dot(va, p)
                s_oth[:, cs], x_oth[:, cs] = score_strip(kblk, qi_n, bi, cs)

        pl.when(near)(functools.partial(run, qi_n - j_n))
        pl.when(jnp.logical_not(near))(functools.partial(run, None))
        return qi_n, j_n

    def run_pairs(count, carry):
        qi, j = carry
        done = []
        for h in range(count):
            done.append((qi, j))
            qi, j = half(h % 2, qi, j)
        for q, jj in done:
            pl.when(is_last(q, jj))(functools.partial(finalize, q))
        return qi, j

    npairs = nq * (nq + 1) // 2
    carry = lax.fori_loop(0, npairs // PAIRS_PER_BLOCK, lambda _, c: run_pairs(PAIRS_PER_BLOCK, c),
                          (jnp.int32(0), jnp.int32(0)))
    if npairs % PAIRS_PER_BLOCK:
        run_pairs(npairs % PAIRS_PER_BLOCK, carry)


def _diff(lam_p, kb, pt, bias, g_col, lam_init):
    B, S, _ = kb.shape
    nb, tb = pt.shape[1], pt.shape[3]
    tq = tk = bias.shape[-1]
    nq = S // tq
    assert tq % tb == 0 and tq % STRIP == 0 and bias.shape[1:] == (2, tk, tq)
    va_rows = DV_B + 16
    return pl.pallas_call(
        functools.partial(_diff_kernel, lam_init=lam_init),
        grid=(B, H_B),
        in_specs=[
            _resident(lam_p.shape),
            pl.BlockSpec((None, nb, DV_B, tb), lambda b, h: (b, 0, h, 0)),
            pl.BlockSpec((None, S, LANES), lambda b, h: (b, 0, h)),
            pl.BlockSpec((None, nb, DV_B, tb), lambda b, h: (b, 0, H_B + h, 0)),
            pl.BlockSpec((None, 2, tk, tq), lambda b, h: (h, 0, 0, 0)),
            _resident(g_col.shape),
        ],
        out_specs=pl.BlockSpec((None, S, DV_B), lambda b, h: (b, 0, h)),
        out_shape=jax.ShapeDtypeStruct((B, S, W_B_OUT), BF16),
        scratch_shapes=[
            pltpu.VMEM((nq, DV_B, 2 * tq), BF16),
            pltpu.VMEM((S // tk, va_rows, tk), BF16),
            pltpu.VMEM((tk, 2 * tq), F32),
            pltpu.VMEM((tk, 2 * tq), F32),
            pltpu.VMEM((1, 2 * tq), F32),
            pltpu.VMEM((1, 2 * tq), F32),
            pltpu.VMEM((ACC_SLOTS, 1, 2 * tq), F32),
            pltpu.VMEM((ACC_SLOTS, va_rows, 2 * tq), F32),
        ],
        compiler_params=_params("parallel", "parallel"),
        name="diff_attn",
    )(lam_p, pt, kb, pt, bias, g_col)


def _toeplitz_kernel(w_ref, o_ref, *, diff):
    width = o_ref.shape[-1]
    rows = width if diff else o_ref.shape[-2]
    t = pltpu.roll(jnp.broadcast_to(w_ref[...], (rows, width)), 0, 1, stride=1, stride_axis=0)
    if not diff:
        o_ref[...] = t
        return
    r = lax.broadcasted_iota(jnp.int32, (rows, width), 0)
    c = lax.broadcasted_iota(jnp.int32, (rows, width), 1)
    o_ref[0] = jnp.where(c >= r, t, NEG_INF).astype(o_ref.dtype)
    o_ref[1] = jnp.where(c < r, t, 0.0).astype(o_ref.dtype)


def _toeplitz(w, rows, diff):
    n, _, width = w.shape
    oshape = (n, 2, rows, width) if diff else (n, rows, width)
    oblock = (None,) + oshape[1:]
    return pl.pallas_call(
        functools.partial(_toeplitz_kernel, diff=diff),
        grid=(n,),
        in_specs=[pl.BlockSpec((None, 1, width), lambda i: (i, 0, 0))],
        out_specs=pl.BlockSpec(oblock, lambda i: (i,) + (0,) * (len(oshape) - 1)),
        out_shape=jax.ShapeDtypeStruct(oshape, BF16 if diff else F32),
        compiler_params=_params("parallel"),
        name="bias_diff" if diff else "bias_dilated",
    )(w)


def _layer_norm(h, g, b):
    mu = jnp.mean(h, axis=-1, keepdims=True)
    d = h - mu
    var = jnp.mean(d * d, axis=-1, keepdims=True)
    return d * lax.rsqrt(var + LN_EPS) * g + b


def _mix_kernel(x_ref, o0_ref, o1_ref, o2_ref, l0_ref, l1_ref, l2_ref, ob_ref, wg_ref, bg_ref, wa_ref, wb_ref,
                wo_ref, g_ref, b_ref, out_ref, *tok_sc):
    def token_major(ref, sc):
        dil, rows, _ = ref.shape
        if dil == 1:
            return lambda sl: ref[0, sl]
        nk = sc.shape[0]
        for res in range(dil):
            for k in range(nk):
                sc[k, pl.ds(res, rows, stride=dil), :] = ref[res, :, k * LANES:(k + 1) * LANES]
        return lambda sl: jnp.concatenate([sc[k, sl] for k in range(nk)], axis=1)

    l_g = (token_major(l0_ref, None), token_major(l1_ref, tok_sc[0]), token_major(l2_ref, tok_sc[1]))
    o_g = (token_major(o0_ref, None), token_major(o1_ref, tok_sc[2]), token_major(o2_ref, tok_sc[3]))
    tm = x_ref.shape[0]
    for r0 in range(0, tm, ROW_SLAB):
        sl = slice(r0, r0 + ROW_SLAB)
        x = x_ref[sl]
        xb = x.astype(BF16)
        l0, l1, l2 = (f(sl) for f in l_g)
        mx = jnp.maximum(jnp.maximum(l0, l1), l2)
        e0, e1, e2 = jnp.exp(l0 - mx), jnp.exp(l1 - mx), jnp.exp(l2 - mx)
        o_a = (e0 * o_g[0](sl) + e1 * o_g[1](sl) + e2 * o_g[2](sl)) / (e0 + e1 + e2)
        y_a = _dot(o_a.astype(BF16), wa_ref[...])
        y_b = _dot(ob_ref[sl], wb_ref[...])
        gate_a = jax.nn.sigmoid(_dot(xb, wg_ref[:, :D_MODEL]) + bg_ref[:, :D_MODEL])
        gate_b = jax.nn.sigmoid(_dot(xb, wg_ref[:, D_MODEL:]) + bg_ref[:, D_MODEL:])
        merged = gate_a * y_a + gate_b * y_b
        mix = _dot(merged.astype(BF16), wo_ref[...])
        out_ref[sl] = _layer_norm(DEEPNORM_ALPHA * x + mix, g_ref[...], b_ref[...])


def _mix(x, o_g, lse_g, o_b, w_gate, b_gate, w_a, w_b, w_o, ln_g, ln_b, tm):
    B, S, D = x.shape
    tok = lambda w: pl.BlockSpec((None, tm, w), lambda b, i: (b, i, 0))
    res = [pl.BlockSpec((None, a.shape[1], tm // a.shape[1], W_A_OUT), lambda b, i: (b, 0, i, 0)) for a in o_g]
    n_sc = 2 * sum(a.shape[1] > 1 for a in o_g)
    return pl.pallas_call(
        _mix_kernel,
        grid=(B, S // tm),
        in_specs=[tok(D)] + res + res + [tok(W_B_OUT)]
        + [_resident(a.shape) for a in (w_gate, b_gate, w_a, w_b, w_o, ln_g, ln_b)],
        out_specs=tok(D),
        out_shape=jax.ShapeDtypeStruct((B, S, D), F32),
        scratch_shapes=[pltpu.VMEM((W_A_OUT // LANES, tm, LANES), F32)] * n_sc,
        compiler_params=_params("parallel", "parallel"),
        name="mix",
    )(x, *o_g, *lse_g, o_b, w_gate, b_gate, w_a, w_b, w_o, ln_g, ln_b)


def _mlp_kernel(x_ref, w1_ref, w2_ref, g_ref, b_ref, out_ref):
    for r0 in range(0, x_ref.shape[0], ROW_SLAB):
        rows = slice(r0, r0 + ROW_SLAB)
        x = x_ref[rows]
        xb = x.astype(BF16)
        ff = jnp.zeros(x.shape, F32)
        for c in range(D_FF // (2 * CHUNK)):
            sl = slice(c * 2 * CHUNK, (c + 1) * 2 * CHUNK)
            h = jnp.maximum(_dot(xb, w1_ref[:, sl]), 0.0)
            ff = ff + _dot((h * h).astype(BF16), w2_ref[sl, :])
        out_ref[rows] = _layer_norm(DEEPNORM_ALPHA * x + ff, g_ref[...], b_ref[...])


def _mlp(x, w1, w2, ln_g, ln_b, tm):
    B, S, D = x.shape
    tok = pl.BlockSpec((None, tm, D), lambda b, i: (b, i, 0))
    return pl.pallas_call(
        _mlp_kernel,
        grid=(B, S // tm),
        in_specs=[tok] + [_resident(a.shape) for a in (w1, w2, ln_g, ln_b)],
        out_specs=tok,
        out_shape=jax.ShapeDtypeStruct((B, S, D), F32),
        compiler_params=_params("parallel", "parallel"),
        name="mlp",
    )(x, w1, w2, ln_g, ln_b)


def _t5_bucket(dist):
    n = jnp.maximum(dist, 0)
    max_exact = NUM_BUCKETS // 2
    nf = jnp.maximum(n, 1).astype(F32)
    large = max_exact + (jnp.log(nf / max_exact) / math.log(T5_MAX_DISTANCE / max_exact)
                         * (NUM_BUCKETS - max_exact)).astype(jnp.int32)
    large = jnp.minimum(large, NUM_BUCKETS - 1)
    return jnp.where(n < max_exact, n, large)


def _dilated_bias(rel_bias, g, dil):
    j = jnp.arange(2 * BLK)
    tab = rel_bias[:, g * H_A:(g + 1) * H_A].astype(F32)
    vec = tab[_t5_bucket(jnp.maximum(BLK - j, 0) * dil)]
    vec = jnp.where((j <= BLK)[:, None], vec, NEG_INF)
    bias = _toeplitz(vec.T.reshape(H_A, 1, 2 * BLK), BLK, diff=False)
    return bias.reshape(H_A // 2, 2 * BLK, 2 * BLK)


def _diff_bias(rel_bias, tq):
    assert tq >= T5_MAX_DISTANCE
    tab = rel_bias[:, N_GROUPS * H_A:].astype(F32)
    tab = (tab - tab[NUM_BUCKETS - 1:NUM_BUCKETS]) * LOG2E
    vec = tab[_t5_bucket(jnp.arange(tq))]
    return _toeplitz(vec.T.reshape(H_B, 1, tq), tq, diff=True)


def kernel(x, w_in, b_gate, lambda_q1, lambda_k1, lambda_q2, lambda_k2, subln_g, rel_bias, w_proj_a, w_proj_b,
           w_out, ln1_g, ln1_b, ln2_g, ln2_b, w_mlp1, w_mlp2):
    B, S, D = x.shape
    tq = min(TQ, S)
    tm = min(TM, S)
    w = w_in[0]
    w_a = w[:, :COLS_A].reshape(D, 3, N_GROUPS, W_A_OUT)
    w_a = jnp.concatenate([w_a[:, :1] * QK_SCALE, w_a[:, 1:]], axis=1).transpose(0, 2, 1, 3).reshape(D, COLS_A)
    w_qk = w[:, COLS_A:COLS_A + COLS_B_QK].reshape(D, 4, H_B, HEAD_DIM)
    w_qb = jnp.concatenate([w_qk[:, 0], w_qk[:, 1]], axis=-1).reshape(D, W_B_OUT)
    w_kb = jnp.concatenate([w_qk[:, 2], w_qk[:, 3]], axis=-1).reshape(D, W_B_OUT)
    w_vb = w[:, COLS_A + COLS_B_QK:COLS_A + COLS_B]
    w_n = jnp.concatenate([w_a, w_kb], axis=1).astype(BF16)
    w_t = jnp.concatenate([w_qb, w_vb], axis=1).T.astype(BF16)
    w_gate = w[:, COLS_A + COLS_B:].astype(BF16)

    *a_g, kb, pt = _proj(x, w_n, w_t, tm, tq)

    o_g, lse_g = [], []
    for g, (win, dil) in enumerate(DIL_PAIRS):
        assert win // dil == BLK
        o, lse = _dilated(a_g[g], _dilated_bias(rel_bias, g, dil), g)
        o_g.append(o)
        lse_g.append(lse)

    lam_init = 0.8 - 0.6 * math.exp(-0.3 * 0)
    lam_p = jnp.concatenate([lambda_q1, lambda_k1, lambda_q2, lambda_k2], axis=0).astype(F32)
    o_b = _diff(lam_p, kb, pt, _diff_bias(rel_bias, min(DIFF_T, S)), subln_g[0].reshape(DV_B, 1), lam_init)

    x1 = _mix(x, o_g, lse_g, o_b, w_gate, b_gate, w_proj_a[0].astype(BF16), w_proj_b[0].astype(BF16),
              w_out[0].astype(BF16), ln1_g, ln1_b, tm)
    return _mlp(x1, w_mlp1[0].astype(BF16), w_mlp2[0].astype(BF16), ln2_g, ln2_b, tm)
```

```python
import functools
import math

import jax
import jax.numpy as jnp
import numpy as np
from jax import lax
from jax.experimental import pallas as pl
from jax.experimental.pallas import tpu as pltpu

D_MODEL = 1024
HEAD_DIM = 64
DIL_PAIRS = ((128, 1), (512, 4), (2048, 16))
N_GROUPS = len(DIL_PAIRS)
H_A = 8
H_B = 8
DV_B = 2 * HEAD_DIM
D_FF = 4 * D_MODEL
NUM_BUCKETS = 32
T5_MAX_DISTANCE = 128
BLK = 128
LN_EPS = 1e-5
NEG_INF = -1e30
W_A_OUT = H_A * HEAD_DIM
W_B_OUT = H_B * DV_B
COLS_A = 3 * N_GROUPS * H_A * HEAD_DIM
COLS_B_QK = 4 * H_B * HEAD_DIM
COLS_B = COLS_B_QK + H_B * DV_B
DEPTH = 1
DEEPNORM_ALPHA = (2.0 * DEPTH) ** 0.25
QK_SCALE = HEAD_DIM ** -0.5
LOG2E = math.log2(math.e)

LANES = 128
COLS_N = COLS_A + H_B * 2 * HEAD_DIM
COLS_T = 2 * W_B_OUT
CHUNK = 512
TQ = 512
DIFF_T = 1024
STRIP = 512
PAIRS_PER_BLOCK = 2
ACC_SLOTS = PAIRS_PER_BLOCK
TM = 512
ROW_SLAB = 256
DILATED_ROWS = 1024
VMEM_LIMIT = 56 * 1024 * 1024

BF16 = jnp.bfloat16
F32 = jnp.float32


def _dot(a, b):
    return jnp.dot(a, b, preferred_element_type=F32)


def _dot_nt(a, b):
    return lax.dot_general(a, b, (((1,), (1,)), ((), ())), preferred_element_type=F32)


def _resident(shape):
    nd = len(shape)
    return pl.BlockSpec(shape, lambda *_: (0,) * nd, pipeline_mode=pl.Buffered(1))


def _params(*sem, flags=None):
    return pltpu.CompilerParams(dimension_semantics=sem, vmem_limit_bytes=VMEM_LIMIT, flags=flags)


def _proj_kernel(x_ref, wn_ref, wt_ref, a0_ref, a1_ref, a2_ref, kb_ref, ot_ref, rows_sc, *, tq):
    xb = x_ref[...].astype(BF16)
    tm = xb.shape[0]
    a_refs = (a0_ref, a1_ref, a2_ref)
    for c in range(COLS_N // CHUNK):
        r = _dot(xb, wn_ref[:, c * CHUNK:(c + 1) * CHUNK])
        g, part = divmod(c, 3)
        if g >= N_GROUPS:
            kb_ref[:, (c - 3 * N_GROUPS) * CHUNK:(c - 3 * N_GROUPS + 1) * CHUNK] = r.astype(BF16)
            continue
        sl = slice(part * W_A_OUT, (part + 1) * W_A_OUT)
        dil = DIL_PAIRS[g][1]
        if dil == 1:
            a_refs[g][0, :, sl] = r.astype(BF16)
            continue
        for k in range(CHUNK // LANES):
            rows_sc[k] = r[:, k * LANES:(k + 1) * LANES]
        for res in range(dil):
            for k in range(CHUNK // LANES):
                piece = rows_sc[k, pl.ds(res, tm // dil, stride=dil), :]
                a_refs[g][res, :, part * W_A_OUT + k * LANES:part * W_A_OUT + (k + 1) * LANES] = piece.astype(BF16)
    for c in range(COLS_T // CHUNK):
        sl = slice(c * CHUNK, (c + 1) * CHUNK)
        r = _dot_nt(wt_ref[sl, :], xb)
        if (c + 1) * CHUNK <= W_B_OUT:
            r = r * (QK_SCALE * LOG2E)
        r = r.astype(BF16)
        for t in range(tm // tq):
            ot_ref[t, sl, :] = r[:, t * tq:(t + 1) * tq]


def _proj(x, w_n, w_t, tm, tq):
    B, S, D = x.shape
    assert CHUNK == W_A_OUT
    dils = [d for _, d in DIL_PAIRS]
    return pl.pallas_call(
        functools.partial(_proj_kernel, tq=tq),
        grid=(B, S // tm),
        in_specs=[
            pl.BlockSpec((None, tm, D), lambda b, i: (b, i, 0)),
            _resident((D, COLS_N)),
            _resident((COLS_T, D)),
        ],
        out_specs=[pl.BlockSpec((None, d, tm // d, 3 * W_A_OUT), lambda b, i: (b, 0, i, 0)) for d in dils] + [
            pl.BlockSpec((None, tm, W_B_OUT), lambda b, i: (b, i, 0)),
            pl.BlockSpec((None, tm // tq, COLS_T, tq), lambda b, i: (b, i, 0, 0)),
        ],
        out_shape=[jax.ShapeDtypeStruct((B, d, S // d, 3 * W_A_OUT), BF16) for d in dils] + [
            jax.ShapeDtypeStruct((B, S, W_B_OUT), BF16),
            jax.ShapeDtypeStruct((B, S // tq, COLS_T, tq), BF16),
        ],
        scratch_shapes=[pltpu.VMEM((CHUNK // LANES, tm, LANES), F32)],
        compiler_params=_params("parallel", "parallel"),
        name="proj",
    )(x, w_n, w_t)


def _dilated_kernel(q_ref, k_ref, v_ref, kp_ref, vp_ref, bias_ref, o_ref, lse_ref, kw_sc, vw_sc):
    nres, tb, _ = q_ref.shape
    first = pl.program_id(2) == 0
    lane = lax.broadcasted_iota(jnp.int32, (BLK, LANES), 1)
    lo = lane < HEAD_DIM
    in_prev = lax.broadcasted_iota(jnp.int32, (2 * BLK, 2 * BLK), 1) < BLK
    ones = jnp.ones((2 * BLK, LANES), BF16)
    for r in range(nres):
        kw_sc[r, :BLK], kw_sc[r, BLK:] = kp_ref[r], k_ref[r]
        vw_sc[r, :BLK], vw_sc[r, BLK:] = vp_ref[r], v_ref[r]
        for jb in range(tb // BLK):
            rows = slice(jb * BLK, (jb + 1) * BLK)
            win = slice(jb * BLK, (jb + 2) * BLK)
            for p in range(H_A // 2):
                cols = slice(p * LANES, (p + 1) * LANES)
                q2 = q_ref[r, rows, cols]
                zero = jnp.zeros_like(q2)
                qd = jnp.concatenate([jnp.where(lo, q2, zero), jnp.where(lo, zero, q2)], axis=0)
                s = _dot_nt(qd, kw_sc[r, win, cols]) + bias_ref[p]
                if jb == 0:
                    s = jnp.where(first & in_prev, NEG_INF, s)
                m = jnp.max(s, axis=-1, keepdims=True)
                pw = jnp.exp((s - m).astype(BF16))
                oa = _dot(pw, jnp.concatenate([vw_sc[r, win, cols], ones], axis=1))
                den = oa[:, LANES:]
                o2 = oa[:, :LANES] / den
                lse = m + jnp.log(den)
                o_ref[r, rows, cols] = jnp.where(lo, o2[:BLK], o2[BLK:]).astype(o_ref.dtype)
                lse_ref[r, rows, cols] = jnp.where(lo, lse[:BLK], lse[BLK:])


def _dilated(a, bias, g):
    B, dil, L, _ = a.shape
    tb = min(DILATED_ROWS, L)
    nres = min(DILATED_ROWS // tb, dil)
    cur = lambda c: pl.BlockSpec((None, nres, tb, W_A_OUT), lambda b, r, n: (b, r, n, c))
    prev = lambda c: pl.BlockSpec(
        (None, nres, BLK, W_A_OUT), lambda b, r, n: (b, r, jnp.maximum(n * (tb // BLK) - 1, 0), c))
    out = pl.BlockSpec((None, nres, tb, W_A_OUT), lambda b, r, n: (b, r, n, 0))
    return pl.pallas_call(
        _dilated_kernel,
        grid=(B, dil // nres, L // tb),
        in_specs=[cur(0), cur(1), cur(2), prev(1), prev(2), _resident(bias.shape)],
        out_specs=[out, out],
        out_shape=[jax.ShapeDtypeStruct((B, dil, L, W_A_OUT), BF16), jax.ShapeDtypeStruct((B, dil, L, W_A_OUT), F32)],
        scratch_shapes=[pltpu.VMEM((nres, BLK + tb, W_A_OUT), BF16)] * 2,
        compiler_params=_params("parallel", "parallel", "arbitrary"),
        name=f"dilated{g}",
    )(a, a, a, a, a, bias)


def _diff_kernel(lam_ref, qt_ref, k_ref, vt_ref, bias_ref, g_ref, o_ref, qd_sc, va_sc, s0_sc, s1_sc, x0_sc, x1_sc,
                 m_sc, acc_sc, *, lam_init):
    nb, _, tb = qt_ref.shape
    tq = tk = bias_ref.shape[-1]
    qb = tq // tb
    nq = nb // qb
    s_bufs, x_bufs = (s0_sc, s1_sc), (x0_sc, x1_sc)
    row = lax.broadcasted_iota(jnp.int32, (DV_B, tb), 0)
    for i in range(nb):
        blk, part = divmod(i, qb)
        qt = qt_ref[i]
        zero = jnp.zeros_like(qt)
        qd_sc[blk, :, part * tb:(part + 1) * tb] = jnp.where(row < HEAD_DIM, qt, zero)
        qd_sc[blk, :, tq + part * tb:tq + (part + 1) * tb] = jnp.where(row < HEAD_DIM, zero, qt)
        va_sc[blk, :DV_B, part * tb:(part + 1) * tb] = vt_ref[i]
    for i in range(nq):
        va_sc[i, DV_B:, :] = jnp.ones((va_sc.shape[1] - DV_B, tk), BF16)

    lam = (jnp.exp(jnp.sum(lam_ref[0:1, :] * lam_ref[1:2, :], axis=-1, keepdims=True))
           - jnp.exp(jnp.sum(lam_ref[2:3, :] * lam_ref[3:4, :], axis=-1, keepdims=True)) + lam_init)

    def reset(a):
        m_sc[a] = jnp.full(m_sc.shape[1:], -jnp.inf, F32)
        acc_sc[a] = jnp.zeros(acc_sc.shape[1:], F32)

    def finalize(qi):
        a = qi % ACC_SLOTS
        acc = acc_sc[a]
        on = acc[:DV_B] / acc[DV_B:DV_B + 1]
        o = on[:, :tq] - lam * on[:, tq:]
        ms = jnp.mean(o * o, axis=0, keepdims=True)
        o = o * lax.rsqrt(ms + LN_EPS) * g_ref[...] * (1.0 - lam_init)
        o_ref[pl.ds(pl.multiple_of(qi * tq, tq), tq), :] = o.T.astype(BF16)
        reset(a)

    def is_last(qi, j):
        return j == qi

    strips = [slice(c * STRIP, (c + 1) * STRIP) for c in range(2 * tq // STRIP)]

    def score_strip(kblk, qi, bi, cs):
        s = _dot(kblk, qd_sc[qi, :, cs])
        if bi is not None:
            bcs = slice(cs.start % tq, cs.start % tq + STRIP)
            s = s + bias_ref[bi, :, bcs].astype(F32)
        return s, jnp.max(s, axis=0, keepdims=True)

    for a in range(ACC_SLOTS):
        reset(a)
    for cs in strips:
        s_bufs[0][:, cs], x_bufs[0][:, cs] = score_strip(k_ref[0:tk, :], 0, 0, cs)

    def half(slot, qi, j):
        s_cur, x_cur, s_oth, x_oth = s_bufs[slot], x_bufs[slot], s_bufs[1 - slot], x_bufs[1 - slot]
        a = qi % ACC_SLOTS
        last = is_last(qi, j)
        j_n = jnp.where(last, 0, j + 1)
        qi_n = jnp.minimum(jnp.where(last, qi + 1, qi), nq - 1)
        near = j_n >= qi_n - 1
        kblk = k_ref[pl.ds(pl.multiple_of(j_n * tk, tk), tk), :]
        va = va_sc[j]

        def run(bi):
            for cs in strips:
                m_old = m_sc[a, :, cs]
                m_new = jnp.maximum(m_old, x_cur[:, cs])
                p = jnp.exp2((s_cur[:, cs] - m_new).astype(BF16))
                alpha = jnp.exp2(m_old - m_new)
                m_sc[a, :, cs] = m_new
                acc_sc[a, :, cs] = acc_sc[a, :, cs] * alpha + _dot(va, p)
                s_oth[:, cs], x_oth[:, cs] = score_strip(kblk, qi_n, bi, cs)

        pl.when(near)(functools.partial(run, qi_n - j_n))
        pl.when(jnp.logical_not(near))(functools.partial(run, None))
        return qi_n, j_n

    def run_pairs(count, carry):
        qi, j = carry
        done = []
        for h in range(count):
            done.append((qi, j))
            qi, j = half(h % 2, qi, j)
        for q, jj in done:
            pl.when(is_last(q, jj))(functools.partial(finalize, q))
        return qi, j

    npairs = nq * (nq + 1) // 2
    carry = lax.fori_loop(0, npairs // PAIRS_PER_BLOCK, lambda _, c: run_pairs(PAIRS_PER_BLOCK, c),
                          (jnp.int32(0), jnp.int32(0)))
    if npairs % PAIRS_PER_BLOCK:
        run_pairs(npairs % PAIRS_PER_BLOCK, carry)


def _diff(lam_p, kb, pt, bias, g_col, lam_init):
    B, S, _ = kb.shape
    nb, tb = pt.shape[1], pt.shape[3]
    tq = tk = bias.shape[-1]
    nq = S // tq
    assert tq % tb == 0 and tq % STRIP == 0 and bias.shape[1:] == (2, tk, tq)
    va_rows = DV_B + 16
    return pl.pallas_call(
        functools.partial(_diff_kernel, lam_init=lam_init),
        grid=(B, H_B),
        in_specs=[
            _resident(lam_p.shape),
            pl.BlockSpec((None, nb, DV_B, tb), lambda b, h: (b, 0, h, 0)),
            pl.BlockSpec((None, S, LANES), lambda b, h: (b, 0, h)),
            pl.BlockSpec((None, nb, DV_B, tb), lambda b, h: (b, 0, H_B + h, 0)),
            pl.BlockSpec((None, 2, tk, tq), lambda b, h: (h, 0, 0, 0)),
            _resident(g_col.shape),
        ],
        out_specs=pl.BlockSpec((None, S, DV_B), lambda b, h: (b, 0, h)),
        out_shape=jax.ShapeDtypeStruct((B, S, W_B_OUT), BF16),
        scratch_shapes=[
            pltpu.VMEM((nq, DV_B, 2 * tq), BF16),
            pltpu.VMEM((S // tk, va_rows, tk), BF16),
            pltpu.VMEM((tk, 2 * tq), F32),
            pltpu.VMEM((tk, 2 * tq), F32),
            pltpu.VMEM((1, 2 * tq), F32),
            pltpu.VMEM((1, 2 * tq), F32),
            pltpu.VMEM((ACC_SLOTS, 1, 2 * tq), F32),
            pltpu.VMEM((ACC_SLOTS, va_rows, 2 * tq), F32),
        ],
        compiler_params=_params("parallel", "parallel"),
        name="diff_attn",
    )(lam_p, pt, kb, pt, bias, g_col)


def _toeplitz_kernel(w_ref, o_ref, *, diff):
    width = o_ref.shape[-1]
    rows = width if diff else o_ref.shape[-2]
    t = pltpu.roll(jnp.broadcast_to(w_ref[...], (rows, width)), 0, 1, stride=1, stride_axis=0)
    if not diff:
        o_ref[...] = t
        return
    r = lax.broadcasted_iota(jnp.int32, (rows, width), 0)
    c = lax.broadcasted_iota(jnp.int32, (rows, width), 1)
    o_ref[0] = jnp.where(c >= r, t, NEG_INF).astype(o_ref.dtype)
    o_ref[1] = jnp.where(c < r, t, 0.0).astype(o_ref.dtype)


def _toeplitz(w, rows, diff):
    n, _, width = w.shape
    oshape = (n, 2, rows, width) if diff else (n, rows, width)
    oblock = (None,) + oshape[1:]
    return pl.pallas_call(
        functools.partial(_toeplitz_kernel, diff=diff),
        grid=(n,),
        in_specs=[pl.BlockSpec((None, 1, width), lambda i: (i, 0, 0))],
        out_specs=pl.BlockSpec(oblock, lambda i: (i,) + (0,) * (len(oshape) - 1)),
        out_shape=jax.ShapeDtypeStruct(oshape, BF16 if diff else F32),
        compiler_params=_params("parallel"),
        name="bias_diff" if diff else "bias_dilated",
    )(w)


def _layer_norm(h, g, b):
    mu = jnp.mean(h, axis=-1, keepdims=True)
    d = h - mu
    var = jnp.mean(d * d, axis=-1, keepdims=True)
    return d * lax.rsqrt(var + LN_EPS) * g + b


def _mix_kernel(x_ref, o0_ref, o1_ref, o2_ref, l0_ref, l1_ref, l2_ref, ob_ref, wg_ref, bg_ref, wa_ref, wb_ref,
                wo_ref, g_ref, b_ref, out_ref, *tok_sc):
    def token_major(ref, sc):
        dil, rows, _ = ref.shape
        if dil == 1:
            return lambda sl: ref[0, sl].astype(F32)
        nk = sc.shape[0]
        for res in range(dil):
            for k in range(nk):
                sc[k, pl.ds(res, rows, stride=dil), :] = ref[res, :, k * LANES:(k + 1) * LANES].astype(F32)
        return lambda sl: jnp.concatenate([sc[k, sl] for k in range(nk)], axis=1)

    l_g = (token_major(l0_ref, None), token_major(l1_ref, tok_sc[0]), token_major(l2_ref, tok_sc[1]))
    o_g = (token_major(o0_ref, None), token_major(o1_ref, tok_sc[2]), token_major(o2_ref, tok_sc[3]))
    tm = x_ref.shape[0]
    for r0 in range(0, tm, ROW_SLAB):
        sl = slice(r0, r0 + ROW_SLAB)
        x = x_ref[sl]
        xb = x.astype(BF16)
        l0, l1, l2 = (f(sl) for f in l_g)
        mx = jnp.maximum(jnp.maximum(l0, l1), l2)
        e0, e1, e2 = jnp.exp(l0 - mx), jnp.exp(l1 - mx), jnp.exp(l2 - mx)
        o_a = (e0 * o_g[0](sl) + e1 * o_g[1](sl) + e2 * o_g[2](sl)) / (e0 + e1 + e2)
        y_a = _dot(o_a.astype(BF16), wa_ref[...])
        y_b = _dot(ob_ref[sl], wb_ref[...])
        gate_a = jax.nn.sigmoid(_dot(xb, wg_ref[:, :D_MODEL]) + bg_ref[:, :D_MODEL])
        gate_b = jax.nn.sigmoid(_dot(xb, wg_ref[:, D_MODEL:]) + bg_ref[:, D_MODEL:])
        merged = gate_a * y_a + gate_b * y_b
        mix = _dot(merged.astype(BF16), wo_ref[...])
        out_ref[sl] = _layer_norm(DEEPNORM_ALPHA * x + mix, g_ref[...], b_ref[...])


def _mix(x, o_g, lse_g, o_b, w_gate, b_gate, w_a, w_b, w_o, ln_g, ln_b, tm):
    B, S, D = x.shape
    tok = lambda w: pl.BlockSpec((None, tm, w), lambda b, i: (b, i, 0))
    res = [pl.BlockSpec((None, a.shape[1], tm // a.shape[1], W_A_OUT), lambda b, i: (b, 0, i, 0)) for a in o_g]
    n_sc = 2 * sum(a.shape[1] > 1 for a in o_g)
    return pl.pallas_call(
        _mix_kernel,
        grid=(B, S // tm),
        in_specs=[tok(D)] + res + res + [tok(W_B_OUT)]
        + [_resident(a.shape) for a in (w_gate, b_gate, w_a, w_b, w_o, ln_g, ln_b)],
        out_specs=tok(D),
        out_shape=jax.ShapeDtypeStruct((B, S, D), F32),
        scratch_shapes=[pltpu.VMEM((W_A_OUT // LANES, tm, LANES), F32)] * n_sc,
        compiler_params=_params("parallel", "parallel"),
        name="mix",
    )(x, *o_g, *lse_g, o_b, w_gate, b_gate, w_a, w_b, w_o, ln_g, ln_b)


def _mlp_kernel(x_ref, w1_ref, w2_ref, g_ref, b_ref, out_ref):
    for r0 in range(0, x_ref.shape[0], ROW_SLAB):
        rows = slice(r0, r0 + ROW_SLAB)
        x = x_ref[rows]
        xb = x.astype(BF16)
        ff = jnp.zeros(x.shape, F32)
        for c in range(D_FF // (2 * CHUNK)):
            sl = slice(c * 2 * CHUNK, (c + 1) * 2 * CHUNK)
            h = jnp.maximum(_dot(xb, w1_ref[:, sl]), 0.0)
            ff = ff + _dot((h * h).astype(BF16), w2_ref[sl, :])
        out_ref[rows] = _layer_norm(DEEPNORM_ALPHA * x + ff, g_ref[...], b_ref[...])


def _mlp(x, w1, w2, ln_g, ln_b, tm):
    B, S, D = x.shape
    tok = pl.BlockSpec((None, tm, D), lambda b, i: (b, i, 0))
    return pl.pallas_call(
        _mlp_kernel,
        grid=(B, S // tm),
        in_specs=[tok] + [_resident(a.shape) for a in (w1, w2, ln_g, ln_b)],
        out_specs=tok,
        out_shape=jax.ShapeDtypeStruct((B, S, D), F32),
        compiler_params=_params("parallel", "parallel"),
        name="mlp",
    )(x, w1, w2, ln_g, ln_b)


def _t5_bucket(dist):
    n = jnp.maximum(dist, 0)
    max_exact = NUM_BUCKETS // 2
    nf = jnp.maximum(n, 1).astype(F32)
    large = max_exact + (jnp.log(nf / max_exact) / math.log(T5_MAX_DISTANCE / max_exact)
                         * (NUM_BUCKETS - max_exact)).astype(jnp.int32)
    large = jnp.minimum(large, NUM_BUCKETS - 1)
    return jnp.where(n < max_exact, n, large)


def _dilated_bias(rel_bias, g, dil):
    j = jnp.arange(2 * BLK)
    tab = rel_bias[:, g * H_A:(g + 1) * H_A].astype(F32)
    vec = tab[_t5_bucket(jnp.maximum(BLK - j, 0) * dil)]
    vec = jnp.where((j <= BLK)[:, None], vec, NEG_INF)
    bias = _toeplitz(vec.T.reshape(H_A, 1, 2 * BLK), BLK, diff=False)
    return bias.reshape(H_A // 2, 2 * BLK, 2 * BLK)


def _diff_bias(rel_bias, tq):
    assert tq >= T5_MAX_DISTANCE
    tab = rel_bias[:, N_GROUPS * H_A:].astype(F32)
    tab = (tab - tab[NUM_BUCKETS - 1:NUM_BUCKETS]) * LOG2E
    vec = tab[_t5_bucket(jnp.arange(tq))]
    return _toeplitz(vec.T.reshape(H_B, 1, tq), tq, diff=True)


def kernel(x, w_in, b_gate, lambda_q1, lambda_k1, lambda_q2, lambda_k2, subln_g, rel_bias, w_proj_a, w_proj_b,
           w_out, ln1_g, ln1_b, ln2_g, ln2_b, w_mlp1, w_mlp2):
    B, S, D = x.shape
    tq = min(TQ, S)
    tm = min(TM, S)
    w = w_in[0]
    w_a = w[:, :COLS_A].reshape(D, 3, N_GROUPS, W_A_OUT)
    w_a = jnp.concatenate([w_a[:, :1] * QK_SCALE, w_a[:, 1:]], axis=1).transpose(0, 2, 1, 3).reshape(D, COLS_A)
    w_qk = w[:, COLS_A:COLS_A + COLS_B_QK].reshape(D, 4, H_B, HEAD_DIM)
    w_qb = jnp.concatenate([w_qk[:, 0], w_qk[:, 1]], axis=-1).reshape(D, W_B_OUT)
    w_kb = jnp.concatenate([w_qk[:, 2], w_qk[:, 3]], axis=-1).reshape(D, W_B_OUT)
    w_vb = w[:, COLS_A + COLS_B_QK:COLS_A + COLS_B]
    w_n = jnp.concatenate([w_a, w_kb], axis=1).astype(BF16)
    w_t = jnp.concatenate([w_qb, w_vb], axis=1).T.astype(BF16)
    w_gate = w[:, COLS_A + COLS_B:].astype(BF16)

    *a_g, kb, pt = _proj(x, w_n, w_t, tm, tq)

    o_g, lse_g = [], []
    for g, (win, dil) in enumerate(DIL_PAIRS):
        assert win // dil == BLK
        o, lse = _dilated(a_g[g], _dilated_bias(rel_bias, g, dil), g)
        o_g.append(o)
        lse_g.append(lse)

    lam_init = 0.8 - 0.6 * math.exp(-0.3 * 0)
    lam_p = jnp.concatenate([lambda_q1, lambda_k1, lambda_q2, lambda_k2], axis=0).astype(F32)
    o_b = _diff(lam_p, kb, pt, _diff_bias(rel_bias, min(DIFF_T, S)), subln_g[0].reshape(DV_B, 1), lam_init)

    x1 = _mix(x, o_g, lse_g, o_b, w_gate, b_gate, w_proj_a[0].astype(BF16), w_proj_b[0].astype(BF16),
              w_out[0].astype(BF16), ln1_g, ln1_b, tm)
    return _mlp(x1, w_mlp1[0].astype(BF16), w_mlp2[0].astype(BF16), ln2_g, ln2_b, tm)
```

```python
import functools
import math

import jax
import jax.numpy as jnp
import numpy as np
from jax import lax
from jax.experimental import pallas as pl
from jax.experimental.pallas import tpu as pltpu

D_MODEL = 1024
HEAD_DIM = 64
DIL_PAIRS = ((128, 1), (512, 4), (2048, 16))
N_GROUPS = len(DIL_PAIRS)
H_A = 8
H_B = 8
DV_B = 2 * HEAD_DIM
D_FF = 4 * D_MODEL
NUM_BUCKETS = 32
T5_MAX_DISTANCE = 128
BLK = 128
LN_EPS = 1e-5
NEG_INF = -1e30
W_A_OUT = H_A * HEAD_DIM
W_B_OUT = H_B * DV_B
COLS_A = 3 * N_GROUPS * H_A * HEAD_DIM
COLS_B_QK = 4 * H_B * HEAD_DIM
COLS_B = COLS_B_QK + H_B * DV_B
DEPTH = 1
DEEPNORM_ALPHA = (2.0 * DEPTH) ** 0.25
QK_SCALE = HEAD_DIM ** -0.5
LOG2E = math.log2(math.e)

LANES = 128
COLS_N = COLS_A + H_B * 2 * HEAD_DIM
COLS_T = 2 * W_B_OUT
CHUNK = 512
TQ = 512
DIFF_T = 1024
STRIP = 512
PAIRS_PER_BLOCK = 2
ACC_SLOTS = PAIRS_PER_BLOCK
TM = 512
ROW_SLAB = 256
DILATED_ROWS = 1024
VMEM_LIMIT = 56 * 1024 * 1024

BF16 = jnp.bfloat16
F32 = jnp.float32


def _dot(a, b):
    return jnp.dot(a, b, preferred_element_type=F32)


def _dot_nt(a, b):
    return lax.dot_general(a, b, (((1,), (1,)), ((), ())), preferred_element_type=F32)


def _resident(shape):
    nd = len(shape)
    return pl.BlockSpec(shape, lambda *_: (0,) * nd, pipeline_mode=pl.Buffered(1))


def _params(*sem, flags=None):
    return pltpu.CompilerParams(dimension_semantics=sem, vmem_limit_bytes=VMEM_LIMIT, flags=flags)


def _proj_kernel(x_ref, wn_ref, wt_ref, a0_ref, a1_ref, a2_ref, kb_ref, ot_ref, rows_sc, *, tq):
    xb = x_ref[...].astype(BF16)
    tm = xb.shape[0]
    a_refs = (a0_ref, a1_ref, a2_ref)
    for c in range(COLS_N // CHUNK):
        r = _dot(xb, wn_ref[:, c * CHUNK:(c + 1) * CHUNK])
        g, part = divmod(c, 3)
        if g >= N_GROUPS:
            kb_ref[:, (c - 3 * N_GROUPS) * CHUNK:(c - 3 * N_GROUPS + 1) * CHUNK] = r.astype(BF16)
            continue
        sl = slice(part * W_A_OUT, (part + 1) * W_A_OUT)
        dil = DIL_PAIRS[g][1]
        if dil == 1:
            a_refs[g][0, :, sl] = r.astype(BF16)
            continue
        for k in range(CHUNK // LANES):
            rows_sc[k] = r[:, k * LANES:(k + 1) * LANES]
        for res in range(dil):
            for k in range(CHUNK // LANES):
                piece = rows_sc[k, pl.ds(res, tm // dil, stride=dil), :]
                a_refs[g][res, :, part * W_A_OUT + k * LANES:part * W_A_OUT + (k + 1) * LANES] = piece.astype(BF16)
    for c in range(COLS_T // CHUNK):
        sl = slice(c * CHUNK, (c + 1) * CHUNK)
        r = _dot_nt(wt_ref[sl, :], xb)
        if (c + 1) * CHUNK <= W_B_OUT:
            r = r * (QK_SCALE * LOG2E)
        r = r.astype(BF16)
        for t in range(tm // tq):
            ot_ref[t, sl, :] = r[:, t * tq:(t + 1) * tq]


def _proj(x, w_n, w_t, tm, tq):
    B, S, D = x.shape
    assert CHUNK == W_A_OUT
    dils = [d for _, d in DIL_PAIRS]
    return pl.pallas_call(
        functools.partial(_proj_kernel, tq=tq),
        grid=(B, S // tm),
        in_specs=[
            pl.BlockSpec((None, tm, D), lambda b, i: (b, i, 0)),
            _resident((D, COLS_N)),
            _resident((COLS_T, D)),
        ],
        out_specs=[pl.BlockSpec((None, d, tm // d, 3 * W_A_OUT), lambda b, i: (b, 0, i, 0)) for d in dils] + [
            pl.BlockSpec((None, tm, W_B_OUT), lambda b, i: (b, i, 0)),
            pl.BlockSpec((None, tm // tq, COLS_T, tq), lambda b, i: (b, i, 0, 0)),
        ],
        out_shape=[jax.ShapeDtypeStruct((B, d, S // d, 3 * W_A_OUT), BF16) for d in dils] + [
            jax.ShapeDtypeStruct((B, S, W_B_OUT), BF16),
            jax.ShapeDtypeStruct((B, S // tq, COLS_T, tq), BF16),
        ],
        scratch_shapes=[pltpu.VMEM((CHUNK // LANES, tm, LANES), F32)],
        compiler_params=_params("parallel", "parallel"),
        name="proj",
    )(x, w_n, w_t)


def _dilated_kernel(q_ref, k_ref, v_ref, kp_ref, vp_ref, bias_ref, o_ref, lse_ref, kw_sc, vw_sc):
    nres, tb, _ = q_ref.shape
    first = pl.program_id(2) == 0
    lane = lax.broadcasted_iota(jnp.int32, (BLK, LANES), 1)
    lo = lane < HEAD_DIM
    in_prev = lax.broadcasted_iota(jnp.int32, (2 * BLK, 2 * BLK), 1) < BLK
    ones = jnp.ones((2 * BLK, LANES), BF16)
    for r in range(nres):
        kw_sc[r, :BLK], kw_sc[r, BLK:] = kp_ref[r], k_ref[r]
        vw_sc[r, :BLK], vw_sc[r, BLK:] = vp_ref[r], v_ref[r]
        for jb in range(tb // BLK):
            rows = slice(jb * BLK, (jb + 1) * BLK)
            win = slice(jb * BLK, (jb + 2) * BLK)
            for p in range(H_A // 2):
                cols = slice(p * LANES, (p + 1) * LANES)
                q2 = q_ref[r, rows, cols]
                zero = jnp.zeros_like(q2)
                qd = jnp.concatenate([jnp.where(lo, q2, zero), jnp.where(lo, zero, q2)], axis=0)
                s = _dot_nt(qd, kw_sc[r, win, cols]) + bias_ref[p]
                if jb == 0:
                    s = jnp.where(first & in_prev, NEG_INF, s)
                m = jnp.max(s, axis=-1, keepdims=True)
                pw = jnp.exp((s - m).astype(BF16))
                oa = _dot(pw, jnp.concatenate([vw_sc[r, win, cols], ones], axis=1))
                den = oa[:, LANES:]
                o2 = oa[:, :LANES] / den
                lse = m + jnp.log(den)
                o_ref[r, rows, cols] = jnp.where(lo, o2[:BLK], o2[BLK:])
                lse_ref[r, rows, cols] = jnp.where(lo, lse[:BLK], lse[BLK:])


def _dilated(a, bias, g):
    B, dil, L, _ = a.shape
    tb = min(DILATED_ROWS, L)
    nres = min(DILATED_ROWS // tb, dil)
    cur = lambda c: pl.BlockSpec((None, nres, tb, W_A_OUT), lambda b, r, n: (b, r, n, c))
    prev = lambda c: pl.BlockSpec(
        (None, nres, BLK, W_A_OUT), lambda b, r, n: (b, r, jnp.maximum(n * (tb // BLK) - 1, 0), c))
    out = pl.BlockSpec((None, nres, tb, W_A_OUT), lambda b, r, n: (b, r, n, 0))
    return pl.pallas_call(
        _dilated_kernel,
        grid=(B, dil // nres, L // tb),
        in_specs=[cur(0), cur(1), cur(2), prev(1), prev(2), _resident(bias.shape)],
        out_specs=[out, out],
        out_shape=[jax.ShapeDtypeStruct((B, dil, L, W_A_OUT), F32)] * 2,
        scratch_shapes=[pltpu.VMEM((nres, BLK + tb, W_A_OUT), BF16)] * 2,
        compiler_params=_params("parallel", "parallel", "arbitrary"),
        name=f"dilated{g}",
    )(a, a, a, a, a, bias)


def _diff_kernel(lam_ref, qt_ref, k_ref, vt_ref, bias_ref, g_ref, o_ref, qd_sc, va_sc, s0_sc, s1_sc, x0_sc, x1_sc,
                 m_sc, acc_sc, *, lam_init):
    nb, _, tb = qt_ref.shape
    tq = tk = bias_ref.shape[-1]
    qb = tq // tb
    nq = nb // qb
    s_bufs, x_bufs = (s0_sc, s1_sc), (x0_sc, x1_sc)
    row = lax.broadcasted_iota(jnp.int32, (DV_B, tb), 0)
    for i in range(nb):
        blk, part = divmod(i, qb)
        qt = qt_ref[i]
        zero = jnp.zeros_like(qt)
        qd_sc[blk, :, part * tb:(part + 1) * tb] = jnp.where(row < HEAD_DIM, qt, zero)
        qd_sc[blk, :, tq + part * tb:tq + (part + 1) * tb] = jnp.where(row < HEAD_DIM, zero, qt)
        va_sc[blk, :DV_B, part * tb:(part + 1) * tb] = vt_ref[i]
    for i in range(nq):
        va_sc[i, DV_B:, :] = jnp.ones((va_sc.shape[1] - DV_B, tk), BF16)

    lam = (jnp.exp(jnp.sum(lam_ref[0:1, :] * lam_ref[1:2, :], axis=-1, keepdims=True))
           - jnp.exp(jnp.sum(lam_ref[2:3, :] * lam_ref[3:4, :], axis=-1, keepdims=True)) + lam_init)

    def reset(a):
        m_sc[a] = jnp.full(m_sc.shape[1:], -jnp.inf, F32)
        acc_sc[a] = jnp.zeros(acc_sc.shape[1:], F32)

    def finalize(qi):
        a = qi % ACC_SLOTS
        acc = acc_sc[a]
        on = acc[:DV_B] / acc[DV_B:DV_B + 1]
        o = on[:, :tq] - lam * on[:, tq:]
        ms = jnp.mean(o * o, axis=0, keepdims=True)
        o = o * lax.rsqrt(ms + LN_EPS) * g_ref[...] * (1.0 - lam_init)
        o_ref[pl.ds(pl.multiple_of(qi * tq, tq), tq), :] = o.T.astype(BF16)
        reset(a)

    def is_last(qi, j):
        return j == qi

    strips = [slice(c * STRIP, (c + 1) * STRIP) for c in range(2 * tq // STRIP)]

    def score_strip(s_buf, x_buf, k0, qi, bi, cs):
        q0 = cs.start % tq
        rows = min(tk, q0 + STRIP) if bi == 0 else tk
        s = _dot(k_ref[pl.ds(k0, rows), :], qd_sc[qi, :, cs])
        if bi is not None:
            s = s + bias_ref[bi, :rows, q0:q0 + STRIP].astype(F32)
        s_buf[:rows, cs] = s
        if rows < tk:
            s_buf[rows:, cs] = jnp.full((tk - rows, STRIP), NEG_INF, F32)
        x_buf[:, cs] = jnp.max(s, axis=0, keepdims=True)

    for a in range(ACC_SLOTS):
        reset(a)
    for cs in strips:
        score_strip(s_bufs[0], x_bufs[0], 0, 0, 0, cs)

    def half(slot, qi, j):
        s_cur, x_cur, s_oth, x_oth = s_bufs[slot], x_bufs[slot], s_bufs[1 - slot], x_bufs[1 - slot]
        a = qi % ACC_SLOTS
        last = is_last(qi, j)
        j_n = jnp.where(last, 0, j + 1)
        qi_n = jnp.minimum(jnp.where(last, qi + 1, qi), nq - 1)
        k0 = pl.multiple_of(j_n * tk, tk)
        va = va_sc[j]

        def run(bi):
            for cs in strips:
                m_old = m_sc[a, :, cs]
                m_new = jnp.maximum(m_old, x_cur[:, cs])
                p = jnp.exp2((s_cur[:, cs] - m_new).astype(BF16))
                alpha = jnp.exp2(m_old - m_new)
                m_sc[a, :, cs] = m_new
                acc_sc[a, :, cs] = acc_sc[a, :, cs] * alpha + _dot(va, p)
                score_strip(s_oth, x_oth, k0, qi_n, bi, cs)

        pl.when(j_n == qi_n)(functools.partial(run, 0))
        pl.when(j_n == qi_n - 1)(functools.partial(run, 1))
        pl.when(j_n < qi_n - 1)(functools.partial(run, None))
        return qi_n, j_n

    def run_pairs(count, carry):
        qi, j = carry
        done = []
        for h in range(count):
            done.append((qi, j))
            qi, j = half(h % 2, qi, j)
        for q, jj in done:
            pl.when(is_last(q, jj))(functools.partial(finalize, q))
        return qi, j

    npairs = nq * (nq + 1) // 2
    carry = lax.fori_loop(0, npairs // PAIRS_PER_BLOCK, lambda _, c: run_pairs(PAIRS_PER_BLOCK, c),
                          (jnp.int32(0), jnp.int32(0)))
    if npairs % PAIRS_PER_BLOCK:
        run_pairs(npairs % PAIRS_PER_BLOCK, carry)


def _diff(lam_p, kb, pt, bias, g_col, lam_init):
    B, S, _ = kb.shape
    nb, tb = pt.shape[1], pt.shape[3]
    tq = tk = bias.shape[-1]
    nq = S // tq
    assert tq % tb == 0 and tq % STRIP == 0 and bias.shape[1:] == (2, tk, tq)
    va_rows = DV_B + 16
    return pl.pallas_call(
        functools.partial(_diff_kernel, lam_init=lam_init),
        grid=(B, H_B),
        in_specs=[
            _resident(lam_p.shape),
            pl.BlockSpec((None, nb, DV_B, tb), lambda b, h: (b, 0, h, 0)),
            pl.BlockSpec((None, S, LANES), lambda b, h: (b, 0, h)),
            pl.BlockSpec((None, nb, DV_B, tb), lambda b, h: (b, 0, H_B + h, 0)),
            pl.BlockSpec((None, 2, tk, tq), lambda b, h: (h, 0, 0, 0)),
            _resident(g_col.shape),
        ],
        out_specs=pl.BlockSpec((None, S, DV_B), lambda b, h: (b, 0, h)),
        out_shape=jax.ShapeDtypeStruct((B, S, W_B_OUT), BF16),
        scratch_shapes=[
            pltpu.VMEM((nq, DV_B, 2 * tq), BF16),
            pltpu.VMEM((S // tk, va_rows, tk), BF16),
            pltpu.VMEM((tk, 2 * tq), F32),
            pltpu.VMEM((tk, 2 * tq), F32),
            pltpu.VMEM((1, 2 * tq), F32),
            pltpu.VMEM((1, 2 * tq), F32),
            pltpu.VMEM((ACC_SLOTS, 1, 2 * tq), F32),
            pltpu.VMEM((ACC_SLOTS, va_rows, 2 * tq), F32),
        ],
        compiler_params=_params("parallel", "parallel"),
        name="diff_attn",
    )(lam_p, pt, kb, pt, bias, g_col)


def _toeplitz_kernel(w_ref, o_ref, *, diff):
    width = o_ref.shape[-1]
    rows = width if diff else o_ref.shape[-2]
    t = pltpu.roll(jnp.broadcast_to(w_ref[...], (rows, width)), 0, 1, stride=1, stride_axis=0)
    if not diff:
        o_ref[...] = t
        return
    r = lax.broadcasted_iota(jnp.int32, (rows, width), 0)
    c = lax.broadcasted_iota(jnp.int32, (rows, width), 1)
    o_ref[0] = jnp.where(c >= r, t, NEG_INF).astype(o_ref.dtype)
    o_ref[1] = jnp.where(c < r, t, 0.0).astype(o_ref.dtype)


def _toeplitz(w, rows, diff):
    n, _, width = w.shape
    oshape = (n, 2, rows, width) if diff else (n, rows, width)
    oblock = (None,) + oshape[1:]
    return pl.pallas_call(
        functools.partial(_toeplitz_kernel, diff=diff),
        grid=(n,),
        in_specs=[pl.BlockSpec((None, 1, width), lambda i: (i, 0, 0))],
        out_specs=pl.BlockSpec(oblock, lambda i: (i,) + (0,) * (len(oshape) - 1)),
        out_shape=jax.ShapeDtypeStruct(oshape, BF16 if diff else F32),
        compiler_params=_params("parallel"),
        name="bias_diff" if diff else "bias_dilated",
    )(w)


def _layer_norm(h, g, b):
    mu = jnp.mean(h, axis=-1, keepdims=True)
    d = h - mu
    var = jnp.mean(d * d, axis=-1, keepdims=True)
    return d * lax.rsqrt(var + LN_EPS) * g + b


def _mix_kernel(x_ref, o0_ref, o1_ref, o2_ref, l0_ref, l1_ref, l2_ref, ob_ref, wg_ref, bg_ref, wa_ref, wb_ref,
                wo_ref, g_ref, b_ref, out_ref, *tok_sc):
    def token_major(ref, sc):
        dil, rows, _ = ref.shape
        if dil == 1:
            return lambda sl: ref[0, sl]
        nk = sc.shape[0]
        for res in range(dil):
            for k in range(nk):
                sc[k, pl.ds(res, rows, stride=dil), :] = ref[res, :, k * LANES:(k + 1) * LANES]
        return lambda sl: jnp.concatenate([sc[k, sl] for k in range(nk)], axis=1)

    l_g = (token_major(l0_ref, None), token_major(l1_ref, tok_sc[0]), token_major(l2_ref, tok_sc[1]))
    o_g = (token_major(o0_ref, None), token_major(o1_ref, tok_sc[2]), token_major(o2_ref, tok_sc[3]))
    tm = x_ref.shape[0]
    for r0 in range(0, tm, ROW_SLAB):
        sl = slice(r0, r0 + ROW_SLAB)
        x = x_ref[sl]
        xb = x.astype(BF16)
        l0, l1, l2 = (f(sl) for f in l_g)
        mx = jnp.maximum(jnp.maximum(l0, l1), l2)
        e0, e1, e2 = jnp.exp(l0 - mx), jnp.exp(l1 - mx), jnp.exp(l2 - mx)
        o_a = (e0 * o_g[0](sl) + e1 * o_g[1](sl) + e2 * o_g[2](sl)) / (e0 + e1 + e2)
        y_a = _dot(o_a.astype(BF16), wa_ref[...])
        y_b = _dot(ob_ref[sl], wb_ref[...])
        gate_a = jax.nn.sigmoid(_dot(xb, wg_ref[:, :D_MODEL]) + bg_ref[:, :D_MODEL])
        gate_b = jax.nn.sigmoid(_dot(xb, wg_ref[:, D_MODEL:]) + bg_ref[:, D_MODEL:])
        merged = gate_a * y_a + gate_b * y_b
        mix = _dot(merged.astype(BF16), wo_ref[...])
        out_ref[sl] = _layer_norm(DEEPNORM_ALPHA * x + mix, g_ref[...], b_ref[...])


def _mix(x, o_g, lse_g, o_b, w_gate, b_gate, w_a, w_b, w_o, ln_g, ln_b, tm):
    B, S, D = x.shape
    tok = lambda w: pl.BlockSpec((None, tm, w), lambda b, i: (b, i, 0))
    res = [pl.BlockSpec((None, a.shape[1], tm // a.shape[1], W_A_OUT), lambda b, i: (b, 0, i, 0)) for a in o_g]
    n_sc = 2 * sum(a.shape[1] > 1 for a in o_g)
    return pl.pallas_call(
        _mix_kernel,
        grid=(B, S // tm),
        in_specs=[tok(D)] + res + res + [tok(W_B_OUT)]
        + [_resident(a.shape) for a in (w_gate, b_gate, w_a, w_b, w_o, ln_g, ln_b)],
        out_specs=tok(D),
        out_shape=jax.ShapeDtypeStruct((B, S, D), F32),
        scratch_shapes=[pltpu.VMEM((W_A_OUT // LANES, tm, LANES), F32)] * n_sc,
        compiler_params=_params("parallel", "parallel"),
        name="mix",
    )(x, *o_g, *lse_g, o_b, w_gate, b_gate, w_a, w_b, w_o, ln_g, ln_b)


def _mlp_kernel(x_ref, w1_ref, w2_ref, g_ref, b_ref, out_ref):
    for r0 in range(0, x_ref.shape[0], ROW_SLAB):
        rows = slice(r0, r0 + ROW_SLAB)
        x = x_ref[rows]
        xb = x.astype(BF16)
        ff = jnp.zeros(x.shape, F32)
        for c in range(D_FF // (2 * CHUNK)):
            sl = slice(c * 2 * CHUNK, (c + 1) * 2 * CHUNK)
            h = jnp.maximum(_dot(xb, w1_ref[:, sl]), 0.0)
            ff = ff + _dot((h * h).astype(BF16), w2_ref[sl, :])
        out_ref[rows] = _layer_norm(DEEPNORM_ALPHA * x + ff, g_ref[...], b_ref[...])


def _mlp(x, w1, w2, ln_g, ln_b, tm):
    B, S, D = x.shape
    tok = pl.BlockSpec((None, tm, D), lambda b, i: (b, i, 0))
    return pl.pallas_call(
        _mlp_kernel,
        grid=(B, S // tm),
        in_specs=[tok] + [_resident(a.shape) for a in (w1, w2, ln_g, ln_b)],
        out_specs=tok,
        out_shape=jax.ShapeDtypeStruct((B, S, D), F32),
        compiler_params=_params("parallel", "parallel"),
        name="mlp",
    )(x, w1, w2, ln_g, ln_b)


def _t5_bucket(dist):
    n = jnp.maximum(dist, 0)
    max_exact = NUM_BUCKETS // 2
    nf = jnp.maximum(n, 1).astype(F32)
    large = max_exact + (jnp.log(nf / max_exact) / math.log(T5_MAX_DISTANCE / max_exact)
                         * (NUM_BUCKETS - max_exact)).astype(jnp.int32)
    large = jnp.minimum(large, NUM_BUCKETS - 1)
    return jnp.where(n < max_exact, n, large)


def _dilated_bias(rel_bias, g, dil):
    j = jnp.arange(2 * BLK)
    tab = rel_bias[:, g * H_A:(g + 1) * H_A].astype(F32)
    vec = tab[_t5_bucket(jnp.maximum(BLK - j, 0) * dil)]
    vec = jnp.where((j <= BLK)[:, None], vec, NEG_INF)
    bias = _toeplitz(vec.T.reshape(H_A, 1, 2 * BLK), BLK, diff=False)
    return bias.reshape(H_A // 2, 2 * BLK, 2 * BLK)


def _diff_bias(rel_bias, tq):
    assert tq >= T5_MAX_DISTANCE
    tab = rel_bias[:, N_GROUPS * H_A:].astype(F32)
    tab = (tab - tab[NUM_BUCKETS - 1:NUM_BUCKETS]) * LOG2E
    vec = tab[_t5_bucket(jnp.arange(tq))]
    return _toeplitz(vec.T.reshape(H_B, 1, tq), tq, diff=True)


def kernel(x, w_in, b_gate, lambda_q1, lambda_k1, lambda_q2, lambda_k2, subln_g, rel_bias, w_proj_a, w_proj_b,
           w_out, ln1_g, ln1_b, ln2_g, ln2_b, w_mlp1, w_mlp2):
    B, S, D = x.shape
    tq = min(TQ, S)
    tm = min(TM, S)
    w = w_in[0]
    w_a = w[:, :COLS_A].reshape(D, 3, N_GROUPS, W_A_OUT)
    w_a = jnp.concatenate([w_a[:, :1] * QK_SCALE, w_a[:, 1:]], axis=1).transpose(0, 2, 1, 3).reshape(D, COLS_A)
    w_qk = w[:, COLS_A:COLS_A + COLS_B_QK].reshape(D, 4, H_B, HEAD_DIM)
    w_qb = jnp.concatenate([w_qk[:, 0], w_qk[:, 1]], axis=-1).reshape(D, W_B_OUT)
    w_kb = jnp.concatenate([w_qk[:, 2], w_qk[:, 3]], axis=-1).reshape(D, W_B_OUT)
    w_vb = w[:, COLS_A + COLS_B_QK:COLS_A + COLS_B]
    w_n = jnp.concatenate([w_a, w_kb], axis=1).astype(BF16)
    w_t = jnp.concatenate([w_qb, w_vb], axis=1).T.astype(BF16)
    w_gate = w[:, COLS_A + COLS_B:].astype(BF16)

    *a_g, kb, pt = _proj(x, w_n, w_t, tm, tq)

    o_g, lse_g = [], []
    for g, (win, dil) in enumerate(DIL_PAIRS):
        assert win // dil == BLK
        o, lse = _dilated(a_g[g], _dilated_bias(rel_bias, g, dil), g)
        o_g.append(o)
        lse_g.append(lse)

    lam_init = 0.8 - 0.6 * math.exp(-0.3 * 0)
    lam_p = jnp.concatenate([lambda_q1, lambda_k1, lambda_q2, lambda_k2], axis=0).astype(F32)
    o_b = _diff(lam_p, kb, pt, _diff_bias(rel_bias, min(DIFF_T, S)), subln_g[0].reshape(DV_B, 1), lam_init)

    x1 = _mix(x, o_g, lse_g, o_b, w_gate, b_gate, w_proj_a[0].astype(BF16), w_proj_b[0].astype(BF16),
              w_out[0].astype(BF16), ln1_g, ln1_b, tm)
    return _mlp(x1, w_mlp1[0].astype(BF16), w_mlp2[0].astype(BF16), ln2_g, ln2_b, tm)
```

```python
import functools
import math

import jax
import jax.numpy as jnp
import numpy as np
from jax import lax
from jax.experimental import pallas as pl
from jax.experimental.pallas import tpu as pltpu

D_MODEL = 1024
HEAD_DIM = 64
DIL_PAIRS = ((128, 1), (512, 4), (2048, 16))
N_GROUPS = len(DIL_PAIRS)
H_A = 8
H_B = 8
DV_B = 2 * HEAD_DIM
D_FF = 4 * D_MODEL
NUM_BUCKETS = 32
T5_MAX_DISTANCE = 128
BLK = 128
LN_EPS = 1e-5
NEG_INF = -1e30
W_A_OUT = H_A * HEAD_DIM
W_B_OUT = H_B * DV_B
COLS_A = 3 * N_GROUPS * H_A * HEAD_DIM
COLS_B_QK = 4 * H_B * HEAD_DIM
COLS_B = COLS_B_QK + H_B * DV_B
DEPTH = 1
DEEPNORM_ALPHA = (2.0 * DEPTH) ** 0.25
QK_SCALE = HEAD_DIM ** -0.5
LOG2E = math.log2(math.e)

LANES = 128
COLS_N = COLS_A + H_B * 2 * HEAD_DIM
COLS_T = 2 * W_B_OUT
CHUNK = 512
TQ = 512
DIFF_T = 1024
STRIP = 512
PAIRS_PER_BLOCK = 2
ACC_SLOTS = PAIRS_PER_BLOCK
TM = 512
ROW_SLAB = 256
DILATED_ROWS = 1024
VMEM_LIMIT = 56 * 1024 * 1024

BF16 = jnp.bfloat16
F32 = jnp.float32


def _dot(a, b):
    return jnp.dot(a, b, preferred_element_type=F32)


def _dot_nt(a, b):
    return lax.dot_general(a, b, (((1,), (1,)), ((), ())), preferred_element_type=F32)


def _resident(shape):
    nd = len(shape)
    return pl.BlockSpec(shape, lambda *_: (0,) * nd, pipeline_mode=pl.Buffered(1))


def _params(*sem, flags=None):
    return pltpu.CompilerParams(dimension_semantics=sem, vmem_limit_bytes=VMEM_LIMIT, flags=flags)


def _proj_kernel(x_ref, wn_ref, wt_ref, a0_ref, a1_ref, a2_ref, kb_ref, ot_ref, rows_sc, *, tq):
    xb = x_ref[...].astype(BF16)
    tm = xb.shape[0]
    a_refs = (a0_ref, a1_ref, a2_ref)
    for c in range(COLS_N // CHUNK):
        r = _dot(xb, wn_ref[:, c * CHUNK:(c + 1) * CHUNK])
        g, part = divmod(c, 3)
        if g >= N_GROUPS:
            kb_ref[:, (c - 3 * N_GROUPS) * CHUNK:(c - 3 * N_GROUPS + 1) * CHUNK] = r.astype(BF16)
            continue
        sl = slice(part * W_A_OUT, (part + 1) * W_A_OUT)
        dil = DIL_PAIRS[g][1]
        if dil == 1:
            a_refs[g][0, :, sl] = r.astype(BF16)
            continue
        for k in range(CHUNK // LANES):
            rows_sc[k] = r[:, k * LANES:(k + 1) * LANES]
        for res in range(dil):
            for k in range(CHUNK // LANES):
                piece = rows_sc[k, pl.ds(res, tm // dil, stride=dil), :]
                a_refs[g][res, :, part * W_A_OUT + k * LANES:part * W_A_OUT + (k + 1) * LANES] = piece.astype(BF16)
    for c in range(COLS_T // CHUNK):
        sl = slice(c * CHUNK, (c + 1) * CHUNK)
        r = _dot_nt(wt_ref[sl, :], xb)
        if (c + 1) * CHUNK <= W_B_OUT:
            r = r * (QK_SCALE * LOG2E)
        r = r.astype(BF16)
        for t in range(tm // tq):
            ot_ref[t, sl, :] = r[:, t * tq:(t + 1) * tq]


def _proj(x, w_n, w_t, tm, tq):
    B, S, D = x.shape
    assert CHUNK == W_A_OUT
    dils = [d for _, d in DIL_PAIRS]
    return pl.pallas_call(
        functools.partial(_proj_kernel, tq=tq),
        grid=(B, S // tm),
        in_specs=[
            pl.BlockSpec((None, tm, D), lambda b, i: (b, i, 0)),
            _resident((D, COLS_N)),
            _resident((COLS_T, D)),
        ],
        out_specs=[pl.BlockSpec((None, d, tm // d, 3 * W_A_OUT), lambda b, i: (b, 0, i, 0)) for d in dils] + [
            pl.BlockSpec((None, tm, W_B_OUT), lambda b, i: (b, i, 0)),
            pl.BlockSpec((None, tm // tq, COLS_T, tq), lambda b, i: (b, i, 0, 0)),
        ],
        out_shape=[jax.ShapeDtypeStruct((B, d, S // d, 3 * W_A_OUT), BF16) for d in dils] + [
            jax.ShapeDtypeStruct((B, S, W_B_OUT), BF16),
            jax.ShapeDtypeStruct((B, S // tq, COLS_T, tq), BF16),
        ],
        scratch_shapes=[pltpu.VMEM((CHUNK // LANES, tm, LANES), F32)],
        compiler_params=_params("parallel", "parallel"),
        name="proj",
    )(x, w_n, w_t)


def _dilated_kernel(q_ref, k_ref, v_ref, kp_ref, vp_ref, bias_ref, o_ref, lse_ref, kw_sc, vw_sc):
    nres, tb, _ = q_ref.shape
    first = pl.program_id(2) == 0
    lane = lax.broadcasted_iota(jnp.int32, (BLK, LANES), 1)
    lo = lane < HEAD_DIM
    in_prev = lax.broadcasted_iota(jnp.int32, (2 * BLK, 2 * BLK), 1) < BLK
    ones = jnp.ones((2 * BLK, LANES), BF16)
    for r in range(nres):
        kw_sc[r, :BLK], kw_sc[r, BLK:] = kp_ref[r], k_ref[r]
        vw_sc[r, :BLK], vw_sc[r, BLK:] = vp_ref[r], v_ref[r]
        for jb in range(tb // BLK):
            rows = slice(jb * BLK, (jb + 1) * BLK)
            win = slice(jb * BLK, (jb + 2) * BLK)
            for p in range(H_A // 2):
                cols = slice(p * LANES, (p + 1) * LANES)
                q2 = q_ref[r, rows, cols]
                zero = jnp.zeros_like(q2)
                qd = jnp.concatenate([jnp.where(lo, q2, zero), jnp.where(lo, zero, q2)], axis=0)
                s = _dot_nt(qd, kw_sc[r, win, cols]) + bias_ref[p]
                if jb == 0:
                    s = jnp.where(first & in_prev, NEG_INF, s)
                m = jnp.max(s, axis=-1, keepdims=True)
                pw = jnp.exp((s - m).astype(BF16))
                oa = _dot(pw, jnp.concatenate([vw_sc[r, win, cols], ones], axis=1))
                den = oa[:, LANES:]
                o2 = oa[:, :LANES] / den
                lse = m + jnp.log(den)
                o_ref[r, rows, cols] = jnp.where(lo, o2[:BLK], o2[BLK:])
                lse_ref[r, rows, cols] = jnp.where(lo, lse[:BLK], lse[BLK:])


def _dilated(a, bias, g):
    B, dil, L, _ = a.shape
    tb = min(DILATED_ROWS, L)
    nres = min(DILATED_ROWS // tb, dil)
    cur = lambda c: pl.BlockSpec((None, nres, tb, W_A_OUT), lambda b, r, n: (b, r, n, c))
    prev = lambda c: pl.BlockSpec(
        (None, nres, BLK, W_A_OUT), lambda b, r, n: (b, r, jnp.maximum(n * (tb // BLK) - 1, 0), c))
    out = pl.BlockSpec((None, nres, tb, W_A_OUT), lambda b, r, n: (b, r, n, 0))
    return pl.pallas_call(
        _dilated_kernel,
        grid=(B, dil // nres, L // tb),
        in_specs=[cur(0), cur(1), cur(2), prev(1), prev(2), _resident(bias.shape)],
        out_specs=[out, out],
        out_shape=[jax.ShapeDtypeStruct((B, dil, L, W_A_OUT), F32)] * 2,
        scratch_shapes=[pltpu.VMEM((nres, BLK + tb, W_A_OUT), BF16)] * 2,
        compiler_params=_params("parallel", "parallel", "arbitrary"),
        name=f"dilated{g}",
    )(a, a, a, a, a, bias)


def _diff_kernel(lam_ref, qt_ref, k_ref, vt_ref, bias_ref, g_ref, o_ref, qd_sc, va_sc, s0_sc, s1_sc, x0_sc, x1_sc,
                 m_sc, acc_sc, *, lam_init):
    nb, _, tb = qt_ref.shape
    tq = tk = bias_ref.shape[-1]
    qb = tq // tb
    nq = nb // qb
    s_bufs, x_bufs = (s0_sc, s1_sc), (x0_sc, x1_sc)
    row = lax.broadcasted_iota(jnp.int32, (DV_B, tb), 0)
    for i in range(nb):
        blk, part = divmod(i, qb)
        qt = qt_ref[i]
        zero = jnp.zeros_like(qt)
        qd_sc[blk, :, part * tb:(part + 1) * tb] = jnp.where(row < HEAD_DIM, qt, zero)
        qd_sc[blk, :, tq + part * tb:tq + (part + 1) * tb] = jnp.where(row < HEAD_DIM, zero, qt)
        va_sc[blk, :DV_B, part * tb:(part + 1) * tb] = vt_ref[i]
    for i in range(nq):
        va_sc[i, DV_B:, :] = jnp.ones((va_sc.shape[1] - DV_B, tk), BF16)

    lam = (jnp.exp(jnp.sum(lam_ref[0:1, :] * lam_ref[1:2, :], axis=-1, keepdims=True))
           - jnp.exp(jnp.sum(lam_ref[2:3, :] * lam_ref[3:4, :], axis=-1, keepdims=True)) + lam_init)

    def reset(a):
        m_sc[a] = jnp.full(m_sc.shape[1:], -jnp.inf, F32)
        acc_sc[a] = jnp.zeros(acc_sc.shape[1:], F32)

    def finalize(qi):
        a = qi % ACC_SLOTS
        acc = acc_sc[a]
        on = acc[:DV_B] / acc[DV_B:DV_B + 1]
        o = on[:, :tq] - lam * on[:, tq:]
        ms = jnp.mean(o * o, axis=0, keepdims=True)
        o = o * lax.rsqrt(ms + LN_EPS) * g_ref[...] * (1.0 - lam_init)
        o_ref[pl.ds(pl.multiple_of(qi * tq, tq), tq), :] = o.T.astype(BF16)
        reset(a)

    def is_last(qi, j):
        return j == qi

    strips = [slice(c * STRIP, (c + 1) * STRIP) for c in range(2 * tq // STRIP)]

    def score_strip(s_buf, x_buf, k0, qi, bi, cs):
        q0 = cs.start % tq
        rows = min(tk, q0 + STRIP) if bi == 0 else tk
        s = _dot(k_ref[pl.ds(k0, rows), :], qd_sc[qi, :, cs])
        if bi == 0:
            s = s + bias_ref[0, :rows, q0:q0 + STRIP].astype(F32)
        elif bi == 1 and q0 < T5_MAX_DISTANCE:
            far = tk - T5_MAX_DISTANCE
            s = jnp.concatenate([s[:far], s[far:] + bias_ref[1, far:, q0:q0 + STRIP].astype(F32)], axis=0)
        s_buf[:rows, cs] = s
        if rows < tk:
            s_buf[rows:, cs] = jnp.full((tk - rows, STRIP), NEG_INF, F32)
        x_buf[:, cs] = jnp.max(s, axis=0, keepdims=True)

    for a in range(ACC_SLOTS):
        reset(a)
    for cs in strips:
        score_strip(s_bufs[0], x_bufs[0], 0, 0, 0, cs)

    def half(slot, qi, j):
        s_cur, x_cur, s_oth, x_oth = s_bufs[slot], x_bufs[slot], s_bufs[1 - slot], x_bufs[1 - slot]
        a = qi % ACC_SLOTS
        last = is_last(qi, j)
        j_n = jnp.where(last, 0, j + 1)
        qi_n = jnp.minimum(jnp.where(last, qi + 1, qi), nq - 1)
        k0 = pl.multiple_of(j_n * tk, tk)
        va = va_sc[j]

        def run(bi):
            for cs in strips:
                m_old = m_sc[a, :, cs]
                m_new = jnp.maximum(m_old, x_cur[:, cs])
                p = jnp.exp2((s_cur[:, cs] - m_new).astype(BF16))
                alpha = jnp.exp2(m_old - m_new)
                m_sc[a, :, cs] = m_new
                acc_sc[a, :, cs] = acc_sc[a, :, cs] * alpha + _dot(va, p)
                score_strip(s_oth, x_oth, k0, qi_n, bi, cs)

        pl.when(j_n == qi_n)(functools.partial(run, 0))
        pl.when(j_n == qi_n - 1)(functools.partial(run, 1))
        pl.when(j_n < qi_n - 1)(functools.partial(run, None))
        return qi_n, j_n

    def run_pairs(count, carry):
        qi, j = carry
        done = []
        for h in range(count):
            done.append((qi, j))
            qi, j = half(h % 2, qi, j)
        for q, jj in done:
            pl.when(is_last(q, jj))(functools.partial(finalize, q))
        return qi, j

    npairs = nq * (nq + 1) // 2
    carry = lax.fori_loop(0, npairs // PAIRS_PER_BLOCK, lambda _, c: run_pairs(PAIRS_PER_BLOCK, c),
                          (jnp.int32(0), jnp.int32(0)))
    if npairs % PAIRS_PER_BLOCK:
        run_pairs(npairs % PAIRS_PER_BLOCK, carry)


def _diff(lam_p, kb, pt, bias, g_col, lam_init):
    B, S, _ = kb.shape
    nb, tb = pt.shape[1], pt.shape[3]
    tq = tk = bias.shape[-1]
    nq = S // tq
    assert tq % tb == 0 and tq % STRIP == 0 and bias.shape[1:] == (2, tk, tq)
    va_rows = DV_B + 16
    return pl.pallas_call(
        functools.partial(_diff_kernel, lam_init=lam_init),
        grid=(B, H_B),
        in_specs=[
            _resident(lam_p.shape),
            pl.BlockSpec((None, nb, DV_B, tb), lambda b, h: (b, 0, h, 0)),
            pl.BlockSpec((None, S, LANES), lambda b, h: (b, 0, h)),
            pl.BlockSpec((None, nb, DV_B, tb), lambda b, h: (b, 0, H_B + h, 0)),
            pl.BlockSpec((None, 2, tk, tq), lambda b, h: (h, 0, 0, 0)),
            _resident(g_col.shape),
        ],
        out_specs=pl.BlockSpec((None, S, DV_B), lambda b, h: (b, 0, h)),
        out_shape=jax.ShapeDtypeStruct((B, S, W_B_OUT), BF16),
        scratch_shapes=[
            pltpu.VMEM((nq, DV_B, 2 * tq), BF16),
            pltpu.VMEM((S // tk, va_rows, tk), BF16),
            pltpu.VMEM((tk, 2 * tq), F32),
            pltpu.VMEM((tk, 2 * tq), F32),
            pltpu.VMEM((1, 2 * tq), F32),
            pltpu.VMEM((1, 2 * tq), F32),
            pltpu.VMEM((ACC_SLOTS, 1, 2 * tq), F32),
            pltpu.VMEM((ACC_SLOTS, va_rows, 2 * tq), F32),
        ],
        compiler_params=_params("parallel", "parallel"),
        name="diff_attn",
    )(lam_p, pt, kb, pt, bias, g_col)


def _toeplitz_kernel(w_ref, o_ref, *, diff):
    width = o_ref.shape[-1]
    rows = width if diff else o_ref.shape[-2]
    t = pltpu.roll(jnp.broadcast_to(w_ref[...], (rows, width)), 0, 1, stride=1, stride_axis=0)
    if not diff:
        o_ref[...] = t
        return
    r = lax.broadcasted_iota(jnp.int32, (rows, width), 0)
    c = lax.broadcasted_iota(jnp.int32, (rows, width), 1)
    o_ref[0] = jnp.where(c >= r, t, NEG_INF).astype(o_ref.dtype)
    o_ref[1] = jnp.where(c < r, t, 0.0).astype(o_ref.dtype)


def _toeplitz(w, rows, diff):
    n, _, width = w.shape
    oshape = (n, 2, rows, width) if diff else (n, rows, width)
    oblock = (None,) + oshape[1:]
    return pl.pallas_call(
        functools.partial(_toeplitz_kernel, diff=diff),
        grid=(n,),
        in_specs=[pl.BlockSpec((None, 1, width), lambda i: (i, 0, 0))],
        out_specs=pl.BlockSpec(oblock, lambda i: (i,) + (0,) * (len(oshape) - 1)),
        out_shape=jax.ShapeDtypeStruct(oshape, BF16 if diff else F32),
        compiler_params=_params("parallel"),
        name="bias_diff" if diff else "bias_dilated",
    )(w)


def _layer_norm(h, g, b):
    mu = jnp.mean(h, axis=-1, keepdims=True)
    d = h - mu
    var = jnp.mean(d * d, axis=-1, keepdims=True)
    return d * lax.rsqrt(var + LN_EPS) * g + b


def _mix_kernel(x_ref, o0_ref, o1_ref, o2_ref, l0_ref, l1_ref, l2_ref, ob_ref, wg_ref, bg_ref, wa_ref, wb_ref,
                wo_ref, g_ref, b_ref, out_ref, *tok_sc):
    def token_major(ref, sc):
        dil, rows, _ = ref.shape
        if dil == 1:
            return lambda sl: ref[0, sl]
        nk = sc.shape[0]
        for res in range(dil):
            for k in range(nk):
                sc[k, pl.ds(res, rows, stride=dil), :] = ref[res, :, k * LANES:(k + 1) * LANES]
        return lambda sl: jnp.concatenate([sc[k, sl] for k in range(nk)], axis=1)

    l_g = (token_major(l0_ref, None), token_major(l1_ref, tok_sc[0]), token_major(l2_ref, tok_sc[1]))
    o_g = (token_major(o0_ref, None), token_major(o1_ref, tok_sc[2]), token_major(o2_ref, tok_sc[3]))
    tm = x_ref.shape[0]
    for r0 in range(0, tm, ROW_SLAB):
        sl = slice(r0, r0 + ROW_SLAB)
        x = x_ref[sl]
        xb = x.astype(BF16)
        l0, l1, l2 = (f(sl) for f in l_g)
        mx = jnp.maximum(jnp.maximum(l0, l1), l2)
        e0, e1, e2 = jnp.exp(l0 - mx), jnp.exp(l1 - mx), jnp.exp(l2 - mx)
        o_a = (e0 * o_g[0](sl) + e1 * o_g[1](sl) + e2 * o_g[2](sl)) / (e0 + e1 + e2)
        y_a = _dot(o_a.astype(BF16), wa_ref[...])
        y_b = _dot(ob_ref[sl], wb_ref[...])
        gate_a = jax.nn.sigmoid(_dot(xb, wg_ref[:, :D_MODEL]) + bg_ref[:, :D_MODEL])
        gate_b = jax.nn.sigmoid(_dot(xb, wg_ref[:, D_MODEL:]) + bg_ref[:, D_MODEL:])
        merged = gate_a * y_a + gate_b * y_b
        mix = _dot(merged.astype(BF16), wo_ref[...])
        out_ref[sl] = _layer_norm(DEEPNORM_ALPHA * x + mix, g_ref[...], b_ref[...])


def _mix(x, o_g, lse_g, o_b, w_gate, b_gate, w_a, w_b, w_o, ln_g, ln_b, tm):
    B, S, D = x.shape
    tok = lambda w: pl.BlockSpec((None, tm, w), lambda b, i: (b, i, 0))
    res = [pl.BlockSpec((None, a.shape[1], tm // a.shape[1], W_A_OUT), lambda b, i: (b, 0, i, 0)) for a in o_g]
    n_sc = 2 * sum(a.shape[1] > 1 for a in o_g)
    return pl.pallas_call(
        _mix_kernel,
        grid=(B, S // tm),
        in_specs=[tok(D)] + res + res + [tok(W_B_OUT)]
        + [_resident(a.shape) for a in (w_gate, b_gate, w_a, w_b, w_o, ln_g, ln_b)],
        out_specs=tok(D),
        out_shape=jax.ShapeDtypeStruct((B, S, D), F32),
        scratch_shapes=[pltpu.VMEM((W_A_OUT // LANES, tm, LANES), F32)] * n_sc,
        compiler_params=_params("parallel", "parallel"),
        name="mix",
    )(x, *o_g, *lse_g, o_b, w_gate, b_gate, w_a, w_b, w_o, ln_g, ln_b)


def _mlp_kernel(x_ref, w1_ref, w2_ref, g_ref, b_ref, out_ref):
    for r0 in range(0, x_ref.shape[0], ROW_SLAB):
        rows = slice(r0, r0 + ROW_SLAB)
        x = x_ref[rows]
        xb = x.astype(BF16)
        ff = jnp.zeros(x.shape, F32)
        for c in range(D_FF // (2 * CHUNK)):
            sl = slice(c * 2 * CHUNK, (c + 1) * 2 * CHUNK)
            h = jnp.maximum(_dot(xb, w1_ref[:, sl]), 0.0)
            ff = ff + _dot((h * h).astype(BF16), w2_ref[sl, :])
        out_ref[rows] = _layer_norm(DEEPNORM_ALPHA * x + ff, g_ref[...], b_ref[...])


def _mlp(x, w1, w2, ln_g, ln_b, tm):
    B, S, D = x.shape
    tok = pl.BlockSpec((None, tm, D), lambda b, i: (b, i, 0))
    return pl.pallas_call(
        _mlp_kernel,
        grid=(B, S // tm),
        in_specs=[tok] + [_resident(a.shape) for a in (w1, w2, ln_g, ln_b)],
        out_specs=tok,
        out_shape=jax.ShapeDtypeStruct((B, S, D), F32),
        compiler_params=_params("parallel", "parallel"),
        name="mlp",
    )(x, w1, w2, ln_g, ln_b)


def _t5_bucket(dist):
    n = jnp.maximum(dist, 0)
    max_exact = NUM_BUCKETS // 2
    nf = jnp.maximum(n, 1).astype(F32)
    large = max_exact + (jnp.log(nf / max_exact) / math.log(T5_MAX_DISTANCE / max_exact)
                         * (NUM_BUCKETS - max_exact)).astype(jnp.int32)
    large = jnp.minimum(large, NUM_BUCKETS - 1)
    return jnp.where(n < max_exact, n, large)


def _dilated_bias(rel_bias, g, dil):
    j = jnp.arange(2 * BLK)
    tab = rel_bias[:, g * H_A:(g + 1) * H_A].astype(F32)
    vec = tab[_t5_bucket(jnp.maximum(BLK - j, 0) * dil)]
    vec = jnp.where((j <= BLK)[:, None], vec, NEG_INF)
    bias = _toeplitz(vec.T.reshape(H_A, 1, 2 * BLK), BLK, diff=False)
    return bias.reshape(H_A // 2, 2 * BLK, 2 * BLK)


def _diff_bias(rel_bias, tq):
    assert tq >= T5_MAX_DISTANCE
    tab = rel_bias[:, N_GROUPS * H_A:].astype(F32)
    tab = (tab - tab[NUM_BUCKETS - 1:NUM_BUCKETS]) * LOG2E
    vec = tab[_t5_bucket(jnp.arange(tq))]
    return _toeplitz(vec.T.reshape(H_B, 1, tq), tq, diff=True)


def kernel(x, w_in, b_gate, lambda_q1, lambda_k1, lambda_q2, lambda_k2, subln_g, rel_bias, w_proj_a, w_proj_b,
           w_out, ln1_g, ln1_b, ln2_g, ln2_b, w_mlp1, w_mlp2):
    B, S, D = x.shape
    tq = min(TQ, S)
    tm = min(TM, S)
    w = w_in[0]
    w_a = w[:, :COLS_A].reshape(D, 3, N_GROUPS, W_A_OUT)
    w_a = jnp.concatenate([w_a[:, :1] * QK_SCALE, w_a[:, 1:]], axis=1).transpose(0, 2, 1, 3).reshape(D, COLS_A)
    w_qk = w[:, COLS_A:COLS_A + COLS_B_QK].reshape(D, 4, H_B, HEAD_DIM)
    w_qb = jnp.concatenate([w_qk[:, 0], w_qk[:, 1]], axis=-1).reshape(D, W_B_OUT)
    w_kb = jnp.concatenate([w_qk[:, 2], w_qk[:, 3]], axis=-1).reshape(D, W_B_OUT)
    w_vb = w[:, COLS_A + COLS_B_QK:COLS_A + COLS_B]
    w_n = jnp.concatenate([w_a, w_kb], axis=1).astype(BF16)
    w_t = jnp.concatenate([w_qb, w_vb], axis=1).T.astype(BF16)
    w_gate = w[:, COLS_A + COLS_B:].astype(BF16)

    *a_g, kb, pt = _proj(x, w_n, w_t, tm, tq)

    o_g, lse_g = [], []
    for g, (win, dil) in enumerate(DIL_PAIRS):
        assert win // dil == BLK
        o, lse = _dilated(a_g[g], _dilated_bias(rel_bias, g, dil), g)
        o_g.append(o)
        lse_g.append(lse)

    lam_init = 0.8 - 0.6 * math.exp(-0.3 * 0)
    lam_p = jnp.concatenate([lambda_q1, lambda_k1, lambda_q2, lambda_k2], axis=0).astype(F32)
    o_b = _diff(lam_p, kb, pt, _diff_bias(rel_bias, min(DIFF_T, S)), subln_g[0].reshape(DV_B, 1), lam_init)

    x1 = _mix(x, o_g, lse_g, o_b, w_gate, b_gate, w_proj_a[0].astype(BF16), w_proj_b[0].astype(BF16),
              w_out[0].astype(BF16), ln1_g, ln1_b, tm)
    return _mlp(x1, w_mlp1[0].astype(BF16), w_mlp2[0].astype(BF16), ln2_g, ln2_b, tm)
```

```python
import functools
import math

import jax
import jax.numpy as jnp
import numpy as np
from jax import lax
from jax.experimental import pallas as pl
from jax.experimental.pallas import tpu as pltpu

D_MODEL = 1024
HEAD_DIM = 64
DIL_PAIRS = ((128, 1), (512, 4), (2048, 16))
N_GROUPS = len(DIL_PAIRS)
H_A = 8
H_B = 8
DV_B = 2 * HEAD_DIM
D_FF = 4 * D_MODEL
NUM_BUCKETS = 32
T5_MAX_DISTANCE = 128
BLK = 128
LN_EPS = 1e-5
NEG_INF = -1e30
W_A_OUT = H_A * HEAD_DIM
W_B_OUT = H_B * DV_B
COLS_A = 3 * N_GROUPS * H_A * HEAD_DIM
COLS_B_QK = 4 * H_B * HEAD_DIM
COLS_B = COLS_B_QK + H_B * DV_B
DEPTH = 1
DEEPNORM_ALPHA = (2.0 * DEPTH) ** 0.25
QK_SCALE = HEAD_DIM ** -0.5
LOG2E = math.log2(math.e)

LANES = 128
COLS_N = COLS_A + H_B * 2 * HEAD_DIM
COLS_T = 2 * W_B_OUT
CHUNK = 512
TQ = 512
DIFF_T = 1024
STRIP = 512
PAIRS_PER_BLOCK = 2
ACC_SLOTS = PAIRS_PER_BLOCK
TM = 512
ROW_SLAB = 256
DILATED_ROWS = 1024
VMEM_LIMIT = 56 * 1024 * 1024

BF16 = jnp.bfloat16
F32 = jnp.float32


def _dot(a, b):
    return jnp.dot(a, b, preferred_element_type=F32)


def _dot_nt(a, b):
    return lax.dot_general(a, b, (((1,), (1,)), ((), ())), preferred_element_type=F32)


def _resident(shape):
    nd = len(shape)
    return pl.BlockSpec(shape, lambda *_: (0,) * nd, pipeline_mode=pl.Buffered(1))


def _params(*sem, flags=None):
    return pltpu.CompilerParams(dimension_semantics=sem, vmem_limit_bytes=VMEM_LIMIT, flags=flags)


def _proj_kernel(x_ref, wn_ref, wt_ref, a0_ref, a1_ref, a2_ref, kb_ref, ot_ref, rows_sc, *, tq):
    xb = x_ref[...].astype(BF16)
    tm = xb.shape[0]
    a_refs = (a0_ref, a1_ref, a2_ref)
    for c in range(COLS_N // CHUNK):
        r = _dot(xb, wn_ref[:, c * CHUNK:(c + 1) * CHUNK])
        g, part = divmod(c, 3)
        if g >= N_GROUPS:
            kb_ref[:, (c - 3 * N_GROUPS) * CHUNK:(c - 3 * N_GROUPS + 1) * CHUNK] = r.astype(BF16)
            continue
        sl = slice(part * W_A_OUT, (part + 1) * W_A_OUT)
        dil = DIL_PAIRS[g][1]
        if dil == 1:
            a_refs[g][0, :, sl] = r.astype(BF16)
            continue
        for k in range(CHUNK // LANES):
            rows_sc[k] = r[:, k * LANES:(k + 1) * LANES]
        for res in range(dil):
            for k in range(CHUNK // LANES):
                piece = rows_sc[k, pl.ds(res, tm // dil, stride=dil), :]
                a_refs[g][res, :, part * W_A_OUT + k * LANES:part * W_A_OUT + (k + 1) * LANES] = piece.astype(BF16)
    for c in range(COLS_T // CHUNK):
        sl = slice(c * CHUNK, (c + 1) * CHUNK)
        r = _dot_nt(wt_ref[sl, :], xb)
        if (c + 1) * CHUNK <= W_B_OUT:
            r = r * (QK_SCALE * LOG2E)
        r = r.astype(BF16)
        for t in range(tm // tq):
            ot_ref[t, sl, :] = r[:, t * tq:(t + 1) * tq]


def _proj(x, w_n, w_t, tm, tq):
    B, S, D = x.shape
    assert CHUNK == W_A_OUT
    dils = [d for _, d in DIL_PAIRS]
    return pl.pallas_call(
        functools.partial(_proj_kernel, tq=tq),
        grid=(B, S // tm),
        in_specs=[
            pl.BlockSpec((None, tm, D), lambda b, i: (b, i, 0)),
            _resident((D, COLS_N)),
            _resident((COLS_T, D)),
        ],
        out_specs=[pl.BlockSpec((None, d, tm // d, 3 * W_A_OUT), lambda b, i: (b, 0, i, 0)) for d in dils] + [
            pl.BlockSpec((None, tm, W_B_OUT), lambda b, i: (b, i, 0)),
            pl.BlockSpec((None, tm // tq, COLS_T, tq), lambda b, i: (b, i, 0, 0)),
        ],
        out_shape=[jax.ShapeDtypeStruct((B, d, S // d, 3 * W_A_OUT), BF16) for d in dils] + [
            jax.ShapeDtypeStruct((B, S, W_B_OUT), BF16),
            jax.ShapeDtypeStruct((B, S // tq, COLS_T, tq), BF16),
        ],
        scratch_shapes=[pltpu.VMEM((CHUNK // LANES, tm, LANES), F32)],
        compiler_params=_params("parallel", "parallel"),
        name="proj",
    )(x, w_n, w_t)


def _dilated_kernel(q_ref, k_ref, v_ref, kp_ref, vp_ref, bias_ref, o_ref, lse_ref, kw_sc, vw_sc):
    nres, tb, _ = q_ref.shape
    first = pl.program_id(2) == 0
    lane = lax.broadcasted_iota(jnp.int32, (BLK, LANES), 1)
    lo = lane < HEAD_DIM
    in_prev = lax.broadcasted_iota(jnp.int32, (2 * BLK, 2 * BLK), 1) < BLK
    ones = jnp.ones((2 * BLK, LANES), BF16)
    for r in range(nres):
        kw_sc[r, :BLK], kw_sc[r, BLK:] = kp_ref[r], k_ref[r]
        vw_sc[r, :BLK], vw_sc[r, BLK:] = vp_ref[r], v_ref[r]
        for jb in range(tb // BLK):
            rows = slice(jb * BLK, (jb + 1) * BLK)
            win = slice(jb * BLK, (jb + 2) * BLK)
            for p in range(H_A // 2):
                cols = slice(p * LANES, (p + 1) * LANES)
                q2 = q_ref[r, rows, cols]
                zero = jnp.zeros_like(q2)
                qd = jnp.concatenate([jnp.where(lo, q2, zero), jnp.where(lo, zero, q2)], axis=0)
                s = _dot_nt(qd, kw_sc[r, win, cols]) + bias_ref[p]
                if jb == 0:
                    s = jnp.where(first & in_prev, NEG_INF, s)
                m = jnp.max(s, axis=-1, keepdims=True)
                pw = jnp.exp((s - m).astype(BF16))
                oa = _dot(pw, jnp.concatenate([vw_sc[r, win, cols], ones], axis=1))
                den = oa[:, LANES:]
                o2 = oa[:, :LANES] / den
                lse = m + jnp.log(den)
                o_ref[r, rows, cols] = jnp.where(lo, o2[:BLK], o2[BLK:])
                lse_ref[r, rows, cols] = jnp.where(lo, lse[:BLK], lse[BLK:])


def _dilated(a, bias, g):
    B, dil, L, _ = a.shape
    tb = min(DILATED_ROWS, L)
    nres = min(DILATED_ROWS // tb, dil)
    cur = lambda c: pl.BlockSpec((None, nres, tb, W_A_OUT), lambda b, r, n: (b, r, n, c))
    prev = lambda c: pl.BlockSpec(
        (None, nres, BLK, W_A_OUT), lambda b, r, n: (b, r, jnp.maximum(n * (tb // BLK) - 1, 0), c))
    out = pl.BlockSpec((None, nres, tb, W_A_OUT), lambda b, r, n: (b, r, n, 0))
    return pl.pallas_call(
        _dilated_kernel,
        grid=(B, dil // nres, L // tb),
        in_specs=[cur(0), cur(1), cur(2), prev(1), prev(2), _resident(bias.shape)],
        out_specs=[out, out],
        out_shape=[jax.ShapeDtypeStruct((B, dil, L, W_A_OUT), F32)] * 2,
        scratch_shapes=[pltpu.VMEM((nres, BLK + tb, W_A_OUT), BF16)] * 2,
        compiler_params=_params("parallel", "parallel", "arbitrary"),
        name=f"dilated{g}",
    )(a, a, a, a, a, bias)


def _diff_kernel(lam_ref, qt_ref, k_ref, vt_ref, bias_ref, g_ref, o_ref, qd_sc, va_sc, s0_sc, s1_sc, x0_sc, x1_sc,
                 m_sc, acc_sc, *, lam_init):
    nb, _, tb = qt_ref.shape
    tq = tk = bias_ref.shape[-1]
    qb = tq // tb
    nq = nb // qb
    s_bufs, x_bufs = (s0_sc, s1_sc), (x0_sc, x1_sc)
    row = lax.broadcasted_iota(jnp.int32, (DV_B, tb), 0)
    for i in range(nb):
        blk, part = divmod(i, qb)
        qt = qt_ref[i]
        zero = jnp.zeros_like(qt)
        qd_sc[blk, :, part * tb:(part + 1) * tb] = jnp.where(row < HEAD_DIM, qt, zero)
        qd_sc[blk, :, tq + part * tb:tq + (part + 1) * tb] = jnp.where(row < HEAD_DIM, zero, qt)
        va_sc[blk, :DV_B, part * tb:(part + 1) * tb] = vt_ref[i]
    for i in range(nq):
        va_sc[i, DV_B:, :] = jnp.ones((va_sc.shape[1] - DV_B, tk), BF16)

    lam = (jnp.exp(jnp.sum(lam_ref[0:1, :] * lam_ref[1:2, :], axis=-1, keepdims=True))
           - jnp.exp(jnp.sum(lam_ref[2:3, :] * lam_ref[3:4, :], axis=-1, keepdims=True)) + lam_init)

    def reset(a):
        m_sc[a] = jnp.full(m_sc.shape[1:], -jnp.inf, F32)
        acc_sc[a] = jnp.zeros(acc_sc.shape[1:], F32)

    def finalize(qi):
        a = qi % ACC_SLOTS
        acc = acc_sc[a]
        on = acc[:DV_B] / acc[DV_B:DV_B + 1]
        o = on[:, :tq] - lam * on[:, tq:]
        ms = jnp.mean(o * o, axis=0, keepdims=True)
        o = o * lax.rsqrt(ms + LN_EPS) * g_ref[...] * (1.0 - lam_init)
        o_ref[pl.ds(pl.multiple_of(qi * tq, tq), tq), :] = o.T.astype(BF16)
        reset(a)

    def is_last(qi, j):
        return j == qi

    strips = [slice(c * STRIP, (c + 1) * STRIP) for c in range(2 * tq // STRIP)]

    def score_strip(s_buf, x_buf, k0, qi, bi, cs):
        q0 = cs.start % tq
        rows = min(tk, q0 + STRIP) if bi == 0 else tk
        s = _dot(k_ref[pl.ds(k0, rows), :], qd_sc[qi, :, cs])
        if bi == 0:
            s = s + bias_ref[0, :rows, q0:q0 + STRIP].astype(F32)
        elif bi == 1 and q0 < T5_MAX_DISTANCE:
            far = tk - T5_MAX_DISTANCE
            s = jnp.concatenate([s[:far], s[far:] + bias_ref[1, far:, q0:q0 + STRIP].astype(F32)], axis=0)
        s_buf[:rows, cs] = s
        if rows < tk:
            s_buf[rows:, cs] = jnp.full((tk - rows, STRIP), NEG_INF, F32)
        x_buf[:, cs] = jnp.max(s, axis=0, keepdims=True)

    for a in range(ACC_SLOTS):
        reset(a)
    for cs in strips:
        score_strip(s_bufs[0], x_bufs[0], 0, 0, 0, cs)

    def half(slot, qi, j):
        s_cur, x_cur, s_oth, x_oth = s_bufs[slot], x_bufs[slot], s_bufs[1 - slot], x_bufs[1 - slot]
        a = qi % ACC_SLOTS
        last = is_last(qi, j)
        j_n = jnp.where(last, 0, j + 1)
        qi_n = jnp.minimum(jnp.where(last, qi + 1, qi), nq - 1)
        k0 = pl.multiple_of(j_n * tk, tk)
        va = va_sc[j]

        def run(bi, cur_diag=False):
            for cs in strips:
                rows = min(tk, cs.start % tq + STRIP) if cur_diag else tk
                m_old = m_sc[a, :, cs]
                m_new = jnp.maximum(m_old, x_cur[:, cs])
                p = jnp.exp2((s_cur[:rows, cs] - m_new).astype(BF16))
                alpha = jnp.exp2(m_old - m_new)
                m_sc[a, :, cs] = m_new
                acc_sc[a, :, cs] = acc_sc[a, :, cs] * alpha + _dot(va[:, :rows], p)
                score_strip(s_oth, x_oth, k0, qi_n, bi, cs)

        far_next = j_n < qi_n - 1
        pl.when(j_n == qi_n)(functools.partial(run, 0))
        pl.when(j_n == qi_n - 1)(functools.partial(run, 1))
        pl.when(far_next & jnp.logical_not(last))(functools.partial(run, None))
        pl.when(far_next & last)(functools.partial(run, None, cur_diag=True))
        return qi_n, j_n

    def run_pairs(count, carry):
        qi, j = carry
        done = []
        for h in range(count):
            done.append((qi, j))
            qi, j = half(h % 2, qi, j)
        for q, jj in done:
            pl.when(is_last(q, jj))(functools.partial(finalize, q))
        return qi, j

    npairs = nq * (nq + 1) // 2
    carry = lax.fori_loop(0, npairs // PAIRS_PER_BLOCK, lambda _, c: run_pairs(PAIRS_PER_BLOCK, c),
                          (jnp.int32(0), jnp.int32(0)))
    if npairs % PAIRS_PER_BLOCK:
        run_pairs(npairs % PAIRS_PER_BLOCK, carry)


def _diff(lam_p, kb, pt, bias, g_col, lam_init):
    B, S, _ = kb.shape
    nb, tb = pt.shape[1], pt.shape[3]
    tq = tk = bias.shape[-1]
    nq = S // tq
    assert tq % tb == 0 and tq % STRIP == 0 and bias.shape[1:] == (2, tk, tq)
    va_rows = DV_B + 16
    return pl.pallas_call(
        functools.partial(_diff_kernel, lam_init=lam_init),
        grid=(B, H_B),
        in_specs=[
            _resident(lam_p.shape),
            pl.BlockSpec((None, nb, DV_B, tb), lambda b, h: (b, 0, h, 0)),
            pl.BlockSpec((None, S, LANES), lambda b, h: (b, 0, h)),
            pl.BlockSpec((None, nb, DV_B, tb), lambda b, h: (b, 0, H_B + h, 0)),
            pl.BlockSpec((None, 2, tk, tq), lambda b, h: (h, 0, 0, 0)),
            _resident(g_col.shape),
        ],
        out_specs=pl.BlockSpec((None, S, DV_B), lambda b, h: (b, 0, h)),
        out_shape=jax.ShapeDtypeStruct((B, S, W_B_OUT), BF16),
        scratch_shapes=[
            pltpu.VMEM((nq, DV_B, 2 * tq), BF16),
            pltpu.VMEM((S // tk, va_rows, tk), BF16),
            pltpu.VMEM((tk, 2 * tq), F32),
            pltpu.VMEM((tk, 2 * tq), F32),
            pltpu.VMEM((1, 2 * tq), F32),
            pltpu.VMEM((1, 2 * tq), F32),
            pltpu.VMEM((ACC_SLOTS, 1, 2 * tq), F32),
            pltpu.VMEM((ACC_SLOTS, va_rows, 2 * tq), F32),
        ],
        compiler_params=_params("parallel", "parallel"),
        name="diff_attn",
    )(lam_p, pt, kb, pt, bias, g_col)


def _toeplitz_kernel(w_ref, o_ref, *, diff):
    width = o_ref.shape[-1]
    rows = width if diff else o_ref.shape[-2]
    t = pltpu.roll(jnp.broadcast_to(w_ref[...], (rows, width)), 0, 1, stride=1, stride_axis=0)
    if not diff:
        o_ref[...] = t
        return
    r = lax.broadcasted_iota(jnp.int32, (rows, width), 0)
    c = lax.broadcasted_iota(jnp.int32, (rows, width), 1)
    o_ref[0] = jnp.where(c >= r, t, NEG_INF).astype(o_ref.dtype)
    o_ref[1] = jnp.where(c < r, t, 0.0).astype(o_ref.dtype)


def _toeplitz(w, rows, diff):
    n, _, width = w.shape
    oshape = (n, 2, rows, width) if diff else (n, rows, width)
    oblock = (None,) + oshape[1:]
    return pl.pallas_call(
        functools.partial(_toeplitz_kernel, diff=diff),
        grid=(n,),
        in_specs=[pl.BlockSpec((None, 1, width), lambda i: (i, 0, 0))],
        out_specs=pl.BlockSpec(oblock, lambda i: (i,) + (0,) * (len(oshape) - 1)),
        out_shape=jax.ShapeDtypeStruct(oshape, BF16 if diff else F32),
        compiler_params=_params("parallel"),
        name="bias_diff" if diff else "bias_dilated",
    )(w)


def _layer_norm(h, g, b):
    mu = jnp.mean(h, axis=-1, keepdims=True)
    d = h - mu
    var = jnp.mean(d * d, axis=-1, keepdims=True)
    return d * lax.rsqrt(var + LN_EPS) * g + b


def _mix_kernel(x_ref, o0_ref, o1_ref, o2_ref, l0_ref, l1_ref, l2_ref, ob_ref, wg_ref, bg_ref, wa_ref, wb_ref,
                wo_ref, g_ref, b_ref, out_ref, *tok_sc):
    def token_major(ref, sc):
        dil, rows, _ = ref.shape
        if dil == 1:
            return lambda sl: ref[0, sl]
        nk = sc.shape[0]
        for res in range(dil):
            for k in range(nk):
                sc[k, pl.ds(res, rows, stride=dil), :] = ref[res, :, k * LANES:(k + 1) * LANES]
        return lambda sl: jnp.concatenate([sc[k, sl] for k in range(nk)], axis=1)

    l_g = (token_major(l0_ref, None), token_major(l1_ref, tok_sc[0]), token_major(l2_ref, tok_sc[1]))
    o_g = (token_major(o0_ref, None), token_major(o1_ref, tok_sc[2]), token_major(o2_ref, tok_sc[3]))
    tm = x_ref.shape[0]
    for r0 in range(0, tm, ROW_SLAB):
        sl = slice(r0, r0 + ROW_SLAB)
        x = x_ref[sl]
        xb = x.astype(BF16)
        l0, l1, l2 = (f(sl) for f in l_g)
        mx = jnp.maximum(jnp.maximum(l0, l1), l2)
        e0, e1, e2 = jnp.exp(l0 - mx), jnp.exp(l1 - mx), jnp.exp(l2 - mx)
        o_a = (e0 * o_g[0](sl) + e1 * o_g[1](sl) + e2 * o_g[2](sl)) / (e0 + e1 + e2)
        y_a = _dot(o_a.astype(BF16), wa_ref[...])
        y_b = _dot(ob_ref[sl], wb_ref[...])
        gate_a = jax.nn.sigmoid(_dot(xb, wg_ref[:, :D_MODEL]) + bg_ref[:, :D_MODEL])
        gate_b = jax.nn.sigmoid(_dot(xb, wg_ref[:, D_MODEL:]) + bg_ref[:, D_MODEL:])
        merged = gate_a * y_a + gate_b * y_b
        mix = _dot(merged.astype(BF16), wo_ref[...])
        out_ref[sl] = _layer_norm(DEEPNORM_ALPHA * x + mix, g_ref[...], b_ref[...])


def _mix(x, o_g, lse_g, o_b, w_gate, b_gate, w_a, w_b, w_o, ln_g, ln_b, tm):
    B, S, D = x.shape
    tok = lambda w: pl.BlockSpec((None, tm, w), lambda b, i: (b, i, 0))
    res = [pl.BlockSpec((None, a.shape[1], tm // a.shape[1], W_A_OUT), lambda b, i: (b, 0, i, 0)) for a in o_g]
    n_sc = 2 * sum(a.shape[1] > 1 for a in o_g)
    return pl.pallas_call(
        _mix_kernel,
        grid=(B, S // tm),
        in_specs=[tok(D)] + res + res + [tok(W_B_OUT)]
        + [_resident(a.shape) for a in (w_gate, b_gate, w_a, w_b, w_o, ln_g, ln_b)],
        out_specs=tok(D),
        out_shape=jax.ShapeDtypeStruct((B, S, D), F32),
        scratch_shapes=[pltpu.VMEM((W_A_OUT // LANES, tm, LANES), F32)] * n_sc,
        compiler_params=_params("parallel", "parallel"),
        name="mix",
    )(x, *o_g, *lse_g, o_b, w_gate, b_gate, w_a, w_b, w_o, ln_g, ln_b)


def _mlp_kernel(x_ref, w1_ref, w2_ref, g_ref, b_ref, out_ref):
    for r0 in range(0, x_ref.shape[0], ROW_SLAB):
        rows = slice(r0, r0 + ROW_SLAB)
        x = x_ref[rows]
        xb = x.astype(BF16)
        ff = jnp.zeros(x.shape, F32)
        for c in range(D_FF // (2 * CHUNK)):
            sl = slice(c * 2 * CHUNK, (c + 1) * 2 * CHUNK)
            h = jnp.maximum(_dot(xb, w1_ref[:, sl]), 0.0)
            ff = ff + _dot((h * h).astype(BF16), w2_ref[sl, :])
        out_ref[rows] = _layer_norm(DEEPNORM_ALPHA * x + ff, g_ref[...], b_ref[...])


def _mlp(x, w1, w2, ln_g, ln_b, tm):
    B, S, D = x.shape
    tok = pl.BlockSpec((None, tm, D), lambda b, i: (b, i, 0))
    return pl.pallas_call(
        _mlp_kernel,
        grid=(B, S // tm),
        in_specs=[tok] + [_resident(a.shape) for a in (w1, w2, ln_g, ln_b)],
        out_specs=tok,
        out_shape=jax.ShapeDtypeStruct((B, S, D), F32),
        compiler_params=_params("parallel", "parallel"),
        name="mlp",
    )(x, w1, w2, ln_g, ln_b)


def _t5_bucket(dist):
    n = jnp.maximum(dist, 0)
    max_exact = NUM_BUCKETS // 2
    nf = jnp.maximum(n, 1).astype(F32)
    large = max_exact + (jnp.log(nf / max_exact) / math.log(T5_MAX_DISTANCE / max_exact)
                         * (NUM_BUCKETS - max_exact)).astype(jnp.int32)
    large = jnp.minimum(large, NUM_BUCKETS - 1)
    return jnp.where(n < max_exact, n, large)


def _dilated_bias(rel_bias, g, dil):
    j = jnp.arange(2 * BLK)
    tab = rel_bias[:, g * H_A:(g + 1) * H_A].astype(F32)
    vec = tab[_t5_bucket(jnp.maximum(BLK - j, 0) * dil)]
    vec = jnp.where((j <= BLK)[:, None], vec, NEG_INF)
    bias = _toeplitz(vec.T.reshape(H_A, 1, 2 * BLK), BLK, diff=False)
    return bias.reshape(H_A // 2, 2 * BLK, 2 * BLK)


def _diff_bias(rel_bias, tq):
    assert tq >= T5_MAX_DISTANCE
    tab = rel_bias[:, N_GROUPS * H_A:].astype(F32)
    tab = (tab - tab[NUM_BUCKETS - 1:NUM_BUCKETS]) * LOG2E
    vec = tab[_t5_bucket(jnp.arange(tq))]
    return _toeplitz(vec.T.reshape(H_B, 1, tq), tq, diff=True)


def kernel(x, w_in, b_gate, lambda_q1, lambda_k1, lambda_q2, lambda_k2, subln_g, rel_bias, w_proj_a, w_proj_b,
           w_out, ln1_g, ln1_b, ln2_g, ln2_b, w_mlp1, w_mlp2):
    B, S, D = x.shape
    tq = min(TQ, S)
    tm = min(TM, S)
    w = w_in[0]
    w_a = w[:, :COLS_A].reshape(D, 3, N_GROUPS, W_A_OUT)
    w_a = jnp.concatenate([w_a[:, :1] * QK_SCALE, w_a[:, 1:]], axis=1).transpose(0, 2, 1, 3).reshape(D, COLS_A)
    w_qk = w[:, COLS_A:COLS_A + COLS_B_QK].reshape(D, 4, H_B, HEAD_DIM)
    w_qb = jnp.concatenate([w_qk[:, 0], w_qk[:, 1]], axis=-1).reshape(D, W_B_OUT)
    w_kb = jnp.concatenate([w_qk[:, 2], w_qk[:, 3]], axis=-1).reshape(D, W_B_OUT)
    w_vb = w[:, COLS_A + COLS_B_QK:COLS_A + COLS_B]
    w_n = jnp.concatenate([w_a, w_kb], axis=1).astype(BF16)
    w_t = jnp.concatenate([w_qb, w_vb], axis=1).T.astype(BF16)
    w_gate = w[:, COLS_A + COLS_B:].astype(BF16)

    *a_g, kb, pt = _proj(x, w_n, w_t, tm, tq)

    o_g, lse_g = [], []
    for g, (win, dil) in enumerate(DIL_PAIRS):
        assert win // dil == BLK
        o, lse = _dilated(a_g[g], _dilated_bias(rel_bias, g, dil), g)
        o_g.append(o)
        lse_g.append(lse)

    lam_init = 0.8 - 0.6 * math.exp(-0.3 * 0)
    lam_p = jnp.concatenate([lambda_q1, lambda_k1, lambda_q2, lambda_k2], axis=0).astype(F32)
    o_b = _diff(lam_p, kb, pt, _diff_bias(rel_bias, min(DIFF_T, S)), subln_g[0].reshape(DV_B, 1), lam_init)

    x1 = _mix(x, o_g, lse_g, o_b, w_gate, b_gate, w_proj_a[0].astype(BF16), w_proj_b[0].astype(BF16),
              w_out[0].astype(BF16), ln1_g, ln1_b, tm)
    return _mlp(x1, w_mlp1[0].astype(BF16), w_mlp2[0].astype(BF16), ln2_g, ln2_b, tm)
```

```python
import functools
import math

import jax
import jax.numpy as jnp
import numpy as np
from jax import lax
from jax.experimental import pallas as pl
from jax.experimental.pallas import tpu as pltpu

D_MODEL = 1024
HEAD_DIM = 64
DIL_PAIRS = ((128, 1), (512, 4), (2048, 16))
N_GROUPS = len(DIL_PAIRS)
H_A = 8
H_B = 8
DV_B = 2 * HEAD_DIM
D_FF = 4 * D_MODEL
NUM_BUCKETS = 32
T5_MAX_DISTANCE = 128
BLK = 128
LN_EPS = 1e-5
NEG_INF = -1e30
W_A_OUT = H_A * HEAD_DIM
W_B_OUT = H_B * DV_B
COLS_A = 3 * N_GROUPS * H_A * HEAD_DIM
COLS_B_QK = 4 * H_B * HEAD_DIM
COLS_B = COLS_B_QK + H_B * DV_B
DEPTH = 1
DEEPNORM_ALPHA = (2.0 * DEPTH) ** 0.25
QK_SCALE = HEAD_DIM ** -0.5
LOG2E = math.log2(math.e)

LANES = 128
COLS_N = COLS_A + H_B * 2 * HEAD_DIM
COLS_T = 2 * W_B_OUT
CHUNK = 512
TQ = 512
DIFF_T = 1024
STRIP = 512
DIAG_STRIP = 256
PAIRS_PER_BLOCK = 2
ACC_SLOTS = PAIRS_PER_BLOCK
TM = 512
ROW_SLAB = 256
DILATED_ROWS = 1024
VMEM_LIMIT = 56 * 1024 * 1024

BF16 = jnp.bfloat16
F32 = jnp.float32


def _dot(a, b):
    return jnp.dot(a, b, preferred_element_type=F32)


def _dot_nt(a, b):
    return lax.dot_general(a, b, (((1,), (1,)), ((), ())), preferred_element_type=F32)


def _resident(shape):
    nd = len(shape)
    return pl.BlockSpec(shape, lambda *_: (0,) * nd, pipeline_mode=pl.Buffered(1))


def _params(*sem, flags=None):
    return pltpu.CompilerParams(dimension_semantics=sem, vmem_limit_bytes=VMEM_LIMIT, flags=flags)


def _proj_kernel(x_ref, wn_ref, wt_ref, a0_ref, a1_ref, a2_ref, kb_ref, ot_ref, rows_sc, *, tq):
    xb = x_ref[...].astype(BF16)
    tm = xb.shape[0]
    a_refs = (a0_ref, a1_ref, a2_ref)
    for c in range(COLS_N // CHUNK):
        r = _dot(xb, wn_ref[:, c * CHUNK:(c + 1) * CHUNK])
        g, part = divmod(c, 3)
        if g >= N_GROUPS:
            kb_ref[:, (c - 3 * N_GROUPS) * CHUNK:(c - 3 * N_GROUPS + 1) * CHUNK] = r.astype(BF16)
            continue
        sl = slice(part * W_A_OUT, (part + 1) * W_A_OUT)
        dil = DIL_PAIRS[g][1]
        if dil == 1:
            a_refs[g][0, :, sl] = r.astype(BF16)
            continue
        for k in range(CHUNK // LANES):
            rows_sc[k] = r[:, k * LANES:(k + 1) * LANES]
        for res in range(dil):
            for k in range(CHUNK // LANES):
                piece = rows_sc[k, pl.ds(res, tm // dil, stride=dil), :]
                a_refs[g][res, :, part * W_A_OUT + k * LANES:part * W_A_OUT + (k + 1) * LANES] = piece.astype(BF16)
    for c in range(COLS_T // CHUNK):
        sl = slice(c * CHUNK, (c + 1) * CHUNK)
        r = _dot_nt(wt_ref[sl, :], xb)
        if (c + 1) * CHUNK <= W_B_OUT:
            r = r * (QK_SCALE * LOG2E)
        r = r.astype(BF16)
        for t in range(tm // tq):
            ot_ref[t, sl, :] = r[:, t * tq:(t + 1) * tq]


def _proj(x, w_n, w_t, tm, tq):
    B, S, D = x.shape
    assert CHUNK == W_A_OUT
    dils = [d for _, d in DIL_PAIRS]
    return pl.pallas_call(
        functools.partial(_proj_kernel, tq=tq),
        grid=(B, S // tm),
        in_specs=[
            pl.BlockSpec((None, tm, D), lambda b, i: (b, i, 0)),
            _resident((D, COLS_N)),
            _resident((COLS_T, D)),
        ],
        out_specs=[pl.BlockSpec((None, d, tm // d, 3 * W_A_OUT), lambda b, i: (b, 0, i, 0)) for d in dils] + [
            pl.BlockSpec((None, tm, W_B_OUT), lambda b, i: (b, i, 0)),
            pl.BlockSpec((None, tm // tq, COLS_T, tq), lambda b, i: (b, i, 0, 0)),
        ],
        out_shape=[jax.ShapeDtypeStruct((B, d, S // d, 3 * W_A_OUT), BF16) for d in dils] + [
            jax.ShapeDtypeStruct((B, S, W_B_OUT), BF16),
            jax.ShapeDtypeStruct((B, S // tq, COLS_T, tq), BF16),
        ],
        scratch_shapes=[pltpu.VMEM((CHUNK // LANES, tm, LANES), F32)],
        compiler_params=_params("parallel", "parallel"),
        name="proj",
    )(x, w_n, w_t)


def _dilated_kernel(q_ref, k_ref, v_ref, kp_ref, vp_ref, bias_ref, o_ref, lse_ref, kw_sc, vw_sc):
    nres, tb, _ = q_ref.shape
    first = pl.program_id(2) == 0
    lane = lax.broadcasted_iota(jnp.int32, (BLK, LANES), 1)
    lo = lane < HEAD_DIM
    in_prev = lax.broadcasted_iota(jnp.int32, (2 * BLK, 2 * BLK), 1) < BLK
    ones = jnp.ones((2 * BLK, LANES), BF16)
    for r in range(nres):
        kw_sc[r, :BLK], kw_sc[r, BLK:] = kp_ref[r], k_ref[r]
        vw_sc[r, :BLK], vw_sc[r, BLK:] = vp_ref[r], v_ref[r]
        for jb in range(tb // BLK):
            rows = slice(jb * BLK, (jb + 1) * BLK)
            win = slice(jb * BLK, (jb + 2) * BLK)
            for p in range(H_A // 2):
                cols = slice(p * LANES, (p + 1) * LANES)
                q2 = q_ref[r, rows, cols]
                zero = jnp.zeros_like(q2)
                qd = jnp.concatenate([jnp.where(lo, q2, zero), jnp.where(lo, zero, q2)], axis=0)
                s = _dot_nt(qd, kw_sc[r, win, cols]) + bias_ref[p]
                if jb == 0:
                    s = jnp.where(first & in_prev, NEG_INF, s)
                m = jnp.max(s, axis=-1, keepdims=True)
                pw = jnp.exp((s - m).astype(BF16))
                oa = _dot(pw, jnp.concatenate([vw_sc[r, win, cols], ones], axis=1))
                den = oa[:, LANES:]
                o2 = oa[:, :LANES] / den
                lse = m + jnp.log(den)
                o_ref[r, rows, cols] = jnp.where(lo, o2[:BLK], o2[BLK:])
                lse_ref[r, rows, cols] = jnp.where(lo, lse[:BLK], lse[BLK:])


def _dilated(a, bias, g):
    B, dil, L, _ = a.shape
    tb = min(DILATED_ROWS, L)
    nres = min(DILATED_ROWS // tb, dil)
    cur = lambda c: pl.BlockSpec((None, nres, tb, W_A_OUT), lambda b, r, n: (b, r, n, c))
    prev = lambda c: pl.BlockSpec(
        (None, nres, BLK, W_A_OUT), lambda b, r, n: (b, r, jnp.maximum(n * (tb // BLK) - 1, 0), c))
    out = pl.BlockSpec((None, nres, tb, W_A_OUT), lambda b, r, n: (b, r, n, 0))
    return pl.pallas_call(
        _dilated_kernel,
        grid=(B, dil // nres, L // tb),
        in_specs=[cur(0), cur(1), cur(2), prev(1), prev(2), _resident(bias.shape)],
        out_specs=[out, out],
        out_shape=[jax.ShapeDtypeStruct((B, dil, L, W_A_OUT), F32)] * 2,
        scratch_shapes=[pltpu.VMEM((nres, BLK + tb, W_A_OUT), BF16)] * 2,
        compiler_params=_params("parallel", "parallel", "arbitrary"),
        name=f"dilated{g}",
    )(a, a, a, a, a, bias)


def _diff_kernel(lam_ref, qt_ref, k_ref, vt_ref, bias_ref, g_ref, o_ref, qd_sc, va_sc, s0_sc, s1_sc, x0_sc, x1_sc,
                 m_sc, acc_sc, *, lam_init):
    nb, _, tb = qt_ref.shape
    tq = tk = bias_ref.shape[-1]
    qb = tq // tb
    nq = nb // qb
    s_bufs, x_bufs = (s0_sc, s1_sc), (x0_sc, x1_sc)
    row = lax.broadcasted_iota(jnp.int32, (DV_B, tb), 0)
    for i in range(nb):
        blk, part = divmod(i, qb)
        qt = qt_ref[i]
        zero = jnp.zeros_like(qt)
        qd_sc[blk, :, part * tb:(part + 1) * tb] = jnp.where(row < HEAD_DIM, qt, zero)
        qd_sc[blk, :, tq + part * tb:tq + (part + 1) * tb] = jnp.where(row < HEAD_DIM, zero, qt)
        va_sc[blk, :DV_B, part * tb:(part + 1) * tb] = vt_ref[i]
    for i in range(nq):
        va_sc[i, DV_B:, :] = jnp.ones((va_sc.shape[1] - DV_B, tk), BF16)

    lam = (jnp.exp(jnp.sum(lam_ref[0:1, :] * lam_ref[1:2, :], axis=-1, keepdims=True))
           - jnp.exp(jnp.sum(lam_ref[2:3, :] * lam_ref[3:4, :], axis=-1, keepdims=True)) + lam_init)

    def reset(a):
        m_sc[a] = jnp.full(m_sc.shape[1:], -jnp.inf, F32)
        acc_sc[a] = jnp.zeros(acc_sc.shape[1:], F32)

    def finalize(qi):
        a = qi % ACC_SLOTS
        acc = acc_sc[a]
        on = acc[:DV_B] / acc[DV_B:DV_B + 1]
        o = on[:, :tq] - lam * on[:, tq:]
        ms = jnp.mean(o * o, axis=0, keepdims=True)
        o = o * lax.rsqrt(ms + LN_EPS) * g_ref[...] * (1.0 - lam_init)
        o_ref[pl.ds(pl.multiple_of(qi * tq, tq), tq), :] = o.T.astype(BF16)
        reset(a)

    def is_last(qi, j):
        return j == qi

    strips = [slice(c * STRIP, (c + 1) * STRIP) for c in range(2 * tq // STRIP)]

    def split(cs, width):
        return [slice(c, c + width) for c in range(cs.start, cs.stop, width)]

    def diag_rows(cs):
        return min(tk, cs.start % tq + cs.stop - cs.start)

    def score_strip(s_buf, x_buf, k0, qi, bi, cs):
        q0, width = cs.start % tq, cs.stop - cs.start
        rows = diag_rows(cs) if bi == 0 else tk
        s = _dot(k_ref[pl.ds(k0, rows), :], qd_sc[qi, :, cs])
        if bi == 0:
            s = s + bias_ref[0, :rows, q0:q0 + width].astype(F32)
        elif bi == 1 and q0 < T5_MAX_DISTANCE:
            far = tk - T5_MAX_DISTANCE
            s = jnp.concatenate([s[:far], s[far:] + bias_ref[1, far:, q0:q0 + width].astype(F32)], axis=0)
        s_buf[:rows, cs] = s
        if rows < tk:
            s_buf[rows:, cs] = jnp.full((tk - rows, width), NEG_INF, F32)
        x_buf[:, cs] = jnp.max(s, axis=0, keepdims=True)

    for a in range(ACC_SLOTS):
        reset(a)
    for cs in strips:
        for sub in split(cs, DIAG_STRIP):
            score_strip(s_bufs[0], x_bufs[0], 0, 0, 0, sub)

    def half(slot, qi, j):
        s_cur, x_cur, s_oth, x_oth = s_bufs[slot], x_bufs[slot], s_bufs[1 - slot], x_bufs[1 - slot]
        a = qi % ACC_SLOTS
        last = is_last(qi, j)
        j_n = jnp.where(last, 0, j + 1)
        qi_n = jnp.minimum(jnp.where(last, qi + 1, qi), nq - 1)
        k0 = pl.multiple_of(j_n * tk, tk)
        va = va_sc[j]

        def run(bi, cur_diag=False):
            for cs in strips:
                for sub in split(cs, DIAG_STRIP if cur_diag else STRIP):
                    rows = diag_rows(sub) if cur_diag else tk
                    m_old = m_sc[a, :, sub]
                    m_new = jnp.maximum(m_old, x_cur[:, sub])
                    p = jnp.exp2((s_cur[:rows, sub] - m_new).astype(BF16))
                    alpha = jnp.exp2(m_old - m_new)
                    m_sc[a, :, sub] = m_new
                    acc_sc[a, :, sub] = acc_sc[a, :, sub] * alpha + _dot(va[:, :rows], p)
                for sub in split(cs, DIAG_STRIP if bi == 0 else STRIP):
                    score_strip(s_oth, x_oth, k0, qi_n, bi, sub)

        far_next = j_n < qi_n - 1
        pl.when(j_n == qi_n)(functools.partial(run, 0))
        pl.when(j_n == qi_n - 1)(functools.partial(run, 1))
        pl.when(far_next & jnp.logical_not(last))(functools.partial(run, None))
        pl.when(far_next & last)(functools.partial(run, None, cur_diag=True))
        return qi_n, j_n

    def run_pairs(count, carry):
        qi, j = carry
        done = []
        for h in range(count):
            done.append((qi, j))
            qi, j = half(h % 2, qi, j)
        for q, jj in done:
            pl.when(is_last(q, jj))(functools.partial(finalize, q))
        return qi, j

    npairs = nq * (nq + 1) // 2
    carry = lax.fori_loop(0, npairs // PAIRS_PER_BLOCK, lambda _, c: run_pairs(PAIRS_PER_BLOCK, c),
                          (jnp.int32(0), jnp.int32(0)))
    if npairs % PAIRS_PER_BLOCK:
        run_pairs(npairs % PAIRS_PER_BLOCK, carry)


def _diff(lam_p, kb, pt, bias, g_col, lam_init):
    B, S, _ = kb.shape
    nb, tb = pt.shape[1], pt.shape[3]
    tq = tk = bias.shape[-1]
    nq = S // tq
    assert tq % tb == 0 and tq % STRIP == 0 and bias.shape[1:] == (2, tk, tq)
    va_rows = DV_B + 16
    return pl.pallas_call(
        functools.partial(_diff_kernel, lam_init=lam_init),
        grid=(B, H_B),
        in_specs=[
            _resident(lam_p.shape),
            pl.BlockSpec((None, nb, DV_B, tb), lambda b, h: (b, 0, h, 0)),
            pl.BlockSpec((None, S, LANES), lambda b, h: (b, 0, h)),
            pl.BlockSpec((None, nb, DV_B, tb), lambda b, h: (b, 0, H_B + h, 0)),
            pl.BlockSpec((None, 2, tk, tq), lambda b, h: (h, 0, 0, 0)),
            _resident(g_col.shape),
        ],
        out_specs=pl.BlockSpec((None, S, DV_B), lambda b, h: (b, 0, h)),
        out_shape=jax.ShapeDtypeStruct((B, S, W_B_OUT), BF16),
        scratch_shapes=[
            pltpu.VMEM((nq, DV_B, 2 * tq), BF16),
            pltpu.VMEM((S // tk, va_rows, tk), BF16),
            pltpu.VMEM((tk, 2 * tq), F32),
            pltpu.VMEM((tk, 2 * tq), F32),
            pltpu.VMEM((1, 2 * tq), F32),
            pltpu.VMEM((1, 2 * tq), F32),
            pltpu.VMEM((ACC_SLOTS, 1, 2 * tq), F32),
            pltpu.VMEM((ACC_SLOTS, va_rows, 2 * tq), F32),
        ],
        compiler_params=_params("parallel", "parallel"),
        name="diff_attn",
    )(lam_p, pt, kb, pt, bias, g_col)


def _toeplitz_kernel(w_ref, o_ref, *, diff):
    width = o_ref.shape[-1]
    rows = width if diff else o_ref.shape[-2]
    t = pltpu.roll(jnp.broadcast_to(w_ref[...], (rows, width)), 0, 1, stride=1, stride_axis=0)
    if not diff:
        o_ref[...] = t
        return
    r = lax.broadcasted_iota(jnp.int32, (rows, width), 0)
    c = lax.broadcasted_iota(jnp.int32, (rows, width), 1)
    o_ref[0] = jnp.where(c >= r, t, NEG_INF).astype(o_ref.dtype)
    o_ref[1] = jnp.where(c < r, t, 0.0).astype(o_ref.dtype)


def _toeplitz(w, rows, diff):
    n, _, width = w.shape
    oshape = (n, 2, rows, width) if diff else (n, rows, width)
    oblock = (None,) + oshape[1:]
    return pl.pallas_call(
        functools.partial(_toeplitz_kernel, diff=diff),
        grid=(n,),
        in_specs=[pl.BlockSpec((None, 1, width), lambda i: (i, 0, 0))],
        out_specs=pl.BlockSpec(oblock, lambda i: (i,) + (0,) * (len(oshape) - 1)),
        out_shape=jax.ShapeDtypeStruct(oshape, BF16 if diff else F32),
        compiler_params=_params("parallel"),
        name="bias_diff" if diff else "bias_dilated",
    )(w)


def _layer_norm(h, g, b):
    mu = jnp.mean(h, axis=-1, keepdims=True)
    d = h - mu
    var = jnp.mean(d * d, axis=-1, keepdims=True)
    return d * lax.rsqrt(var + LN_EPS) * g + b


def _mix_kernel(x_ref, o0_ref, o1_ref, o2_ref, l0_ref, l1_ref, l2_ref, ob_ref, wg_ref, bg_ref, wa_ref, wb_ref,
                wo_ref, g_ref, b_ref, out_ref, *tok_sc):
    def token_major(ref, sc):
        dil, rows, _ = ref.shape
        if dil == 1:
            return lambda sl: ref[0, sl]
        nk = sc.shape[0]
        for res in range(dil):
            for k in range(nk):
                sc[k, pl.ds(res, rows, stride=dil), :] = ref[res, :, k * LANES:(k + 1) * LANES]
        return lambda sl: jnp.concatenate([sc[k, sl] for k in range(nk)], axis=1)

    l_g = (token_major(l0_ref, None), token_major(l1_ref, tok_sc[0]), token_major(l2_ref, tok_sc[1]))
    o_g = (token_major(o0_ref, None), token_major(o1_ref, tok_sc[2]), token_major(o2_ref, tok_sc[3]))
    tm = x_ref.shape[0]
    for r0 in range(0, tm, ROW_SLAB):
        sl = slice(r0, r0 + ROW_SLAB)
        x = x_ref[sl]
        xb = x.astype(BF16)
        l0, l1, l2 = (f(sl) for f in l_g)
        mx = jnp.maximum(jnp.maximum(l0, l1), l2)
        e0, e1, e2 = jnp.exp(l0 - mx), jnp.exp(l1 - mx), jnp.exp(l2 - mx)
        o_a = (e0 * o_g[0](sl) + e1 * o_g[1](sl) + e2 * o_g[2](sl)) / (e0 + e1 + e2)
        y_a = _dot(o_a.astype(BF16), wa_ref[...])
        y_b = _dot(ob_ref[sl], wb_ref[...])
        gate_a = jax.nn.sigmoid(_dot(xb, wg_ref[:, :D_MODEL]) + bg_ref[:, :D_MODEL])
        gate_b = jax.nn.sigmoid(_dot(xb, wg_ref[:, D_MODEL:]) + bg_ref[:, D_MODEL:])
        merged = gate_a * y_a + gate_b * y_b
        mix = _dot(merged.astype(BF16), wo_ref[...])
        out_ref[sl] = _layer_norm(DEEPNORM_ALPHA * x + mix, g_ref[...], b_ref[...])


def _mix(x, o_g, lse_g, o_b, w_gate, b_gate, w_a, w_b, w_o, ln_g, ln_b, tm):
    B, S, D = x.shape
    tok = lambda w: pl.BlockSpec((None, tm, w), lambda b, i: (b, i, 0))
    res = [pl.BlockSpec((None, a.shape[1], tm // a.shape[1], W_A_OUT), lambda b, i: (b, 0, i, 0)) for a in o_g]
    n_sc = 2 * sum(a.shape[1] > 1 for a in o_g)
    return pl.pallas_call(
        _mix_kernel,
        grid=(B, S // tm),
        in_specs=[tok(D)] + res + res + [tok(W_B_OUT)]
        + [_resident(a.shape) for a in (w_gate, b_gate, w_a, w_b, w_o, ln_g, ln_b)],
        out_specs=tok(D),
        out_shape=jax.ShapeDtypeStruct((B, S, D), F32),
        scratch_shapes=[pltpu.VMEM((W_A_OUT // LANES, tm, LANES), F32)] * n_sc,
        compiler_params=_params("parallel", "parallel"),
        name="mix",
    )(x, *o_g, *lse_g, o_b, w_gate, b_gate, w_a, w_b, w_o, ln_g, ln_b)


def _mlp_kernel(x_ref, w1_ref, w2_ref, g_ref, b_ref, out_ref):
    for r0 in range(0, x_ref.shape[0], ROW_SLAB):
        rows = slice(r0, r0 + ROW_SLAB)
        x = x_ref[rows]
        xb = x.astype(BF16)
        ff = jnp.zeros(x.shape, F32)
        for c in range(D_FF // (2 * CHUNK)):
            sl = slice(c * 2 * CHUNK, (c + 1) * 2 * CHUNK)
            h = jnp.maximum(_dot(xb, w1_ref[:, sl]), 0.0)
            ff = ff + _dot((h * h).astype(BF16), w2_ref[sl, :])
        out_ref[rows] = _layer_norm(DEEPNORM_ALPHA * x + ff, g_ref[...], b_ref[...])


def _mlp(x, w1, w2, ln_g, ln_b, tm):
    B, S, D = x.shape
    tok = pl.BlockSpec((None, tm, D), lambda b, i: (b, i, 0))
    return pl.pallas_call(
        _mlp_kernel,
        grid=(B, S // tm),
        in_specs=[tok] + [_resident(a.shape) for a in (w1, w2, ln_g, ln_b)],
        out_specs=tok,
        out_shape=jax.ShapeDtypeStruct((B, S, D), F32),
        compiler_params=_params("parallel", "parallel"),
        name="mlp",
    )(x, w1, w2, ln_g, ln_b)


def _t5_bucket(dist):
    n = jnp.maximum(dist, 0)
    max_exact = NUM_BUCKETS // 2
    nf = jnp.maximum(n, 1).astype(F32)
    large = max_exact + (jnp.log(nf / max_exact) / math.log(T5_MAX_DISTANCE / max_exact)
                         * (NUM_BUCKETS - max_exact)).astype(jnp.int32)
    large = jnp.minimum(large, NUM_BUCKETS - 1)
    return jnp.where(n < max_exact, n, large)


def _dilated_bias(rel_bias, g, dil):
    j = jnp.arange(2 * BLK)
    tab = rel_bias[:, g * H_A:(g + 1) * H_A].astype(F32)
    vec = tab[_t5_bucket(jnp.maximum(BLK - j, 0) * dil)]
    vec = jnp.where((j <= BLK)[:, None], vec, NEG_INF)
    bias = _toeplitz(vec.T.reshape(H_A, 1, 2 * BLK), BLK, diff=False)
    return bias.reshape(H_A // 2, 2 * BLK, 2 * BLK)


def _diff_bias(rel_bias, tq):
    assert tq >= T5_MAX_DISTANCE
    tab = rel_bias[:, N_GROUPS * H_A:].astype(F32)
    tab = (tab - tab[NUM_BUCKETS - 1:NUM_BUCKETS]) * LOG2E
    vec = tab[_t5_bucket(jnp.arange(tq))]
    return _toeplitz(vec.T.reshape(H_B, 1, tq), tq, diff=True)


def kernel(x, w_in, b_gate, lambda_q1, lambda_k1, lambda_q2, lambda_k2, subln_g, rel_bias, w_proj_a, w_proj_b,
           w_out, ln1_g, ln1_b, ln2_g, ln2_b, w_mlp1, w_mlp2):
    B, S, D = x.shape
    tq = min(TQ, S)
    tm = min(TM, S)
    w = w_in[0]
    w_a = w[:, :COLS_A].reshape(D, 3, N_GROUPS, W_A_OUT)
    w_a = jnp.concatenate([w_a[:, :1] * QK_SCALE, w_a[:, 1:]], axis=1).transpose(0, 2, 1, 3).reshape(D, COLS_A)
    w_qk = w[:, COLS_A:COLS_A + COLS_B_QK].reshape(D, 4, H_B, HEAD_DIM)
    w_qb = jnp.concatenate([w_qk[:, 0], w_qk[:, 1]], axis=-1).reshape(D, W_B_OUT)
    w_kb = jnp.concatenate([w_qk[:, 2], w_qk[:, 3]], axis=-1).reshape(D, W_B_OUT)
    w_vb = w[:, COLS_A + COLS_B_QK:COLS_A + COLS_B]
    w_n = jnp.concatenate([w_a, w_kb], axis=1).astype(BF16)
    w_t = jnp.concatenate([w_qb, w_vb], axis=1).T.astype(BF16)
    w_gate = w[:, COLS_A + COLS_B:].astype(BF16)

    *a_g, kb, pt = _proj(x, w_n, w_t, tm, tq)

    o_g, lse_g = [], []
    for g, (win, dil) in enumerate(DIL_PAIRS):
        assert win // dil == BLK
        o, lse = _dilated(a_g[g], _dilated_bias(rel_bias, g, dil), g)
        o_g.append(o)
        lse_g.append(lse)

    lam_init = 0.8 - 0.6 * math.exp(-0.3 * 0)
    lam_p = jnp.concatenate([lambda_q1, lambda_k1, lambda_q2, lambda_k2], axis=0).astype(F32)
    o_b = _diff(lam_p, kb, pt, _diff_bias(rel_bias, min(DIFF_T, S)), subln_g[0].reshape(DV_B, 1), lam_init)

    x1 = _mix(x, o_g, lse_g, o_b, w_gate, b_gate, w_proj_a[0].astype(BF16), w_proj_b[0].astype(BF16),
              w_out[0].astype(BF16), ln1_g, ln1_b, tm)
    return _mlp(x1, w_mlp1[0].astype(BF16), w_mlp2[0].astype(BF16), ln2_g, ln2_b, tm)
```

```python
import functools
import math

import jax
import jax.numpy as jnp
import numpy as np
from jax import lax
from jax.experimental import pallas as pl
from jax.experimental.pallas import tpu as pltpu

D_MODEL = 1024
HEAD_DIM = 64
DIL_PAIRS = ((128, 1), (512, 4), (2048, 16))
N_GROUPS = len(DIL_PAIRS)
H_A = 8
H_B = 8
DV_B = 2 * HEAD_DIM
D_FF = 4 * D_MODEL
NUM_BUCKETS = 32
T5_MAX_DISTANCE = 128
BLK = 128
LN_EPS = 1e-5
NEG_INF = -1e30
W_A_OUT = H_A * HEAD_DIM
W_B_OUT = H_B * DV_B
COLS_A = 3 * N_GROUPS * H_A * HEAD_DIM
COLS_B_QK = 4 * H_B * HEAD_DIM
COLS_B = COLS_B_QK + H_B * DV_B
DEPTH = 1
DEEPNORM_ALPHA = (2.0 * DEPTH) ** 0.25
QK_SCALE = HEAD_DIM ** -0.5
LOG2E = math.log2(math.e)

LANES = 128
COLS_N = COLS_A + H_B * 2 * HEAD_DIM
COLS_T = 2 * W_B_OUT
CHUNK = 512
TQ = 512
DIFF_T = 1024
STRIP = 512
DIAG_STRIP = 256
PAIRS_PER_BLOCK = 2
ACC_SLOTS = PAIRS_PER_BLOCK
TM = 512
ROW_SLAB = 256
DILATED_ROWS = 1024
VMEM_LIMIT = 56 * 1024 * 1024

BF16 = jnp.bfloat16
F32 = jnp.float32


def _dot(a, b):
    return jnp.dot(a, b, preferred_element_type=F32)


def _dot_nt(a, b):
    return lax.dot_general(a, b, (((1,), (1,)), ((), ())), preferred_element_type=F32)


def _resident(shape):
    nd = len(shape)
    return pl.BlockSpec(shape, lambda *_: (0,) * nd, pipeline_mode=pl.Buffered(1))


def _params(*sem, flags=None):
    return pltpu.CompilerParams(dimension_semantics=sem, vmem_limit_bytes=VMEM_LIMIT, flags=flags)


def _proj_kernel(x_ref, wn_ref, wt_ref, a0_ref, a1_ref, a2_ref, kb_ref, ot_ref, rows_sc, *, tq):
    xb = x_ref[...].astype(BF16)
    tm = xb.shape[0]
    a_refs = (a0_ref, a1_ref, a2_ref)
    for c in range(COLS_N // CHUNK):
        r = _dot(xb, wn_ref[:, c * CHUNK:(c + 1) * CHUNK])
        g, part = divmod(c, 3)
        if g >= N_GROUPS:
            kb_ref[:, (c - 3 * N_GROUPS) * CHUNK:(c - 3 * N_GROUPS + 1) * CHUNK] = r.astype(BF16)
            continue
        sl = slice(part * W_A_OUT, (part + 1) * W_A_OUT)
        dil = DIL_PAIRS[g][1]
        if dil == 1:
            a_refs[g][0, :, sl] = r.astype(BF16)
            continue
        for k in range(CHUNK // LANES):
            rows_sc[k] = r[:, k * LANES:(k + 1) * LANES]
        for res in range(dil):
            for k in range(CHUNK // LANES):
                piece = rows_sc[k, pl.ds(res, tm // dil, stride=dil), :]
                a_refs[g][res, :, part * W_A_OUT + k * LANES:part * W_A_OUT + (k + 1) * LANES] = piece.astype(BF16)
    for c in range(COLS_T // CHUNK):
        sl = slice(c * CHUNK, (c + 1) * CHUNK)
        r = _dot_nt(wt_ref[sl, :], xb)
        if (c + 1) * CHUNK <= W_B_OUT:
            r = r * (QK_SCALE * LOG2E)
        r = r.astype(BF16)
        for t in range(tm // tq):
            ot_ref[t, sl, :] = r[:, t * tq:(t + 1) * tq]


def _proj(x, w_n, w_t, tm, tq):
    B, S, D = x.shape
    assert CHUNK == W_A_OUT
    dils = [d for _, d in DIL_PAIRS]
    return pl.pallas_call(
        functools.partial(_proj_kernel, tq=tq),
        grid=(B, S // tm),
        in_specs=[
            pl.BlockSpec((None, tm, D), lambda b, i: (b, i, 0)),
            _resident((D, COLS_N)),
            _resident((COLS_T, D)),
        ],
        out_specs=[pl.BlockSpec((None, d, tm // d, 3 * W_A_OUT), lambda b, i: (b, 0, i, 0)) for d in dils] + [
            pl.BlockSpec((None, tm, W_B_OUT), lambda b, i: (b, i, 0)),
            pl.BlockSpec((None, tm // tq, COLS_T, tq), lambda b, i: (b, i, 0, 0)),
        ],
        out_shape=[jax.ShapeDtypeStruct((B, d, S // d, 3 * W_A_OUT), BF16) for d in dils] + [
            jax.ShapeDtypeStruct((B, S, W_B_OUT), BF16),
            jax.ShapeDtypeStruct((B, S // tq, COLS_T, tq), BF16),
        ],
        scratch_shapes=[pltpu.VMEM((CHUNK // LANES, tm, LANES), F32)],
        compiler_params=_params("parallel", "parallel"),
        name="proj",
    )(x, w_n, w_t)


def _dilated_kernel(q_ref, k_ref, v_ref, kp_ref, vp_ref, bias_ref, o_ref, lse_ref, kw_sc, vw_sc):
    nres, tb, _ = q_ref.shape
    first = pl.program_id(2) == 0
    lane = lax.broadcasted_iota(jnp.int32, (BLK, LANES), 1)
    lo = lane < HEAD_DIM
    in_prev = lax.broadcasted_iota(jnp.int32, (2 * BLK, 2 * BLK), 1) < BLK
    ones = jnp.ones((2 * BLK, LANES), BF16)
    for r in range(nres):
        kw_sc[r, :BLK], kw_sc[r, BLK:] = kp_ref[r], k_ref[r]
        vw_sc[r, :BLK], vw_sc[r, BLK:] = vp_ref[r], v_ref[r]
        for jb in range(tb // BLK):
            rows = slice(jb * BLK, (jb + 1) * BLK)
            win = slice(jb * BLK, (jb + 2) * BLK)
            for p in range(H_A // 2):
                cols = slice(p * LANES, (p + 1) * LANES)
                q2 = q_ref[r, rows, cols]
                zero = jnp.zeros_like(q2)
                qd = jnp.concatenate([jnp.where(lo, q2, zero), jnp.where(lo, zero, q2)], axis=0)
                s = _dot_nt(qd, kw_sc[r, win, cols]) + bias_ref[p]
                if jb == 0:
                    s = jnp.where(first & in_prev, NEG_INF, s)
                m = jnp.max(s, axis=-1, keepdims=True)
                pw = jnp.exp((s - m).astype(BF16))
                oa = _dot(pw, jnp.concatenate([vw_sc[r, win, cols], ones], axis=1))
                den = oa[:, LANES:]
                o2 = oa[:, :LANES] / den
                lse = m + jnp.log(den)
                o_ref[r, rows, cols] = jnp.where(lo, o2[:BLK], o2[BLK:])
                lse_ref[r, rows, cols] = jnp.where(lo, lse[:BLK], lse[BLK:])


def _dilated(a, bias, g):
    B, dil, L, _ = a.shape
    tb = min(DILATED_ROWS, L)
    nres = min(DILATED_ROWS // tb, dil)
    cur = lambda c: pl.BlockSpec((None, nres, tb, W_A_OUT), lambda b, r, n: (b, r, n, c))
    prev = lambda c: pl.BlockSpec(
        (None, nres, BLK, W_A_OUT), lambda b, r, n: (b, r, jnp.maximum(n * (tb // BLK) - 1, 0), c))
    out = pl.BlockSpec((None, nres, tb, W_A_OUT), lambda b, r, n: (b, r, n, 0))
    return pl.pallas_call(
        _dilated_kernel,
        grid=(B, dil // nres, L // tb),
        in_specs=[cur(0), cur(1), cur(2), prev(1), prev(2), _resident(bias.shape)],
        out_specs=[out, out],
        out_shape=[jax.ShapeDtypeStruct((B, dil, L, W_A_OUT), F32)] * 2,
        scratch_shapes=[pltpu.VMEM((nres, BLK + tb, W_A_OUT), BF16)] * 2,
        compiler_params=_params("parallel", "parallel", "arbitrary"),
        name=f"dilated{g}",
    )(a, a, a, a, a, bias)


def _diff_kernel(lam_ref, qt_ref, k_ref, vt_ref, bias_ref, g_ref, o_ref, qd_sc, va_sc, s0_sc, s1_sc, x0_sc, x1_sc,
                 m_sc, acc_sc, *, lam_init):
    nb, _, tb = qt_ref.shape
    tq = tk = bias_ref.shape[-1]
    qb = tq // tb
    nq = nb // qb
    s_bufs, x_bufs = (s0_sc, s1_sc), (x0_sc, x1_sc)
    row = lax.broadcasted_iota(jnp.int32, (DV_B, tb), 0)
    for i in range(nb):
        blk, part = divmod(i, qb)
        qt = qt_ref[i]
        zero = jnp.zeros_like(qt)
        qd_sc[blk, :, part * tb:(part + 1) * tb] = jnp.where(row < HEAD_DIM, qt, zero)
        qd_sc[blk, :, tq + part * tb:tq + (part + 1) * tb] = jnp.where(row < HEAD_DIM, zero, qt)
        va_sc[blk, :DV_B, part * tb:(part + 1) * tb] = vt_ref[i]
    for i in range(nq):
        va_sc[i, DV_B:, :] = jnp.ones((va_sc.shape[1] - DV_B, tk), BF16)

    lam = (jnp.exp(jnp.sum(lam_ref[0:1, :] * lam_ref[1:2, :], axis=-1, keepdims=True))
           - jnp.exp(jnp.sum(lam_ref[2:3, :] * lam_ref[3:4, :], axis=-1, keepdims=True)) + lam_init)

    def reset(a):
        m_sc[a] = jnp.full(m_sc.shape[1:], -jnp.inf, F32)
        acc_sc[a] = jnp.zeros(acc_sc.shape[1:], F32)

    def finalize(qi):
        a = qi % ACC_SLOTS
        acc = acc_sc[a]
        on = acc[:DV_B] / acc[DV_B:DV_B + 1]
        o = on[:, :tq] - lam * on[:, tq:]
        ms = jnp.mean(o * o, axis=0, keepdims=True)
        o = o * lax.rsqrt(ms + LN_EPS) * g_ref[...] * (1.0 - lam_init)
        o_ref[pl.ds(pl.multiple_of(qi * tq, tq), tq), :] = o.T.astype(BF16)
        reset(a)

    def is_last(qi, j):
        return j == qi

    strips = [slice(c * STRIP, (c + 1) * STRIP) for c in range(2 * tq // STRIP)]

    def split(cs, width):
        return [slice(c, c + width) for c in range(cs.start, cs.stop, width)]

    def diag_rows(cs):
        return min(tk, cs.start % tq + cs.stop - cs.start)

    def score_strip(s_buf, x_buf, k0, qi, bi, cs):
        q0, width = cs.start % tq, cs.stop - cs.start
        rows = diag_rows(cs) if bi == 0 else tk
        s = _dot(k_ref[pl.ds(k0, rows), :], qd_sc[qi, :, cs])
        if bi == 0:
            s = s + bias_ref[0, :rows, q0:q0 + width].astype(F32)
        elif bi == 1 and q0 < T5_MAX_DISTANCE:
            far = tk - T5_MAX_DISTANCE
            s = jnp.concatenate([s[:far], s[far:] + bias_ref[1, far:, q0:q0 + width].astype(F32)], axis=0)
        s_buf[:rows, cs] = s
        if rows < tk:
            s_buf[rows:, cs] = jnp.full((tk - rows, width), NEG_INF, F32)
        x_buf[:, cs] = jnp.max(s, axis=0, keepdims=True)

    for a in range(ACC_SLOTS):
        reset(a)
    for cs in strips:
        for sub in split(cs, DIAG_STRIP):
            score_strip(s_bufs[0], x_bufs[0], 0, 0, 0, sub)

    def half(slot, qi, j):
        s_cur, x_cur, s_oth, x_oth = s_bufs[slot], x_bufs[slot], s_bufs[1 - slot], x_bufs[1 - slot]
        a = qi % ACC_SLOTS
        last = is_last(qi, j)
        j_n = jnp.where(last, 0, j + 1)
        qi_n = jnp.minimum(jnp.where(last, qi + 1, qi), nq - 1)
        k0 = pl.multiple_of(j_n * tk, tk)
        va = va_sc[j]

        def run(bi, cur_diag=False, produce=True):
            for cs in strips:
                for sub in split(cs, DIAG_STRIP if cur_diag else STRIP):
                    rows = diag_rows(sub) if cur_diag else tk
                    m_old = m_sc[a, :, sub]
                    m_new = jnp.maximum(m_old, x_cur[:, sub])
                    p = jnp.exp2((s_cur[:rows, sub] - m_new).astype(BF16))
                    alpha = jnp.exp2(m_old - m_new)
                    m_sc[a, :, sub] = m_new
                    acc_sc[a, :, sub] = acc_sc[a, :, sub] * alpha + _dot(va[:, :rows], p)
                for sub in split(cs, DIAG_STRIP if bi == 0 else STRIP) if produce else ():
                    score_strip(s_oth, x_oth, k0, qi_n, bi, sub)

        final = last & (qi == nq - 1)
        more = jnp.logical_not(final)
        far_next = j_n < qi_n - 1
        pl.when(more & (j_n == qi_n))(functools.partial(run, 0))
        pl.when(more & (j_n == qi_n - 1))(functools.partial(run, 1))
        pl.when(more & far_next & jnp.logical_not(last))(functools.partial(run, None))
        pl.when(more & far_next & last)(functools.partial(run, None, cur_diag=True))
        pl.when(final)(functools.partial(run, None, cur_diag=True, produce=False))
        return qi_n, j_n

    def run_pairs(count, carry):
        qi, j = carry
        done = []
        for h in range(count):
            done.append((qi, j))
            qi, j = half(h % 2, qi, j)
        for q, jj in done:
            pl.when(is_last(q, jj))(functools.partial(finalize, q))
        return qi, j

    npairs = nq * (nq + 1) // 2
    carry = lax.fori_loop(0, npairs // PAIRS_PER_BLOCK, lambda _, c: run_pairs(PAIRS_PER_BLOCK, c),
                          (jnp.int32(0), jnp.int32(0)))
    if npairs % PAIRS_PER_BLOCK:
        run_pairs(npairs % PAIRS_PER_BLOCK, carry)


def _diff(lam_p, kb, pt, bias, g_col, lam_init):
    B, S, _ = kb.shape
    nb, tb = pt.shape[1], pt.shape[3]
    tq = tk = bias.shape[-1]
    nq = S // tq
    assert tq % tb == 0 and tq % STRIP == 0 and bias.shape[1:] == (2, tk, tq)
    va_rows = DV_B + 16
    return pl.pallas_call(
        functools.partial(_diff_kernel, lam_init=lam_init),
        grid=(B, H_B),
        in_specs=[
            _resident(lam_p.shape),
            pl.BlockSpec((None, nb, DV_B, tb), lambda b, h: (b, 0, h, 0)),
            pl.BlockSpec((None, S, LANES), lambda b, h: (b, 0, h)),
            pl.BlockSpec((None, nb, DV_B, tb), lambda b, h: (b, 0, H_B + h, 0)),
            pl.BlockSpec((None, 2, tk, tq), lambda b, h: (h, 0, 0, 0)),
            _resident(g_col.shape),
        ],
        out_specs=pl.BlockSpec((None, S, DV_B), lambda b, h: (b, 0, h)),
        out_shape=jax.ShapeDtypeStruct((B, S, W_B_OUT), BF16),
        scratch_shapes=[
            pltpu.VMEM((nq, DV_B, 2 * tq), BF16),
            pltpu.VMEM((S // tk, va_rows, tk), BF16),
            pltpu.VMEM((tk, 2 * tq), F32),
            pltpu.VMEM((tk, 2 * tq), F32),
            pltpu.VMEM((1, 2 * tq), F32),
            pltpu.VMEM((1, 2 * tq), F32),
            pltpu.VMEM((ACC_SLOTS, 1, 2 * tq), F32),
            pltpu.VMEM((ACC_SLOTS, va_rows, 2 * tq), F32),
        ],
        compiler_params=_params("parallel", "parallel"),
        name="diff_attn",
    )(lam_p, pt, kb, pt, bias, g_col)


def _toeplitz_kernel(w_ref, o_ref, *, diff):
    width = o_ref.shape[-1]
    rows = width if diff else o_ref.shape[-2]
    t = pltpu.roll(jnp.broadcast_to(w_ref[...], (rows, width)), 0, 1, stride=1, stride_axis=0)
    if not diff:
        o_ref[...] = t
        return
    r = lax.broadcasted_iota(jnp.int32, (rows, width), 0)
    c = lax.broadcasted_iota(jnp.int32, (rows, width), 1)
    o_ref[0] = jnp.where(c >= r, t, NEG_INF).astype(o_ref.dtype)
    o_ref[1] = jnp.where(c < r, t, 0.0).astype(o_ref.dtype)


def _toeplitz(w, rows, diff):
    n, _, width = w.shape
    oshape = (n, 2, rows, width) if diff else (n, rows, width)
    oblock = (None,) + oshape[1:]
    return pl.pallas_call(
        functools.partial(_toeplitz_kernel, diff=diff),
        grid=(n,),
        in_specs=[pl.BlockSpec((None, 1, width), lambda i: (i, 0, 0))],
        out_specs=pl.BlockSpec(oblock, lambda i: (i,) + (0,) * (len(oshape) - 1)),
        out_shape=jax.ShapeDtypeStruct(oshape, BF16 if diff else F32),
        compiler_params=_params("parallel"),
        name="bias_diff" if diff else "bias_dilated",
    )(w)


def _layer_norm(h, g, b):
    mu = jnp.mean(h, axis=-1, keepdims=True)
    d = h - mu
    var = jnp.mean(d * d, axis=-1, keepdims=True)
    return d * lax.rsqrt(var + LN_EPS) * g + b


def _mix_kernel(x_ref, o0_ref, o1_ref, o2_ref, l0_ref, l1_ref, l2_ref, ob_ref, wg_ref, bg_ref, wa_ref, wb_ref,
                wo_ref, g_ref, b_ref, out_ref, *tok_sc):
    def token_major(ref, sc):
        dil, rows, _ = ref.shape
        if dil == 1:
            return lambda sl: ref[0, sl]
        nk = sc.shape[0]
        for res in range(dil):
            for k in range(nk):
                sc[k, pl.ds(res, rows, stride=dil), :] = ref[res, :, k * LANES:(k + 1) * LANES]
        return lambda sl: jnp.concatenate([sc[k, sl] for k in range(nk)], axis=1)

    l_g = (token_major(l0_ref, None), token_major(l1_ref, tok_sc[0]), token_major(l2_ref, tok_sc[1]))
    o_g = (token_major(o0_ref, None), token_major(o1_ref, tok_sc[2]), token_major(o2_ref, tok_sc[3]))
    tm = x_ref.shape[0]
    for r0 in range(0, tm, ROW_SLAB):
        sl = slice(r0, r0 + ROW_SLAB)
        x = x_ref[sl]
        xb = x.astype(BF16)
        l0, l1, l2 = (f(sl) for f in l_g)
        mx = jnp.maximum(jnp.maximum(l0, l1), l2)
        e0, e1, e2 = jnp.exp(l0 - mx), jnp.exp(l1 - mx), jnp.exp(l2 - mx)
        o_a = (e0 * o_g[0](sl) + e1 * o_g[1](sl) + e2 * o_g[2](sl)) / (e0 + e1 + e2)
        y_a = _dot(o_a.astype(BF16), wa_ref[...])
        y_b = _dot(ob_ref[sl], wb_ref[...])
        gate_a = jax.nn.sigmoid(_dot(xb, wg_ref[:, :D_MODEL]) + bg_ref[:, :D_MODEL])
        gate_b = jax.nn.sigmoid(_dot(xb, wg_ref[:, D_MODEL:]) + bg_ref[:, D_MODEL:])
        merged = gate_a * y_a + gate_b * y_b
        mix = _dot(merged.astype(BF16), wo_ref[...])
        out_ref[sl] = _layer_norm(DEEPNORM_ALPHA * x + mix, g_ref[...], b_ref[...])


def _mix(x, o_g, lse_g, o_b, w_gate, b_gate, w_a, w_b, w_o, ln_g, ln_b, tm):
    B, S, D = x.shape
    tok = lambda w: pl.BlockSpec((None, tm, w), lambda b, i: (b, i, 0))
    res = [pl.BlockSpec((None, a.shape[1], tm // a.shape[1], W_A_OUT), lambda b, i: (b, 0, i, 0)) for a in o_g]
    n_sc = 2 * sum(a.shape[1] > 1 for a in o_g)
    return pl.pallas_call(
        _mix_kernel,
        grid=(B, S // tm),
        in_specs=[tok(D)] + res + res + [tok(W_B_OUT)]
        + [_resident(a.shape) for a in (w_gate, b_gate, w_a, w_b, w_o, ln_g, ln_b)],
        out_specs=tok(D),
        out_shape=jax.ShapeDtypeStruct((B, S, D), F32),
        scratch_shapes=[pltpu.VMEM((W_A_OUT // LANES, tm, LANES), F32)] * n_sc,
        compiler_params=_params("parallel", "parallel"),
        name="mix",
    )(x, *o_g, *lse_g, o_b, w_gate, b_gate, w_a, w_b, w_o, ln_g, ln_b)


def _mlp_kernel(x_ref, w1_ref, w2_ref, g_ref, b_ref, out_ref):
    for r0 in range(0, x_ref.shape[0], ROW_SLAB):
        rows = slice(r0, r0 + ROW_SLAB)
        x = x_ref[rows]
        xb = x.astype(BF16)
        ff = jnp.zeros(x.shape, F32)
        for c in range(D_FF // (2 * CHUNK)):
            sl = slice(c * 2 * CHUNK, (c + 1) * 2 * CHUNK)
            h = jnp.maximum(_dot(xb, w1_ref[:, sl]), 0.0)
            ff = ff + _dot((h * h).astype(BF16), w2_ref[sl, :])
        out_ref[rows] = _layer_norm(DEEPNORM_ALPHA * x + ff, g_ref[...], b_ref[...])


def _mlp(x, w1, w2, ln_g, ln_b, tm):
    B, S, D = x.shape
    tok = pl.BlockSpec((None, tm, D), lambda b, i: (b, i, 0))
    return pl.pallas_call(
        _mlp_kernel,
        grid=(B, S // tm),
        in_specs=[tok] + [_resident(a.shape) for a in (w1, w2, ln_g, ln_b)],
        out_specs=tok,
        out_shape=jax.ShapeDtypeStruct((B, S, D), F32),
        compiler_params=_params("parallel", "parallel"),
        name="mlp",
    )(x, w1, w2, ln_g, ln_b)


def _t5_bucket(dist):
    n = jnp.maximum(dist, 0)
    max_exact = NUM_BUCKETS // 2
    nf = jnp.maximum(n, 1).astype(F32)
    large = max_exact + (jnp.log(nf / max_exact) / math.log(T5_MAX_DISTANCE / max_exact)
                         * (NUM_BUCKETS - max_exact)).astype(jnp.int32)
    large = jnp.minimum(large, NUM_BUCKETS - 1)
    return jnp.where(n < max_exact, n, large)


def _dilated_bias(rel_bias, g, dil):
    j = jnp.arange(2 * BLK)
    tab = rel_bias[:, g * H_A:(g + 1) * H_A].astype(F32)
    vec = tab[_t5_bucket(jnp.maximum(BLK - j, 0) * dil)]
    vec = jnp.where((j <= BLK)[:, None], vec, NEG_INF)
    bias = _toeplitz(vec.T.reshape(H_A, 1, 2 * BLK), BLK, diff=False)
    return bias.reshape(H_A // 2, 2 * BLK, 2 * BLK)


def _diff_bias(rel_bias, tq):
    assert tq >= T5_MAX_DISTANCE
    tab = rel_bias[:, N_GROUPS * H_A:].astype(F32)
    tab = (tab - tab[NUM_BUCKETS - 1:NUM_BUCKETS]) * LOG2E
    vec = tab[_t5_bucket(jnp.arange(tq))]
    return _toeplitz(vec.T.reshape(H_B, 1, tq), tq, diff=True)


def kernel(x, w_in, b_gate, lambda_q1, lambda_k1, lambda_q2, lambda_k2, subln_g, rel_bias, w_proj_a, w_proj_b,
           w_out, ln1_g, ln1_b, ln2_g, ln2_b, w_mlp1, w_mlp2):
    B, S, D = x.shape
    tq = min(TQ, S)
    tm = min(TM, S)
    w = w_in[0]
    w_a = w[:, :COLS_A].reshape(D, 3, N_GROUPS, W_A_OUT)
    w_a = jnp.concatenate([w_a[:, :1] * QK_SCALE, w_a[:, 1:]], axis=1).transpose(0, 2, 1, 3).reshape(D, COLS_A)
    w_qk = w[:, COLS_A:COLS_A + COLS_B_QK].reshape(D, 4, H_B, HEAD_DIM)
    w_qb = jnp.concatenate([w_qk[:, 0], w_qk[:, 1]], axis=-1).reshape(D, W_B_OUT)
    w_kb = jnp.concatenate([w_qk[:, 2], w_qk[:, 3]], axis=-1).reshape(D, W_B_OUT)
    w_vb = w[:, COLS_A + COLS_B_QK:COLS_A + COLS_B]
    w_n = jnp.concatenate([w_a, w_kb], axis=1).astype(BF16)
    w_t = jnp.concatenate([w_qb, w_vb], axis=1).T.astype(BF16)
    w_gate = w[:, COLS_A + COLS_B:].astype(BF16)

    *a_g, kb, pt = _proj(x, w_n, w_t, tm, tq)

    o_g, lse_g = [], []
    for g, (win, dil) in enumerate(DIL_PAIRS):
        assert win // dil == BLK
        o, lse = _dilated(a_g[g], _dilated_bias(rel_bias, g, dil), g)
        o_g.append(o)
        lse_g.append(lse)

    lam_init = 0.8 - 0.6 * math.exp(-0.3 * 0)
    lam_p = jnp.concatenate([lambda_q1, lambda_k1, lambda_q2, lambda_k2], axis=0).astype(F32)
    o_b = _diff(lam_p, kb, pt, _diff_bias(rel_bias, min(DIFF_T, S)), subln_g[0].reshape(DV_B, 1), lam_init)

    x1 = _mix(x, o_g, lse_g, o_b, w_gate, b_gate, w_proj_a[0].astype(BF16), w_proj_b[0].astype(BF16),
              w_out[0].astype(BF16), ln1_g, ln1_b, tm)
    return _mlp(x1, w_mlp1[0].astype(BF16), w_mlp2[0].astype(BF16), ln2_g, ln2_b, tm)
```

```python
import functools
import math

import jax
import jax.numpy as jnp
import numpy as np
from jax import lax
from jax.experimental import pallas as pl
from jax.experimental.pallas import tpu as pltpu

D_MODEL = 1024
HEAD_DIM = 64
DIL_PAIRS = ((128, 1), (512, 4), (2048, 16))
N_GROUPS = len(DIL_PAIRS)
H_A = 8
H_B = 8
DV_B = 2 * HEAD_DIM
D_FF = 4 * D_MODEL
NUM_BUCKETS = 32
T5_MAX_DISTANCE = 128
BLK = 128
LN_EPS = 1e-5
NEG_INF = -1e30
W_A_OUT = H_A * HEAD_DIM
W_B_OUT = H_B * DV_B
COLS_A = 3 * N_GROUPS * H_A * HEAD_DIM
COLS_B_QK = 4 * H_B * HEAD_DIM
COLS_B = COLS_B_QK + H_B * DV_B
DEPTH = 1
DEEPNORM_ALPHA = (2.0 * DEPTH) ** 0.25
QK_SCALE = HEAD_DIM ** -0.5
LOG2E = math.log2(math.e)

LANES = 128
COLS_N = COLS_A + H_B * 2 * HEAD_DIM
COLS_T = 2 * W_B_OUT
CHUNK = 512
TQ = 512
DIFF_T = 1024
STRIP = 512
DIAG_STRIP = 256
PAIRS_PER_BLOCK = 2
ACC_SLOTS = PAIRS_PER_BLOCK
TM = 512
ROW_SLAB = 256
DILATED_ROWS = 1024
VMEM_LIMIT = 56 * 1024 * 1024

BF16 = jnp.bfloat16
F32 = jnp.float32


def _dot(a, b):
    return jnp.dot(a, b, preferred_element_type=F32)


def _dot_nt(a, b):
    return lax.dot_general(a, b, (((1,), (1,)), ((), ())), preferred_element_type=F32)


def _resident(shape):
    nd = len(shape)
    return pl.BlockSpec(shape, lambda *_: (0,) * nd, pipeline_mode=pl.Buffered(1))


def _params(*sem, flags=None):
    return pltpu.CompilerParams(dimension_semantics=sem, vmem_limit_bytes=VMEM_LIMIT, flags=flags)


def _proj_kernel(x_ref, wn_ref, wt_ref, a0_ref, a1_ref, a2_ref, kb_ref, ot_ref, rows_sc, *, tq):
    xb = x_ref[...].astype(BF16)
    tm = xb.shape[0]
    a_refs = (a0_ref, a1_ref, a2_ref)
    for c in range(COLS_N // CHUNK):
        r = _dot(xb, wn_ref[:, c * CHUNK:(c + 1) * CHUNK])
        g, part = divmod(c, 3)
        if g >= N_GROUPS:
            kb_ref[:, (c - 3 * N_GROUPS) * CHUNK:(c - 3 * N_GROUPS + 1) * CHUNK] = r.astype(BF16)
            continue
        sl = slice(part * W_A_OUT, (part + 1) * W_A_OUT)
        dil = DIL_PAIRS[g][1]
        if dil == 1:
            a_refs[g][0, :, sl] = r.astype(BF16)
            continue
        for k in range(CHUNK // LANES):
            rows_sc[k] = r[:, k * LANES:(k + 1) * LANES]
        for res in range(dil):
            for k in range(CHUNK // LANES):
                piece = rows_sc[k, pl.ds(res, tm // dil, stride=dil), :]
                a_refs[g][res, :, part * W_A_OUT + k * LANES:part * W_A_OUT + (k + 1) * LANES] = piece.astype(BF16)
    for c in range(COLS_T // CHUNK):
        sl = slice(c * CHUNK, (c + 1) * CHUNK)
        r = _dot_nt(wt_ref[sl, :], xb)
        if (c + 1) * CHUNK <= W_B_OUT:
            r = r * (QK_SCALE * LOG2E)
        r = r.astype(BF16)
        for t in range(tm // tq):
            ot_ref[t, sl, :] = r[:, t * tq:(t + 1) * tq]


def _proj(x, w_n, w_t, tm, tq):
    B, S, D = x.shape
    assert CHUNK == W_A_OUT
    dils = [d for _, d in DIL_PAIRS]
    return pl.pallas_call(
        functools.partial(_proj_kernel, tq=tq),
        grid=(B, S // tm),
        in_specs=[
            pl.BlockSpec((None, tm, D), lambda b, i: (b, i, 0)),
            _resident((D, COLS_N)),
            _resident((COLS_T, D)),
        ],
        out_specs=[pl.BlockSpec((None, d, tm // d, 3 * W_A_OUT), lambda b, i: (b, 0, i, 0)) for d in dils] + [
            pl.BlockSpec((None, tm, W_B_OUT), lambda b, i: (b, i, 0)),
            pl.BlockSpec((None, tm // tq, COLS_T, tq), lambda b, i: (b, i, 0, 0)),
        ],
        out_shape=[jax.ShapeDtypeStruct((B, d, S // d, 3 * W_A_OUT), BF16) for d in dils] + [
            jax.ShapeDtypeStruct((B, S, W_B_OUT), BF16),
            jax.ShapeDtypeStruct((B, S // tq, COLS_T, tq), BF16),
        ],
        scratch_shapes=[pltpu.VMEM((CHUNK // LANES, tm, LANES), F32)],
        compiler_params=_params("parallel", "parallel"),
        name="proj",
    )(x, w_n, w_t)


def _dilated_kernel(q_ref, k_ref, v_ref, kp_ref, vp_ref, bias_ref, o_ref, lse_ref, kw_sc, vw_sc):
    nres, tb, _ = q_ref.shape
    first = pl.program_id(2) == 0
    lane = lax.broadcasted_iota(jnp.int32, (BLK, LANES), 1)
    lo = lane < HEAD_DIM
    in_prev = lax.broadcasted_iota(jnp.int32, (2 * BLK, 2 * BLK), 1) < BLK
    ones = jnp.ones((2 * BLK, LANES), BF16)
    for r in range(nres):
        kw_sc[r, :BLK], kw_sc[r, BLK:] = kp_ref[r], k_ref[r]
        vw_sc[r, :BLK], vw_sc[r, BLK:] = vp_ref[r], v_ref[r]
        for jb in range(tb // BLK):
            rows = slice(jb * BLK, (jb + 1) * BLK)
            win = slice(jb * BLK, (jb + 2) * BLK)
            for p in range(H_A // 2):
                cols = slice(p * LANES, (p + 1) * LANES)
                q2 = q_ref[r, rows, cols]
                zero = jnp.zeros_like(q2)
                qd = jnp.concatenate([jnp.where(lo, q2, zero), jnp.where(lo, zero, q2)], axis=0)
                s = _dot_nt(qd, kw_sc[r, win, cols]) + bias_ref[p]
                if jb == 0:
                    s = jnp.where(first & in_prev, NEG_INF, s)
                m = jnp.max(s, axis=-1, keepdims=True)
                pw = jnp.exp((s - m).astype(BF16))
                oa = _dot(pw, jnp.concatenate([vw_sc[r, win, cols], ones], axis=1))
                den = oa[:, LANES:]
                o2 = oa[:, :LANES] / den
                lse = m + jnp.log(den)
                o_ref[r, rows, cols] = jnp.where(lo, o2[:BLK], o2[BLK:])
                lse_ref[r, rows, cols] = jnp.where(lo, lse[:BLK], lse[BLK:])


def _dilated(a, bias, g):
    B, dil, L, _ = a.shape
    tb = min(DILATED_ROWS, L)
    nres = min(DILATED_ROWS // tb, dil)
    cur = lambda c: pl.BlockSpec((None, nres, tb, W_A_OUT), lambda b, r, n: (b, r, n, c))
    prev = lambda c: pl.BlockSpec(
        (None, nres, BLK, W_A_OUT), lambda b, r, n: (b, r, jnp.maximum(n * (tb // BLK) - 1, 0), c))
    out = pl.BlockSpec((None, nres, tb, W_A_OUT), lambda b, r, n: (b, r, n, 0))
    return pl.pallas_call(
        _dilated_kernel,
        grid=(B, dil // nres, L // tb),
        in_specs=[cur(0), cur(1), cur(2), prev(1), prev(2), _resident(bias.shape)],
        out_specs=[out, out],
        out_shape=[jax.ShapeDtypeStruct((B, dil, L, W_A_OUT), F32)] * 2,
        scratch_shapes=[pltpu.VMEM((nres, BLK + tb, W_A_OUT), BF16)] * 2,
        compiler_params=_params("parallel", "parallel", "arbitrary"),
        name=f"dilated{g}",
    )(a, a, a, a, a, bias)


def _diff_kernel(lam_ref, qt_ref, k_ref, vt_ref, bias_ref, g_ref, o_ref, qd_sc, va_sc, s0_sc, s1_sc, x0_sc, x1_sc,
                 m_sc, acc_sc, *, lam_init):
    nb, _, tb = qt_ref.shape
    tq = tk = bias_ref.shape[-1]
    qb = tq // tb
    nq = nb // qb
    s_bufs, x_bufs = (s0_sc, s1_sc), (x0_sc, x1_sc)
    row = lax.broadcasted_iota(jnp.int32, (DV_B, tb), 0)
    for i in range(nb):
        blk, part = divmod(i, qb)
        qt = qt_ref[i]
        zero = jnp.zeros_like(qt)
        qd_sc[blk, :, part * tb:(part + 1) * tb] = jnp.where(row < HEAD_DIM, qt, zero)
        qd_sc[blk, :, tq + part * tb:tq + (part + 1) * tb] = jnp.where(row < HEAD_DIM, zero, qt)
        va_sc[blk, :DV_B, part * tb:(part + 1) * tb] = vt_ref[i]
    for i in range(nq):
        va_sc[i, DV_B:, :] = jnp.ones((va_sc.shape[1] - DV_B, tk), BF16)

    lam = (jnp.exp(jnp.sum(lam_ref[0:1, :] * lam_ref[1:2, :], axis=-1, keepdims=True))
           - jnp.exp(jnp.sum(lam_ref[2:3, :] * lam_ref[3:4, :], axis=-1, keepdims=True)) + lam_init)

    def reset(a):
        m_sc[a] = jnp.full(m_sc.shape[1:], -jnp.inf, F32)
        acc_sc[a] = jnp.zeros(acc_sc.shape[1:], F32)

    def finalize(qi):
        a = qi % ACC_SLOTS
        acc = acc_sc[a]
        on = acc[:DV_B] / acc[DV_B:DV_B + 1]
        o = on[:, :tq] - lam * on[:, tq:]
        ms = jnp.mean(o * o, axis=0, keepdims=True)
        o = o * lax.rsqrt(ms + LN_EPS) * g_ref[...] * (1.0 - lam_init)
        o_ref[pl.ds(pl.multiple_of(qi * tq, tq), tq), :] = o.T.astype(BF16)
        reset(a)

    def is_last(qi, j):
        return j == qi

    strips = [slice(c * STRIP, (c + 1) * STRIP) for c in range(2 * tq // STRIP)]

    def split(cs, width):
        return [slice(c, c + width) for c in range(cs.start, cs.stop, width)]

    def diag_rows(cs):
        return min(tk, cs.start % tq + cs.stop - cs.start)

    def score_strip(s_buf, x_buf, k0, qi, bi, cs):
        q0, width = cs.start % tq, cs.stop - cs.start
        rows = diag_rows(cs) if bi == 0 else tk
        s = _dot(k_ref[pl.ds(k0, rows), :], qd_sc[qi, :, cs])
        if bi == 0:
            s = s + bias_ref[0, :rows, q0:q0 + width].astype(F32)
        elif bi == 1 and q0 < T5_MAX_DISTANCE:
            far = tk - T5_MAX_DISTANCE
            s = jnp.concatenate([s[:far], s[far:] + bias_ref[1, far:, q0:q0 + width].astype(F32)], axis=0)
        s_buf[:rows, cs] = s
        if rows < tk:
            s_buf[rows:, cs] = jnp.full((tk - rows, width), NEG_INF, F32)
        x_buf[:, cs] = jnp.max(s, axis=0, keepdims=True)

    for a in range(ACC_SLOTS):
        reset(a)
    for cs in strips:
        for sub in split(cs, DIAG_STRIP):
            score_strip(s_bufs[0], x_bufs[0], 0, 0, 0, sub)

    def half(slot, qi, j):
        s_cur, x_cur, s_oth, x_oth = s_bufs[slot], x_bufs[slot], s_bufs[1 - slot], x_bufs[1 - slot]
        a = qi % ACC_SLOTS
        last = is_last(qi, j)
        j_n = jnp.where(last, 0, j + 1)
        qi_n = jnp.minimum(jnp.where(last, qi + 1, qi), nq - 1)
        k0 = pl.multiple_of(j_n * tk, tk)
        va = va_sc[j]

        def run(bi, cur_diag=False, produce=True, fin=False):
            if fin:
                finalize(qi - 1)
            for cs in strips:
                for sub in split(cs, DIAG_STRIP if cur_diag else STRIP):
                    rows = diag_rows(sub) if cur_diag else tk
                    m_old = m_sc[a, :, sub]
                    m_new = jnp.maximum(m_old, x_cur[:, sub])
                    p = jnp.exp2((s_cur[:rows, sub] - m_new).astype(BF16))
                    alpha = jnp.exp2(m_old - m_new)
                    m_sc[a, :, sub] = m_new
                    acc_sc[a, :, sub] = acc_sc[a, :, sub] * alpha + _dot(va[:, :rows], p)
                for sub in split(cs, DIAG_STRIP if bi == 0 else STRIP) if produce else ():
                    score_strip(s_oth, x_oth, k0, qi_n, bi, sub)

        final = last & (qi == nq - 1)
        more = jnp.logical_not(final)
        first = (j == 0) & (qi > 0)
        far_next = j_n < qi_n - 1
        for fin, sel in ((False, jnp.logical_not(first)), (True, first)):
            pl.when(sel & more & (j_n == qi_n))(functools.partial(run, 0, fin=fin))
            pl.when(sel & more & (j_n == qi_n - 1))(functools.partial(run, 1, fin=fin))
            pl.when(sel & more & far_next & jnp.logical_not(last))(functools.partial(run, None, fin=fin))
        pl.when(more & far_next & last)(functools.partial(run, None, cur_diag=True))
        pl.when(final)(functools.partial(run, None, cur_diag=True, produce=False))
        return qi_n, j_n

    def run_pairs(count, carry):
        qi, j = carry
        for h in range(count):
            qi, j = half(h % 2, qi, j)
        return qi, j

    npairs = nq * (nq + 1) // 2
    carry = lax.fori_loop(0, npairs // PAIRS_PER_BLOCK, lambda _, c: run_pairs(PAIRS_PER_BLOCK, c),
                          (jnp.int32(0), jnp.int32(0)))
    if npairs % PAIRS_PER_BLOCK:
        run_pairs(npairs % PAIRS_PER_BLOCK, carry)
    finalize(nq - 1)


def _diff(lam_p, kb, pt, bias, g_col, lam_init):
    B, S, _ = kb.shape
    nb, tb = pt.shape[1], pt.shape[3]
    tq = tk = bias.shape[-1]
    nq = S // tq
    assert tq % tb == 0 and tq % STRIP == 0 and bias.shape[1:] == (2, tk, tq)
    va_rows = DV_B + 16
    return pl.pallas_call(
        functools.partial(_diff_kernel, lam_init=lam_init),
        grid=(B, H_B),
        in_specs=[
            _resident(lam_p.shape),
            pl.BlockSpec((None, nb, DV_B, tb), lambda b, h: (b, 0, h, 0)),
            pl.BlockSpec((None, S, LANES), lambda b, h: (b, 0, h)),
            pl.BlockSpec((None, nb, DV_B, tb), lambda b, h: (b, 0, H_B + h, 0)),
            pl.BlockSpec((None, 2, tk, tq), lambda b, h: (h, 0, 0, 0)),
            _resident(g_col.shape),
        ],
        out_specs=pl.BlockSpec((None, S, DV_B), lambda b, h: (b, 0, h)),
        out_shape=jax.ShapeDtypeStruct((B, S, W_B_OUT), BF16),
        scratch_shapes=[
            pltpu.VMEM((nq, DV_B, 2 * tq), BF16),
            pltpu.VMEM((S // tk, va_rows, tk), BF16),
            pltpu.VMEM((tk, 2 * tq), F32),
            pltpu.VMEM((tk, 2 * tq), F32),
            pltpu.VMEM((1, 2 * tq), F32),
            pltpu.VMEM((1, 2 * tq), F32),
            pltpu.VMEM((ACC_SLOTS, 1, 2 * tq), F32),
            pltpu.VMEM((ACC_SLOTS, va_rows, 2 * tq), F32),
        ],
        compiler_params=_params("parallel", "parallel"),
        name="diff_attn",
    )(lam_p, pt, kb, pt, bias, g_col)


def _toeplitz_kernel(w_ref, o_ref, *, diff):
    width = o_ref.shape[-1]
    rows = width if diff else o_ref.shape[-2]
    t = pltpu.roll(jnp.broadcast_to(w_ref[...], (rows, width)), 0, 1, stride=1, stride_axis=0)
    if not diff:
        o_ref[...] = t
        return
    r = lax.broadcasted_iota(jnp.int32, (rows, width), 0)
    c = lax.broadcasted_iota(jnp.int32, (rows, width), 1)
    o_ref[0] = jnp.where(c >= r, t, NEG_INF).astype(o_ref.dtype)
    o_ref[1] = jnp.where(c < r, t, 0.0).astype(o_ref.dtype)


def _toeplitz(w, rows, diff):
    n, _, width = w.shape
    oshape = (n, 2, rows, width) if diff else (n, rows, width)
    oblock = (None,) + oshape[1:]
    return pl.pallas_call(
        functools.partial(_toeplitz_kernel, diff=diff),
        grid=(n,),
        in_specs=[pl.BlockSpec((None, 1, width), lambda i: (i, 0, 0))],
        out_specs=pl.BlockSpec(oblock, lambda i: (i,) + (0,) * (len(oshape) - 1)),
        out_shape=jax.ShapeDtypeStruct(oshape, BF16 if diff else F32),
        compiler_params=_params("parallel"),
        name="bias_diff" if diff else "bias_dilated",
    )(w)


def _layer_norm(h, g, b):
    mu = jnp.mean(h, axis=-1, keepdims=True)
    d = h - mu
    var = jnp.mean(d * d, axis=-1, keepdims=True)
    return d * lax.rsqrt(var + LN_EPS) * g + b


def _mix_kernel(x_ref, o0_ref, o1_ref, o2_ref, l0_ref, l1_ref, l2_ref, ob_ref, wg_ref, bg_ref, wa_ref, wb_ref,
                wo_ref, g_ref, b_ref, out_ref, *tok_sc):
    def token_major(ref, sc):
        dil, rows, _ = ref.shape
        if dil == 1:
            return lambda sl: ref[0, sl]
        nk = sc.shape[0]
        for res in range(dil):
            for k in range(nk):
                sc[k, pl.ds(res, rows, stride=dil), :] = ref[res, :, k * LANES:(k + 1) * LANES]
        return lambda sl: jnp.concatenate([sc[k, sl] for k in range(nk)], axis=1)

    l_g = (token_major(l0_ref, None), token_major(l1_ref, tok_sc[0]), token_major(l2_ref, tok_sc[1]))
    o_g = (token_major(o0_ref, None), token_major(o1_ref, tok_sc[2]), token_major(o2_ref, tok_sc[3]))
    tm = x_ref.shape[0]
    for r0 in range(0, tm, ROW_SLAB):
        sl = slice(r0, r0 + ROW_SLAB)
        x = x_ref[sl]
        xb = x.astype(BF16)
        l0, l1, l2 = (f(sl) for f in l_g)
        mx = jnp.maximum(jnp.maximum(l0, l1), l2)
        e0, e1, e2 = jnp.exp(l0 - mx), jnp.exp(l1 - mx), jnp.exp(l2 - mx)
        o_a = (e0 * o_g[0](sl) + e1 * o_g[1](sl) + e2 * o_g[2](sl)) / (e0 + e1 + e2)
        y_a = _dot(o_a.astype(BF16), wa_ref[...])
        y_b = _dot(ob_ref[sl], wb_ref[...])
        gate_a = jax.nn.sigmoid(_dot(xb, wg_ref[:, :D_MODEL]) + bg_ref[:, :D_MODEL])
        gate_b = jax.nn.sigmoid(_dot(xb, wg_ref[:, D_MODEL:]) + bg_ref[:, D_MODEL:])
        merged = gate_a * y_a + gate_b * y_b
        mix = _dot(merged.astype(BF16), wo_ref[...])
        out_ref[sl] = _layer_norm(DEEPNORM_ALPHA * x + mix, g_ref[...], b_ref[...])


def _mix(x, o_g, lse_g, o_b, w_gate, b_gate, w_a, w_b, w_o, ln_g, ln_b, tm):
    B, S, D = x.shape
    tok = lambda w: pl.BlockSpec((None, tm, w), lambda b, i: (b, i, 0))
    res = [pl.BlockSpec((None, a.shape[1], tm // a.shape[1], W_A_OUT), lambda b, i: (b, 0, i, 0)) for a in o_g]
    n_sc = 2 * sum(a.shape[1] > 1 for a in o_g)
    return pl.pallas_call(
        _mix_kernel,
        grid=(B, S // tm),
        in_specs=[tok(D)] + res + res + [tok(W_B_OUT)]
        + [_resident(a.shape) for a in (w_gate, b_gate, w_a, w_b, w_o, ln_g, ln_b)],
        out_specs=tok(D),
        out_shape=jax.ShapeDtypeStruct((B, S, D), F32),
        scratch_shapes=[pltpu.VMEM((W_A_OUT // LANES, tm, LANES), F32)] * n_sc,
        compiler_params=_params("parallel", "parallel"),
        name="mix",
    )(x, *o_g, *lse_g, o_b, w_gate, b_gate, w_a, w_b, w_o, ln_g, ln_b)


def _mlp_kernel(x_ref, w1_ref, w2_ref, g_ref, b_ref, out_ref):
    for r0 in range(0, x_ref.shape[0], ROW_SLAB):
        rows = slice(r0, r0 + ROW_SLAB)
        x = x_ref[rows]
        xb = x.astype(BF16)
        ff = jnp.zeros(x.shape, F32)
        for c in range(D_FF // (2 * CHUNK)):
            sl = slice(c * 2 * CHUNK, (c + 1) * 2 * CHUNK)
            h = jnp.maximum(_dot(xb, w1_ref[:, sl]), 0.0)
            ff = ff + _dot((h * h).astype(BF16), w2_ref[sl, :])
        out_ref[rows] = _layer_norm(DEEPNORM_ALPHA * x + ff, g_ref[...], b_ref[...])


def _mlp(x, w1, w2, ln_g, ln_b, tm):
    B, S, D = x.shape
    tok = pl.BlockSpec((None, tm, D), lambda b, i: (b, i, 0))
    return pl.pallas_call(
        _mlp_kernel,
        grid=(B, S // tm),
        in_specs=[tok] + [_resident(a.shape) for a in (w1, w2, ln_g, ln_b)],
        out_specs=tok,
        out_shape=jax.ShapeDtypeStruct((B, S, D), F32),
        compiler_params=_params("parallel", "parallel"),
        name="mlp",
    )(x, w1, w2, ln_g, ln_b)


def _t5_bucket(dist):
    n = jnp.maximum(dist, 0)
    max_exact = NUM_BUCKETS // 2
    nf = jnp.maximum(n, 1).astype(F32)
    large = max_exact + (jnp.log(nf / max_exact) / math.log(T5_MAX_DISTANCE / max_exact)
                         * (NUM_BUCKETS - max_exact)).astype(jnp.int32)
    large = jnp.minimum(large, NUM_BUCKETS - 1)
    return jnp.where(n < max_exact, n, large)


def _dilated_bias(rel_bias, g, dil):
    j = jnp.arange(2 * BLK)
    tab = rel_bias[:, g * H_A:(g + 1) * H_A].astype(F32)
    vec = tab[_t5_bucket(jnp.maximum(BLK - j, 0) * dil)]
    vec = jnp.where((j <= BLK)[:, None], vec, NEG_INF)
    bias = _toeplitz(vec.T.reshape(H_A, 1, 2 * BLK), BLK, diff=False)
    return bias.reshape(H_A // 2, 2 * BLK, 2 * BLK)


def _diff_bias(rel_bias, tq):
    assert tq >= T5_MAX_DISTANCE
    tab = rel_bias[:, N_GROUPS * H_A:].astype(F32)
    tab = (tab - tab[NUM_BUCKETS - 1:NUM_BUCKETS]) * LOG2E
    vec = tab[_t5_bucket(jnp.arange(tq))]
    return _toeplitz(vec.T.reshape(H_B, 1, tq), tq, diff=True)


def kernel(x, w_in, b_gate, lambda_q1, lambda_k1, lambda_q2, lambda_k2, subln_g, rel_bias, w_proj_a, w_proj_b,
           w_out, ln1_g, ln1_b, ln2_g, ln2_b, w_mlp1, w_mlp2):
    B, S, D = x.shape
    tq = min(TQ, S)
    tm = min(TM, S)
    w = w_in[0]
    w_a = w[:, :COLS_A].reshape(D, 3, N_GROUPS, W_A_OUT)
    w_a = jnp.concatenate([w_a[:, :1] * QK_SCALE, w_a[:, 1:]], axis=1).transpose(0, 2, 1, 3).reshape(D, COLS_A)
    w_qk = w[:, COLS_A:COLS_A + COLS_B_QK].reshape(D, 4, H_B, HEAD_DIM)
    w_qb = jnp.concatenate([w_qk[:, 0], w_qk[:, 1]], axis=-1).reshape(D, W_B_OUT)
    w_kb = jnp.concatenate([w_qk[:, 2], w_qk[:, 3]], axis=-1).reshape(D, W_B_OUT)
    w_vb = w[:, COLS_A + COLS_B_QK:COLS_A + COLS_B]
    w_n = jnp.concatenate([w_a, w_kb], axis=1).astype(BF16)
    w_t = jnp.concatenate([w_qb, w_vb], axis=1).T.astype(BF16)
    w_gate = w[:, COLS_A + COLS_B:].astype(BF16)

    *a_g, kb, pt = _proj(x, w_n, w_t, tm, tq)

    o_g, lse_g = [], []
    for g, (win, dil) in enumerate(DIL_PAIRS):
        assert win // dil == BLK
        o, lse = _dilated(a_g[g], _dilated_bias(rel_bias, g, dil), g)
        o_g.append(o)
        lse_g.append(lse)

    lam_init = 0.8 - 0.6 * math.exp(-0.3 * 0)
    lam_p = jnp.concatenate([lambda_q1, lambda_k1, lambda_q2, lambda_k2], axis=0).astype(F32)
    o_b = _diff(lam_p, kb, pt, _diff_bias(rel_bias, min(DIFF_T, S)), subln_g[0].reshape(DV_B, 1), lam_init)

    x1 = _mix(x, o_g, lse_g, o_b, w_gate, b_gate, w_proj_a[0].astype(BF16), w_proj_b[0].astype(BF16),
              w_out[0].astype(BF16), ln1_g, ln1_b, tm)
    return _mlp(x1, w_mlp1[0].astype(BF16), w_mlp2[0].astype(BF16), ln2_g, ln2_b, tm)
```

```python
import functools
import math

import jax
import jax.numpy as jnp
import numpy as np
from jax import lax
from jax.experimental import pallas as pl
from jax.experimental.pallas import tpu as pltpu

D_MODEL = 1024
HEAD_DIM = 64
DIL_PAIRS = ((128, 1), (512, 4), (2048, 16))
N_GROUPS = len(DIL_PAIRS)
H_A = 8
H_B = 8
DV_B = 2 * HEAD_DIM
D_FF = 4 * D_MODEL
NUM_BUCKETS = 32
T5_MAX_DISTANCE = 128
BLK = 128
LN_EPS = 1e-5
NEG_INF = -1e30
W_A_OUT = H_A * HEAD_DIM
W_B_OUT = H_B * DV_B
COLS_A = 3 * N_GROUPS * H_A * HEAD_DIM
COLS_B_QK = 4 * H_B * HEAD_DIM
COLS_B = COLS_B_QK + H_B * DV_B
DEPTH = 1
DEEPNORM_ALPHA = (2.0 * DEPTH) ** 0.25
QK_SCALE = HEAD_DIM ** -0.5
LOG2E = math.log2(math.e)

LANES = 128
COLS_N = COLS_A + H_B * 2 * HEAD_DIM
COLS_T = 2 * W_B_OUT
CHUNK = 512
TQ = 512
DIFF_T = 1024
STRIP = 512
DIAG_STRIP = 256
PAIRS_PER_BLOCK = 2
ACC_SLOTS = PAIRS_PER_BLOCK
TM = 512
ROW_SLAB = 256
DILATED_ROWS = 1024
VMEM_LIMIT = 56 * 1024 * 1024

BF16 = jnp.bfloat16
F32 = jnp.float32


def _dot(a, b):
    return jnp.dot(a, b, preferred_element_type=F32)


def _dot_nt(a, b):
    return lax.dot_general(a, b, (((1,), (1,)), ((), ())), preferred_element_type=F32)


def _resident(shape):
    nd = len(shape)
    return pl.BlockSpec(shape, lambda *_: (0,) * nd, pipeline_mode=pl.Buffered(1))


def _params(*sem, flags=None):
    return pltpu.CompilerParams(dimension_semantics=sem, vmem_limit_bytes=VMEM_LIMIT, flags=flags)


def _proj_kernel(x_ref, wn_ref, wt_ref, a0_ref, a1_ref, a2_ref, kb_ref, ot_ref, rows_sc, *, tq):
    xb = x_ref[...].astype(BF16)
    tm = xb.shape[0]
    a_refs = (a0_ref, a1_ref, a2_ref)
    for c in range(COLS_N // CHUNK):
        r = _dot(xb, wn_ref[:, c * CHUNK:(c + 1) * CHUNK])
        g, part = divmod(c, 3)
        if g >= N_GROUPS:
            kb_ref[:, (c - 3 * N_GROUPS) * CHUNK:(c - 3 * N_GROUPS + 1) * CHUNK] = r.astype(BF16)
            continue
        sl = slice(part * W_A_OUT, (part + 1) * W_A_OUT)
        dil = DIL_PAIRS[g][1]
        if dil == 1:
            a_refs[g][0, :, sl] = r.astype(BF16)
            continue
        for k in range(CHUNK // LANES):
            rows_sc[k] = r[:, k * LANES:(k + 1) * LANES]
        for res in range(dil):
            for k in range(CHUNK // LANES):
                piece = rows_sc[k, pl.ds(res, tm // dil, stride=dil), :]
                a_refs[g][res, :, part * W_A_OUT + k * LANES:part * W_A_OUT + (k + 1) * LANES] = piece.astype(BF16)
    for c in range(COLS_T // CHUNK):
        sl = slice(c * CHUNK, (c + 1) * CHUNK)
        r = _dot_nt(wt_ref[sl, :], xb)
        if (c + 1) * CHUNK <= W_B_OUT:
            r = r * (QK_SCALE * LOG2E)
        r = r.astype(BF16)
        for t in range(tm // tq):
            ot_ref[t, sl, :] = r[:, t * tq:(t + 1) * tq]


def _proj(x, w_n, w_t, tm, tq):
    B, S, D = x.shape
    assert CHUNK == W_A_OUT
    dils = [d for _, d in DIL_PAIRS]
    return pl.pallas_call(
        functools.partial(_proj_kernel, tq=tq),
        grid=(B, S // tm),
        in_specs=[
            pl.BlockSpec((None, tm, D), lambda b, i: (b, i, 0)),
            _resident((D, COLS_N)),
            _resident((COLS_T, D)),
        ],
        out_specs=[pl.BlockSpec((None, d, tm // d, 3 * W_A_OUT), lambda b, i: (b, 0, i, 0)) for d in dils] + [
            pl.BlockSpec((None, tm, W_B_OUT), lambda b, i: (b, i, 0)),
            pl.BlockSpec((None, tm // tq, COLS_T, tq), lambda b, i: (b, i, 0, 0)),
        ],
        out_shape=[jax.ShapeDtypeStruct((B, d, S // d, 3 * W_A_OUT), BF16) for d in dils] + [
            jax.ShapeDtypeStruct((B, S, W_B_OUT), BF16),
            jax.ShapeDtypeStruct((B, S // tq, COLS_T, tq), BF16),
        ],
        scratch_shapes=[pltpu.VMEM((CHUNK // LANES, tm, LANES), F32)],
        compiler_params=_params("parallel", "parallel"),
        name="proj",
    )(x, w_n, w_t)


def _dilated_kernel(q_ref, k_ref, v_ref, kp_ref, vp_ref, bias_ref, o_ref, lse_ref, kw_sc, vw_sc):
    nres, tb, _ = q_ref.shape
    first = pl.program_id(2) == 0
    lane = lax.broadcasted_iota(jnp.int32, (BLK, LANES), 1)
    lo = lane < HEAD_DIM
    in_prev = lax.broadcasted_iota(jnp.int32, (2 * BLK, 2 * BLK), 1) < BLK
    ones = jnp.ones((2 * BLK, LANES), BF16)
    for r in range(nres):
        kw_sc[r, :BLK], kw_sc[r, BLK:] = kp_ref[r], k_ref[r]
        vw_sc[r, :BLK], vw_sc[r, BLK:] = vp_ref[r], v_ref[r]
        for jb in range(tb // BLK):
            rows = slice(jb * BLK, (jb + 1) * BLK)
            win = slice(jb * BLK, (jb + 2) * BLK)
            for p in range(H_A // 2):
                cols = slice(p * LANES, (p + 1) * LANES)
                q2 = q_ref[r, rows, cols]
                zero = jnp.zeros_like(q2)
                qd = jnp.concatenate([jnp.where(lo, q2, zero), jnp.where(lo, zero, q2)], axis=0)
                s = _dot_nt(qd, kw_sc[r, win, cols]) + bias_ref[p]
                if jb == 0:
                    s = jnp.where(first & in_prev, NEG_INF, s)
                m = jnp.max(s, axis=-1, keepdims=True)
                pw = jnp.exp((s - m).astype(BF16))
                oa = _dot(pw, jnp.concatenate([vw_sc[r, win, cols], ones], axis=1))
                den = oa[:, LANES:]
                o2 = oa[:, :LANES] / den
                lse = m + jnp.log(den)
                o_ref[r, rows, cols] = jnp.where(lo, o2[:BLK], o2[BLK:])
                lse_ref[r, rows, cols] = jnp.where(lo, lse[:BLK], lse[BLK:])


def _dilated(a, bias, g):
    B, dil, L, _ = a.shape
    tb = min(DILATED_ROWS, L)
    nres = min(DILATED_ROWS // tb, dil)
    cur = lambda c: pl.BlockSpec((None, nres, tb, W_A_OUT), lambda b, r, n: (b, r, n, c))
    prev = lambda c: pl.BlockSpec(
        (None, nres, BLK, W_A_OUT), lambda b, r, n: (b, r, jnp.maximum(n * (tb // BLK) - 1, 0), c))
    out = pl.BlockSpec((None, nres, tb, W_A_OUT), lambda b, r, n: (b, r, n, 0))
    return pl.pallas_call(
        _dilated_kernel,
        grid=(B, dil // nres, L // tb),
        in_specs=[cur(0), cur(1), cur(2), prev(1), prev(2), _resident(bias.shape)],
        out_specs=[out, out],
        out_shape=[jax.ShapeDtypeStruct((B, dil, L, W_A_OUT), F32)] * 2,
        scratch_shapes=[pltpu.VMEM((nres, BLK + tb, W_A_OUT), BF16)] * 2,
        compiler_params=_params("parallel", "parallel", "arbitrary"),
        name=f"dilated{g}",
    )(a, a, a, a, a, bias)


def _diff_kernel(lam_ref, qt_ref, k_ref, vt_ref, bias_ref, g_ref, o_ref, qd_sc, va_sc, s0_sc, s1_sc, x0_sc, x1_sc,
                 m_sc, acc_sc, *, lam_init):
    nb, _, tb = qt_ref.shape
    tq = tk = bias_ref.shape[-1]
    qb = tq // tb
    nq = nb // qb
    s_bufs, x_bufs = (s0_sc, s1_sc), (x0_sc, x1_sc)
    row = lax.broadcasted_iota(jnp.int32, (DV_B, tb), 0)
    for i in range(nb):
        blk, part = divmod(i, qb)
        qt = qt_ref[i]
        zero = jnp.zeros_like(qt)
        qd_sc[blk, :, part * tb:(part + 1) * tb] = jnp.where(row < HEAD_DIM, qt, zero)
        qd_sc[blk, :, tq + part * tb:tq + (part + 1) * tb] = jnp.where(row < HEAD_DIM, zero, qt)
        va_sc[blk, :DV_B, part * tb:(part + 1) * tb] = vt_ref[i]
    for i in range(nq):
        va_sc[i, DV_B:, :] = jnp.ones((va_sc.shape[1] - DV_B, tk), BF16)

    lam = (jnp.exp(jnp.sum(lam_ref[0:1, :] * lam_ref[1:2, :], axis=-1, keepdims=True))
           - jnp.exp(jnp.sum(lam_ref[2:3, :] * lam_ref[3:4, :], axis=-1, keepdims=True)) + lam_init)

    def reset(a):
        m_sc[a] = jnp.full(m_sc.shape[1:], -jnp.inf, F32)
        acc_sc[a] = jnp.zeros(acc_sc.shape[1:], F32)

    def finalize(qi):
        a = qi % ACC_SLOTS
        acc = acc_sc[a]
        on = acc[:DV_B] / acc[DV_B:DV_B + 1]
        o = on[:, :tq] - lam * on[:, tq:]
        ms = jnp.mean(o * o, axis=0, keepdims=True)
        o = o * lax.rsqrt(ms + LN_EPS) * g_ref[...] * (1.0 - lam_init)
        o_ref[pl.ds(pl.multiple_of(qi * tq, tq), tq), :] = o.T.astype(BF16)
        reset(a)

    def is_last(qi, j):
        return j == qi

    strips = [slice(c * STRIP, (c + 1) * STRIP) for c in range(2 * tq // STRIP)]

    def split(cs, width):
        return [slice(c, c + width) for c in range(cs.start, cs.stop, width)]

    def diag_rows(cs):
        return min(tk, cs.start % tq + cs.stop - cs.start)

    def score_strip(s_buf, x_buf, k0, qi, bi, cs):
        q0, width = cs.start % tq, cs.stop - cs.start
        rows = diag_rows(cs) if bi == 0 else tk
        s = _dot(k_ref[pl.ds(k0, rows), :], qd_sc[qi, :, cs])
        if bi == 0:
            s = s + bias_ref[0, :rows, q0:q0 + width].astype(F32)
        elif bi == 1 and q0 < T5_MAX_DISTANCE:
            far = tk - T5_MAX_DISTANCE
            s = jnp.concatenate([s[:far], s[far:] + bias_ref[1, far:, q0:q0 + width].astype(F32)], axis=0)
        s_buf[:rows, cs] = s
        if rows < tk:
            s_buf[rows:, cs] = jnp.full((tk - rows, width), NEG_INF, F32)
        x_buf[:, cs] = jnp.max(s, axis=0, keepdims=True)

    for a in range(ACC_SLOTS):
        reset(a)
    for cs in strips:
        for sub in split(cs, DIAG_STRIP):
            score_strip(s_bufs[0], x_bufs[0], 0, 0, 0, sub)

    def half(slot, qi, j):
        s_cur, x_cur, s_oth, x_oth = s_bufs[slot], x_bufs[slot], s_bufs[1 - slot], x_bufs[1 - slot]
        a = qi % ACC_SLOTS
        last = is_last(qi, j)
        j_n = jnp.where(last, 0, j + 1)
        qi_n = jnp.minimum(jnp.where(last, qi + 1, qi), nq - 1)
        k0 = pl.multiple_of(j_n * tk, tk)
        va = va_sc[j]

        def run(bi, cur_diag=False, produce=True, fin=False):
            if fin:
                finalize(qi - 1)
            for cs in strips:
                for sub in split(cs, DIAG_STRIP if bi == 0 else STRIP) if produce else ():
                    score_strip(s_oth, x_oth, k0, qi_n, bi, sub)
                for sub in split(cs, DIAG_STRIP if cur_diag else STRIP):
                    rows = diag_rows(sub) if cur_diag else tk
                    m_old = m_sc[a, :, sub]
                    m_new = jnp.maximum(m_old, x_cur[:, sub])
                    p = jnp.exp2((s_cur[:rows, sub] - m_new).astype(BF16))
                    alpha = jnp.exp2(m_old - m_new)
                    m_sc[a, :, sub] = m_new
                    acc_sc[a, :, sub] = acc_sc[a, :, sub] * alpha + _dot(va[:, :rows], p)

        final = last & (qi == nq - 1)
        more = jnp.logical_not(final)
        first = (j == 0) & (qi > 0)
        far_next = j_n < qi_n - 1
        for fin, sel in ((False, jnp.logical_not(first)), (True, first)):
            pl.when(sel & more & (j_n == qi_n))(functools.partial(run, 0, fin=fin))
            pl.when(sel & more & (j_n == qi_n - 1))(functools.partial(run, 1, fin=fin))
            pl.when(sel & more & far_next & jnp.logical_not(last))(functools.partial(run, None, fin=fin))
        pl.when(more & far_next & last)(functools.partial(run, None, cur_diag=True))
        pl.when(final)(functools.partial(run, None, cur_diag=True, produce=False))
        return qi_n, j_n

    def run_pairs(count, carry):
        qi, j = carry
        for h in range(count):
            qi, j = half(h % 2, qi, j)
        return qi, j

    npairs = nq * (nq + 1) // 2
    carry = lax.fori_loop(0, npairs // PAIRS_PER_BLOCK, lambda _, c: run_pairs(PAIRS_PER_BLOCK, c),
                          (jnp.int32(0), jnp.int32(0)))
    if npairs % PAIRS_PER_BLOCK:
        run_pairs(npairs % PAIRS_PER_BLOCK, carry)
    finalize(nq - 1)


def _diff(lam_p, kb, pt, bias, g_col, lam_init):
    B, S, _ = kb.shape
    nb, tb = pt.shape[1], pt.shape[3]
    tq = tk = bias.shape[-1]
    nq = S // tq
    assert tq % tb == 0 and tq % STRIP == 0 and bias.shape[1:] == (2, tk, tq)
    va_rows = DV_B + 16
    return pl.pallas_call(
        functools.partial(_diff_kernel, lam_init=lam_init),
        grid=(B, H_B),
        in_specs=[
            _resident(lam_p.shape),
            pl.BlockSpec((None, nb, DV_B, tb), lambda b, h: (b, 0, h, 0)),
            pl.BlockSpec((None, S, LANES), lambda b, h: (b, 0, h)),
            pl.BlockSpec((None, nb, DV_B, tb), lambda b, h: (b, 0, H_B + h, 0)),
            pl.BlockSpec((None, 2, tk, tq), lambda b, h: (h, 0, 0, 0)),
            _resident(g_col.shape),
        ],
        out_specs=pl.BlockSpec((None, S, DV_B), lambda b, h: (b, 0, h)),
        out_shape=jax.ShapeDtypeStruct((B, S, W_B_OUT), BF16),
        scratch_shapes=[
            pltpu.VMEM((nq, DV_B, 2 * tq), BF16),
            pltpu.VMEM((S // tk, va_rows, tk), BF16),
            pltpu.VMEM((tk, 2 * tq), F32),
            pltpu.VMEM((tk, 2 * tq), F32),
            pltpu.VMEM((1, 2 * tq), F32),
            pltpu.VMEM((1, 2 * tq), F32),
            pltpu.VMEM((ACC_SLOTS, 1, 2 * tq), F32),
            pltpu.VMEM((ACC_SLOTS, va_rows, 2 * tq), F32),
        ],
        compiler_params=_params("parallel", "parallel"),
        name="diff_attn",
    )(lam_p, pt, kb, pt, bias, g_col)


def _toeplitz_kernel(w_ref, o_ref, *, diff):
    width = o_ref.shape[-1]
    rows = width if diff else o_ref.shape[-2]
    t = pltpu.roll(jnp.broadcast_to(w_ref[...], (rows, width)), 0, 1, stride=1, stride_axis=0)
    if not diff:
        o_ref[...] = t
        return
    r = lax.broadcasted_iota(jnp.int32, (rows, width), 0)
    c = lax.broadcasted_iota(jnp.int32, (rows, width), 1)
    o_ref[0] = jnp.where(c >= r, t, NEG_INF).astype(o_ref.dtype)
    o_ref[1] = jnp.where(c < r, t, 0.0).astype(o_ref.dtype)


def _toeplitz(w, rows, diff):
    n, _, width = w.shape
    oshape = (n, 2, rows, width) if diff else (n, rows, width)
    oblock = (None,) + oshape[1:]
    return pl.pallas_call(
        functools.partial(_toeplitz_kernel, diff=diff),
        grid=(n,),
        in_specs=[pl.BlockSpec((None, 1, width), lambda i: (i, 0, 0))],
        out_specs=pl.BlockSpec(oblock, lambda i: (i,) + (0,) * (len(oshape) - 1)),
        out_shape=jax.ShapeDtypeStruct(oshape, BF16 if diff else F32),
        compiler_params=_params("parallel"),
        name="bias_diff" if diff else "bias_dilated",
    )(w)


def _layer_norm(h, g, b):
    mu = jnp.mean(h, axis=-1, keepdims=True)
    d = h - mu
    var = jnp.mean(d * d, axis=-1, keepdims=True)
    return d * lax.rsqrt(var + LN_EPS) * g + b


def _mix_kernel(x_ref, o0_ref, o1_ref, o2_ref, l0_ref, l1_ref, l2_ref, ob_ref, wg_ref, bg_ref, wa_ref, wb_ref,
                wo_ref, g_ref, b_ref, out_ref, *tok_sc):
    def token_major(ref, sc):
        dil, rows, _ = ref.shape
        if dil == 1:
            return lambda sl: ref[0, sl]
        nk = sc.shape[0]
        for res in range(dil):
            for k in range(nk):
                sc[k, pl.ds(res, rows, stride=dil), :] = ref[res, :, k * LANES:(k + 1) * LANES]
        return lambda sl: jnp.concatenate([sc[k, sl] for k in range(nk)], axis=1)

    l_g = (token_major(l0_ref, None), token_major(l1_ref, tok_sc[0]), token_major(l2_ref, tok_sc[1]))
    o_g = (token_major(o0_ref, None), token_major(o1_ref, tok_sc[2]), token_major(o2_ref, tok_sc[3]))
    tm = x_ref.shape[0]
    for r0 in range(0, tm, ROW_SLAB):
        sl = slice(r0, r0 + ROW_SLAB)
        x = x_ref[sl]
        xb = x.astype(BF16)
        l0, l1, l2 = (f(sl) for f in l_g)
        mx = jnp.maximum(jnp.maximum(l0, l1), l2)
        e0, e1, e2 = jnp.exp(l0 - mx), jnp.exp(l1 - mx), jnp.exp(l2 - mx)
        o_a = (e0 * o_g[0](sl) + e1 * o_g[1](sl) + e2 * o_g[2](sl)) / (e0 + e1 + e2)
        y_a = _dot(o_a.astype(BF16), wa_ref[...])
        y_b = _dot(ob_ref[sl], wb_ref[...])
        gate_a = jax.nn.sigmoid(_dot(xb, wg_ref[:, :D_MODEL]) + bg_ref[:, :D_MODEL])
        gate_b = jax.nn.sigmoid(_dot(xb, wg_ref[:, D_MODEL:]) + bg_ref[:, D_MODEL:])
        merged = gate_a * y_a + gate_b * y_b
        mix = _dot(merged.astype(BF16), wo_ref[...])
        out_ref[sl] = _layer_norm(DEEPNORM_ALPHA * x + mix, g_ref[...], b_ref[...])


def _mix(x, o_g, lse_g, o_b, w_gate, b_gate, w_a, w_b, w_o, ln_g, ln_b, tm):
    B, S, D = x.shape
    tok = lambda w: pl.BlockSpec((None, tm, w), lambda b, i: (b, i, 0))
    res = [pl.BlockSpec((None, a.shape[1], tm // a.shape[1], W_A_OUT), lambda b, i: (b, 0, i, 0)) for a in o_g]
    n_sc = 2 * sum(a.shape[1] > 1 for a in o_g)
    return pl.pallas_call(
        _mix_kernel,
        grid=(B, S // tm),
        in_specs=[tok(D)] + res + res + [tok(W_B_OUT)]
        + [_resident(a.shape) for a in (w_gate, b_gate, w_a, w_b, w_o, ln_g, ln_b)],
        out_specs=tok(D),
        out_shape=jax.ShapeDtypeStruct((B, S, D), F32),
        scratch_shapes=[pltpu.VMEM((W_A_OUT // LANES, tm, LANES), F32)] * n_sc,
        compiler_params=_params("parallel", "parallel"),
        name="mix",
    )(x, *o_g, *lse_g, o_b, w_gate, b_gate, w_a, w_b, w_o, ln_g, ln_b)


def _mlp_kernel(x_ref, w1_ref, w2_ref, g_ref, b_ref, out_ref):
    for r0 in range(0, x_ref.shape[0], ROW_SLAB):
        rows = slice(r0, r0 + ROW_SLAB)
        x = x_ref[rows]
        xb = x.astype(BF16)
        ff = jnp.zeros(x.shape, F32)
        for c in range(D_FF // (2 * CHUNK)):
            sl = slice(c * 2 * CHUNK, (c + 1) * 2 * CHUNK)
            h = jnp.maximum(_dot(xb, w1_ref[:, sl]), 0.0)
            ff = ff + _dot((h * h).astype(BF16), w2_ref[sl, :])
        out_ref[rows] = _layer_norm(DEEPNORM_ALPHA * x + ff, g_ref[...], b_ref[...])


def _mlp(x, w1, w2, ln_g, ln_b, tm):
    B, S, D = x.shape
    tok = pl.BlockSpec((None, tm, D), lambda b, i: (b, i, 0))
    return pl.pallas_call(
        _mlp_kernel,
        grid=(B, S // tm),
        in_specs=[tok] + [_resident(a.shape) for a in (w1, w2, ln_g, ln_b)],
        out_specs=tok,
        out_shape=jax.ShapeDtypeStruct((B, S, D), F32),
        compiler_params=_params("parallel", "parallel"),
        name="mlp",
    )(x, w1, w2, ln_g, ln_b)


def _t5_bucket(dist):
    n = jnp.maximum(dist, 0)
    max_exact = NUM_BUCKETS // 2
    nf = jnp.maximum(n, 1).astype(F32)
    large = max_exact + (jnp.log(nf / max_exact) / math.log(T5_MAX_DISTANCE / max_exact)
                         * (NUM_BUCKETS - max_exact)).astype(jnp.int32)
    large = jnp.minimum(large, NUM_BUCKETS - 1)
    return jnp.where(n < max_exact, n, large)


def _dilated_bias(rel_bias, g, dil):
    j = jnp.arange(2 * BLK)
    tab = rel_bias[:, g * H_A:(g + 1) * H_A].astype(F32)
    vec = tab[_t5_bucket(jnp.maximum(BLK - j, 0) * dil)]
    vec = jnp.where((j <= BLK)[:, None], vec, NEG_INF)
    bias = _toeplitz(vec.T.reshape(H_A, 1, 2 * BLK), BLK, diff=False)
    return bias.reshape(H_A // 2, 2 * BLK, 2 * BLK)


def _diff_bias(rel_bias, tq):
    assert tq >= T5_MAX_DISTANCE
    tab = rel_bias[:, N_GROUPS * H_A:].astype(F32)
    tab = (tab - tab[NUM_BUCKETS - 1:NUM_BUCKETS]) * LOG2E
    vec = tab[_t5_bucket(jnp.arange(tq))]
    return _toeplitz(vec.T.reshape(H_B, 1, tq), tq, diff=True)


def kernel(x, w_in, b_gate, lambda_q1, lambda_k1, lambda_q2, lambda_k2, subln_g, rel_bias, w_proj_a, w_proj_b,
           w_out, ln1_g, ln1_b, ln2_g, ln2_b, w_mlp1, w_mlp2):
    B, S, D = x.shape
    tq = min(TQ, S)
    tm = min(TM, S)
    w = w_in[0]
    w_a = w[:, :COLS_A].reshape(D, 3, N_GROUPS, W_A_OUT)
    w_a = jnp.concatenate([w_a[:, :1] * QK_SCALE, w_a[:, 1:]], axis=1).transpose(0, 2, 1, 3).reshape(D, COLS_A)
    w_qk = w[:, COLS_A:COLS_A + COLS_B_QK].reshape(D, 4, H_B, HEAD_DIM)
    w_qb = jnp.concatenate([w_qk[:, 0], w_qk[:, 1]], axis=-1).reshape(D, W_B_OUT)
    w_kb = jnp.concatenate([w_qk[:, 2], w_qk[:, 3]], axis=-1).reshape(D, W_B_OUT)
    w_vb = w[:, COLS_A + COLS_B_QK:COLS_A + COLS_B]
    w_n = jnp.concatenate([w_a, w_kb], axis=1).astype(BF16)
    w_t = jnp.concatenate([w_qb, w_vb], axis=1).T.astype(BF16)
    w_gate = w[:, COLS_A + COLS_B:].astype(BF16)

    *a_g, kb, pt = _proj(x, w_n, w_t, tm, tq)

    o_g, lse_g = [], []
    for g, (win, dil) in enumerate(DIL_PAIRS):
        assert win // dil == BLK
        o, lse = _dilated(a_g[g], _dilated_bias(rel_bias, g, dil), g)
        o_g.append(o)
        lse_g.append(lse)

    lam_init = 0.8 - 0.6 * math.exp(-0.3 * 0)
    lam_p = jnp.concatenate([lambda_q1, lambda_k1, lambda_q2, lambda_k2], axis=0).astype(F32)
    o_b = _diff(lam_p, kb, pt, _diff_bias(rel_bias, min(DIFF_T, S)), subln_g[0].reshape(DV_B, 1), lam_init)

    x1 = _mix(x, o_g, lse_g, o_b, w_gate, b_gate, w_proj_a[0].astype(BF16), w_proj_b[0].astype(BF16),
              w_out[0].astype(BF16), ln1_g, ln1_b, tm)
    return _mlp(x1, w_mlp1[0].astype(BF16), w_mlp2[0].astype(BF16), ln2_g, ln2_b, tm)
```

```python
import functools
import math

import jax
import jax.numpy as jnp
import numpy as np
from jax import lax
from jax.experimental import pallas as pl
from jax.experimental.pallas import tpu as pltpu

D_MODEL = 1024
HEAD_DIM = 64
DIL_PAIRS = ((128, 1), (512, 4), (2048, 16))
N_GROUPS = len(DIL_PAIRS)
H_A = 8
H_B = 8
DV_B = 2 * HEAD_DIM
D_FF = 4 * D_MODEL
NUM_BUCKETS = 32
T5_MAX_DISTANCE = 128
BLK = 128
LN_EPS = 1e-5
NEG_INF = -1e30
W_A_OUT = H_A * HEAD_DIM
W_B_OUT = H_B * DV_B
COLS_A = 3 * N_GROUPS * H_A * HEAD_DIM
COLS_B_QK = 4 * H_B * HEAD_DIM
COLS_B = COLS_B_QK + H_B * DV_B
DEPTH = 1
DEEPNORM_ALPHA = (2.0 * DEPTH) ** 0.25
QK_SCALE = HEAD_DIM ** -0.5
LOG2E = math.log2(math.e)

LANES = 128
COLS_N = COLS_A + H_B * 2 * HEAD_DIM
COLS_T = 2 * W_B_OUT
CHUNK = 512
TQ = 512
DIFF_T = 1024
STRIP = 256
DIAG_STRIP = 256
PAIRS_PER_BLOCK = 2
ACC_SLOTS = PAIRS_PER_BLOCK
TM = 512
ROW_SLAB = 256
DILATED_ROWS = 1024
VMEM_LIMIT = 56 * 1024 * 1024

BF16 = jnp.bfloat16
F32 = jnp.float32


def _dot(a, b):
    return jnp.dot(a, b, preferred_element_type=F32)


def _dot_nt(a, b):
    return lax.dot_general(a, b, (((1,), (1,)), ((), ())), preferred_element_type=F32)


def _resident(shape):
    nd = len(shape)
    return pl.BlockSpec(shape, lambda *_: (0,) * nd, pipeline_mode=pl.Buffered(1))


def _params(*sem, flags=None):
    return pltpu.CompilerParams(dimension_semantics=sem, vmem_limit_bytes=VMEM_LIMIT, flags=flags)


def _proj_kernel(x_ref, wn_ref, wt_ref, a0_ref, a1_ref, a2_ref, kb_ref, ot_ref, rows_sc, *, tq):
    xb = x_ref[...].astype(BF16)
    tm = xb.shape[0]
    a_refs = (a0_ref, a1_ref, a2_ref)
    for c in range(COLS_N // CHUNK):
        r = _dot(xb, wn_ref[:, c * CHUNK:(c + 1) * CHUNK])
        g, part = divmod(c, 3)
        if g >= N_GROUPS:
            kb_ref[:, (c - 3 * N_GROUPS) * CHUNK:(c - 3 * N_GROUPS + 1) * CHUNK] = r.astype(BF16)
            continue
        sl = slice(part * W_A_OUT, (part + 1) * W_A_OUT)
        dil = DIL_PAIRS[g][1]
        if dil == 1:
            a_refs[g][0, :, sl] = r.astype(BF16)
            continue
        for k in range(CHUNK // LANES):
            rows_sc[k] = r[:, k * LANES:(k + 1) * LANES]
        for res in range(dil):
            for k in range(CHUNK // LANES):
                piece = rows_sc[k, pl.ds(res, tm // dil, stride=dil), :]
                a_refs[g][res, :, part * W_A_OUT + k * LANES:part * W_A_OUT + (k + 1) * LANES] = piece.astype(BF16)
    for c in range(COLS_T // CHUNK):
        sl = slice(c * CHUNK, (c + 1) * CHUNK)
        r = _dot_nt(wt_ref[sl, :], xb)
        if (c + 1) * CHUNK <= W_B_OUT:
            r = r * (QK_SCALE * LOG2E)
        r = r.astype(BF16)
        for t in range(tm // tq):
            ot_ref[t, sl, :] = r[:, t * tq:(t + 1) * tq]


def _proj(x, w_n, w_t, tm, tq):
    B, S, D = x.shape
    assert CHUNK == W_A_OUT
    dils = [d for _, d in DIL_PAIRS]
    return pl.pallas_call(
        functools.partial(_proj_kernel, tq=tq),
        grid=(B, S // tm),
        in_specs=[
            pl.BlockSpec((None, tm, D), lambda b, i: (b, i, 0)),
            _resident((D, COLS_N)),
            _resident((COLS_T, D)),
        ],
        out_specs=[pl.BlockSpec((None, d, tm // d, 3 * W_A_OUT), lambda b, i: (b, 0, i, 0)) for d in dils] + [
            pl.BlockSpec((None, tm, W_B_OUT), lambda b, i: (b, i, 0)),
            pl.BlockSpec((None, tm // tq, COLS_T, tq), lambda b, i: (b, i, 0, 0)),
        ],
        out_shape=[jax.ShapeDtypeStruct((B, d, S // d, 3 * W_A_OUT), BF16) for d in dils] + [
            jax.ShapeDtypeStruct((B, S, W_B_OUT), BF16),
            jax.ShapeDtypeStruct((B, S // tq, COLS_T, tq), BF16),
        ],
        scratch_shapes=[pltpu.VMEM((CHUNK // LANES, tm, LANES), F32)],
        compiler_params=_params("parallel", "parallel"),
        name="proj",
    )(x, w_n, w_t)


def _dilated_kernel(q_ref, k_ref, v_ref, kp_ref, vp_ref, bias_ref, o_ref, lse_ref, kw_sc, vw_sc):
    nres, tb, _ = q_ref.shape
    first = pl.program_id(2) == 0
    lane = lax.broadcasted_iota(jnp.int32, (BLK, LANES), 1)
    lo = lane < HEAD_DIM
    in_prev = lax.broadcasted_iota(jnp.int32, (2 * BLK, 2 * BLK), 1) < BLK
    ones = jnp.ones((2 * BLK, LANES), BF16)
    for r in range(nres):
        kw_sc[r, :BLK], kw_sc[r, BLK:] = kp_ref[r], k_ref[r]
        vw_sc[r, :BLK], vw_sc[r, BLK:] = vp_ref[r], v_ref[r]
        for jb in range(tb // BLK):
            rows = slice(jb * BLK, (jb + 1) * BLK)
            win = slice(jb * BLK, (jb + 2) * BLK)
            for p in range(H_A // 2):
                cols = slice(p * LANES, (p + 1) * LANES)
                q2 = q_ref[r, rows, cols]
                zero = jnp.zeros_like(q2)
                qd = jnp.concatenate([jnp.where(lo, q2, zero), jnp.where(lo, zero, q2)], axis=0)
                s = _dot_nt(qd, kw_sc[r, win, cols]) + bias_ref[p]
                if jb == 0:
                    s = jnp.where(first & in_prev, NEG_INF, s)
                m = jnp.max(s, axis=-1, keepdims=True)
                pw = jnp.exp((s - m).astype(BF16))
                oa = _dot(pw, jnp.concatenate([vw_sc[r, win, cols], ones], axis=1))
                den = oa[:, LANES:]
                o2 = oa[:, :LANES] / den
                lse = m + jnp.log(den)
                o_ref[r, rows, cols] = jnp.where(lo, o2[:BLK], o2[BLK:])
                lse_ref[r, rows, cols] = jnp.where(lo, lse[:BLK], lse[BLK:])


def _dilated(a, bias, g):
    B, dil, L, _ = a.shape
    tb = min(DILATED_ROWS, L)
    nres = min(DILATED_ROWS // tb, dil)
    cur = lambda c: pl.BlockSpec((None, nres, tb, W_A_OUT), lambda b, r, n: (b, r, n, c))
    prev = lambda c: pl.BlockSpec(
        (None, nres, BLK, W_A_OUT), lambda b, r, n: (b, r, jnp.maximum(n * (tb // BLK) - 1, 0), c))
    out = pl.BlockSpec((None, nres, tb, W_A_OUT), lambda b, r, n: (b, r, n, 0))
    return pl.pallas_call(
        _dilated_kernel,
        grid=(B, dil // nres, L // tb),
        in_specs=[cur(0), cur(1), cur(2), prev(1), prev(2), _resident(bias.shape)],
        out_specs=[out, out],
        out_shape=[jax.ShapeDtypeStruct((B, dil, L, W_A_OUT), F32)] * 2,
        scratch_shapes=[pltpu.VMEM((nres, BLK + tb, W_A_OUT), BF16)] * 2,
        compiler_params=_params("parallel", "parallel", "arbitrary"),
        name=f"dilated{g}",
    )(a, a, a, a, a, bias)


def _diff_kernel(lam_ref, qt_ref, k_ref, vt_ref, bias_ref, g_ref, o_ref, qd_sc, va_sc, s0_sc, s1_sc, x0_sc, x1_sc,
                 m_sc, acc_sc, *, lam_init):
    nb, _, tb = qt_ref.shape
    tq = tk = bias_ref.shape[-1]
    qb = tq // tb
    nq = nb // qb
    s_bufs, x_bufs = (s0_sc, s1_sc), (x0_sc, x1_sc)
    row = lax.broadcasted_iota(jnp.int32, (DV_B, tb), 0)
    for i in range(nb):
        blk, part = divmod(i, qb)
        qt = qt_ref[i]
        zero = jnp.zeros_like(qt)
        qd_sc[blk, :, part * tb:(part + 1) * tb] = jnp.where(row < HEAD_DIM, qt, zero)
        qd_sc[blk, :, tq + part * tb:tq + (part + 1) * tb] = jnp.where(row < HEAD_DIM, zero, qt)
        va_sc[blk, :DV_B, part * tb:(part + 1) * tb] = vt_ref[i]
    for i in range(nq):
        va_sc[i, DV_B:, :] = jnp.ones((va_sc.shape[1] - DV_B, tk), BF16)

    lam = (jnp.exp(jnp.sum(lam_ref[0:1, :] * lam_ref[1:2, :], axis=-1, keepdims=True))
           - jnp.exp(jnp.sum(lam_ref[2:3, :] * lam_ref[3:4, :], axis=-1, keepdims=True)) + lam_init)

    def reset(a):
        m_sc[a] = jnp.full(m_sc.shape[1:], -jnp.inf, F32)
        acc_sc[a] = jnp.zeros(acc_sc.shape[1:], F32)

    def finalize(qi):
        a = qi % ACC_SLOTS
        acc = acc_sc[a]
        on = acc[:DV_B] / acc[DV_B:DV_B + 1]
        o = on[:, :tq] - lam * on[:, tq:]
        ms = jnp.mean(o * o, axis=0, keepdims=True)
        o = o * lax.rsqrt(ms + LN_EPS) * g_ref[...] * (1.0 - lam_init)
        o_ref[pl.ds(pl.multiple_of(qi * tq, tq), tq), :] = o.T.astype(BF16)
        reset(a)

    def is_last(qi, j):
        return j == qi

    strips = [slice(c * STRIP, (c + 1) * STRIP) for c in range(2 * tq // STRIP)]

    def split(cs, width):
        return [slice(c, c + width) for c in range(cs.start, cs.stop, width)]

    def diag_rows(cs):
        return min(tk, cs.start % tq + cs.stop - cs.start)

    def score_strip(s_buf, x_buf, k0, qi, bi, cs):
        q0, width = cs.start % tq, cs.stop - cs.start
        rows = diag_rows(cs) if bi == 0 else tk
        s = _dot(k_ref[pl.ds(k0, rows), :], qd_sc[qi, :, cs])
        if bi == 0:
            s = s + bias_ref[0, :rows, q0:q0 + width].astype(F32)
        elif bi == 1 and q0 < T5_MAX_DISTANCE:
            far = tk - T5_MAX_DISTANCE
            s = jnp.concatenate([s[:far], s[far:] + bias_ref[1, far:, q0:q0 + width].astype(F32)], axis=0)
        s_buf[:rows, cs] = s
        if rows < tk:
            s_buf[rows:, cs] = jnp.full((tk - rows, width), NEG_INF, F32)
        x_buf[:, cs] = jnp.max(s, axis=0, keepdims=True)

    for a in range(ACC_SLOTS):
        reset(a)
    for cs in strips:
        for sub in split(cs, DIAG_STRIP):
            score_strip(s_bufs[0], x_bufs[0], 0, 0, 0, sub)

    def half(slot, qi, j):
        s_cur, x_cur, s_oth, x_oth = s_bufs[slot], x_bufs[slot], s_bufs[1 - slot], x_bufs[1 - slot]
        a = qi % ACC_SLOTS
        last = is_last(qi, j)
        j_n = jnp.where(last, 0, j + 1)
        qi_n = jnp.minimum(jnp.where(last, qi + 1, qi), nq - 1)
        k0 = pl.multiple_of(j_n * tk, tk)
        va = va_sc[j]

        def run(bi, cur_diag=False, produce=True, fin=False):
            if fin:
                finalize(qi - 1)
            for cs in strips:
                for sub in split(cs, DIAG_STRIP if bi == 0 else STRIP) if produce else ():
                    score_strip(s_oth, x_oth, k0, qi_n, bi, sub)
                for sub in split(cs, DIAG_STRIP if cur_diag else STRIP):
                    rows = diag_rows(sub) if cur_diag else tk
                    m_old = m_sc[a, :, sub]
                    m_new = jnp.maximum(m_old, x_cur[:, sub])
                    p = jnp.exp2((s_cur[:rows, sub] - m_new).astype(BF16))
                    alpha = jnp.exp2(m_old - m_new)
                    m_sc[a, :, sub] = m_new
                    acc_sc[a, :, sub] = acc_sc[a, :, sub] * alpha + _dot(va[:, :rows], p)

        final = last & (qi == nq - 1)
        more = jnp.logical_not(final)
        first = (j == 0) & (qi > 0)
        far_next = j_n < qi_n - 1
        for fin, sel in ((False, jnp.logical_not(first)), (True, first)):
            pl.when(sel & more & (j_n == qi_n))(functools.partial(run, 0, fin=fin))
            pl.when(sel & more & (j_n == qi_n - 1))(functools.partial(run, 1, fin=fin))
            pl.when(sel & more & far_next & jnp.logical_not(last))(functools.partial(run, None, fin=fin))
        pl.when(more & far_next & last)(functools.partial(run, None, cur_diag=True))
        pl.when(final)(functools.partial(run, None, cur_diag=True, produce=False))
        return qi_n, j_n

    def run_pairs(count, carry):
        qi, j = carry
        for h in range(count):
            qi, j = half(h % 2, qi, j)
        return qi, j

    npairs = nq * (nq + 1) // 2
    carry = lax.fori_loop(0, npairs // PAIRS_PER_BLOCK, lambda _, c: run_pairs(PAIRS_PER_BLOCK, c),
                          (jnp.int32(0), jnp.int32(0)))
    if npairs % PAIRS_PER_BLOCK:
        run_pairs(npairs % PAIRS_PER_BLOCK, carry)
    finalize(nq - 1)


def _diff(lam_p, kb, pt, bias, g_col, lam_init):
    B, S, _ = kb.shape
    nb, tb = pt.shape[1], pt.shape[3]
    tq = tk = bias.shape[-1]
    nq = S // tq
    assert tq % tb == 0 and tq % STRIP == 0 and bias.shape[1:] == (2, tk, tq)
    va_rows = DV_B + 16
    return pl.pallas_call(
        functools.partial(_diff_kernel, lam_init=lam_init),
        grid=(B, H_B),
        in_specs=[
            _resident(lam_p.shape),
            pl.BlockSpec((None, nb, DV_B, tb), lambda b, h: (b, 0, h, 0)),
            pl.BlockSpec((None, S, LANES), lambda b, h: (b, 0, h)),
            pl.BlockSpec((None, nb, DV_B, tb), lambda b, h: (b, 0, H_B + h, 0)),
            pl.BlockSpec((None, 2, tk, tq), lambda b, h: (h, 0, 0, 0)),
            _resident(g_col.shape),
        ],
        out_specs=pl.BlockSpec((None, S, DV_B), lambda b, h: (b, 0, h)),
        out_shape=jax.ShapeDtypeStruct((B, S, W_B_OUT), BF16),
        scratch_shapes=[
            pltpu.VMEM((nq, DV_B, 2 * tq), BF16),
            pltpu.VMEM((S // tk, va_rows, tk), BF16),
            pltpu.VMEM((tk, 2 * tq), F32),
            pltpu.VMEM((tk, 2 * tq), F32),
            pltpu.VMEM((1, 2 * tq), F32),
            pltpu.VMEM((1, 2 * tq), F32),
            pltpu.VMEM((ACC_SLOTS, 1, 2 * tq), F32),
            pltpu.VMEM((ACC_SLOTS, va_rows, 2 * tq), F32),
        ],
        compiler_params=_params("parallel", "parallel"),
        name="diff_attn",
    )(lam_p, pt, kb, pt, bias, g_col)


def _toeplitz_kernel(w_ref, o_ref, *, diff):
    width = o_ref.shape[-1]
    rows = width if diff else o_ref.shape[-2]
    t = pltpu.roll(jnp.broadcast_to(w_ref[...], (rows, width)), 0, 1, stride=1, stride_axis=0)
    if not diff:
        o_ref[...] = t
        return
    r = lax.broadcasted_iota(jnp.int32, (rows, width), 0)
    c = lax.broadcasted_iota(jnp.int32, (rows, width), 1)
    o_ref[0] = jnp.where(c >= r, t, NEG_INF).astype(o_ref.dtype)
    o_ref[1] = jnp.where(c < r, t, 0.0).astype(o_ref.dtype)


def _toeplitz(w, rows, diff):
    n, _, width = w.shape
    oshape = (n, 2, rows, width) if diff else (n, rows, width)
    oblock = (None,) + oshape[1:]
    return pl.pallas_call(
        functools.partial(_toeplitz_kernel, diff=diff),
        grid=(n,),
        in_specs=[pl.BlockSpec((None, 1, width), lambda i: (i, 0, 0))],
        out_specs=pl.BlockSpec(oblock, lambda i: (i,) + (0,) * (len(oshape) - 1)),
        out_shape=jax.ShapeDtypeStruct(oshape, BF16 if diff else F32),
        compiler_params=_params("parallel"),
        name="bias_diff" if diff else "bias_dilated",
    )(w)


def _layer_norm(h, g, b):
    mu = jnp.mean(h, axis=-1, keepdims=True)
    d = h - mu
    var = jnp.mean(d * d, axis=-1, keepdims=True)
    return d * lax.rsqrt(var + LN_EPS) * g + b


def _mix_kernel(x_ref, o0_ref, o1_ref, o2_ref, l0_ref, l1_ref, l2_ref, ob_ref, wg_ref, bg_ref, wa_ref, wb_ref,
                wo_ref, g_ref, b_ref, out_ref, *tok_sc):
    def token_major(ref, sc):
        dil, rows, _ = ref.shape
        if dil == 1:
            return lambda sl: ref[0, sl]
        nk = sc.shape[0]
        for res in range(dil):
            for k in range(nk):
                sc[k, pl.ds(res, rows, stride=dil), :] = ref[res, :, k * LANES:(k + 1) * LANES]
        return lambda sl: jnp.concatenate([sc[k, sl] for k in range(nk)], axis=1)

    l_g = (token_major(l0_ref, None), token_major(l1_ref, tok_sc[0]), token_major(l2_ref, tok_sc[1]))
    o_g = (token_major(o0_ref, None), token_major(o1_ref, tok_sc[2]), token_major(o2_ref, tok_sc[3]))
    tm = x_ref.shape[0]
    for r0 in range(0, tm, ROW_SLAB):
        sl = slice(r0, r0 + ROW_SLAB)
        x = x_ref[sl]
        xb = x.astype(BF16)
        l0, l1, l2 = (f(sl) for f in l_g)
        mx = jnp.maximum(jnp.maximum(l0, l1), l2)
        e0, e1, e2 = jnp.exp(l0 - mx), jnp.exp(l1 - mx), jnp.exp(l2 - mx)
        o_a = (e0 * o_g[0](sl) + e1 * o_g[1](sl) + e2 * o_g[2](sl)) / (e0 + e1 + e2)
        y_a = _dot(o_a.astype(BF16), wa_ref[...])
        y_b = _dot(ob_ref[sl], wb_ref[...])
        gate_a = jax.nn.sigmoid(_dot(xb, wg_ref[:, :D_MODEL]) + bg_ref[:, :D_MODEL])
        gate_b = jax.nn.sigmoid(_dot(xb, wg_ref[:, D_MODEL:]) + bg_ref[:, D_MODEL:])
        merged = gate_a * y_a + gate_b * y_b
        mix = _dot(merged.astype(BF16), wo_ref[...])
        out_ref[sl] = _layer_norm(DEEPNORM_ALPHA * x + mix, g_ref[...], b_ref[...])


def _mix(x, o_g, lse_g, o_b, w_gate, b_gate, w_a, w_b, w_o, ln_g, ln_b, tm):
    B, S, D = x.shape
    tok = lambda w: pl.BlockSpec((None, tm, w), lambda b, i: (b, i, 0))
    res = [pl.BlockSpec((None, a.shape[1], tm // a.shape[1], W_A_OUT), lambda b, i: (b, 0, i, 0)) for a in o_g]
    n_sc = 2 * sum(a.shape[1] > 1 for a in o_g)
    return pl.pallas_call(
        _mix_kernel,
        grid=(B, S // tm),
        in_specs=[tok(D)] + res + res + [tok(W_B_OUT)]
        + [_resident(a.shape) for a in (w_gate, b_gate, w_a, w_b, w_o, ln_g, ln_b)],
        out_specs=tok(D),
        out_shape=jax.ShapeDtypeStruct((B, S, D), F32),
        scratch_shapes=[pltpu.VMEM((W_A_OUT // LANES, tm, LANES), F32)] * n_sc,
        compiler_params=_params("parallel", "parallel"),
        name="mix",
    )(x, *o_g, *lse_g, o_b, w_gate, b_gate, w_a, w_b, w_o, ln_g, ln_b)


def _mlp_kernel(x_ref, w1_ref, w2_ref, g_ref, b_ref, out_ref):
    for r0 in range(0, x_ref.shape[0], ROW_SLAB):
        rows = slice(r0, r0 + ROW_SLAB)
        x = x_ref[rows]
        xb = x.astype(BF16)
        ff = jnp.zeros(x.shape, F32)
        for c in range(D_FF // (2 * CHUNK)):
            sl = slice(c * 2 * CHUNK, (c + 1) * 2 * CHUNK)
            h = jnp.maximum(_dot(xb, w1_ref[:, sl]), 0.0)
            ff = ff + _dot((h * h).astype(BF16), w2_ref[sl, :])
        out_ref[rows] = _layer_norm(DEEPNORM_ALPHA * x + ff, g_ref[...], b_ref[...])


def _mlp(x, w1, w2, ln_g, ln_b, tm):
    B, S, D = x.shape
    tok = pl.BlockSpec((None, tm, D), lambda b, i: (b, i, 0))
    return pl.pallas_call(
        _mlp_kernel,
        grid=(B, S // tm),
        in_specs=[tok] + [_resident(a.shape) for a in (w1, w2, ln_g, ln_b)],
        out_specs=tok,
        out_shape=jax.ShapeDtypeStruct((B, S, D), F32),
        compiler_params=_params("parallel", "parallel"),
        name="mlp",
    )(x, w1, w2, ln_g, ln_b)


def _t5_bucket(dist):
    n = jnp.maximum(dist, 0)
    max_exact = NUM_BUCKETS // 2
    nf = jnp.maximum(n, 1).astype(F32)
    large = max_exact + (jnp.log(nf / max_exact) / math.log(T5_MAX_DISTANCE / max_exact)
                         * (NUM_BUCKETS - max_exact)).astype(jnp.int32)
    large = jnp.minimum(large, NUM_BUCKETS - 1)
    return jnp.where(n < max_exact, n, large)


def _dilated_bias(rel_bias, g, dil):
    j = jnp.arange(2 * BLK)
    tab = rel_bias[:, g * H_A:(g + 1) * H_A].astype(F32)
    vec = tab[_t5_bucket(jnp.maximum(BLK - j, 0) * dil)]
    vec = jnp.where((j <= BLK)[:, None], vec, NEG_INF)
    bias = _toeplitz(vec.T.reshape(H_A, 1, 2 * BLK), BLK, diff=False)
    return bias.reshape(H_A // 2, 2 * BLK, 2 * BLK)


def _diff_bias(rel_bias, tq):
    assert tq >= T5_MAX_DISTANCE
    tab = rel_bias[:, N_GROUPS * H_A:].astype(F32)
    tab = (tab - tab[NUM_BUCKETS - 1:NUM_BUCKETS]) * LOG2E
    vec = tab[_t5_bucket(jnp.arange(tq))]
    return _toeplitz(vec.T.reshape(H_B, 1, tq), tq, diff=True)


def kernel(x, w_in, b_gate, lambda_q1, lambda_k1, lambda_q2, lambda_k2, subln_g, rel_bias, w_proj_a, w_proj_b,
           w_out, ln1_g, ln1_b, ln2_g, ln2_b, w_mlp1, w_mlp2):
    B, S, D = x.shape
    tq = min(TQ, S)
    tm = min(TM, S)
    w = w_in[0]
    w_a = w[:, :COLS_A].reshape(D, 3, N_GROUPS, W_A_OUT)
    w_a = jnp.concatenate([w_a[:, :1] * QK_SCALE, w_a[:, 1:]], axis=1).transpose(0, 2, 1, 3).reshape(D, COLS_A)
    w_qk = w[:, COLS_A:COLS_A + COLS_B_QK].reshape(D, 4, H_B, HEAD_DIM)
    w_qb = jnp.concatenate([w_qk[:, 0], w_qk[:, 1]], axis=-1).reshape(D, W_B_OUT)
    w_kb = jnp.concatenate([w_qk[:, 2], w_qk[:, 3]], axis=-1).reshape(D, W_B_OUT)
    w_vb = w[:, COLS_A + COLS_B_QK:COLS_A + COLS_B]
    w_n = jnp.concatenate([w_a, w_kb], axis=1).astype(BF16)
    w_t = jnp.concatenate([w_qb, w_vb], axis=1).T.astype(BF16)
    w_gate = w[:, COLS_A + COLS_B:].astype(BF16)

    *a_g, kb, pt = _proj(x, w_n, w_t, tm, tq)

    o_g, lse_g = [], []
    for g, (win, dil) in enumerate(DIL_PAIRS):
        assert win // dil == BLK
        o, lse = _dilated(a_g[g], _dilated_bias(rel_bias, g, dil), g)
        o_g.append(o)
        lse_g.append(lse)

    lam_init = 0.8 - 0.6 * math.exp(-0.3 * 0)
    lam_p = jnp.concatenate([lambda_q1, lambda_k1, lambda_q2, lambda_k2], axis=0).astype(F32)
    o_b = _diff(lam_p, kb, pt, _diff_bias(rel_bias, min(DIFF_T, S)), subln_g[0].reshape(DV_B, 1), lam_init)

    x1 = _mix(x, o_g, lse_g, o_b, w_gate, b_gate, w_proj_a[0].astype(BF16), w_proj_b[0].astype(BF16),
              w_out[0].astype(BF16), ln1_g, ln1_b, tm)
    return _mlp(x1, w_mlp1[0].astype(BF16), w_mlp2[0].astype(BF16), ln2_g, ln2_b, tm)
```

```python
import functools
import math

import jax
import jax.numpy as jnp
import numpy as np
from jax import lax
from jax.experimental import pallas as pl
from jax.experimental.pallas import tpu as pltpu

D_MODEL = 1024
HEAD_DIM = 64
DIL_PAIRS = ((128, 1), (512, 4), (2048, 16))
N_GROUPS = len(DIL_PAIRS)
H_A = 8
H_B = 8
DV_B = 2 * HEAD_DIM
D_FF = 4 * D_MODEL
NUM_BUCKETS = 32
T5_MAX_DISTANCE = 128
BLK = 128
LN_EPS = 1e-5
NEG_INF = -1e30
W_A_OUT = H_A * HEAD_DIM
W_B_OUT = H_B * DV_B
COLS_A = 3 * N_GROUPS * H_A * HEAD_DIM
COLS_B_QK = 4 * H_B * HEAD_DIM
COLS_B = COLS_B_QK + H_B * DV_B
DEPTH = 1
DEEPNORM_ALPHA = (2.0 * DEPTH) ** 0.25
QK_SCALE = HEAD_DIM ** -0.5
LOG2E = math.log2(math.e)

LANES = 128
COLS_N = COLS_A + H_B * 2 * HEAD_DIM
COLS_T = 2 * W_B_OUT
CHUNK = 512
TQ = 512
DIFF_T = 1024
STRIP = 256
PAIRS_PER_BLOCK = 2
ACC_SLOTS = PAIRS_PER_BLOCK
TM = 512
ROW_SLAB = 256
DILATED_ROWS = 1024
VMEM_LIMIT = 56 * 1024 * 1024

BF16 = jnp.bfloat16
F32 = jnp.float32


def _dot(a, b):
    return jnp.dot(a, b, preferred_element_type=F32)


def _dot_nt(a, b):
    return lax.dot_general(a, b, (((1,), (1,)), ((), ())), preferred_element_type=F32)


def _resident(shape):
    nd = len(shape)
    return pl.BlockSpec(shape, lambda *_: (0,) * nd, pipeline_mode=pl.Buffered(1))


def _params(*sem, flags=None):
    return pltpu.CompilerParams(dimension_semantics=sem, vmem_limit_bytes=VMEM_LIMIT, flags=flags)


def _proj_kernel(x_ref, wn_ref, wt_ref, a0_ref, a1_ref, a2_ref, kb_ref, ot_ref, rows_sc, *, tq):
    xb = x_ref[...].astype(BF16)
    tm = xb.shape[0]
    a_refs = (a0_ref, a1_ref, a2_ref)
    for c in range(COLS_N // CHUNK):
        r = _dot(xb, wn_ref[:, c * CHUNK:(c + 1) * CHUNK])
        g, part = divmod(c, 3)
        if g >= N_GROUPS:
            kb_ref[:, (c - 3 * N_GROUPS) * CHUNK:(c - 3 * N_GROUPS + 1) * CHUNK] = r.astype(BF16)
            continue
        sl = slice(part * W_A_OUT, (part + 1) * W_A_OUT)
        dil = DIL_PAIRS[g][1]
        if dil == 1:
            a_refs[g][0, :, sl] = r.astype(BF16)
            continue
        for k in range(CHUNK // LANES):
            rows_sc[k] = r[:, k * LANES:(k + 1) * LANES]
        for res in range(dil):
            for k in range(CHUNK // LANES):
                piece = rows_sc[k, pl.ds(res, tm // dil, stride=dil), :]
                a_refs[g][res, :, part * W_A_OUT + k * LANES:part * W_A_OUT + (k + 1) * LANES] = piece.astype(BF16)
    for c in range(COLS_T // CHUNK):
        sl = slice(c * CHUNK, (c + 1) * CHUNK)
        r = _dot_nt(wt_ref[sl, :], xb)
        if (c + 1) * CHUNK <= W_B_OUT:
            r = r * (QK_SCALE * LOG2E)
        r = r.astype(BF16)
        for t in range(tm // tq):
            ot_ref[t, sl, :] = r[:, t * tq:(t + 1) * tq]


def _proj(x, w_n, w_t, tm, tq):
    B, S, D = x.shape
    assert CHUNK == W_A_OUT
    dils = [d for _, d in DIL_PAIRS]
    return pl.pallas_call(
        functools.partial(_proj_kernel, tq=tq),
        grid=(B, S // tm),
        in_specs=[
            pl.BlockSpec((None, tm, D), lambda b, i: (b, i, 0)),
            _resident((D, COLS_N)),
            _resident((COLS_T, D)),
        ],
        out_specs=[pl.BlockSpec((None, d, tm // d, 3 * W_A_OUT), lambda b, i: (b, 0, i, 0)) for d in dils] + [
            pl.BlockSpec((None, tm, W_B_OUT), lambda b, i: (b, i, 0)),
            pl.BlockSpec((None, tm // tq, COLS_T, tq), lambda b, i: (b, i, 0, 0)),
        ],
        out_shape=[jax.ShapeDtypeStruct((B, d, S // d, 3 * W_A_OUT), BF16) for d in dils] + [
            jax.ShapeDtypeStruct((B, S, W_B_OUT), BF16),
            jax.ShapeDtypeStruct((B, S // tq, COLS_T, tq), BF16),
        ],
        scratch_shapes=[pltpu.VMEM((CHUNK // LANES, tm, LANES), F32)],
        compiler_params=_params("parallel", "parallel"),
        name="proj",
    )(x, w_n, w_t)


def _dilated_kernel(q_ref, k_ref, v_ref, kp_ref, vp_ref, bias_ref, o_ref, lse_ref, kw_sc, vw_sc):
    nres, tb, _ = q_ref.shape
    first = pl.program_id(2) == 0
    lane = lax.broadcasted_iota(jnp.int32, (BLK, LANES), 1)
    lo = lane < HEAD_DIM
    in_prev = lax.broadcasted_iota(jnp.int32, (2 * BLK, 2 * BLK), 1) < BLK
    ones = jnp.ones((2 * BLK, LANES), BF16)
    for r in range(nres):
        kw_sc[r, :BLK], kw_sc[r, BLK:] = kp_ref[r], k_ref[r]
        vw_sc[r, :BLK], vw_sc[r, BLK:] = vp_ref[r], v_ref[r]
        for jb in range(tb // BLK):
            rows = slice(jb * BLK, (jb + 1) * BLK)
            win = slice(jb * BLK, (jb + 2) * BLK)
            for p in range(H_A // 2):
                cols = slice(p * LANES, (p + 1) * LANES)
                q2 = q_ref[r, rows, cols]
                zero = jnp.zeros_like(q2)
                qd = jnp.concatenate([jnp.where(lo, q2, zero), jnp.where(lo, zero, q2)], axis=0)
                s = _dot_nt(qd, kw_sc[r, win, cols]) + bias_ref[p]
                if jb == 0:
                    s = jnp.where(first & in_prev, NEG_INF, s)
                m = jnp.max(s, axis=-1, keepdims=True)
                pw = jnp.exp((s - m).astype(BF16))
                oa = _dot(pw, jnp.concatenate([vw_sc[r, win, cols], ones], axis=1))
                den = oa[:, LANES:]
                o2 = oa[:, :LANES] / den
                lse = m + jnp.log(den)
                o_ref[r, rows, cols] = jnp.where(lo, o2[:BLK], o2[BLK:])
                lse_ref[r, rows, cols] = jnp.where(lo, lse[:BLK], lse[BLK:])


def _dilated(a, bias, g):
    B, dil, L, _ = a.shape
    tb = min(DILATED_ROWS, L)
    nres = min(DILATED_ROWS // tb, dil)
    cur = lambda c: pl.BlockSpec((None, nres, tb, W_A_OUT), lambda b, r, n: (b, r, n, c))
    prev = lambda c: pl.BlockSpec(
        (None, nres, BLK, W_A_OUT), lambda b, r, n: (b, r, jnp.maximum(n * (tb // BLK) - 1, 0), c))
    out = pl.BlockSpec((None, nres, tb, W_A_OUT), lambda b, r, n: (b, r, n, 0))
    return pl.pallas_call(
        _dilated_kernel,
        grid=(B, dil // nres, L // tb),
        in_specs=[cur(0), cur(1), cur(2), prev(1), prev(2), _resident(bias.shape)],
        out_specs=[out, out],
        out_shape=[jax.ShapeDtypeStruct((B, dil, L, W_A_OUT), F32)] * 2,
        scratch_shapes=[pltpu.VMEM((nres, BLK + tb, W_A_OUT), BF16)] * 2,
        compiler_params=_params("parallel", "parallel", "arbitrary"),
        name=f"dilated{g}",
    )(a, a, a, a, a, bias)


def _diff_kernel(lam_ref, qt_ref, k_ref, vt_ref, bias_ref, g_ref, o_ref, qd_sc, va_sc, s0_sc, s1_sc, x0_sc, x1_sc,
                 m_sc, acc_sc, *, lam_init):
    nb, _, tb = qt_ref.shape
    tq = tk = bias_ref.shape[-1]
    qb = tq // tb
    nq = nb // qb
    s_bufs, x_bufs = (s0_sc, s1_sc), (x0_sc, x1_sc)
    row = lax.broadcasted_iota(jnp.int32, (DV_B, tb), 0)
    for i in range(nb):
        blk, part = divmod(i, qb)
        qt = qt_ref[i]
        zero = jnp.zeros_like(qt)
        qd_sc[blk, :, part * tb:(part + 1) * tb] = jnp.where(row < HEAD_DIM, qt, zero)
        qd_sc[blk, :, tq + part * tb:tq + (part + 1) * tb] = jnp.where(row < HEAD_DIM, zero, qt)
        va_sc[blk, :DV_B, part * tb:(part + 1) * tb] = vt_ref[i]
    for i in range(nq):
        va_sc[i, DV_B:, :] = jnp.ones((va_sc.shape[1] - DV_B, tk), BF16)

    lam = (jnp.exp(jnp.sum(lam_ref[0:1, :] * lam_ref[1:2, :], axis=-1, keepdims=True))
           - jnp.exp(jnp.sum(lam_ref[2:3, :] * lam_ref[3:4, :], axis=-1, keepdims=True)) + lam_init)

    def reset(a):
        m_sc[a] = jnp.full(m_sc.shape[1:], -jnp.inf, F32)
        acc_sc[a] = jnp.zeros(acc_sc.shape[1:], F32)

    def finalize(qi):
        a = qi % ACC_SLOTS
        acc = acc_sc[a]
        on = acc[:DV_B] / acc[DV_B:DV_B + 1]
        o = on[:, :tq] - lam * on[:, tq:]
        ms = jnp.mean(o * o, axis=0, keepdims=True)
        o = o * lax.rsqrt(ms + LN_EPS) * g_ref[...] * (1.0 - lam_init)
        o_ref[pl.ds(pl.multiple_of(qi * tq, tq), tq), :] = o.T.astype(BF16)
        reset(a)

    def is_last(qi, j):
        return j == qi

    strips = [slice(c * STRIP, (c + 1) * STRIP) for c in range(2 * tq // STRIP)]

    def diag_rows(cs):
        return min(tk, cs.start % tq + cs.stop - cs.start)

    def score_strip(s_buf, x_buf, k0, qi, bi, cs):
        q0, width = cs.start % tq, cs.stop - cs.start
        rows = diag_rows(cs) if bi == 0 else tk
        s = _dot(k_ref[pl.ds(k0, rows), :], qd_sc[qi, :, cs])
        if bi == 0:
            s = s + bias_ref[0, :rows, q0:q0 + width].astype(F32)
        elif bi == 1 and q0 < T5_MAX_DISTANCE:
            far = tk - T5_MAX_DISTANCE
            s = jnp.concatenate([s[:far], s[far:] + bias_ref[1, far:, q0:q0 + width].astype(F32)], axis=0)
        s_buf[:rows, cs] = s
        if rows < tk:
            s_buf[rows:, cs] = jnp.full((tk - rows, width), NEG_INF, F32)
        x_buf[:, cs] = jnp.max(s, axis=0, keepdims=True)

    for a in range(ACC_SLOTS):
        reset(a)
    for cs in strips:
        score_strip(s_bufs[0], x_bufs[0], 0, 0, 0, cs)

    def half(slot, qi, j):
        s_cur, x_cur, s_oth, x_oth = s_bufs[slot], x_bufs[slot], s_bufs[1 - slot], x_bufs[1 - slot]
        a = qi % ACC_SLOTS
        last = is_last(qi, j)
        j_n = jnp.where(last, 0, j + 1)
        qi_n = jnp.minimum(jnp.where(last, qi + 1, qi), nq - 1)
        k0 = pl.multiple_of(j_n * tk, tk)
        va = va_sc[j]

        def run(bi, cur_diag=False, produce=True, fin=False):
            if fin:
                finalize(qi - 1)
            for cs in strips:
                if produce:
                    score_strip(s_oth, x_oth, k0, qi_n, bi, cs)
                rows = diag_rows(cs) if cur_diag else tk
                m_old = m_sc[a, :, cs]
                m_new = jnp.maximum(m_old, x_cur[:, cs])
                p = jnp.exp2((s_cur[:rows, cs] - m_new).astype(BF16))
                alpha = jnp.exp2(m_old - m_new)
                m_sc[a, :, cs] = m_new
                acc_sc[a, :, cs] = acc_sc[a, :, cs] * alpha + _dot(va[:, :rows], p)

        final = last & (qi == nq - 1)
        more = jnp.logical_not(final)
        first = (j == 0) & (qi > 0)
        far_next = j_n < qi_n - 1
        for fin, sel in ((False, jnp.logical_not(first)), (True, first)):
            pl.when(sel & more & (j_n == qi_n))(functools.partial(run, 0, fin=fin))
            pl.when(sel & more & (j_n == qi_n - 1))(functools.partial(run, 1, fin=fin))
            pl.when(sel & more & far_next & jnp.logical_not(last))(functools.partial(run, None, fin=fin))
        pl.when(more & far_next & last)(functools.partial(run, None, cur_diag=True))
        pl.when(final)(functools.partial(run, None, cur_diag=True, produce=False))
        return qi_n, j_n

    def run_pairs(count, carry):
        qi, j = carry
        for h in range(count):
            qi, j = half(h % 2, qi, j)
        return qi, j

    npairs = nq * (nq + 1) // 2
    carry = lax.fori_loop(0, npairs // PAIRS_PER_BLOCK, lambda _, c: run_pairs(PAIRS_PER_BLOCK, c),
                          (jnp.int32(0), jnp.int32(0)))
    if npairs % PAIRS_PER_BLOCK:
        run_pairs(npairs % PAIRS_PER_BLOCK, carry)
    finalize(nq - 1)


def _diff(lam_p, kb, pt, bias, g_col, lam_init):
    B, S, _ = kb.shape
    nb, tb = pt.shape[1], pt.shape[3]
    tq = tk = bias.shape[-1]
    nq = S // tq
    assert tq % tb == 0 and tq % STRIP == 0 and bias.shape[1:] == (2, tk, tq)
    va_rows = DV_B + 16
    return pl.pallas_call(
        functools.partial(_diff_kernel, lam_init=lam_init),
        grid=(B, H_B),
        in_specs=[
            _resident(lam_p.shape),
            pl.BlockSpec((None, nb, DV_B, tb), lambda b, h: (b, 0, h, 0)),
            pl.BlockSpec((None, S, LANES), lambda b, h: (b, 0, h)),
            pl.BlockSpec((None, nb, DV_B, tb), lambda b, h: (b, 0, H_B + h, 0)),
            pl.BlockSpec((None, 2, tk, tq), lambda b, h: (h, 0, 0, 0)),
            _resident(g_col.shape),
        ],
        out_specs=pl.BlockSpec((None, S, DV_B), lambda b, h: (b, 0, h)),
        out_shape=jax.ShapeDtypeStruct((B, S, W_B_OUT), BF16),
        scratch_shapes=[
            pltpu.VMEM((nq, DV_B, 2 * tq), BF16),
            pltpu.VMEM((S // tk, va_rows, tk), BF16),
            pltpu.VMEM((tk, 2 * tq), F32),
            pltpu.VMEM((tk, 2 * tq), F32),
            pltpu.VMEM((1, 2 * tq), F32),
            pltpu.VMEM((1, 2 * tq), F32),
            pltpu.VMEM((ACC_SLOTS, 1, 2 * tq), F32),
            pltpu.VMEM((ACC_SLOTS, va_rows, 2 * tq), F32),
        ],
        compiler_params=_params("parallel", "parallel"),
        name="diff_attn",
    )(lam_p, pt, kb, pt, bias, g_col)


def _toeplitz_kernel(w_ref, o_ref, *, diff):
    width = o_ref.shape[-1]
    rows = width if diff else o_ref.shape[-2]
    t = pltpu.roll(jnp.broadcast_to(w_ref[...], (rows, width)), 0, 1, stride=1, stride_axis=0)
    if not diff:
        o_ref[...] = t
        return
    r = lax.broadcasted_iota(jnp.int32, (rows, width), 0)
    c = lax.broadcasted_iota(jnp.int32, (rows, width), 1)
    o_ref[0] = jnp.where(c >= r, t, NEG_INF).astype(o_ref.dtype)
    o_ref[1] = jnp.where(c < r, t, 0.0).astype(o_ref.dtype)


def _toeplitz(w, rows, diff):
    n, _, width = w.shape
    oshape = (n, 2, rows, width) if diff else (n, rows, width)
    oblock = (None,) + oshape[1:]
    return pl.pallas_call(
        functools.partial(_toeplitz_kernel, diff=diff),
        grid=(n,),
        in_specs=[pl.BlockSpec((None, 1, width), lambda i: (i, 0, 0))],
        out_specs=pl.BlockSpec(oblock, lambda i: (i,) + (0,) * (len(oshape) - 1)),
        out_shape=jax.ShapeDtypeStruct(oshape, BF16 if diff else F32),
        compiler_params=_params("parallel"),
        name="bias_diff" if diff else "bias_dilated",
    )(w)


def _layer_norm(h, g, b):
    mu = jnp.mean(h, axis=-1, keepdims=True)
    d = h - mu
    var = jnp.mean(d * d, axis=-1, keepdims=True)
    return d * lax.rsqrt(var + LN_EPS) * g + b


def _mix_kernel(x_ref, o0_ref, o1_ref, o2_ref, l0_ref, l1_ref, l2_ref, ob_ref, wg_ref, bg_ref, wa_ref, wb_ref,
                wo_ref, g_ref, b_ref, out_ref, *tok_sc):
    def token_major(ref, sc):
        dil, rows, _ = ref.shape
        if dil == 1:
            return lambda sl: ref[0, sl]
        nk = sc.shape[0]
        for res in range(dil):
            for k in range(nk):
                sc[k, pl.ds(res, rows, stride=dil), :] = ref[res, :, k * LANES:(k + 1) * LANES]
        return lambda sl: jnp.concatenate([sc[k, sl] for k in range(nk)], axis=1)

    l_g = (token_major(l0_ref, None), token_major(l1_ref, tok_sc[0]), token_major(l2_ref, tok_sc[1]))
    o_g = (token_major(o0_ref, None), token_major(o1_ref, tok_sc[2]), token_major(o2_ref, tok_sc[3]))
    tm = x_ref.shape[0]
    for r0 in range(0, tm, ROW_SLAB):
        sl = slice(r0, r0 + ROW_SLAB)
        x = x_ref[sl]
        xb = x.astype(BF16)
        l0, l1, l2 = (f(sl) for f in l_g)
        mx = jnp.maximum(jnp.maximum(l0, l1), l2)
        e0, e1, e2 = jnp.exp(l0 - mx), jnp.exp(l1 - mx), jnp.exp(l2 - mx)
        o_a = (e0 * o_g[0](sl) + e1 * o_g[1](sl) + e2 * o_g[2](sl)) / (e0 + e1 + e2)
        y_a = _dot(o_a.astype(BF16), wa_ref[...])
        y_b = _dot(ob_ref[sl], wb_ref[...])
        gate_a = jax.nn.sigmoid(_dot(xb, wg_ref[:, :D_MODEL]) + bg_ref[:, :D_MODEL])
        gate_b = jax.nn.sigmoid(_dot(xb, wg_ref[:, D_MODEL:]) + bg_ref[:, D_MODEL:])
        merged = gate_a * y_a + gate_b * y_b
        mix = _dot(merged.astype(BF16), wo_ref[...])
        out_ref[sl] = _layer_norm(DEEPNORM_ALPHA * x + mix, g_ref[...], b_ref[...])


def _mix(x, o_g, lse_g, o_b, w_gate, b_gate, w_a, w_b, w_o, ln_g, ln_b, tm):
    B, S, D = x.shape
    tok = lambda w: pl.BlockSpec((None, tm, w), lambda b, i: (b, i, 0))
    res = [pl.BlockSpec((None, a.shape[1], tm // a.shape[1], W_A_OUT), lambda b, i: (b, 0, i, 0)) for a in o_g]
    n_sc = 2 * sum(a.shape[1] > 1 for a in o_g)
    return pl.pallas_call(
        _mix_kernel,
        grid=(B, S // tm),
        in_specs=[tok(D)] + res + res + [tok(W_B_OUT)]
        + [_resident(a.shape) for a in (w_gate, b_gate, w_a, w_b, w_o, ln_g, ln_b)],
        out_specs=tok(D),
        out_shape=jax.ShapeDtypeStruct((B, S, D), F32),
        scratch_shapes=[pltpu.VMEM((W_A_OUT // LANES, tm, LANES), F32)] * n_sc,
        compiler_params=_params("parallel", "parallel"),
        name="mix",
    )(x, *o_g, *lse_g, o_b, w_gate, b_gate, w_a, w_b, w_o, ln_g, ln_b)


def _mlp_kernel(x_ref, w1_ref, w2_ref, g_ref, b_ref, out_ref):
    for r0 in range(0, x_ref.shape[0], ROW_SLAB):
        rows = slice(r0, r0 + ROW_SLAB)
        x = x_ref[rows]
        xb = x.astype(BF16)
        ff = jnp.zeros(x.shape, F32)
        for c in range(D_FF // (2 * CHUNK)):
            sl = slice(c * 2 * CHUNK, (c + 1) * 2 * CHUNK)
            h = jnp.maximum(_dot(xb, w1_ref[:, sl]), 0.0)
            ff = ff + _dot((h * h).astype(BF16), w2_ref[sl, :])
        out_ref[rows] = _layer_norm(DEEPNORM_ALPHA * x + ff, g_ref[...], b_ref[...])


def _mlp(x, w1, w2, ln_g, ln_b, tm):
    B, S, D = x.shape
    tok = pl.BlockSpec((None, tm, D), lambda b, i: (b, i, 0))
    return pl.pallas_call(
        _mlp_kernel,
        grid=(B, S // tm),
        in_specs=[tok] + [_resident(a.shape) for a in (w1, w2, ln_g, ln_b)],
        out_specs=tok,
        out_shape=jax.ShapeDtypeStruct((B, S, D), F32),
        compiler_params=_params("parallel", "parallel"),
        name="mlp",
    )(x, w1, w2, ln_g, ln_b)


def _t5_bucket(dist):
    n = jnp.maximum(dist, 0)
    max_exact = NUM_BUCKETS // 2
    nf = jnp.maximum(n, 1).astype(F32)
    large = max_exact + (jnp.log(nf / max_exact) / math.log(T5_MAX_DISTANCE / max_exact)
                         * (NUM_BUCKETS - max_exact)).astype(jnp.int32)
    large = jnp.minimum(large, NUM_BUCKETS - 1)
    return jnp.where(n < max_exact, n, large)


def _dilated_bias(rel_bias, g, dil):
    j = jnp.arange(2 * BLK)
    tab = rel_bias[:, g * H_A:(g + 1) * H_A].astype(F32)
    vec = tab[_t5_bucket(jnp.maximum(BLK - j, 0) * dil)]
    vec = jnp.where((j <= BLK)[:, None], vec, NEG_INF)
    bias = _toeplitz(vec.T.reshape(H_A, 1, 2 * BLK), BLK, diff=False)
    return bias.reshape(H_A // 2, 2 * BLK, 2 * BLK)


def _diff_bias(rel_bias, tq):
    assert tq >= T5_MAX_DISTANCE
    tab = rel_bias[:, N_GROUPS * H_A:].astype(F32)
    tab = (tab - tab[NUM_BUCKETS - 1:NUM_BUCKETS]) * LOG2E
    vec = tab[_t5_bucket(jnp.arange(tq))]
    return _toeplitz(vec.T.reshape(H_B, 1, tq), tq, diff=True)


def kernel(x, w_in, b_gate, lambda_q1, lambda_k1, lambda_q2, lambda_k2, subln_g, rel_bias, w_proj_a, w_proj_b,
           w_out, ln1_g, ln1_b, ln2_g, ln2_b, w_mlp1, w_mlp2):
    B, S, D = x.shape
    tq = min(TQ, S)
    tm = min(TM, S)
    w = w_in[0]
    w_a = w[:, :COLS_A].reshape(D, 3, N_GROUPS, W_A_OUT)
    w_a = jnp.concatenate([w_a[:, :1] * QK_SCALE, w_a[:, 1:]], axis=1).transpose(0, 2, 1, 3).reshape(D, COLS_A)
    w_qk = w[:, COLS_A:COLS_A + COLS_B_QK].reshape(D, 4, H_B, HEAD_DIM)
    w_qb = jnp.concatenate([w_qk[:, 0], w_qk[:, 1]], axis=-1).reshape(D, W_B_OUT)
    w_kb = jnp.concatenate([w_qk[:, 2], w_qk[:, 3]], axis=-1).reshape(D, W_B_OUT)
    w_vb = w[:, COLS_A + COLS_B_QK:COLS_A + COLS_B]
    w_n = jnp.concatenate([w_a, w_kb], axis=1).astype(BF16)
    w_t = jnp.concatenate([w_qb, w_vb], axis=1).T.astype(BF16)
    w_gate = w[:, COLS_A + COLS_B:].astype(BF16)

    *a_g, kb, pt = _proj(x, w_n, w_t, tm, tq)

    o_g, lse_g = [], []
    for g, (win, dil) in enumerate(DIL_PAIRS):
        assert win // dil == BLK
        o, lse = _dilated(a_g[g], _dilated_bias(rel_bias, g, dil), g)
        o_g.append(o)
        lse_g.append(lse)

    lam_init = 0.8 - 0.6 * math.exp(-0.3 * 0)
    lam_p = jnp.concatenate([lambda_q1, lambda_k1, lambda_q2, lambda_k2], axis=0).astype(F32)
    o_b = _diff(lam_p, kb, pt, _diff_bias(rel_bias, min(DIFF_T, S)), subln_g[0].reshape(DV_B, 1), lam_init)

    x1 = _mix(x, o_g, lse_g, o_b, w_gate, b_gate, w_proj_a[0].astype(BF16), w_proj_b[0].astype(BF16),
              w_out[0].astype(BF16), ln1_g, ln1_b, tm)
    return _mlp(x1, w_mlp1[0].astype(BF16), w_mlp2[0].astype(BF16), ln2_g, ln2_b, tm)
```

```python
import functools
import math

import jax
import jax.numpy as jnp
import numpy as np
from jax import lax
from jax.experimental import pallas as pl
from jax.experimental.pallas import tpu as pltpu

D_MODEL = 1024
HEAD_DIM = 64
DIL_PAIRS = ((128, 1), (512, 4), (2048, 16))
N_GROUPS = len(DIL_PAIRS)
H_A = 8
H_B = 8
DV_B = 2 * HEAD_DIM
D_FF = 4 * D_MODEL
NUM_BUCKETS = 32
T5_MAX_DISTANCE = 128
BLK = 128
LN_EPS = 1e-5
NEG_INF = -1e30
W_A_OUT = H_A * HEAD_DIM
W_B_OUT = H_B * DV_B
COLS_A = 3 * N_GROUPS * H_A * HEAD_DIM
COLS_B_QK = 4 * H_B * HEAD_DIM
COLS_B = COLS_B_QK + H_B * DV_B
DEPTH = 1
DEEPNORM_ALPHA = (2.0 * DEPTH) ** 0.25
QK_SCALE = HEAD_DIM ** -0.5
LOG2E = math.log2(math.e)

LANES = 128
COLS_N = COLS_A + H_B * 2 * HEAD_DIM
COLS_T = 2 * W_B_OUT
CHUNK = 512
TQ = 512
DIFF_T = 1024
STRIP = 256
PAIRS_PER_BLOCK = 2
ACC_SLOTS = PAIRS_PER_BLOCK
TM = 512
ROW_SLAB = 256
DILATED_ROWS = 2048
VMEM_LIMIT = 56 * 1024 * 1024

BF16 = jnp.bfloat16
F32 = jnp.float32


def _dot(a, b):
    return jnp.dot(a, b, preferred_element_type=F32)


def _dot_nt(a, b):
    return lax.dot_general(a, b, (((1,), (1,)), ((), ())), preferred_element_type=F32)


def _resident(shape):
    nd = len(shape)
    return pl.BlockSpec(shape, lambda *_: (0,) * nd, pipeline_mode=pl.Buffered(1))


def _params(*sem, flags=None):
    return pltpu.CompilerParams(dimension_semantics=sem, vmem_limit_bytes=VMEM_LIMIT, flags=flags)


def _proj_kernel(x_ref, wn_ref, wt_ref, a0_ref, a1_ref, a2_ref, kb_ref, ot_ref, rows_sc, *, tq):
    tm = x_ref.shape[0]
    a_refs = (a0_ref, a1_ref, a2_ref)
    for r0 in range(0, tm, ROW_SLAB):
        xb = x_ref[r0:r0 + ROW_SLAB].astype(BF16)
        for c in range(COLS_N // CHUNK):
            r = _dot(xb, wn_ref[:, c * CHUNK:(c + 1) * CHUNK])
            g, part = divmod(c, 3)
            if g >= N_GROUPS:
                kb_ref[r0:r0 + ROW_SLAB, (c - 3 * N_GROUPS) * CHUNK:(c - 3 * N_GROUPS + 1) * CHUNK] = r.astype(BF16)
                continue
            sl = slice(part * W_A_OUT, (part + 1) * W_A_OUT)
            dil = DIL_PAIRS[g][1]
            if dil == 1:
                a_refs[g][0, r0:r0 + ROW_SLAB, sl] = r.astype(BF16)
                continue
            for k in range(CHUNK // LANES):
                rows_sc[k] = r[:, k * LANES:(k + 1) * LANES]
            out_rows = slice(r0 // dil, (r0 + ROW_SLAB) // dil)
            for res in range(dil):
                for k in range(CHUNK // LANES):
                    piece = rows_sc[k, pl.ds(res, ROW_SLAB // dil, stride=dil), :]
                    cols = slice(part * W_A_OUT + k * LANES, part * W_A_OUT + (k + 1) * LANES)
                    a_refs[g][res, out_rows, cols] = piece.astype(BF16)
        for c in range(COLS_T // CHUNK):
            sl = slice(c * CHUNK, (c + 1) * CHUNK)
            r = _dot_nt(wt_ref[sl, :], xb)
            if (c + 1) * CHUNK <= W_B_OUT:
                r = r * (QK_SCALE * LOG2E)
            t, lane0 = divmod(r0, tq)
            ot_ref[t, sl, lane0:lane0 + ROW_SLAB] = r.astype(BF16)


def _proj(x, w_n, w_t, tm, tq):
    B, S, D = x.shape
    assert CHUNK == W_A_OUT
    dils = [d for _, d in DIL_PAIRS]
    return pl.pallas_call(
        functools.partial(_proj_kernel, tq=tq),
        grid=(B, S // tm),
        in_specs=[
            pl.BlockSpec((None, tm, D), lambda b, i: (b, i, 0)),
            _resident((D, COLS_N)),
            _resident((COLS_T, D)),
        ],
        out_specs=[pl.BlockSpec((None, d, tm // d, 3 * W_A_OUT), lambda b, i: (b, 0, i, 0)) for d in dils] + [
            pl.BlockSpec((None, tm, W_B_OUT), lambda b, i: (b, i, 0)),
            pl.BlockSpec((None, tm // tq, COLS_T, tq), lambda b, i: (b, i, 0, 0)),
        ],
        out_shape=[jax.ShapeDtypeStruct((B, d, S // d, 3 * W_A_OUT), BF16) for d in dils] + [
            jax.ShapeDtypeStruct((B, S, W_B_OUT), BF16),
            jax.ShapeDtypeStruct((B, S // tq, COLS_T, tq), BF16),
        ],
        scratch_shapes=[pltpu.VMEM((CHUNK // LANES, ROW_SLAB, LANES), F32)],
        compiler_params=_params("parallel", "parallel"),
        name="proj",
    )(x, w_n, w_t)


def _dilated_kernel(q_ref, k_ref, v_ref, kp_ref, vp_ref, bias_ref, o_ref, lse_ref, kw_sc, vw_sc):
    nres, tb, _ = q_ref.shape
    first = pl.program_id(2) == 0
    lane = lax.broadcasted_iota(jnp.int32, (BLK, LANES), 1)
    lo = lane < HEAD_DIM
    in_prev = lax.broadcasted_iota(jnp.int32, (2 * BLK, 2 * BLK), 1) < BLK
    ones = jnp.ones((2 * BLK, LANES), BF16)
    for r in range(nres):
        kw_sc[r, :BLK], kw_sc[r, BLK:] = kp_ref[r], k_ref[r]
        vw_sc[r, :BLK], vw_sc[r, BLK:] = vp_ref[r], v_ref[r]
        for jb in range(tb // BLK):
            rows = slice(jb * BLK, (jb + 1) * BLK)
            win = slice(jb * BLK, (jb + 2) * BLK)
            for p in range(H_A // 2):
                cols = slice(p * LANES, (p + 1) * LANES)
                q2 = q_ref[r, rows, cols]
                zero = jnp.zeros_like(q2)
                qd = jnp.concatenate([jnp.where(lo, q2, zero), jnp.where(lo, zero, q2)], axis=0)
                s = _dot_nt(qd, kw_sc[r, win, cols]) + bias_ref[p]
                if jb == 0:
                    s = jnp.where(first & in_prev, NEG_INF, s)
                m = jnp.max(s, axis=-1, keepdims=True)
                pw = jnp.exp((s - m).astype(BF16))
                oa = _dot(pw, jnp.concatenate([vw_sc[r, win, cols], ones], axis=1))
                den = oa[:, LANES:]
                o2 = oa[:, :LANES] / den
                lse = m + jnp.log(den)
                o_ref[r, rows, cols] = jnp.where(lo, o2[:BLK], o2[BLK:])
                lse_ref[r, rows, cols] = jnp.where(lo, lse[:BLK], lse[BLK:])


def _dilated(a, bias, g):
    B, dil, L, _ = a.shape
    tb = min(DILATED_ROWS, L)
    nres = min(DILATED_ROWS // tb, dil)
    cur = lambda c: pl.BlockSpec((None, nres, tb, W_A_OUT), lambda b, r, n: (b, r, n, c))
    prev = lambda c: pl.BlockSpec(
        (None, nres, BLK, W_A_OUT), lambda b, r, n: (b, r, jnp.maximum(n * (tb // BLK) - 1, 0), c))
    out = pl.BlockSpec((None, nres, tb, W_A_OUT), lambda b, r, n: (b, r, n, 0))
    return pl.pallas_call(
        _dilated_kernel,
        grid=(B, dil // nres, L // tb),
        in_specs=[cur(0), cur(1), cur(2), prev(1), prev(2), _resident(bias.shape)],
        out_specs=[out, out],
        out_shape=[jax.ShapeDtypeStruct((B, dil, L, W_A_OUT), F32)] * 2,
        scratch_shapes=[pltpu.VMEM((nres, BLK + tb, W_A_OUT), BF16)] * 2,
        compiler_params=_params("parallel", "parallel", "arbitrary"),
        name=f"dilated{g}",
    )(a, a, a, a, a, bias)


def _diff_kernel(lam_ref, qt_ref, k_ref, vt_ref, bias_ref, g_ref, o_ref, qd_sc, va_sc, s0_sc, s1_sc, x0_sc, x1_sc,
                 m_sc, acc_sc, *, lam_init):
    nb, _, tb = qt_ref.shape
    tq = tk = bias_ref.shape[-1]
    qb = tq // tb
    nq = nb // qb
    s_bufs, x_bufs = (s0_sc, s1_sc), (x0_sc, x1_sc)
    row = lax.broadcasted_iota(jnp.int32, (DV_B, tb), 0)
    for i in range(nb):
        blk, part = divmod(i, qb)
        qt = qt_ref[i]
        zero = jnp.zeros_like(qt)
        qd_sc[blk, :, part * tb:(part + 1) * tb] = jnp.where(row < HEAD_DIM, qt, zero)
        qd_sc[blk, :, tq + part * tb:tq + (part + 1) * tb] = jnp.where(row < HEAD_DIM, zero, qt)
        va_sc[blk, :DV_B, part * tb:(part + 1) * tb] = vt_ref[i]
    for i in range(nq):
        va_sc[i, DV_B:, :] = jnp.ones((va_sc.shape[1] - DV_B, tk), BF16)

    lam = (jnp.exp(jnp.sum(lam_ref[0:1, :] * lam_ref[1:2, :], axis=-1, keepdims=True))
           - jnp.exp(jnp.sum(lam_ref[2:3, :] * lam_ref[3:4, :], axis=-1, keepdims=True)) + lam_init)

    def reset(a):
        m_sc[a] = jnp.full(m_sc.shape[1:], -jnp.inf, F32)
        acc_sc[a] = jnp.zeros(acc_sc.shape[1:], F32)

    def finalize(qi):
        a = qi % ACC_SLOTS
        acc = acc_sc[a]
        on = acc[:DV_B] / acc[DV_B:DV_B + 1]
        o = on[:, :tq] - lam * on[:, tq:]
        ms = jnp.mean(o * o, axis=0, keepdims=True)
        o = o * lax.rsqrt(ms + LN_EPS) * g_ref[...] * (1.0 - lam_init)
        o_ref[pl.ds(pl.multiple_of(qi * tq, tq), tq), :] = o.T.astype(BF16)
        reset(a)

    def is_last(qi, j):
        return j == qi

    strips = [slice(c * STRIP, (c + 1) * STRIP) for c in range(2 * tq // STRIP)]

    def diag_rows(cs):
        return min(tk, cs.start % tq + cs.stop - cs.start)

    def score_strip(s_buf, x_buf, k0, qi, bi, cs):
        q0, width = cs.start % tq, cs.stop - cs.start
        rows = diag_rows(cs) if bi == 0 else tk
        s = _dot(k_ref[pl.ds(k0, rows), :], qd_sc[qi, :, cs])
        if bi == 0:
            s = s + bias_ref[0, :rows, q0:q0 + width].astype(F32)
        elif bi == 1 and q0 < T5_MAX_DISTANCE:
            far = tk - T5_MAX_DISTANCE
            s = jnp.concatenate([s[:far], s[far:] + bias_ref[1, far:, q0:q0 + width].astype(F32)], axis=0)
        s_buf[:rows, cs] = s
        if rows < tk:
            s_buf[rows:, cs] = jnp.full((tk - rows, width), NEG_INF, F32)
        x_buf[:, cs] = jnp.max(s, axis=0, keepdims=True)

    for a in range(ACC_SLOTS):
        reset(a)
    for cs in strips:
        score_strip(s_bufs[0], x_bufs[0], 0, 0, 0, cs)

    def half(slot, qi, j):
        s_cur, x_cur, s_oth, x_oth = s_bufs[slot], x_bufs[slot], s_bufs[1 - slot], x_bufs[1 - slot]
        a = qi % ACC_SLOTS
        last = is_last(qi, j)
        j_n = jnp.where(last, 0, j + 1)
        qi_n = jnp.minimum(jnp.where(last, qi + 1, qi), nq - 1)
        k0 = pl.multiple_of(j_n * tk, tk)
        va = va_sc[j]

        def run(bi, cur_diag=False, produce=True, fin=False):
            if fin:
                finalize(qi - 1)
            for cs in strips:
                if produce:
                    score_strip(s_oth, x_oth, k0, qi_n, bi, cs)
                rows = diag_rows(cs) if cur_diag else tk
                m_old = m_sc[a, :, cs]
                m_new = jnp.maximum(m_old, x_cur[:, cs])
                p = jnp.exp2((s_cur[:rows, cs] - m_new).astype(BF16))
                alpha = jnp.exp2(m_old - m_new)
                m_sc[a, :, cs] = m_new
                acc_sc[a, :, cs] = acc_sc[a, :, cs] * alpha + _dot(va[:, :rows], p)

        final = last & (qi == nq - 1)
        more = jnp.logical_not(final)
        first = (j == 0) & (qi > 0)
        far_next = j_n < qi_n - 1
        for fin, sel in ((False, jnp.logical_not(first)), (True, first)):
            pl.when(sel & more & (j_n == qi_n))(functools.partial(run, 0, fin=fin))
            pl.when(sel & more & (j_n == qi_n - 1))(functools.partial(run, 1, fin=fin))
            pl.when(sel & more & far_next & jnp.logical_not(last))(functools.partial(run, None, fin=fin))
        pl.when(more & far_next & last)(functools.partial(run, None, cur_diag=True))
        pl.when(final)(functools.partial(run, None, cur_diag=True, produce=False))
        return qi_n, j_n

    def run_pairs(count, carry):
        qi, j = carry
        for h in range(count):
            qi, j = half(h % 2, qi, j)
        return qi, j

    npairs = nq * (nq + 1) // 2
    carry = lax.fori_loop(0, npairs // PAIRS_PER_BLOCK, lambda _, c: run_pairs(PAIRS_PER_BLOCK, c),
                          (jnp.int32(0), jnp.int32(0)))
    if npairs % PAIRS_PER_BLOCK:
        run_pairs(npairs % PAIRS_PER_BLOCK, carry)
    finalize(nq - 1)


def _diff(lam_p, kb, pt, bias, g_col, lam_init):
    B, S, _ = kb.shape
    nb, tb = pt.shape[1], pt.shape[3]
    tq = tk = bias.shape[-1]
    nq = S // tq
    assert tq % tb == 0 and tq % STRIP == 0 and bias.shape[1:] == (2, tk, tq)
    va_rows = DV_B + 16
    return pl.pallas_call(
        functools.partial(_diff_kernel, lam_init=lam_init),
        grid=(B, H_B),
        in_specs=[
            _resident(lam_p.shape),
            pl.BlockSpec((None, nb, DV_B, tb), lambda b, h: (b, 0, h, 0)),
            pl.BlockSpec((None, S, LANES), lambda b, h: (b, 0, h)),
            pl.BlockSpec((None, nb, DV_B, tb), lambda b, h: (b, 0, H_B + h, 0)),
            pl.BlockSpec((None, 2, tk, tq), lambda b, h: (h, 0, 0, 0)),
            _resident(g_col.shape),
        ],
        out_specs=pl.BlockSpec((None, S, DV_B), lambda b, h: (b, 0, h)),
        out_shape=jax.ShapeDtypeStruct((B, S, W_B_OUT), BF16),
        scratch_shapes=[
            pltpu.VMEM((nq, DV_B, 2 * tq), BF16),
            pltpu.VMEM((S // tk, va_rows, tk), BF16),
            pltpu.VMEM((tk, 2 * tq), F32),
            pltpu.VMEM((tk, 2 * tq), F32),
            pltpu.VMEM((1, 2 * tq), F32),
            pltpu.VMEM((1, 2 * tq), F32),
            pltpu.VMEM((ACC_SLOTS, 1, 2 * tq), F32),
            pltpu.VMEM((ACC_SLOTS, va_rows, 2 * tq), F32),
        ],
        compiler_params=_params("parallel", "parallel"),
        name="diff_attn",
    )(lam_p, pt, kb, pt, bias, g_col)


def _toeplitz_kernel(w_ref, o_ref, *, diff):
    width = o_ref.shape[-1]
    rows = width if diff else o_ref.shape[-2]
    t = pltpu.roll(jnp.broadcast_to(w_ref[...], (rows, width)), 0, 1, stride=1, stride_axis=0)
    if not diff:
        o_ref[...] = t
        return
    r = lax.broadcasted_iota(jnp.int32, (rows, width), 0)
    c = lax.broadcasted_iota(jnp.int32, (rows, width), 1)
    o_ref[0] = jnp.where(c >= r, t, NEG_INF).astype(o_ref.dtype)
    o_ref[1] = jnp.where(c < r, t, 0.0).astype(o_ref.dtype)


def _toeplitz(w, rows, diff):
    n, _, width = w.shape
    oshape = (n, 2, rows, width) if diff else (n, rows, width)
    oblock = (None,) + oshape[1:]
    return pl.pallas_call(
        functools.partial(_toeplitz_kernel, diff=diff),
        grid=(n,),
        in_specs=[pl.BlockSpec((None, 1, width), lambda i: (i, 0, 0))],
        out_specs=pl.BlockSpec(oblock, lambda i: (i,) + (0,) * (len(oshape) - 1)),
        out_shape=jax.ShapeDtypeStruct(oshape, BF16 if diff else F32),
        compiler_params=_params("parallel"),
        name="bias_diff" if diff else "bias_dilated",
    )(w)


def _layer_norm(h, g, b):
    mu = jnp.mean(h, axis=-1, keepdims=True)
    d = h - mu
    var = jnp.mean(d * d, axis=-1, keepdims=True)
    return d * lax.rsqrt(var + LN_EPS) * g + b


def _mix_kernel(x_ref, o0_ref, o1_ref, o2_ref, l0_ref, l1_ref, l2_ref, ob_ref, wg_ref, bg_ref, wa_ref, wb_ref,
                wo_ref, g_ref, b_ref, out_ref, *tok_sc):
    def token_major(ref, sc):
        dil, rows, _ = ref.shape
        if dil == 1:
            return lambda sl: ref[0, sl]
        nk = sc.shape[0]
        for res in range(dil):
            for k in range(nk):
                sc[k, pl.ds(res, rows, stride=dil), :] = ref[res, :, k * LANES:(k + 1) * LANES]
        return lambda sl: jnp.concatenate([sc[k, sl] for k in range(nk)], axis=1)

    l_g = (token_major(l0_ref, None), token_major(l1_ref, tok_sc[0]), token_major(l2_ref, tok_sc[1]))
    o_g = (token_major(o0_ref, None), token_major(o1_ref, tok_sc[2]), token_major(o2_ref, tok_sc[3]))
    tm = x_ref.shape[0]
    for r0 in range(0, tm, ROW_SLAB):
        sl = slice(r0, r0 + ROW_SLAB)
        x = x_ref[sl]
        xb = x.astype(BF16)
        l0, l1, l2 = (f(sl) for f in l_g)
        mx = jnp.maximum(jnp.maximum(l0, l1), l2)
        e0, e1, e2 = jnp.exp(l0 - mx), jnp.exp(l1 - mx), jnp.exp(l2 - mx)
        o_a = (e0 * o_g[0](sl) + e1 * o_g[1](sl) + e2 * o_g[2](sl)) / (e0 + e1 + e2)
        y_a = _dot(o_a.astype(BF16), wa_ref[...])
        y_b = _dot(ob_ref[sl], wb_ref[...])
        gate_a = jax.nn.sigmoid(_dot(xb, wg_ref[:, :D_MODEL]) + bg_ref[:, :D_MODEL])
        gate_b = jax.nn.sigmoid(_dot(xb, wg_ref[:, D_MODEL:]) + bg_ref[:, D_MODEL:])
        merged = gate_a * y_a + gate_b * y_b
        mix = _dot(merged.astype(BF16), wo_ref[...])
        out_ref[sl] = _layer_norm(DEEPNORM_ALPHA * x + mix, g_ref[...], b_ref[...])


def _mix(x, o_g, lse_g, o_b, w_gate, b_gate, w_a, w_b, w_o, ln_g, ln_b, tm):
    B, S, D = x.shape
    tok = lambda w: pl.BlockSpec((None, tm, w), lambda b, i: (b, i, 0))
    res = [pl.BlockSpec((None, a.shape[1], tm // a.shape[1], W_A_OUT), lambda b, i: (b, 0, i, 0)) for a in o_g]
    n_sc = 2 * sum(a.shape[1] > 1 for a in o_g)
    return pl.pallas_call(
        _mix_kernel,
        grid=(B, S // tm),
        in_specs=[tok(D)] + res + res + [tok(W_B_OUT)]
        + [_resident(a.shape) for a in (w_gate, b_gate, w_a, w_b, w_o, ln_g, ln_b)],
        out_specs=tok(D),
        out_shape=jax.ShapeDtypeStruct((B, S, D), F32),
        scratch_shapes=[pltpu.VMEM((W_A_OUT // LANES, tm, LANES), F32)] * n_sc,
        compiler_params=_params("parallel", "parallel"),
        name="mix",
    )(x, *o_g, *lse_g, o_b, w_gate, b_gate, w_a, w_b, w_o, ln_g, ln_b)


def _mlp_kernel(x_ref, w1_ref, w2_ref, g_ref, b_ref, out_ref):
    for r0 in range(0, x_ref.shape[0], ROW_SLAB):
        rows = slice(r0, r0 + ROW_SLAB)
        x = x_ref[rows]
        xb = x.astype(BF16)
        ff = jnp.zeros(x.shape, F32)
        for c in range(D_FF // (2 * CHUNK)):
            sl = slice(c * 2 * CHUNK, (c + 1) * 2 * CHUNK)
            h = jnp.maximum(_dot(xb, w1_ref[:, sl]), 0.0)
            ff = ff + _dot((h * h).astype(BF16), w2_ref[sl, :])
        out_ref[rows] = _layer_norm(DEEPNORM_ALPHA * x + ff, g_ref[...], b_ref[...])


def _mlp(x, w1, w2, ln_g, ln_b, tm):
    B, S, D = x.shape
    tok = pl.BlockSpec((None, tm, D), lambda b, i: (b, i, 0))
    return pl.pallas_call(
        _mlp_kernel,
        grid=(B, S // tm),
        in_specs=[tok] + [_resident(a.shape) for a in (w1, w2, ln_g, ln_b)],
        out_specs=tok,
        out_shape=jax.ShapeDtypeStruct((B, S, D), F32),
        compiler_params=_params("parallel", "parallel"),
        name="mlp",
    )(x, w1, w2, ln_g, ln_b)


def _t5_bucket(dist):
    n = jnp.maximum(dist, 0)
    max_exact = NUM_BUCKETS // 2
    nf = jnp.maximum(n, 1).astype(F32)
    large = max_exact + (jnp.log(nf / max_exact) / math.log(T5_MAX_DISTANCE / max_exact)
                         * (NUM_BUCKETS - max_exact)).astype(jnp.int32)
    large = jnp.minimum(large, NUM_BUCKETS - 1)
    return jnp.where(n < max_exact, n, large)


def _dilated_bias(rel_bias, g, dil):
    j = jnp.arange(2 * BLK)
    tab = rel_bias[:, g * H_A:(g + 1) * H_A].astype(F32)
    vec = tab[_t5_bucket(jnp.maximum(BLK - j, 0) * dil)]
    vec = jnp.where((j <= BLK)[:, None], vec, NEG_INF)
    bias = _toeplitz(vec.T.reshape(H_A, 1, 2 * BLK), BLK, diff=False)
    return bias.reshape(H_A // 2, 2 * BLK, 2 * BLK)


def _diff_bias(rel_bias, tq):
    assert tq >= T5_MAX_DISTANCE
    tab = rel_bias[:, N_GROUPS * H_A:].astype(F32)
    tab = (tab - tab[NUM_BUCKETS - 1:NUM_BUCKETS]) * LOG2E
    vec = tab[_t5_bucket(jnp.arange(tq))]
    return _toeplitz(vec.T.reshape(H_B, 1, tq), tq, diff=True)


def kernel(x, w_in, b_gate, lambda_q1, lambda_k1, lambda_q2, lambda_k2, subln_g, rel_bias, w_proj_a, w_proj_b,
           w_out, ln1_g, ln1_b, ln2_g, ln2_b, w_mlp1, w_mlp2):
    B, S, D = x.shape
    tq = min(TQ, S)
    tm = min(TM, S)
    w = w_in[0]
    w_a = w[:, :COLS_A].reshape(D, 3, N_GROUPS, W_A_OUT)
    w_a = jnp.concatenate([w_a[:, :1] * QK_SCALE, w_a[:, 1:]], axis=1).transpose(0, 2, 1, 3).reshape(D, COLS_A)
    w_qk = w[:, COLS_A:COLS_A + COLS_B_QK].reshape(D, 4, H_B, HEAD_DIM)
    w_qb = jnp.concatenate([w_qk[:, 0], w_qk[:, 1]], axis=-1).reshape(D, W_B_OUT)
    w_kb = jnp.concatenate([w_qk[:, 2], w_qk[:, 3]], axis=-1).reshape(D, W_B_OUT)
    w_vb = w[:, COLS_A + COLS_B_QK:COLS_A + COLS_B]
    w_n = jnp.concatenate([w_a, w_kb], axis=1).astype(BF16)
    w_t = jnp.concatenate([w_qb, w_vb], axis=1).T.astype(BF16)
    w_gate = w[:, COLS_A + COLS_B:].astype(BF16)

    *a_g, kb, pt = _proj(x, w_n, w_t, tm, tq)

    o_g, lse_g = [], []
    for g, (win, dil) in enumerate(DIL_PAIRS):
        assert win // dil == BLK
        o, lse = _dilated(a_g[g], _dilated_bias(rel_bias, g, dil), g)
        o_g.append(o)
        lse_g.append(lse)

    lam_init = 0.8 - 0.6 * math.exp(-0.3 * 0)
    lam_p = jnp.concatenate([lambda_q1, lambda_k1, lambda_q2, lambda_k2], axis=0).astype(F32)
    o_b = _diff(lam_p, kb, pt, _diff_bias(rel_bias, min(DIFF_T, S)), subln_g[0].reshape(DV_B, 1), lam_init)

    x1 = _mix(x, o_g, lse_g, o_b, w_gate, b_gate, w_proj_a[0].astype(BF16), w_proj_b[0].astype(BF16),
              w_out[0].astype(BF16), ln1_g, ln1_b, tm)
    return _mlp(x1, w_mlp1[0].astype(BF16), w_mlp2[0].astype(BF16), ln2_g, ln2_b, tm)
```

```python
import functools
import math

import jax
import jax.numpy as jnp
import numpy as np
from jax import lax
from jax.experimental import pallas as pl
from jax.experimental.pallas import tpu as pltpu

D_MODEL = 1024
HEAD_DIM = 64
DIL_PAIRS = ((128, 1), (512, 4), (2048, 16))
N_GROUPS = len(DIL_PAIRS)
H_A = 8
H_B = 8
DV_B = 2 * HEAD_DIM
D_FF = 4 * D_MODEL
NUM_BUCKETS = 32
T5_MAX_DISTANCE = 128
BLK = 128
LN_EPS = 1e-5
NEG_INF = -1e30
W_A_OUT = H_A * HEAD_DIM
W_B_OUT = H_B * DV_B
COLS_A = 3 * N_GROUPS * H_A * HEAD_DIM
COLS_B_QK = 4 * H_B * HEAD_DIM
COLS_B = COLS_B_QK + H_B * DV_B
DEPTH = 1
DEEPNORM_ALPHA = (2.0 * DEPTH) ** 0.25
QK_SCALE = HEAD_DIM ** -0.5
LOG2E = math.log2(math.e)

LANES = 128
COLS_N = COLS_A + H_B * 2 * HEAD_DIM
COLS_T = 2 * W_B_OUT
CHUNK = 512
TQ = 512
DIFF_T = 1024
STRIP = 256
PAIRS_PER_BLOCK = 2
ACC_SLOTS = PAIRS_PER_BLOCK
TM = 512
ROW_SLAB = 256
DILATED_ROWS = 2048
VMEM_LIMIT = 56 * 1024 * 1024

BF16 = jnp.bfloat16
F32 = jnp.float32


def _dot(a, b):
    return jnp.dot(a, b, preferred_element_type=F32)


def _dot_nt(a, b):
    return lax.dot_general(a, b, (((1,), (1,)), ((), ())), preferred_element_type=F32)


def _resident(shape):
    nd = len(shape)
    return pl.BlockSpec(shape, lambda *_: (0,) * nd, pipeline_mode=pl.Buffered(1))


def _params(*sem, flags=None):
    return pltpu.CompilerParams(dimension_semantics=sem, vmem_limit_bytes=VMEM_LIMIT, flags=flags)


def _proj_kernel(x_ref, wn_ref, wt_ref, a0_ref, a1_ref, a2_ref, kb_ref, ot_ref, rows_sc, *, tq):
    xb = x_ref[...].astype(BF16)
    tm = xb.shape[0]
    a_refs = (a0_ref, a1_ref, a2_ref)
    for c in range(COLS_N // CHUNK):
        r = _dot(xb, wn_ref[:, c * CHUNK:(c + 1) * CHUNK])
        g, part = divmod(c, 3)
        if g >= N_GROUPS:
            kb_ref[:, (c - 3 * N_GROUPS) * CHUNK:(c - 3 * N_GROUPS + 1) * CHUNK] = r.astype(BF16)
            continue
        sl = slice(part * W_A_OUT, (part + 1) * W_A_OUT)
        dil = DIL_PAIRS[g][1]
        if dil == 1:
            a_refs[g][0, :, sl] = r.astype(BF16)
            continue
        for k in range(CHUNK // LANES):
            rows_sc[k] = r[:, k * LANES:(k + 1) * LANES]
        for res in range(dil):
            for k in range(CHUNK // LANES):
                piece = rows_sc[k, pl.ds(res, tm // dil, stride=dil), :]
                a_refs[g][res, :, part * W_A_OUT + k * LANES:part * W_A_OUT + (k + 1) * LANES] = piece.astype(BF16)
    for c in range(COLS_T // CHUNK):
        sl = slice(c * CHUNK, (c + 1) * CHUNK)
        r = _dot_nt(wt_ref[sl, :], xb)
        if (c + 1) * CHUNK <= W_B_OUT:
            r = r * (QK_SCALE * LOG2E)
        r = r.astype(BF16)
        for t in range(tm // tq):
            ot_ref[t, sl, :] = r[:, t * tq:(t + 1) * tq]


def _proj(x, w_n, w_t, tm, tq):
    B, S, D = x.shape
    assert CHUNK == W_A_OUT
    dils = [d for _, d in DIL_PAIRS]
    return pl.pallas_call(
        functools.partial(_proj_kernel, tq=tq),
        grid=(B, S // tm),
        in_specs=[
            pl.BlockSpec((None, tm, D), lambda b, i: (b, i, 0)),
            _resident((D, COLS_N)),
            _resident((COLS_T, D)),
        ],
        out_specs=[pl.BlockSpec((None, d, tm // d, 3 * W_A_OUT), lambda b, i: (b, 0, i, 0)) for d in dils] + [
            pl.BlockSpec((None, tm, W_B_OUT), lambda b, i: (b, i, 0)),
            pl.BlockSpec((None, tm // tq, COLS_T, tq), lambda b, i: (b, i, 0, 0)),
        ],
        out_shape=[jax.ShapeDtypeStruct((B, d, S // d, 3 * W_A_OUT), BF16) for d in dils] + [
            jax.ShapeDtypeStruct((B, S, W_B_OUT), BF16),
            jax.ShapeDtypeStruct((B, S // tq, COLS_T, tq), BF16),
        ],
        scratch_shapes=[pltpu.VMEM((CHUNK // LANES, tm, LANES), F32)],
        compiler_params=_params("parallel", "parallel"),
        name="proj",
    )(x, w_n, w_t)


def _dilated_kernel(q_ref, k_ref, v_ref, kp_ref, vp_ref, bias_ref, o_ref, lse_ref, kw_sc, vw_sc):
    nres, tb, _ = q_ref.shape
    first = pl.program_id(2) == 0
    lane = lax.broadcasted_iota(jnp.int32, (BLK, LANES), 1)
    lo = lane < HEAD_DIM
    in_prev = lax.broadcasted_iota(jnp.int32, (2 * BLK, 2 * BLK), 1) < BLK
    ones = jnp.ones((2 * BLK, LANES), BF16)
    for r in range(nres):
        kw_sc[r, :BLK], kw_sc[r, BLK:] = kp_ref[r], k_ref[r]
        vw_sc[r, :BLK], vw_sc[r, BLK:] = vp_ref[r], v_ref[r]
        for jb in range(tb // BLK):
            rows = slice(jb * BLK, (jb + 1) * BLK)
            win = slice(jb * BLK, (jb + 2) * BLK)
            for p in range(H_A // 2):
                cols = slice(p * LANES, (p + 1) * LANES)
                q2 = q_ref[r, rows, cols]
                zero = jnp.zeros_like(q2)
                qd = jnp.concatenate([jnp.where(lo, q2, zero), jnp.where(lo, zero, q2)], axis=0)
                s = _dot_nt(qd, kw_sc[r, win, cols]) + bias_ref[p]
                if jb == 0:
                    s = jnp.where(first & in_prev, NEG_INF, s)
                m = jnp.max(s, axis=-1, keepdims=True)
                pw = jnp.exp((s - m).astype(BF16))
                oa = _dot(pw, jnp.concatenate([vw_sc[r, win, cols], ones], axis=1))
                den = oa[:, LANES:]
                o2 = oa[:, :LANES] / den
                lse = m + jnp.log(den)
                o_ref[r, rows, cols] = jnp.where(lo, o2[:BLK], o2[BLK:])
                lse_ref[r, rows, cols] = jnp.where(lo, lse[:BLK], lse[BLK:])


def _dilated(a, bias, g):
    B, dil, L, _ = a.shape
    tb = min(DILATED_ROWS, L)
    nres = min(DILATED_ROWS // tb, dil)
    cur = lambda c: pl.BlockSpec((None, nres, tb, W_A_OUT), lambda b, r, n: (b, r, n, c))
    prev = lambda c: pl.BlockSpec(
        (None, nres, BLK, W_A_OUT), lambda b, r, n: (b, r, jnp.maximum(n * (tb // BLK) - 1, 0), c))
    out = pl.BlockSpec((None, nres, tb, W_A_OUT), lambda b, r, n: (b, r, n, 0))
    return pl.pallas_call(
        _dilated_kernel,
        grid=(B, dil // nres, L // tb),
        in_specs=[cur(0), cur(1), cur(2), prev(1), prev(2), _resident(bias.shape)],
        out_specs=[out, out],
        out_shape=[jax.ShapeDtypeStruct((B, dil, L, W_A_OUT), F32)] * 2,
        scratch_shapes=[pltpu.VMEM((nres, BLK + tb, W_A_OUT), BF16)] * 2,
        compiler_params=_params("parallel", "parallel", "arbitrary"),
        name=f"dilated{g}",
    )(a, a, a, a, a, bias)


def _diff_kernel(lam_ref, qt_ref, k_ref, vt_ref, bias_ref, g_ref, o_ref, qd_sc, va_sc, s0_sc, s1_sc, x0_sc, x1_sc,
                 m_sc, acc_sc, *, lam_init):
    nb, _, tb = qt_ref.shape
    tq = tk = bias_ref.shape[-1]
    qb = tq // tb
    nq = nb // qb
    s_bufs, x_bufs = (s0_sc, s1_sc), (x0_sc, x1_sc)
    row = lax.broadcasted_iota(jnp.int32, (DV_B, tb), 0)
    for i in range(nb):
        blk, part = divmod(i, qb)
        qt = qt_ref[i]
        zero = jnp.zeros_like(qt)
        qd_sc[blk, :, part * tb:(part + 1) * tb] = jnp.where(row < HEAD_DIM, qt, zero)
        qd_sc[blk, :, tq + part * tb:tq + (part + 1) * tb] = jnp.where(row < HEAD_DIM, zero, qt)
        va_sc[blk, :DV_B, part * tb:(part + 1) * tb] = vt_ref[i]
    for i in range(nq):
        va_sc[i, DV_B:, :] = jnp.ones((va_sc.shape[1] - DV_B, tk), BF16)

    lam = (jnp.exp(jnp.sum(lam_ref[0:1, :] * lam_ref[1:2, :], axis=-1, keepdims=True))
           - jnp.exp(jnp.sum(lam_ref[2:3, :] * lam_ref[3:4, :], axis=-1, keepdims=True)) + lam_init)

    def reset(a):
        m_sc[a] = jnp.full(m_sc.shape[1:], -jnp.inf, F32)
        acc_sc[a] = jnp.zeros(acc_sc.shape[1:], F32)

    def finalize(qi):
        a = qi % ACC_SLOTS
        acc = acc_sc[a]
        on = acc[:DV_B] / acc[DV_B:DV_B + 1]
        o = on[:, :tq] - lam * on[:, tq:]
        ms = jnp.mean(o * o, axis=0, keepdims=True)
        o = o * lax.rsqrt(ms + LN_EPS) * g_ref[...] * (1.0 - lam_init)
        o_ref[pl.ds(pl.multiple_of(qi * tq, tq), tq), :] = o.T.astype(BF16)
        reset(a)

    def is_last(qi, j):
        return j == qi

    strips = [slice(c * STRIP, (c + 1) * STRIP) for c in range(2 * tq // STRIP)]

    def diag_rows(cs):
        return min(tk, cs.start % tq + cs.stop - cs.start)

    def score_strip(s_buf, x_buf, k0, qi, bi, cs):
        q0, width = cs.start % tq, cs.stop - cs.start
        rows = diag_rows(cs) if bi == 0 else tk
        s = _dot(k_ref[pl.ds(k0, rows), :], qd_sc[qi, :, cs])
        if bi == 0:
            s = s + bias_ref[0, :rows, q0:q0 + width].astype(F32)
        elif bi == 1 and q0 < T5_MAX_DISTANCE:
            far = tk - T5_MAX_DISTANCE
            s = jnp.concatenate([s[:far], s[far:] + bias_ref[1, far:, q0:q0 + width].astype(F32)], axis=0)
        s_buf[:rows, cs] = s
        if rows < tk:
            s_buf[rows:, cs] = jnp.full((tk - rows, width), NEG_INF, F32)
        x_buf[:, cs] = jnp.max(s, axis=0, keepdims=True)

    for a in range(ACC_SLOTS):
        reset(a)
    for cs in strips:
        score_strip(s_bufs[0], x_bufs[0], 0, 0, 0, cs)

    def half(slot, qi, j):
        s_cur, x_cur, s_oth, x_oth = s_bufs[slot], x_bufs[slot], s_bufs[1 - slot], x_bufs[1 - slot]
        a = qi % ACC_SLOTS
        last = is_last(qi, j)
        j_n = jnp.where(last, 0, j + 1)
        qi_n = jnp.minimum(jnp.where(last, qi + 1, qi), nq - 1)
        k0 = pl.multiple_of(j_n * tk, tk)
        va = va_sc[j]

        def run(bi, cur_diag=False, produce=True, fin=False):
            if fin:
                finalize(qi - 1)
            for cs in strips:
                if produce:
                    score_strip(s_oth, x_oth, k0, qi_n, bi, cs)
                rows = diag_rows(cs) if cur_diag else tk
                m_old = m_sc[a, :, cs]
                m_new = jnp.maximum(m_old, x_cur[:, cs])
                p = jnp.exp2((s_cur[:rows, cs] - m_new).astype(BF16))
                alpha = jnp.exp2(m_old - m_new)
                m_sc[a, :, cs] = m_new
                acc_sc[a, :, cs] = acc_sc[a, :, cs] * alpha + _dot(va[:, :rows], p)

        final = last & (qi == nq - 1)
        more = jnp.logical_not(final)
        first = (j == 0) & (qi > 0)
        far_next = j_n < qi_n - 1
        for fin, sel in ((False, jnp.logical_not(first)), (True, first)):
            pl.when(sel & more & (j_n == qi_n))(functools.partial(run, 0, fin=fin))
            pl.when(sel & more & (j_n == qi_n - 1))(functools.partial(run, 1, fin=fin))
            pl.when(sel & more & far_next & jnp.logical_not(last))(functools.partial(run, None, fin=fin))
        pl.when(more & far_next & last)(functools.partial(run, None, cur_diag=True))
        pl.when(final)(functools.partial(run, None, cur_diag=True, produce=False))
        return qi_n, j_n

    def run_pairs(count, carry):
        qi, j = carry
        for h in range(count):
            qi, j = half(h % 2, qi, j)
        return qi, j

    npairs = nq * (nq + 1) // 2
    carry = lax.fori_loop(0, npairs // PAIRS_PER_BLOCK, lambda _, c: run_pairs(PAIRS_PER_BLOCK, c),
                          (jnp.int32(0), jnp.int32(0)))
    if npairs % PAIRS_PER_BLOCK:
        run_pairs(npairs % PAIRS_PER_BLOCK, carry)
    finalize(nq - 1)


def _diff(lam_p, kb, pt, bias, g_col, lam_init):
    B, S, _ = kb.shape
    nb, tb = pt.shape[1], pt.shape[3]
    tq = tk = bias.shape[-1]
    nq = S // tq
    assert tq % tb == 0 and tq % STRIP == 0 and bias.shape[1:] == (2, tk, tq)
    va_rows = DV_B + 16
    return pl.pallas_call(
        functools.partial(_diff_kernel, lam_init=lam_init),
        grid=(B, H_B),
        in_specs=[
            _resident(lam_p.shape),
            pl.BlockSpec((None, nb, DV_B, tb), lambda b, h: (b, 0, h, 0)),
            pl.BlockSpec((None, S, LANES), lambda b, h: (b, 0, h)),
            pl.BlockSpec((None, nb, DV_B, tb), lambda b, h: (b, 0, H_B + h, 0)),
            pl.BlockSpec((None, 2, tk, tq), lambda b, h: (h, 0, 0, 0)),
            _resident(g_col.shape),
        ],
        out_specs=pl.BlockSpec((None, S, DV_B), lambda b, h: (b, 0, h)),
        out_shape=jax.ShapeDtypeStruct((B, S, W_B_OUT), BF16),
        scratch_shapes=[
            pltpu.VMEM((nq, DV_B, 2 * tq), BF16),
            pltpu.VMEM((S // tk, va_rows, tk), BF16),
            pltpu.VMEM((tk, 2 * tq), F32),
            pltpu.VMEM((tk, 2 * tq), F32),
            pltpu.VMEM((1, 2 * tq), F32),
            pltpu.VMEM((1, 2 * tq), F32),
            pltpu.VMEM((ACC_SLOTS, 1, 2 * tq), F32),
            pltpu.VMEM((ACC_SLOTS, va_rows, 2 * tq), F32),
        ],
        compiler_params=_params("parallel", "parallel"),
        name="diff_attn",
    )(lam_p, pt, kb, pt, bias, g_col)


def _toeplitz_kernel(w_ref, o_ref, *, diff):
    width = o_ref.shape[-1]
    rows = width if diff else o_ref.shape[-2]
    t = pltpu.roll(jnp.broadcast_to(w_ref[...], (rows, width)), 0, 1, stride=1, stride_axis=0)
    if not diff:
        o_ref[...] = t
        return
    r = lax.broadcasted_iota(jnp.int32, (rows, width), 0)
    c = lax.broadcasted_iota(jnp.int32, (rows, width), 1)
    o_ref[0] = jnp.where(c >= r, t, NEG_INF).astype(o_ref.dtype)
    o_ref[1] = jnp.where(c < r, t, 0.0).astype(o_ref.dtype)


def _toeplitz(w, rows, diff):
    n, _, width = w.shape
    oshape = (n, 2, rows, width) if diff else (n, rows, width)
    oblock = (None,) + oshape[1:]
    return pl.pallas_call(
        functools.partial(_toeplitz_kernel, diff=diff),
        grid=(n,),
        in_specs=[pl.BlockSpec((None, 1, width), lambda i: (i, 0, 0))],
        out_specs=pl.BlockSpec(oblock, lambda i: (i,) + (0,) * (len(oshape) - 1)),
        out_shape=jax.ShapeDtypeStruct(oshape, BF16 if diff else F32),
        compiler_params=_params("parallel"),
        name="bias_diff" if diff else "bias_dilated",
    )(w)


def _layer_norm(h, g, b):
    mu = jnp.mean(h, axis=-1, keepdims=True)
    d = h - mu
    var = jnp.mean(d * d, axis=-1, keepdims=True)
    return d * lax.rsqrt(var + LN_EPS) * g + b


def _mix_kernel(x_ref, o0_ref, o1_ref, o2_ref, l0_ref, l1_ref, l2_ref, ob_ref, wg_ref, bg_ref, wa_ref, wb_ref,
                wo_ref, g_ref, b_ref, out_ref, *tok_sc):
    def token_major(ref, sc):
        dil, rows, _ = ref.shape
        if dil == 1:
            return lambda sl: ref[0, sl]
        nk = sc.shape[0]
        for res in range(dil):
            for k in range(nk):
                sc[k, pl.ds(res, rows, stride=dil), :] = ref[res, :, k * LANES:(k + 1) * LANES]
        return lambda sl: jnp.concatenate([sc[k, sl] for k in range(nk)], axis=1)

    l_g = (token_major(l0_ref, None), token_major(l1_ref, tok_sc[0]), token_major(l2_ref, tok_sc[1]))
    o_g = (token_major(o0_ref, None), token_major(o1_ref, tok_sc[2]), token_major(o2_ref, tok_sc[3]))
    tm = x_ref.shape[0]
    for r0 in range(0, tm, ROW_SLAB):
        sl = slice(r0, r0 + ROW_SLAB)
        x = x_ref[sl]
        xb = x.astype(BF16)
        l0, l1, l2 = (f(sl) for f in l_g)
        mx = jnp.maximum(jnp.maximum(l0, l1), l2)
        e0, e1, e2 = jnp.exp(l0 - mx), jnp.exp(l1 - mx), jnp.exp(l2 - mx)
        o_a = (e0 * o_g[0](sl) + e1 * o_g[1](sl) + e2 * o_g[2](sl)) / (e0 + e1 + e2)
        y_a = _dot(o_a.astype(BF16), wa_ref[...])
        y_b = _dot(ob_ref[sl], wb_ref[...])
        gate_a = jax.nn.sigmoid(_dot(xb, wg_ref[:, :D_MODEL]) + bg_ref[:, :D_MODEL])
        gate_b = jax.nn.sigmoid(_dot(xb, wg_ref[:, D_MODEL:]) + bg_ref[:, D_MODEL:])
        merged = gate_a * y_a + gate_b * y_b
        mix = _dot(merged.astype(BF16), wo_ref[...])
        out_ref[sl] = _layer_norm(DEEPNORM_ALPHA * x + mix, g_ref[...], b_ref[...])


def _mix(x, o_g, lse_g, o_b, w_gate, b_gate, w_a, w_b, w_o, ln_g, ln_b, tm):
    B, S, D = x.shape
    tok = lambda w: pl.BlockSpec((None, tm, w), lambda b, i: (b, i, 0))
    res = [pl.BlockSpec((None, a.shape[1], tm // a.shape[1], W_A_OUT), lambda b, i: (b, 0, i, 0)) for a in o_g]
    n_sc = 2 * sum(a.shape[1] > 1 for a in o_g)
    return pl.pallas_call(
        _mix_kernel,
        grid=(B, S // tm),
        in_specs=[tok(D)] + res + res + [tok(W_B_OUT)]
        + [_resident(a.shape) for a in (w_gate, b_gate, w_a, w_b, w_o, ln_g, ln_b)],
        out_specs=tok(D),
        out_shape=jax.ShapeDtypeStruct((B, S, D), F32),
        scratch_shapes=[pltpu.VMEM((W_A_OUT // LANES, tm, LANES), F32)] * n_sc,
        compiler_params=_params("parallel", "parallel"),
        name="mix",
    )(x, *o_g, *lse_g, o_b, w_gate, b_gate, w_a, w_b, w_o, ln_g, ln_b)


def _mlp_kernel(x_ref, w1_ref, w2_ref, g_ref, b_ref, out_ref):
    for r0 in range(0, x_ref.shape[0], ROW_SLAB):
        rows = slice(r0, r0 + ROW_SLAB)
        x = x_ref[rows]
        xb = x.astype(BF16)
        ff = jnp.zeros(x.shape, F32)
        for c in range(D_FF // (2 * CHUNK)):
            sl = slice(c * 2 * CHUNK, (c + 1) * 2 * CHUNK)
            h = jnp.maximum(_dot(xb, w1_ref[:, sl]), 0.0)
            ff = ff + _dot((h * h).astype(BF16), w2_ref[sl, :])
        out_ref[rows] = _layer_norm(DEEPNORM_ALPHA * x + ff, g_ref[...], b_ref[...])


def _mlp(x, w1, w2, ln_g, ln_b, tm):
    B, S, D = x.shape
    tok = pl.BlockSpec((None, tm, D), lambda b, i: (b, i, 0))
    return pl.pallas_call(
        _mlp_kernel,
        grid=(B, S // tm),
        in_specs=[tok] + [_resident(a.shape) for a in (w1, w2, ln_g, ln_b)],
        out_specs=tok,
        out_shape=jax.ShapeDtypeStruct((B, S, D), F32),
        compiler_params=_params("parallel", "parallel"),
        name="mlp",
    )(x, w1, w2, ln_g, ln_b)


def _t5_bucket(dist):
    n = jnp.maximum(dist, 0)
    max_exact = NUM_BUCKETS // 2
    nf = jnp.maximum(n, 1).astype(F32)
    large = max_exact + (jnp.log(nf / max_exact) / math.log(T5_MAX_DISTANCE / max_exact)
                         * (NUM_BUCKETS - max_exact)).astype(jnp.int32)
    large = jnp.minimum(large, NUM_BUCKETS - 1)
    return jnp.where(n < max_exact, n, large)


def _dilated_bias(rel_bias, g, dil):
    j = jnp.arange(2 * BLK)
    tab = rel_bias[:, g * H_A:(g + 1) * H_A].astype(F32)
    vec = tab[_t5_bucket(jnp.maximum(BLK - j, 0) * dil)]
    vec = jnp.where((j <= BLK)[:, None], vec, NEG_INF)
    bias = _toeplitz(vec.T.reshape(H_A, 1, 2 * BLK), BLK, diff=False)
    return bias.reshape(H_A // 2, 2 * BLK, 2 * BLK)


def _diff_bias(rel_bias, tq):
    assert tq >= T5_MAX_DISTANCE
    tab = rel_bias[:, N_GROUPS * H_A:].astype(F32)
    tab = (tab - tab[NUM_BUCKETS - 1:NUM_BUCKETS]) * LOG2E
    vec = tab[_t5_bucket(jnp.arange(tq))]
    return _toeplitz(vec.T.reshape(H_B, 1, tq), tq, diff=True)


def kernel(x, w_in, b_gate, lambda_q1, lambda_k1, lambda_q2, lambda_k2, subln_g, rel_bias, w_proj_a, w_proj_b,
           w_out, ln1_g, ln1_b, ln2_g, ln2_b, w_mlp1, w_mlp2):
    B, S, D = x.shape
    tq = min(TQ, S)
    tm = min(TM, S)
    w = w_in[0]
    w_a = w[:, :COLS_A].reshape(D, 3, N_GROUPS, W_A_OUT)
    w_a = jnp.concatenate([w_a[:, :1] * QK_SCALE, w_a[:, 1:]], axis=1).transpose(0, 2, 1, 3).reshape(D, COLS_A)
    w_qk = w[:, COLS_A:COLS_A + COLS_B_QK].reshape(D, 4, H_B, HEAD_DIM)
    w_qb = jnp.concatenate([w_qk[:, 0], w_qk[:, 1]], axis=-1).reshape(D, W_B_OUT)
    w_kb = jnp.concatenate([w_qk[:, 2], w_qk[:, 3]], axis=-1).reshape(D, W_B_OUT)
    w_vb = w[:, COLS_A + COLS_B_QK:COLS_A + COLS_B]
    w_n = jnp.concatenate([w_a, w_kb], axis=1).astype(BF16)
    w_t = jnp.concatenate([w_qb, w_vb], axis=1).T.astype(BF16)
    w_gate = w[:, COLS_A + COLS_B:].astype(BF16)

    *a_g, kb, pt = _proj(x, w_n, w_t, tm, tq)

    o_g, lse_g = [], []
    for g, (win, dil) in enumerate(DIL_PAIRS):
        assert win // dil == BLK
        o, lse = _dilated(a_g[g], _dilated_bias(rel_bias, g, dil), g)
        o_g.append(o)
        lse_g.append(lse)

    lam_init = 0.8 - 0.6 * math.exp(-0.3 * 0)
    lam_p = jnp.concatenate([lambda_q1, lambda_k1, lambda_q2, lambda_k2], axis=0).astype(F32)
    o_b = _diff(lam_p, kb, pt, _diff_bias(rel_bias, min(DIFF_T, S)), subln_g[0].reshape(DV_B, 1), lam_init)

    x1 = _mix(x, o_g, lse_g, o_b, w_gate, b_gate, w_proj_a[0].astype(BF16), w_proj_b[0].astype(BF16),
              w_out[0].astype(BF16), ln1_g, ln1_b, tm)
    return _mlp(x1, w_mlp1[0].astype(BF16), w_mlp2[0].astype(BF16), ln2_g, ln2_b, tm)
```
